```python
import math
import jax, jax.numpy as jnp
from jax import lax
import numpy as np

D_MODEL = 1024
BATCH = 32
SEQ = 2048
DEPTH = 1
DEC_BATCH = 128
DEC_SEQ = 1
PAST_LEN = 8192
PAGE_SIZE = 128

N_HEADS_A = 8
HEAD_DIM_A = 64
N_KV_A = 2
GROUP_R = N_HEADS_A // N_KV_A
D_A = N_HEADS_A * HEAD_DIM_A
KV_W = N_KV_A * HEAD_DIM_A
CMP_STRIDE = 16
CMP_LEN = 2 * CMP_STRIDE
SEL_BLOCK = 64
TOP_K_BLOCKS = 16
WINDOW = 512
N_HEADS_M = 4
HEAD_DIM_M = 128
D_M = N_HEADS_M * HEAD_DIM_M
CONV_W = 4
CHUNK = 64
N_BUCKETS = 32
MAX_DISTANCE = 128
Q_BLOCK = 128
D_FF = 256 * ((8 * D_MODEL // 3 + 255) // 256)
EPS = 1e-6
SPLITS = (D_A, KV_W, KV_W, KV_W, KV_W, KV_W, KV_W, 3 * N_HEADS_A, 2 * D_M, D_M, D_M, N_HEADS_M, N_HEADS_M)
D_IN = sum(SPLITS)

kernel_name = 'hymba_nsa_mlstm_macaron_step'


def rms_norm(x, g):
    xf = x.astype(jnp.float32)
    y = xf * lax.rsqrt(jnp.mean(xf * xf, axis=-1, keepdims=True) + EPS)
    return (y * g.astype(jnp.float32)).astype(x.dtype)


def swiglu(x, w_gate, w_up, w_down):
    return (jax.nn.silu(x @ w_gate) * (x @ w_up)) @ w_down


def t5_bucket(dist):
    n = jnp.maximum(dist, 0)
    exact = N_BUCKETS // 2
    nf = jnp.maximum(n, 1).astype(jnp.float32)
    large = exact + (jnp.log(nf / exact) / math.log(MAX_DISTANCE / exact) * (N_BUCKETS - exact)).astype(jnp.int32)
    return jnp.where(n < exact, n, jnp.minimum(large, N_BUCKETS - 1))


def masked_softmax(logits, mask):
    lf = logits.astype(jnp.float32)
    mx = jnp.max(jnp.where(mask, lf, -jnp.inf), axis=-1, keepdims=True)
    mx = jnp.where(jnp.isfinite(mx), mx, 0.0)
    e = jnp.where(mask, jnp.exp(lf - mx), 0.0)
    return e / jnp.maximum(jnp.sum(e, axis=-1, keepdims=True), 1e-30)


def last_rows(a, n):
    pad = max(n - a.shape[1], 0)
    a = jnp.pad(a, [(0, 0), (pad, 0)] + [(0, 0)] * (a.ndim - 2))
    return a[:, a.shape[1] - n:]


def cover_matrix(n_cmp, n_sel):
    cs = np.arange(n_cmp)[:, None] * CMP_STRIDE
    bs = np.arange(n_sel)[None, :] * SEL_BLOCK
    shared = np.clip(np.minimum(cs + CMP_LEN, bs + SEL_BLOCK) - np.maximum(cs, bs), 0, None)
    return jnp.asarray(shared / CMP_LEN, dtype=jnp.float32)


def compress(rows, w, b):
    n, t, g, d = rows.shape
    n16 = t // CMP_STRIDE
    ch = rows[:, :n16 * CMP_STRIDE].reshape(n, n16, CMP_STRIDE, g, d)
    first = jnp.einsum('njigd,ide->njge', ch, w[:CMP_STRIDE])
    second = jnp.einsum('njigd,ide->njge', ch, w[CMP_STRIDE:])
    return first[:, :-1] + second[:, 1:] + b


def compress_kv(cmp_rows, kn_cmp_g, w_cmp_k, b_cmp_k, w_cmp_v, b_cmp_v):
    kc = rms_norm(compress(cmp_rows[:, :, 0], w_cmp_k, b_cmp_k), kn_cmp_g)
    vc = compress(cmp_rows[:, :, 1], w_cmp_v, b_cmp_v)
    return kc, vc


def attend_shared(qg, k, v, qpos, kpos, mask, rel_table):
    nq, ns = qpos.shape[0], kpos.shape[0]
    bias = rel_table[t5_bucket(qpos[:, None] - kpos[None, :])]
    bias = bias.reshape(nq, ns, N_KV_A, GROUP_R).transpose(0, 2, 3, 1)
    logits = jnp.einsum('nqgrd,nsgd->nqgrs', qg, k) + bias
    p = masked_softmax(logits, mask[:, None, None, :])
    return jnp.einsum('nqgrs,nsgd->nqgrd', p.astype(v.dtype), v), p


def nsa_attend(q, qpos, kc, vc, kw, vw, wpos, gates, rel_table, gather_sel, n_sel):
    nb, nq = q.shape[:2]
    qg = q.reshape(nb, nq, N_KV_A, GROUP_R, HEAD_DIM_A)
    n_cmp = kc.shape[1]
    cmp_end = jnp.arange(n_cmp) * CMP_STRIDE + (CMP_LEN - 1)
    o_c, p_c = attend_shared(qg, kc, vc, qpos, cmp_end, cmp_end[None, :] <= qpos[:, None], rel_table)
    imp = jnp.einsum('nqgrj,jb->nqgb', p_c, cover_matrix(n_cmp, n_sel))
    blk = jnp.arange(n_sel)[None, :]
    cur = (qpos // SEL_BLOCK)[:, None]
    valid = blk * SEL_BLOCK <= qpos[:, None]
    forced = valid & ((blk == 0) | (blk == cur) | (blk == cur - 1))
    score = jnp.where(forced[:, None, :], jnp.inf, jnp.where(valid[:, None, :], imp, -jnp.inf))
    _, idx = lax.top_k(score, min(TOP_K_BLOCKS, n_sel))
    ks, vs = gather_sel(idx)
    n_keys = idx.shape[-1] * SEL_BLOCK
    ks = ks.reshape(nb, nq, N_KV_A, n_keys, HEAD_DIM_A)
    vs = vs.reshape(nb, nq, N_KV_A, n_keys, HEAD_DIM_A)
    kpos = (idx[..., None] * SEL_BLOCK + jnp.arange(SEL_BLOCK)).reshape(nb, nq, N_KV_A, n_keys)
    tbl = rel_table.reshape(N_BUCKETS, N_KV_A, GROUP_R)
    bucket = t5_bucket(qpos[None, :, None, None] - kpos)
    bias = jnp.moveaxis(tbl[bucket, jnp.arange(N_KV_A)[:, None]], -1, 3)
    logits = jnp.einsum('nqgrd,nqgsd->nqgrs', qg, ks) + bias
    p_s = masked_softmax(logits, (kpos <= qpos[None, :, None, None])[:, :, :, None, :])
    o_s = jnp.einsum('nqgrs,nqgsd->nqgrd', p_s.astype(vs.dtype), vs)
    wmask = (wpos[None, :] <= qpos[:, None]) & (qpos[:, None] - wpos[None, :] < WINDOW) & (wpos[None, :] >= 0)
    o_w, _ = attend_shared(qg, kw, vw, qpos, wpos, wmask, rel_table)
    o = jnp.stack([o_c, o_s, o_w], axis=2).reshape(nb, nq, 3, N_HEADS_A, HEAD_DIM_A)
    return jnp.sum(gates[..., None] * o, axis=2).reshape(nb, nq, D_A)


def nsa_prompt(q, kc, vc, sel_kv, win_kv, gates, rel_table):
    nb, t = q.shape[:2]
    nqb = t // Q_BLOCK
    n_sel = -(-t // SEL_BLOCK)
    win_pad = jnp.pad(win_kv, ((0, 0), (WINDOW, 0), (0, 0), (0, 0), (0, 0)))
    q_items = q.reshape(nb * nqb, Q_BLOCK, N_HEADS_A, HEAD_DIM_A)
    g_items = gates.reshape(nb * nqb, Q_BLOCK, 3, N_HEADS_A)
    item_ids = jnp.arange(nb * nqb)
    b_ids = item_ids // nqb
    q_starts = (item_ids % nqb) * Q_BLOCK
    garange = jnp.arange(N_KV_A)[None, None, :, None, None]

    def item(args):
        qi, gi, b, qs = args
        qpos = qs + jnp.arange(Q_BLOCK)
        src = sel_kv[b]

        def gather_sel(idx):
            pos = idx[..., None] * SEL_BLOCK + jnp.arange(SEL_BLOCK)
            rows = src[pos, :, garange, :]
            return rows[..., 0, :], rows[..., 1, :]

        w = lax.dynamic_slice(win_pad, (b, qs, 0, 0, 0), (1, WINDOW + Q_BLOCK, 2, N_KV_A, HEAD_DIM_A))
        wpos = qs - WINDOW + jnp.arange(WINDOW + Q_BLOCK)
        out = nsa_attend(qi[None], qpos, kc[b][None], vc[b][None], w[:, :, 0], w[:, :, 1], wpos,
                         gi[None], rel_table, gather_sel, n_sel)
        return out[0]

    out = lax.map(item, (q_items, g_items, b_ids, q_starts))
    return out.reshape(nb, t, D_A)


def nsa_sample(q, cmp_new, sel_new, win_new, gates, cache_cmp, cache_sel, cache_win, page_table, rel_table,
               kn_cmp_g, w_cmp_k, b_cmp_k, w_cmp_v, b_cmp_v):
    nb, ns = q.shape[:2]
    t_tot = PAST_LEN + ns
    past_cmp = cache_cmp[page_table].reshape((nb, PAST_LEN) + cache_cmp.shape[2:])
    full_cmp = jnp.concatenate([past_cmp.astype(cmp_new.dtype), cmp_new], axis=1)
    kc, vc = compress_kv(full_cmp, kn_cmp_g, w_cmp_k, b_cmp_k, w_cmp_v, b_cmp_v)
    wb = cache_win.shape[1]
    win = jnp.concatenate([cache_win.astype(win_new.dtype), win_new], axis=1)
    wpos = PAST_LEN - wb + jnp.arange(wb + ns)
    qpos = PAST_LEN + jnp.arange(ns)
    bidx = jnp.arange(nb)[:, None, None, None, None]
    garange = jnp.arange(N_KV_A)[None, None, :, None, None]

    def gather_sel(idx):
        pos = idx[..., None] * SEL_BLOCK + jnp.arange(SEL_BLOCK)
        ppos = jnp.minimum(pos, PAST_LEN - 1)
        page = page_table[bidx, ppos // PAGE_SIZE]
        past_rows = cache_sel[page, ppos % PAGE_SIZE, :, garange, :].astype(sel_new.dtype)
        new_rows = sel_new[bidx, jnp.clip(pos - PAST_LEN, 0, ns - 1), :, garange, :]
        rows = jnp.where((pos < PAST_LEN)[..., None, None], past_rows, new_rows)
        return rows[..., 0, :], rows[..., 1, :]

    out = nsa_attend(q, qpos, kc, vc, win[:, :, 0], win[:, :, 1], wpos, gates, rel_table, gather_sel,
                     -(-t_tot // SEL_BLOCK))
    return out, last_rows(win, wb)


def causal_conv_silu(x_ext, w, b):
    length = x_ext.shape[1] - (CONV_W - 1)
    y = sum(x_ext[:, i:i + length] * w[i] for i in range(CONV_W)) + b
    return jax.nn.silu(y)


def mlstm_chunk(state, inp):
    c, n, m = state
    q, k, v, ig, lf = inp
    length = q.shape[1]
    b = jnp.cumsum(lf, axis=1)
    causal = jnp.tril(jnp.ones((length, length), bool))
    dlog = b[:, :, None, :] - b[:, None, :, :] + ig[:, None, :, :]
    dlog = jnp.where(causal[None, :, :, None], dlog, -jnp.inf)
    inter = b + m[:, None, :]
    m_t = jnp.maximum(inter, jnp.max(dlog, axis=2))
    s_qk = jnp.einsum('nthd,nshd->ntsh', q, k) * jnp.exp(dlog - m_t[:, :, None, :])
    dec = jnp.exp(inter - m_t)
    num = jnp.einsum('ntsh,nshd->nthd', s_qk, v) + dec[..., None] * jnp.einsum('nhde,nthe->nthd', c, q)
    den = jnp.sum(s_qk, axis=2) + dec * jnp.einsum('nhd,nthd->nth', n, q)
    h = num / jnp.maximum(jnp.abs(den), jnp.exp(-m_t))[..., None]
    m_new = m_t[:, -1]
    w_end = jnp.exp(b[:, -1:, :] - b + ig - m_new[:, None, :])
    dec_end = jnp.exp(b[:, -1] + m - m_new)
    c_new = dec_end[..., None, None] * c + jnp.einsum('nsh,nshd,nshe->nhde', w_end, v, k)
    n_new = dec_end[..., None] * n + jnp.einsum('nsh,nshd->nhd', w_end, k)
    return (c_new, n_new, m_new), h


def mlstm_mix(qk_ext, v_m, o_m, i_m, f_m, c0, n0, m0, conv_w, conv_b, b_if, mn_g, chunk):
    nb, t, _ = v_m.shape
    f32 = jnp.float32
    qk = causal_conv_silu(qk_ext, conv_w, conv_b)
    heads = lambda a: a.reshape(nb, t, N_HEADS_M, HEAD_DIM_M).astype(f32)
    q = heads(qk[..., :D_M])
    k = heads(qk[..., D_M:]) * (HEAD_DIM_M ** -0.5)
    v = heads(v_m)
    ig = (i_m + b_if[:N_HEADS_M]).astype(f32)
    lf = jax.nn.log_sigmoid((f_m + b_if[N_HEADS_M:]).astype(f32))
    nc = t // chunk
    to_chunks = lambda a: jnp.moveaxis(a.reshape((nb, nc, chunk) + a.shape[2:]), 1, 0)
    (c, n, m), h = lax.scan(mlstm_chunk, (c0.astype(f32), n0.astype(f32), m0.astype(f32)),
                            (to_chunks(q), to_chunks(k), to_chunks(v), to_chunks(ig), to_chunks(lf)))
    h = jnp.moveaxis(h, 0, 1).reshape(nb, t, N_HEADS_M, HEAD_DIM_M)
    h = rms_norm(h, mn_g.reshape(N_HEADS_M, HEAD_DIM_M)).reshape(nb, t, D_M)
    return jax.nn.sigmoid(o_m) * h.astype(o_m.dtype), c, n, m


def pre_mix(x, g_ffn1, w1_gate, w1_up, w1_down, g_mix, w_in, qn_g, kn_sel_g, kn_win_g):
    x = x + 0.5 * swiglu(rms_norm(x, g_ffn1), w1_gate, w1_up, w1_down)
    offsets = np.cumsum(SPLITS)[:-1].tolist()
    q_a, k_c, v_c, k_s, v_s, k_w, v_w, g_a, qk_m, v_m, o_m, i_m, f_m = jnp.split(rms_norm(x, g_mix) @ w_in, offsets, axis=-1)
    nb, t, _ = x.shape
    heads = lambda a, h: a.reshape(nb, t, h, HEAD_DIM_A)
    q = rms_norm(heads(q_a, N_HEADS_A), qn_g) * (HEAD_DIM_A ** -0.5)
    cmp_kv = jnp.stack([heads(k_c, N_KV_A), heads(v_c, N_KV_A)], axis=2)
    sel_kv = jnp.stack([rms_norm(heads(k_s, N_KV_A), kn_sel_g), heads(v_s, N_KV_A)], axis=2)
    win_kv = jnp.stack([rms_norm(heads(k_w, N_KV_A), kn_win_g), heads(v_w, N_KV_A)], axis=2)
    gates = jax.nn.sigmoid(g_a.reshape(nb, t, 3, N_HEADS_A))
    return x, q, cmp_kv, sel_kv, win_kv, gates, qk_m, v_m, o_m, i_m, f_m


def post_mix(x, a_out, m_out, w_out, g_ffn2, w2_gate, w2_up, w2_down):
    x = x + jnp.concatenate([a_out, m_out], axis=-1) @ w_out
    return x + 0.5 * swiglu(rms_norm(x, g_ffn2), w2_gate, w2_up, w2_down)


def prompt_layer(x, rel_table, g_ffn1, w1_gate, w1_up, w1_down, g_mix, w_in, qn_g, kn_cmp_g, kn_sel_g, kn_win_g,
                 w_cmp_k, b_cmp_k, w_cmp_v, b_cmp_v, conv_w, conv_b, b_if, mn_g, w_out, g_ffn2, w2_gate, w2_up, w2_down):
    x, q, cmp_kv, sel_kv, win_kv, gates, qk_m, v_m, o_m, i_m, f_m = pre_mix(
        x, g_ffn1, w1_gate, w1_up, w1_down, g_mix, w_in, qn_g, kn_sel_g, kn_win_g)
    kc, vc = compress_kv(cmp_kv, kn_cmp_g, w_cmp_k, b_cmp_k, w_cmp_v, b_cmp_v)
    a_out = nsa_prompt(q, kc, vc, sel_kv, win_kv, gates, rel_table)
    nb = x.shape[0]
    zeros = lambda *s: jnp.zeros((nb,) + s, jnp.float32)
    qk_ext = jnp.pad(qk_m, ((0, 0), (CONV_W - 1, 0), (0, 0)))
    m_out, c, n, m = mlstm_mix(qk_ext, v_m, o_m, i_m, f_m, zeros(N_HEADS_M, HEAD_DIM_M, HEAD_DIM_M),
                               zeros(N_HEADS_M, HEAD_DIM_M), zeros(N_HEADS_M), conv_w, conv_b, b_if, mn_g, CHUNK)
    y = post_mix(x, a_out, m_out, w_out, g_ffn2, w2_gate, w2_up, w2_down)
    win_buf = min(WINDOW, PAST_LEN)
    return y, (cmp_kv, sel_kv, last_rows(win_kv, win_buf), c, n, m, last_rows(qk_m, CONV_W - 1))


def sample_layer(x, cache_cmp, cache_sel, cache_win, c0, n0, m0, conv0, page_table, rel_table,
                 g_ffn1, w1_gate, w1_up, w1_down, g_mix, w_in, qn_g, kn_cmp_g, kn_sel_g, kn_win_g,
                 w_cmp_k, b_cmp_k, w_cmp_v, b_cmp_v, conv_w, conv_b, b_if, mn_g, w_out, g_ffn2, w2_gate, w2_up, w2_down):
    x, q, cmp_kv, sel_kv, win_kv, gates, qk_m, v_m, o_m, i_m, f_m = pre_mix(
        x, g_ffn1, w1_gate, w1_up, w1_down, g_mix, w_in, qn_g, kn_sel_g, kn_win_g)
    a_out, win_new = nsa_sample(q, cmp_kv, sel_kv, win_kv, gates, cache_cmp, cache_sel, cache_win, page_table,
                                rel_table, kn_cmp_g, w_cmp_k, b_cmp_k, w_cmp_v, b_cmp_v)
    qk_ext = jnp.concatenate([conv0.astype(qk_m.dtype), qk_m], axis=1)
    m_out, c, n, m = mlstm_mix(qk_ext, v_m, o_m, i_m, f_m, c0, n0, m0, conv_w, conv_b, b_if, mn_g, x.shape[1])
    y = post_mix(x, a_out, m_out, w_out, g_ffn2, w2_gate, w2_up, w2_down)
    return y, (cmp_kv, sel_kv, win_new, c, n, m, last_rows(qk_ext, CONV_W - 1))


def setup_inputs(seed: int = 0) -> dict:
    key = jax.random.key(seed)
    k = jax.random.split(key, 40)
    f32 = jnp.float32
    n_pages = PAST_LEN // PAGE_SIZE
    n_pool = (DEC_BATCH * n_pages * 5) // 4
    win_buf = min(WINDOW, PAST_LEN)
    L = DEPTH
    kv_row = (2, N_KV_A, HEAD_DIM_A)

    def nrm(kk, shape, scale):
        return scale * jax.random.normal(kk, shape, f32)

    def gain(kk, shape):
        return 1.0 + nrm(kk, shape, 0.02)

    b_i = nrm(k[20], (L, N_HEADS_M), 0.1)
    b_f = jnp.linspace(3.0, 6.0, N_HEADS_M, dtype=f32) + nrm(k[21], (L, N_HEADS_M), 0.1)
    page_table = jax.random.permutation(k[9], n_pool)[:DEC_BATCH * n_pages].reshape(DEC_BATCH, n_pages).astype(jnp.int32)
    return {
        'x_prompt': nrm(k[0], (BATCH, SEQ, D_MODEL), 1.0),
        'x_sample': nrm(k[1], (DEC_BATCH, DEC_SEQ, D_MODEL), 1.0),
        'cache_cmp_kv': nrm(k[2], (L, n_pool, PAGE_SIZE) + kv_row, 1.0),
        'cache_sel_kv': nrm(k[3], (L, n_pool, PAGE_SIZE) + kv_row, 1.0),
        'cache_win_kv': nrm(k[4], (L, DEC_BATCH, win_buf) + kv_row, 1.0),
        'state_C': nrm(k[5], (L, DEC_BATCH, N_HEADS_M, HEAD_DIM_M, HEAD_DIM_M), 0.05),
        'state_n': nrm(k[6], (L, DEC_BATCH, N_HEADS_M, HEAD_DIM_M), 0.1),
        'state_m': nrm(k[7], (L, DEC_BATCH, N_HEADS_M), 1.0),
        'state_conv': nrm(k[8], (L, DEC_BATCH, CONV_W - 1, 2 * D_M), 1.0),
        'page_table': page_table,
        'rel_bias': nrm(k[10], (N_BUCKETS, N_HEADS_A), 0.5),
        'g_ffn1': gain(k[11], (L, D_MODEL)),
        'w1_gate': nrm(k[12], (L, D_MODEL, D_FF), D_MODEL ** -0.5),
        'w1_up': nrm(k[13], (L, D_MODEL, D_FF), D_MODEL ** -0.5),
        'w1_down': nrm(k[14], (L, D_FF, D_MODEL), D_FF ** -0.5),
        'g_mix': gain(k[15], (L, D_MODEL)),
        'w_in': nrm(k[16], (L, D_MODEL, D_IN), D_MODEL ** -0.5),
        'qn_g': gain(k[17], (L, HEAD_DIM_A)),
        'kn_cmp_g': gain(k[18], (L, HEAD_DIM_A)),
        'kn_sel_g': gain(k[19], (L, HEAD_DIM_A)),
        'kn_win_g': gain(k[22], (L, HEAD_DIM_A)),
        'w_cmp_k': nrm(k[23], (L, CMP_LEN, HEAD_DIM_A, HEAD_DIM_A), (CMP_LEN * HEAD_DIM_A) ** -0.5),
        'b_cmp_k': nrm(k[24], (L, HEAD_DIM_A), 0.02),
        'w_cmp_v': nrm(k[25], (L, CMP_LEN, HEAD_DIM_A, HEAD_DIM_A), (CMP_LEN * HEAD_DIM_A) ** -0.5),
        'b_cmp_v': nrm(k[26], (L, HEAD_DIM_A), 0.02),
        'conv_w': nrm(k[27], (L, CONV_W, 2 * D_M), CONV_W ** -0.5),
        'conv_b': nrm(k[28], (L, 2 * D_M), 0.02),
        'b_if': jnp.concatenate([b_i, b_f], axis=-1),
        'mn_g': gain(k[29], (L, D_M)),
        'w_out': nrm(k[30], (L, D_A + D_M, D_MODEL), (D_A + D_M) ** -0.5),
        'g_ffn2': gain(k[31], (L, D_MODEL)),
        'w2_gate': nrm(k[32], (L, D_MODEL, D_FF), D_MODEL ** -0.5),
        'w2_up': nrm(k[33], (L, D_MODEL, D_FF), D_MODEL ** -0.5),
        'w2_down': nrm(k[34], (L, D_FF, D_MODEL), D_FF ** -0.5),
    }


def reference(x_prompt, x_sample, cache_cmp_kv, cache_sel_kv, cache_win_kv, state_C, state_n, state_m, state_conv,
              page_table, rel_bias, g_ffn1, w1_gate, w1_up, w1_down, g_mix, w_in, qn_g, kn_cmp_g, kn_sel_g, kn_win_g,
              w_cmp_k, b_cmp_k, w_cmp_v, b_cmp_v, conv_w, conv_b, b_if, mn_g, w_out, g_ffn2, w2_gate, w2_up, w2_down):
    layer_ws = (g_ffn1, w1_gate, w1_up, w1_down, g_mix, w_in, qn_g, kn_cmp_g, kn_sel_g, kn_win_g,
                w_cmp_k, b_cmp_k, w_cmp_v, b_cmp_v, conv_w, conv_b, b_if, mn_g, w_out, g_ffn2, w2_gate, w2_up, w2_down)
    y_prompt, y_sample = x_prompt, x_sample
    p_states, s_states = [], []
    for l in range(DEPTH):
        lw = [w[l] for w in layer_ws]
        y_prompt, sp = prompt_layer(y_prompt, rel_bias, *lw)
        y_sample, ss = sample_layer(y_sample, cache_cmp_kv[l], cache_sel_kv[l], cache_win_kv[l], state_C[l],
                                    state_n[l], state_m[l], state_conv[l], page_table, rel_bias, *lw)
        p_states.append(sp)
        s_states.append(ss)
    cmp_p, sel_p, win_p, c_p, n_p, m_p, conv_p = [jnp.stack(s) for s in zip(*p_states)]
    cmp_s, sel_s, win_s, c_s, n_s, m_s, conv_s = [jnp.stack(s) for s in zip(*s_states)]
    return (y_prompt, y_sample, cmp_p, sel_p, win_p, c_p, n_p, m_p, conv_p,
            cmp_s, sel_s, win_s, c_s, n_s, m_s, conv_s)
```

```python
import functools
import math

import jax
import jax.numpy as jnp
import numpy as np
from jax import lax
from jax.experimental import pallas as pl
from jax.experimental.pallas import tpu as pltpu

F32 = jnp.float32
BF16 = jnp.bfloat16

D_MODEL = 1024
PAGE_SIZE = 128
N_HEADS_A = 8
HEAD_DIM_A = 64
N_KV_A = 2
GROUP_R = N_HEADS_A // N_KV_A
D_A = N_HEADS_A * HEAD_DIM_A
KV_W = N_KV_A * HEAD_DIM_A
CMP_STRIDE = 16
CMP_LEN = 2 * CMP_STRIDE
SEL_BLOCK = 64
TOP_K_BLOCKS = 16
WINDOW = 512
N_HEADS_M = 4
HEAD_DIM_M = 128
D_M = N_HEADS_M * HEAD_DIM_M
CONV_W = 4
N_BUCKETS = 32
MAX_DISTANCE = 128
EPS = 1e-6

LANES = 128
Q_TILE = 128
KV_ROW = 2 * KV_W
CHUNK_ROW = CMP_STRIDE * KV_ROW
NEG = -1e30
VMEM_LIMIT = 56 * 1024 * 1024

NT_DIMS = (((1,), (1,)), ((), ()))


def _dot(a, b):
    return jnp.dot(a, b, preferred_element_type=F32)


def _dot_nt(a, b):
    return lax.dot_general(a, b, NT_DIMS, preferred_element_type=F32)


def _split3(x):
    x1 = x.astype(BF16)
    r1 = x - x1.astype(F32)
    x2 = r1.astype(BF16)
    x3 = (r1 - x2.astype(F32)).astype(BF16)
    return x1, x2, x3


def _dot_exact_rhs(x, m):
    x1, x2, x3 = _split3(x)
    return _dot(x1, m) + _dot(x2, m) + _dot(x3, m)


def _dot_exact_lhs(m, x):
    x1, x2, x3 = _split3(x)
    return _dot(m, x1) + _dot(m, x2) + _dot(m, x3)


def _rms_rows(x, g):
    ms = jnp.mean(x * x, axis=-1, keepdims=True)
    return x * lax.rsqrt(ms + EPS) * g


def _rms_two_groups(k, g):
    sq = k * k
    lane = lax.broadcasted_iota(jnp.int32, sq.shape, 1)
    lo = lane < HEAD_DIM_A
    s0 = jnp.sum(jnp.where(lo, sq, 0.0), axis=-1, keepdims=True)
    s1 = jnp.sum(jnp.where(lo, 0.0, sq), axis=-1, keepdims=True)
    ms = jnp.where(lo, s0, s1) * (1.0 / HEAD_DIM_A)
    return k * lax.rsqrt(ms + EPS) * g


def _sigmoid(x):
    return jax.nn.sigmoid(x)


def _log_sigmoid(x):
    return jnp.minimum(x, 0.0) - jnp.log1p(jnp.exp(-jnp.abs(x)))


def _const_spec(shape):
    nd = len(shape)
    return pl.BlockSpec(shape, lambda *_: (0,) * nd, pipeline_mode=pl.Buffered(1))


def _params(*sem):
    return pltpu.CompilerParams(dimension_semantics=sem, vmem_limit_bytes=VMEM_LIMIT)


def _swiglu_residual(x, g_ref, wg_ref, wu_ref, wd_ref):
    xn = _rms_rows(x, g_ref[...]).astype(BF16)
    d_ff = wg_ref.shape[1]
    n_split = 2 if d_ff % (2 * LANES) == 0 else 1
    step = d_ff // n_split
    acc = jnp.zeros_like(x)
    for c in range(n_split):
        hg = _dot(xn, wg_ref[:, c * step:(c + 1) * step])
        hu = _dot(xn, wu_ref[:, c * step:(c + 1) * step])
        h = (hg * _sigmoid(hg)) * hu
        acc = acc + _dot(h.astype(BF16), wd_ref[c * step:(c + 1) * step, :])
    return x + 0.5 * acc


def _ffn_kernel(x_ref, g_ref, wg_ref, wu_ref, wd_ref, y_ref):
    y_ref[...] = _swiglu_residual(x_ref[...], g_ref, wg_ref, wu_ref, wd_ref)


def _post_kernel(x_ref, a_ref, m_ref, woa_ref, wom_ref, g_ref, wg_ref, wu_ref, wd_ref, y_ref):
    x = x_ref[...] + (_dot(a_ref[...], woa_ref[...]) + _dot(m_ref[...], wom_ref[...]))
    y_ref[...] = _swiglu_residual(x, g_ref, wg_ref, wu_ref, wd_ref)


def _token_tile(n):
    return 512 if n % 512 == 0 else n


def _ffn(x, g, wg, wu, wd):
    n = x.shape[0]
    tm = _token_tile(n)
    row = lambda w: pl.BlockSpec((tm, w), lambda i: (i, 0))
    return pl.pallas_call(
        _ffn_kernel,
        grid=(n // tm,),
        in_specs=[row(D_MODEL), _const_spec(g.shape), _const_spec(wg.shape), _const_spec(wu.shape),
                  _const_spec(wd.shape)],
        out_specs=row(D_MODEL),
        out_shape=jax.ShapeDtypeStruct((n, D_MODEL), F32),
        compiler_params=_params("parallel"),
        name="ffn1",
    )(x, g, wg, wu, wd)


def _post(x, a_pad, m_out, woa, wom, g, wg, wu, wd):
    n = x.shape[0]
    tm = _token_tile(n)
    row = lambda w: pl.BlockSpec((tm, w), lambda i: (i, 0))
    return pl.pallas_call(
        _post_kernel,
        grid=(n // tm,),
        in_specs=[row(D_MODEL), row(a_pad.shape[1]), row(m_out.shape[1]), _const_spec(woa.shape),
                  _const_spec(wom.shape), _const_spec(g.shape), _const_spec(wg.shape), _const_spec(wu.shape),
                  _const_spec(wd.shape)],
        out_specs=row(D_MODEL),
        out_shape=jax.ShapeDtypeStruct((n, D_MODEL), F32),
        compiler_params=_params("parallel"),
        name="post_mix_ffn2",
    )(x, a_pad, m_out, woa, wom, g, wg, wu, wd)


def _inproj_kernel(x_ref, g_ref, wq_ref, wkv_ref, wm_ref, ws_ref, qg_ref, kg_ref,
                   q_out, cmp_out, sel_out, win_out, ksb, vsb, kwb, vwb, qk_out, v_out, o_out, small_out,
                   gate_out):
    xn = _rms_rows(x_ref[...], g_ref[...]).astype(BF16)
    q = _dot(xn, wq_ref[...])
    for h in range(N_HEADS_A):
        qh = q[:, h * LANES:(h + 1) * LANES]
        ms = jnp.sum(qh * qh, axis=-1, keepdims=True) * (1.0 / HEAD_DIM_A)
        q_out[:, h * LANES:(h + 1) * LANES] = (qh * lax.rsqrt(ms + EPS) * qg_ref[...]).astype(BF16)
    kv = _dot(xn, wkv_ref[...])
    cmp_out[...] = kv[:, 0:KV_ROW]
    ks = _rms_two_groups(kv[:, 2 * KV_W:3 * KV_W], kg_ref[0:1, :])
    vs = kv[:, 3 * KV_W:4 * KV_W]
    kw = _rms_two_groups(kv[:, 4 * KV_W:5 * KV_W], kg_ref[1:2, :])
    vw = kv[:, 5 * KV_W:6 * KV_W]
    sel_out[:, 0:KV_W] = ks
    sel_out[:, KV_W:KV_ROW] = vs
    win_out[:, 0:KV_W] = kw
    win_out[:, KV_W:KV_ROW] = vw
    ksb[...] = ks.astype(BF16)
    vsb[...] = vs.astype(BF16)
    kwb[...] = kw.astype(BF16)
    vwb[...] = vw.astype(BF16)
    m = _dot(xn, wm_ref[...])
    qk_out[...] = m[:, 0:2 * D_M]
    v_out[...] = m[:, 2 * D_M:3 * D_M]
    o_out[...] = m[:, 3 * D_M:4 * D_M]
    small = _dot(xn, ws_ref[...])
    small_out[...] = small
    gate_out[...] = _sigmoid(small)


def _inproj(x, g, wq, wkv, wm, ws, qg, kg):
    n = x.shape[0]
    tm = _token_tile(n)
    row = lambda w: pl.BlockSpec((tm, w), lambda i: (i, 0))
    widths = [(N_HEADS_A * LANES, BF16), (KV_ROW, F32), (KV_ROW, F32), (KV_ROW, F32),
              (KV_W, BF16), (KV_W, BF16), (KV_W, BF16), (KV_W, BF16),
              (2 * D_M, F32), (D_M, F32), (D_M, F32), (LANES, F32), (LANES, F32)]
    return pl.pallas_call(
        _inproj_kernel,
        grid=(n // tm,),
        in_specs=[row(D_MODEL)] + [_const_spec(a.shape) for a in (g, wq, wkv, wm, ws, qg, kg)],
        out_specs=[row(w) for w, _ in widths],
        out_shape=[jax.ShapeDtypeStruct((n, w), dt) for w, dt in widths],
        compiler_params=_params("parallel"),
        name="in_proj",
    )(x, g, wq, wkv, wm, ws, qg, kg)


def _t5_bucket_np(dist):
    n = np.maximum(dist, 0)
    exact = N_BUCKETS // 2
    nf = np.maximum(n, 1).astype(np.float32)
    ratio = np.log(nf / np.float32(exact)) / np.float32(math.log(MAX_DISTANCE / exact))
    large = exact + (ratio * np.float32(N_BUCKETS - exact)).astype(np.int32)
    return np.where(n < exact, n, np.minimum(large, N_BUCKETS - 1)).astype(np.int32)


def _bias_kernel(tbl_ref, bkt_ref, out_ref):
    bkt = bkt_ref[0]
    for h in range(N_HEADS_A):
        far = tbl_ref[N_BUCKETS - 1, h]
        acc = jnp.zeros(bkt.shape, F32)
        for b in range(N_BUCKETS - 1):
            acc = jnp.where(bkt == b, tbl_ref[b, h] - far, acc)
        out_ref[0, h] = acc


def _bias_tiles(rel_bias, buckets):
    n = buckets.shape[0]
    return pl.pallas_call(
        _bias_kernel,
        grid=(n,),
        in_specs=[pl.BlockSpec(memory_space=pltpu.SMEM),
                  pl.BlockSpec((1, LANES, LANES), lambda i: (i, 0, 0))],
        out_specs=pl.BlockSpec((1, N_HEADS_A, LANES, LANES), lambda i: (i, 0, 0, 0)),
        out_shape=jax.ShapeDtypeStruct((n, N_HEADS_A, LANES, LANES), F32),
        compiler_params=_params("parallel"),
        name="bias_tiles",
    )(rel_bias, buckets)


def _compress_rows(x_bf, w_ref, b_ref, g_ref):
    nc = x_bf.shape[0]
    fs = _dot(x_bf, w_ref[...])
    first = fs[:, 0:KV_ROW]
    nxt = pltpu.roll(fs[:, KV_ROW:2 * KV_ROW], nc - 1, 0)
    blk = first + nxt + b_ref[...]
    kc = _rms_two_groups(blk[:, 0:KV_W], g_ref[...])
    vc = blk[:, KV_W:KV_ROW]
    complete = lax.broadcasted_iota(jnp.int32, kc.shape, 0) < nc - 1
    return jnp.where(complete, kc, 0.0), jnp.where(complete, vc, 0.0)


def _compress_kernel(x_ref, w_ref, b_ref, g_ref, kc_ref, vc_ref):
    kc, vc = _compress_rows(x_ref[0].astype(BF16), w_ref, b_ref, g_ref)
    kc_ref[0] = kc.astype(BF16)
    vc_ref[0] = vc.astype(BF16)


def _compress(cmp_rows, w_big, b_big, g_k):
    nb, nc, _ = cmp_rows.shape
    blk = pl.BlockSpec((1, nc, KV_W), lambda b: (b, 0, 0))
    return pl.pallas_call(
        _compress_kernel,
        grid=(nb,),
        in_specs=[pl.BlockSpec((1, nc, CHUNK_ROW), lambda b: (b, 0, 0)), _const_spec(w_big.shape),
                  _const_spec(b_big.shape), _const_spec(g_k.shape)],
        out_specs=[blk, blk],
        out_shape=[jax.ShapeDtypeStruct((nb, nc, KV_W), BF16)] * 2,
        compiler_params=_params("parallel"),
        name="compress_prompt",
    )(cmp_rows, w_big, b_big, g_k)


def _online_update(carry, qg, k_tile, v_tile, addmask):
    m, l, acc = carry
    s = _dot_nt(qg, k_tile).reshape(GROUP_R, Q_TILE, -1) + addmask
    m_new = jnp.maximum(m, jnp.max(s, axis=-1, keepdims=True))
    alpha = jnp.exp(m - m_new)
    p = jnp.exp(s - m_new)
    l = alpha * l + jnp.sum(p, axis=-1, keepdims=True)
    pv = _dot(p.reshape(GROUP_R * Q_TILE, -1).astype(BF16), v_tile)
    acc = alpha * acc + pv.reshape(GROUP_R, Q_TILE, LANES)
    return m_new, l, acc


def _rank_rows(score):
    n_blk = score.shape[0]
    blk = lax.broadcasted_iota(jnp.int32, score.shape, 0)
    rank = jnp.zeros(score.shape, F32)
    for b in range(n_blk):
        row = score[b:b + 1, :]
        rank = rank + jnp.where(blk > b, jnp.where(row >= score, 1.0, 0.0), jnp.where(row > score, 1.0, 0.0))
    return rank


def _nsa_prompt_kernel(q_ref, gate_ref, kc_ref, vc_ref, ks_ref, vs_ref, kw_ref, vw_ref, bnear_ref, bcmp_ref,
                       covt_ref, exp_ref, a_ref, mb_ref, *, n_cmp):
    i = pl.program_id(1)
    n_tiles = mb_ref.shape[0]
    n_sel = covt_ref.shape[0]
    q0 = i * Q_TILE
    qi_col = lax.broadcasted_iota(jnp.int32, (Q_TILE, LANES), 0)
    ki_row = lax.broadcasted_iota(jnp.int32, (Q_TILE, LANES), 1)
    causal_add = jnp.where(ki_row <= qi_col, 0.0, NEG)
    upper_add = jnp.where(ki_row > qi_col, 0.0, NEG)
    eye = jnp.where(ki_row == qi_col, 1.0, 0.0).astype(BF16)
    gates = gate_ref[0]
    lane = lax.broadcasted_iota(jnp.int32, (Q_TILE, LANES), 1)

    for g in range(N_KV_A):
        heads = range(g * GROUP_R, (g + 1) * GROUP_R)
        qg = jnp.concatenate([q_ref[0, :, h * LANES:(h + 1) * LANES] for h in heads], axis=0)
        rows = slice(g * GROUP_R * Q_TILE, (g + 1) * GROUP_R * Q_TILE)

        ncp = kc_ref.shape[1]
        j_row = lax.broadcasted_iota(jnp.int32, (Q_TILE, ncp), 1)
        q_col = q0 + lax.broadcasted_iota(jnp.int32, (Q_TILE, ncp), 0)
        usable = (j_row * CMP_STRIDE + (CMP_LEN - 1) <= q_col) & (j_row < n_cmp)
        s = _dot_nt(qg, kc_ref[0]).reshape(GROUP_R, Q_TILE, ncp)
        s = s + bcmp_ref[0, rows, :].reshape(GROUP_R, Q_TILE, ncp) + jnp.where(usable, 0.0, NEG)
        any_usable = jnp.where(q_col[:, 0:1] >= CMP_LEN - 1, 1.0, 0.0)
        e = jnp.exp(s - jnp.max(s, axis=-1, keepdims=True)) * any_usable
        p_c = e / jnp.maximum(jnp.sum(e, axis=-1, keepdims=True), 1e-30)
        o_c = _dot(p_c.reshape(GROUP_R * Q_TILE, ncp).astype(BF16), vc_ref[0]).reshape(GROUP_R, Q_TILE, LANES)

        p_sum = jnp.sum(p_c, axis=0)
        p1, p2, p3 = _split3(p_sum)
        cov_t = covt_ref[...]
        imp_t = _dot_nt(cov_t, p1) + _dot_nt(cov_t, p2) + _dot_nt(cov_t, p3)
        blk = lax.broadcasted_iota(jnp.int32, (n_sel, Q_TILE), 0)
        qpos = q0 + lax.broadcasted_iota(jnp.int32, (n_sel, Q_TILE), 1)
        cur = qpos // SEL_BLOCK
        valid = blk * SEL_BLOCK <= qpos
        forced = valid & ((blk == 0) | (blk == cur) | (blk == cur - 1))
        score = jnp.where(forced, -NEG, jnp.where(valid, imp_t, NEG))
        sel_t = jnp.where(_rank_rows(score) < float(min(TOP_K_BLOCKS, n_sel)), 1.0, 0.0).astype(BF16)
        sel = _dot_nt(eye, sel_t).astype(BF16)
        mask_all = _dot(sel, exp_ref[...])
        for t in range(n_tiles):
            mb_ref[t] = (mask_all[:, t * LANES:(t + 1) * LANES] - 1.0) * (-NEG)

        bnear = [bnear_ref[d, rows, :].reshape(GROUP_R, Q_TILE, LANES) for d in range(2)]
        init = (jnp.full((GROUP_R, Q_TILE, 1), NEG, F32), jnp.zeros((GROUP_R, Q_TILE, 1), F32),
                jnp.zeros((GROUP_R, Q_TILE, LANES), F32))

        def tile(ref, t):
            return ref[0, pl.ds(pl.multiple_of(t * LANES, LANES), LANES), :]

        def far_body(t, carry):
            return _online_update(carry, qg, tile(ks_ref, t), tile(vs_ref, t), mb_ref[t])

        carry = lax.fori_loop(0, jnp.maximum(i - 1, 0), far_body, init)
        t1 = jnp.maximum(i - 1, 0)
        sub_add = mb_ref[t1] + jnp.where(i >= 1, 0.0, NEG)
        carry = _online_update(carry, qg, tile(ks_ref, t1), tile(vs_ref, t1), bnear[1] + sub_add)
        carry = _online_update(carry, qg, tile(ks_ref, i), tile(vs_ref, i), bnear[0] + (mb_ref[i] + causal_add))
        o_s = carry[2] / carry[1]

        carry = init
        n_win_tiles = WINDOW // LANES
        for d in range(n_win_tiles, -1, -1):
            t = jnp.maximum(i - d, 0)
            add = jnp.where(i >= d, 0.0, NEG)
            if d == n_win_tiles:
                add = add + upper_add
            elif d == 0:
                add = bnear[0] + causal_add
            elif d == 1:
                add = bnear[1] + add
            else:
                add = jnp.broadcast_to(add, (Q_TILE, LANES))
            carry = _online_update(carry, qg, tile(kw_ref, t), tile(vw_ref, t), add)
        o_w = carry[2] / carry[1]

        own_half = (lane >= g * HEAD_DIM_A) & (lane < (g + 1) * HEAD_DIM_A)
        for r, h in enumerate(heads):
            out = (gates[:, h:h + 1] * o_c[r] + gates[:, N_HEADS_A + h:N_HEADS_A + h + 1] * o_s[r]
                   + gates[:, 2 * N_HEADS_A + h:2 * N_HEADS_A + h + 1] * o_w[r])
            a_ref[0, :, h * LANES:(h + 1) * LANES] = jnp.where(own_half, out, 0.0).astype(BF16)


def _nsa_prompt(q_pad, gates, kc, vc, ksb, vsb, kwb, vwb, bnear, bcmp, cov_t, expand, n_cmp):
    nb, t, _ = q_pad.shape
    nqb = t // Q_TILE
    seq = lambda a: pl.BlockSpec((1,) + a.shape[1:], lambda b, i: (b, 0, 0))
    return pl.pallas_call(
        functools.partial(_nsa_prompt_kernel, n_cmp=n_cmp),
        grid=(nb, nqb),
        in_specs=[pl.BlockSpec((1, Q_TILE, N_HEADS_A * LANES), lambda b, i: (b, i, 0)),
                  pl.BlockSpec((1, Q_TILE, LANES), lambda b, i: (b, i, 0)),
                  seq(kc), seq(vc), seq(ksb), seq(vsb), seq(kwb), seq(vwb),
                  _const_spec(bnear.shape),
                  pl.BlockSpec((1,) + bcmp.shape[1:], lambda b, i: (i, 0, 0)),
                  _const_spec(cov_t.shape), _const_spec(expand.shape)],
        out_specs=pl.BlockSpec((1, Q_TILE, N_HEADS_A * LANES), lambda b, i: (b, i, 0)),
        out_shape=jax.ShapeDtypeStruct((nb, t, N_HEADS_A * LANES), BF16),
        scratch_shapes=[pltpu.VMEM((t // LANES, Q_TILE, LANES), F32)],
        compiler_params=_params("parallel", "arbitrary"),
        name="nsa_prompt",
    )(q_pad, gates, kc, vc, ksb, vsb, kwb, vwb, bnear, bcmp, cov_t, expand)


def _head_norm_gate(h_all, mn_ref, o):
    outs = []
    for h in range(N_HEADS_M):
        hs = slice(h * HEAD_DIM_M, (h + 1) * HEAD_DIM_M)
        outs.append(_rms_rows(h_all[h], mn_ref[:, hs]))
    return _sigmoid(o) * jnp.concatenate(outs, axis=-1)


def _mlstm_prompt_kernel(qk_ref, v_ref, o_ref, small_ref, cw_ref, cb_ref, bif_ref, mn_ref, tri_ref, triu_ref,
                         y_ref, c_ref, n_ref, m_ref, xbuf):
    chunk = qk_ref.shape[1]
    pad = 8

    @pl.when(pl.program_id(1) == 0)
    def _():
        xbuf[0:pad, :] = jnp.zeros((pad, 2 * D_M), F32)
        c_ref[...] = jnp.zeros_like(c_ref)
        n_ref[...] = jnp.zeros_like(n_ref)
        m_ref[...] = jnp.zeros_like(m_ref)

    x = qk_ref[0]
    xbuf[pad:pad + chunk, :] = x
    y = xbuf[pad - 3:pad - 3 + chunk, :] * cw_ref[0:1, :]
    y = y + xbuf[pad - 2:pad - 2 + chunk, :] * cw_ref[1:2, :]
    y = y + xbuf[pad - 1:pad - 1 + chunk, :] * cw_ref[2:3, :]
    y = y + x * cw_ref[3:4, :] + cb_ref[...]
    qkc = y * _sigmoid(y)
    xbuf[0:pad, :] = xbuf[chunk:chunk + pad, :]

    gi = small_ref[0] + bif_ref[...]
    gi_t = gi.T
    b_col = _dot_exact_lhs(tri_ref[...], _log_sigmoid(gi))
    b_row = _dot_exact_rhs(_log_sigmoid(gi_t), triu_ref[...])
    t_col = lax.broadcasted_iota(jnp.int32, (chunk, chunk), 0)
    s_row = lax.broadcasted_iota(jnp.int32, (chunk, chunk), 1)
    causal = s_row <= t_col
    m_all = m_ref[0]
    lane_row = lax.broadcasted_iota(jnp.int32, m_all.shape, 1)

    h_all = []
    for h in range(N_HEADS_M):
        hs = slice(h * HEAD_DIM_M, (h + 1) * HEAD_DIM_M)
        ks = slice(D_M + h * HEAD_DIM_M, D_M + (h + 1) * HEAD_DIM_M)
        q = qkc[:, hs]
        k = qkc[:, ks] * F32(HEAD_DIM_M ** -0.5)
        v = v_ref[0, :, hs]
        qb, kb = q.astype(BF16), k.astype(BF16)
        ci, fi = 3 * N_HEADS_A + h, 3 * N_HEADS_A + N_HEADS_M + h
        bt = b_col[:, fi:fi + 1]
        bs = b_row[fi:fi + 1, :]
        ig_row = gi_t[ci:ci + 1, :]
        ig_col = gi[:, ci:ci + 1]
        m_prev = m_all[:, h:h + 1]
        c_prev = c_ref[0, h]
        n_prev = n_ref[0, h:h + 1, :]

        dlog = jnp.where(causal, bt - bs + ig_row, NEG)
        inter = bt + m_prev
        m_t = jnp.maximum(inter, jnp.max(dlog, axis=-1, keepdims=True))
        s_qk = _dot_nt(qb, kb) * jnp.exp(dlog - m_t)
        dec = jnp.exp(inter - m_t)
        num = _dot(s_qk.astype(BF16), v.astype(BF16)) + dec * _dot_nt(qb, c_prev.astype(BF16))
        den = jnp.sum(s_qk, axis=-1, keepdims=True) + dec * jnp.sum(q * n_prev, axis=-1, keepdims=True)
        h_all.append(num / jnp.maximum(jnp.abs(den), jnp.exp(-m_t)))

        m_new = m_t[chunk - 1:chunk, :]
        b_end = bt[chunk - 1:chunk, :]
        w_end = jnp.exp(b_end - bt + ig_col - m_new)
        dec_end = jnp.exp(b_end + m_prev - m_new)
        wv_t = (w_end * v).T.astype(BF16)
        c_ref[0, h] = dec_end * c_prev + _dot(wv_t, kb)
        n_ref[0, h:h + 1, :] = dec_end * n_prev + jnp.sum(w_end * k, axis=0, keepdims=True)
        m_all = jnp.where(lane_row == h, m_new, m_all)

    m_ref[0] = m_all
    y_ref[0] = _head_norm_gate(h_all, mn_ref, o_ref[0]).astype(BF16)


def _mlstm_prompt(qk, v, o, small, conv_w, conv_b, bif, mn_g, chunk):
    nb, t, _ = qk.shape
    tri = jnp.asarray(np.tril(np.ones((chunk, chunk), np.float32)), BF16)
    triu = jnp.asarray(np.triu(np.ones((chunk, chunk), np.float32)), BF16)
    tok = lambda w: pl.BlockSpec((1, chunk, w), lambda b, c: (b, c, 0))
    return pl.pallas_call(
        _mlstm_prompt_kernel,
        grid=(nb, t // chunk),
        in_specs=[tok(2 * D_M), tok(D_M), tok(D_M), tok(LANES)]
        + [_const_spec(a.shape) for a in (conv_w, conv_b, bif, mn_g, tri, triu)],
        out_specs=[tok(D_M),
                   pl.BlockSpec((1, N_HEADS_M, HEAD_DIM_M, HEAD_DIM_M), lambda b, c: (b, 0, 0, 0)),
                   pl.BlockSpec((1, N_HEADS_M, HEAD_DIM_M), lambda b, c: (b, 0, 0)),
                   pl.BlockSpec((1, 1, LANES), lambda b, c: (b, 0, 0))],
        out_shape=[jax.ShapeDtypeStruct((nb, t, D_M), BF16),
                   jax.ShapeDtypeStruct((nb, N_HEADS_M, HEAD_DIM_M, HEAD_DIM_M), F32),
                   jax.ShapeDtypeStruct((nb, N_HEADS_M, HEAD_DIM_M), F32),
                   jax.ShapeDtypeStruct((nb, 1, LANES), F32)],
        scratch_shapes=[pltpu.VMEM((chunk + 8, 2 * D_M), F32)],
        compiler_params=_params("parallel", "arbitrary"),
        name="mlstm_prompt",
    )(qk, v, o, small, conv_w, conv_b, bif, mn_g, tri, triu)


def _softmax_rows(s):
    e = jnp.exp(s - jnp.max(s, axis=-1, keepdims=True))
    return e, jnp.sum(e, axis=-1, keepdims=True)


def _sample_cmp_kernel(pt_ref, *refs, n_pages, n_sel, qpos):
    pages = refs[:n_pages]
    w_ref, b_ref, g_ref, q_ref, bias_ref, cov_ref, oc_ref, idx_ref, xs_ref = refs[n_pages:]
    rows = pages[0].shape[1]
    for u in range(0, n_pages, 2):
        pair = jnp.concatenate([pages[u][0], pages[u + 1][0]], axis=0)
        xs_ref[u * rows:(u + 2) * rows, :] = pair.astype(BF16)
    kc, vc = _compress_rows(xs_ref[...], w_ref, b_ref, g_ref)
    nc = kc.shape[0]
    q = q_ref[0]
    j_row = lax.broadcasted_iota(jnp.int32, (N_HEADS_A, nc), 1)
    s = _dot_nt(q, kc.astype(BF16)) + bias_ref[...] + jnp.where(j_row < nc - 1, 0.0, NEG)
    e, l = _softmax_rows(s)
    p = e / jnp.maximum(l, 1e-30)
    oc_ref[0] = _dot(p.astype(BF16), vc.astype(BF16))

    nsp = cov_ref.shape[1]
    blk_row = lax.broadcasted_iota(jnp.int32, (1, nsp), 1)
    sub = lax.broadcasted_iota(jnp.int32, (nsp, nsp), 0)
    lan = lax.broadcasted_iota(jnp.int32, (nsp, nsp), 1)
    cur = qpos // SEL_BLOCK
    valid = (blk_row * SEL_BLOCK <= qpos) & (blk_row < n_sel)
    forced = valid & ((blk_row == 0) | (blk_row == cur) | (blk_row == cur - 1))
    k_col = lax.broadcasted_iota(jnp.int32, (TOP_K_BLOCKS, nsp), 0).astype(F32)
    blk_f = lax.broadcasted_iota(jnp.int32, (TOP_K_BLOCKS, nsp), 1).astype(F32)
    for g in range(N_KV_A):
        p_sum = jnp.sum(p[g * GROUP_R:(g + 1) * GROUP_R, :], axis=0, keepdims=True)
        imp = _dot_exact_rhs(jnp.broadcast_to(p_sum, (8, nc)), cov_ref[...])[0:1, :]
        score = jnp.where(forced, -NEG, jnp.where(valid, imp, NEG))
        score_col = jnp.sum(jnp.where(sub == lan, score, 0.0), axis=-1, keepdims=True)
        ahead = jnp.where(sub < lan, jnp.where(score_col >= score, 1.0, 0.0), jnp.where(score_col > score, 1.0, 0.0))
        rank = jnp.sum(ahead, axis=0, keepdims=True)
        chosen = jnp.sum(jnp.where(rank == k_col, blk_f, 0.0), axis=-1, keepdims=True)
        idx_ref[0, g * TOP_K_BLOCKS:(g + 1) * TOP_K_BLOCKS, :] = jnp.broadcast_to(
            chosen, (TOP_K_BLOCKS, LANES)).astype(jnp.int32)


def _sample_cmp(page_table, cache_cmp, w_big, b_big, g_k, q3, bias_c, cover, n_sel, qpos):
    ns, n_pages = page_table.shape
    rows = PAGE_SIZE // CMP_STRIDE
    pool = cache_cmp.reshape(-1, rows, CHUNK_ROW)
    nc = n_pages * rows
    page_spec = lambda u: pl.BlockSpec((1, rows, CHUNK_ROW), lambda s, pt: (pt[s, u], 0, 0))
    const = lambda a: pl.BlockSpec(a.shape, lambda s, pt: (0,) * a.ndim, pipeline_mode=pl.Buffered(1))
    grid_spec = pltpu.PrefetchScalarGridSpec(
        num_scalar_prefetch=1,
        grid=(ns,),
        in_specs=[page_spec(u) for u in range(n_pages)]
        + [const(w_big), const(b_big), const(g_k),
           pl.BlockSpec((1, N_HEADS_A, LANES), lambda s, pt: (s, 0, 0)), const(bias_c), const(cover)],
        out_specs=[pl.BlockSpec((1, N_HEADS_A, LANES), lambda s, pt: (s, 0, 0)),
                   pl.BlockSpec((1, N_KV_A * TOP_K_BLOCKS, LANES), lambda s, pt: (s, 0, 0))],
        scratch_shapes=[pltpu.VMEM((nc, CHUNK_ROW), BF16)],
    )
    return pl.pallas_call(
        functools.partial(_sample_cmp_kernel, n_pages=n_pages, n_sel=n_sel, qpos=qpos),
        grid_spec=grid_spec,
        out_shape=[jax.ShapeDtypeStruct((ns, N_HEADS_A, LANES), F32),
                   jax.ShapeDtypeStruct((ns, N_KV_A * TOP_K_BLOCKS, LANES), jnp.int32)],
        compiler_params=_params("arbitrary"),
        name="sample_cmp_topk",
    )(page_table, *([pool] * n_pages), w_big, b_big, g_k, q3, bias_c, cover)


def _sample_attn_kernel(idx_ref, pt_ref, *refs, n_past_blocks):
    n_slots = N_KV_A * TOP_K_BLOCKS
    blocks = refs[:n_slots]
    (win_ref, q_ref, selnew_ref, winnew_ref, oc_ref, gate_ref, bsel_ref, bwin_ref, b0_ref,
     a_ref, winout_ref) = refs[n_slots:]
    s_id = pl.program_id(0)
    q = q_ref[0]
    qf = q.astype(F32)
    lane = lax.broadcasted_iota(jnp.int32, (N_HEADS_A, LANES), 1)
    row = lax.broadcasted_iota(jnp.int32, (N_HEADS_A, LANES), 0)
    lo = lane < SEL_BLOCK

    def with_new_key(s, v_mat, new_row):
        s_new = jnp.sum(qf * new_row[:, 0:KV_W], axis=-1, keepdims=True) + b0_ref[:, 0:1]
        m = jnp.maximum(jnp.max(s, axis=-1, keepdims=True), s_new)
        p = jnp.exp(s - m)
        p_new = jnp.exp(s_new - m)
        l = jnp.sum(p, axis=-1, keepdims=True) + p_new
        return (_dot(p.astype(BF16), v_mat) + p_new * new_row[:, KV_W:KV_ROW]) / l

    sel_new = selnew_ref[0]
    o_s = []
    for g in range(N_KV_A):
        slots = range(g * TOP_K_BLOCKS, (g + 1) * TOP_K_BLOCKS)
        k_cat = jnp.concatenate([blocks[k][0, :, 0:KV_W] for k in slots], axis=0).astype(BF16)
        v_cat = jnp.concatenate([blocks[k][0, :, KV_W:KV_ROW] for k in slots], axis=0).astype(BF16)
        adds = []
        for k in slots:
            b = idx_ref[s_id * n_slots + k]
            add = jnp.where(b == n_past_blocks - 2, bsel_ref[0], jnp.where(b == n_past_blocks - 1, bsel_ref[1], 0.0))
            adds.append(add + jnp.where(b >= n_past_blocks, NEG, 0.0))
        bias = jnp.concatenate([jnp.where(lo, adds[2 * v], adds[2 * v + 1]) for v in range(TOP_K_BLOCKS // 2)],
                               axis=-1)
        o_s.append(with_new_key(_dot_nt(q, k_cat) + bias, v_cat, sel_new))
    o_sel = jnp.where(row < GROUP_R, o_s[0], o_s[1])

    win = win_ref[0]
    wb = win.shape[0]
    w_lane = lax.broadcasted_iota(jnp.int32, (N_HEADS_A, wb), 1)
    s_w = _dot_nt(q, win[:, 0:KV_W].astype(BF16)) + bwin_ref[...] + jnp.where(w_lane == 0, NEG, 0.0)
    o_win = with_new_key(s_w, win[:, KV_W:KV_ROW].astype(BF16), winnew_ref[0])

    na = N_HEADS_A
    out = gate_ref[0, 0:na] * oc_ref[0] + gate_ref[0, na:2 * na] * o_sel + gate_ref[0, 2 * na:3 * na] * o_win
    a_ref[0] = jnp.where(lo == (row < GROUP_R), out, 0.0).astype(BF16)
    winout_ref[0, 0:wb - 1, :] = win_ref[0, 1:wb, :]
    winout_ref[0, wb - 1:wb, :] = winnew_ref[0]


def _sample_attn(idx, page_table, cache_sel, cache_win, q3, sel_new, win_new, o_c, gate_b, bsel, bwin, b0):
    ns, n_pages = page_table.shape
    n_slots = N_KV_A * TOP_K_BLOCKS
    per_page = PAGE_SIZE // SEL_BLOCK
    n_past_blocks = n_pages * per_page
    pool = cache_sel.reshape(-1, SEL_BLOCK, KV_ROW)
    wb = cache_win.shape[1]

    def slot_spec(k):
        def index(s, idx_ref, pt_ref):
            b = jnp.minimum(idx_ref[s * n_slots + k], n_past_blocks - 1)
            return (pt_ref[s, b // per_page] * per_page + b % per_page, 0, 0)
        return pl.BlockSpec((1, SEL_BLOCK, KV_ROW), index)

    per_seq = lambda a: pl.BlockSpec((1,) + a.shape[1:], lambda s, *_: (s,) + (0,) * (a.ndim - 1))
    const = lambda a: pl.BlockSpec(a.shape, lambda s, *_: (0,) * a.ndim, pipeline_mode=pl.Buffered(1))
    grid_spec = pltpu.PrefetchScalarGridSpec(
        num_scalar_prefetch=2,
        grid=(ns,),
        in_specs=[slot_spec(k) for k in range(n_slots)]
        + [per_seq(cache_win), per_seq(q3), per_seq(sel_new), per_seq(win_new), per_seq(o_c), per_seq(gate_b),
           const(bsel), const(bwin), const(b0)],
        out_specs=[pl.BlockSpec((1, N_HEADS_A, LANES), lambda s, *_: (s, 0, 0)),
                   pl.BlockSpec((1, wb, KV_ROW), lambda s, *_: (s, 0, 0))],
    )
    return pl.pallas_call(
        functools.partial(_sample_attn_kernel, n_past_blocks=n_past_blocks),
        grid_spec=grid_spec,
        out_shape=[jax.ShapeDtypeStruct((ns, N_HEADS_A, LANES), BF16),
                   jax.ShapeDtypeStruct((ns, wb, KV_ROW), F32)],
        compiler_params=_params("arbitrary"),
        name="sample_sel_win",
    )(idx, page_table, *([pool] * n_slots), cache_win, q3, sel_new, win_new, o_c, gate_b, bsel, bwin, b0)


def _mlstm_step_kernel(qk_ref, conv_ref, v_ref, o_ref, small_ref, c_ref, n_ref, m_ref, cw_ref, cb_ref, bif_ref,
                       mn_ref, y_ref, c_out, n_out, m_out):
    seqs = qk_ref.shape[0]
    sub = lax.broadcasted_iota(jnp.int32, (HEAD_DIM_M, HEAD_DIM_M), 0)
    lan = lax.broadcasted_iota(jnp.int32, (HEAD_DIM_M, HEAD_DIM_M), 1)
    eye = sub == lan
    lane_row = lax.broadcasted_iota(jnp.int32, (1, LANES), 1)
    for s in range(seqs):
        hist = conv_ref[s]
        y = hist[0:1, :] * cw_ref[0:1, :]
        y = y + hist[1:2, :] * cw_ref[1:2, :]
        y = y + hist[2:3, :] * cw_ref[2:3, :]
        y = y + qk_ref[s] * cw_ref[3:4, :] + cb_ref[...]
        qkc = y * _sigmoid(y)
        gi = small_ref[s] + bif_ref[...]
        lf_all = _log_sigmoid(gi)
        m_all = m_ref[s]
        m_new_all = m_all
        h_all = []
        for h in range(N_HEADS_M):
            hs = slice(h * HEAD_DIM_M, (h + 1) * HEAD_DIM_M)
            ks = slice(D_M + h * HEAD_DIM_M, D_M + (h + 1) * HEAD_DIM_M)
            q = qkc[:, hs]
            k = qkc[:, ks] * F32(HEAD_DIM_M ** -0.5)
            v = v_ref[s][:, hs]
            ci, fi = 3 * N_HEADS_A + h, 3 * N_HEADS_A + N_HEADS_M + h
            ig = gi[:, ci:ci + 1]
            lf = lf_all[:, fi:fi + 1]
            m_prev = m_all[:, h:h + 1]
            c_prev = c_ref[s, h]
            n_prev = n_ref[s, h:h + 1, :]
            inter = lf + m_prev
            m_t = jnp.maximum(inter, ig)
            w_in = jnp.exp(ig - m_t)
            dec = jnp.exp(inter - m_t)
            s_qk = jnp.sum(q * k, axis=-1, keepdims=True) * w_in
            cq = _dot_nt(jnp.broadcast_to(q, (8, HEAD_DIM_M)).astype(BF16), c_prev.astype(BF16))[0:1, :]
            num = s_qk * v + dec * cq
            den = s_qk + dec * jnp.sum(n_prev * q, axis=-1, keepdims=True)
            h_all.append(num / jnp.maximum(jnp.abs(den), jnp.exp(-m_t)))
            v_col = jnp.sum(jnp.where(eye, v, 0.0), axis=-1, keepdims=True)
            c_out[s, h] = dec * c_prev + (w_in * v_col) * k
            n_out[s, h:h + 1, :] = dec * n_prev + w_in * k
            m_new_all = jnp.where(lane_row == h, m_t, m_new_all)
        m_out[s] = m_new_all
        y_ref[s] = _head_norm_gate(h_all, mn_ref, o_ref[s]).astype(BF16)


def _mlstm_step(qk, conv0, v, o, small, c0, n0, m0, conv_w, conv_b, bif, mn_g):
    ns = qk.shape[0]
    sb = 8 if ns % 8 == 0 else 1
    per = lambda a: pl.BlockSpec((sb,) + a.shape[1:], lambda i: (i,) + (0,) * (a.ndim - 1))
    ins = (qk, conv0, v, o, small, c0, n0, m0)
    outs = [jax.ShapeDtypeStruct((ns, 1, D_M), BF16), jax.ShapeDtypeStruct(c0.shape, F32),
            jax.ShapeDtypeStruct(n0.shape, F32), jax.ShapeDtypeStruct(m0.shape, F32)]
    return pl.pallas_call(
        _mlstm_step_kernel,
        grid=(ns // sb,),
        in_specs=[per(a) for a in ins] + [_const_spec(a.shape) for a in (conv_w, conv_b, bif, mn_g)],
        out_specs=[per(a) for a in outs],
        out_shape=outs,
        compiler_params=_params("parallel"),
        name="mlstm_step",
    )(*ins, conv_w, conv_b, bif, mn_g)


def _cover_np(n_cmp, n_sel):
    cs = np.arange(n_cmp)[:, None] * CMP_STRIDE
    bs = np.arange(n_sel)[None, :] * SEL_BLOCK
    shared = np.clip(np.minimum(cs + CMP_LEN, bs + SEL_BLOCK) - np.maximum(cs, bs), 0, None)
    return (shared / CMP_LEN).astype(np.float32)


def _prep_in_proj(w_in):
    d = w_in.shape[0]
    wq = w_in[:, :D_A].reshape(d, N_HEADS_A, HEAD_DIM_A)
    z = jnp.zeros_like(wq)
    g0 = jnp.concatenate([wq[:, :GROUP_R], z[:, :GROUP_R]], axis=-1)
    g1 = jnp.concatenate([z[:, GROUP_R:], wq[:, GROUP_R:]], axis=-1)
    wq_pad = jnp.concatenate([g0, g1], axis=1).reshape(d, N_HEADS_A * LANES)
    o_kv = D_A
    o_gate = o_kv + 6 * KV_W
    o_m = o_gate + 3 * N_HEADS_A
    o_if = o_m + 4 * D_M
    w_kv = w_in[:, o_kv:o_gate]
    w_m = w_in[:, o_m:o_if]
    w_small = jnp.concatenate([w_in[:, o_gate:o_m], w_in[:, o_if:],
                               jnp.zeros((d, LANES - 3 * N_HEADS_A - 2 * N_HEADS_M), w_in.dtype)], axis=1)
    return [w.astype(BF16) for w in (wq_pad, w_kv, w_m, w_small)]


def _prep_compress(w_k, b_k, w_v, b_v):
    per_c = jnp.stack([w_k, w_k, w_v, w_v]).reshape(4, 2, CMP_STRIDE, HEAD_DIM_A, HEAD_DIM_A)
    w_big = jnp.einsum('chtde,cf->tcdhfe', per_c, jnp.eye(4, dtype=w_k.dtype))
    w_big = w_big.reshape(CHUNK_ROW, 2 * KV_ROW).astype(BF16)
    b_big = jnp.concatenate([b_k, b_k, b_v, b_v]).reshape(1, KV_ROW)
    return w_big, b_big


def _prep_out_proj(w_out):
    wa = w_out[:D_A].reshape(N_HEADS_A, HEAD_DIM_A, -1)
    z = jnp.zeros_like(wa)
    g0 = jnp.concatenate([wa[:GROUP_R], z[:GROUP_R]], axis=1)
    g1 = jnp.concatenate([z[GROUP_R:], wa[GROUP_R:]], axis=1)
    woa = jnp.concatenate([g0, g1], axis=0).reshape(N_HEADS_A * LANES, -1)
    return woa.astype(BF16), w_out[D_A:].astype(BF16)


def _bucket_tiles(t, past, wb):
    r = np.arange(LANES)
    tiles = [_t5_bucket_np(d * LANES + r[:, None] - r[None, :]) for d in range(2)]
    for i in range(t // Q_TILE):
        tiles.append(_t5_bucket_np(i * Q_TILE + r[:, None] - (CMP_STRIDE * r[None, :] + CMP_LEN - 1)))
    flat = np.arange(LANES * LANES).reshape(LANES, LANES)
    tiles.append(_t5_bucket_np(past - (CMP_STRIDE * flat + CMP_LEN - 1)))
    tiles.append(_t5_bucket_np(wb - flat))
    tiles.append(_t5_bucket_np(past - (r[:, None] * SEL_BLOCK + r[None, :] % SEL_BLOCK)))
    return np.stack(tiles).astype(np.int32)


def kernel(x_prompt, x_sample, cache_cmp_kv, cache_sel_kv, cache_win_kv, state_C, state_n, state_m, state_conv, page_table, rel_bias, g_ffn1, w1_gate, w1_up, w1_down, g_mix, w_in, qn_g, kn_cmp_g, kn_sel_g, kn_win_g, w_cmp_k, b_cmp_k, w_cmp_v, b_cmp_v, conv_w, conv_b, b_if, mn_g, w_out, g_ffn2, w2_gate, w2_up, w2_down):
    assert x_prompt.shape[2] == D_MODEL and g_ffn1.shape[0] == 1
    nb, t, _ = x_prompt.shape
    ns, ds, _ = x_sample.shape
    assert ds == 1 and t % Q_TILE == 0
    n_pages = page_table.shape[1]
    past = n_pages * PAGE_SIZE
    wb = cache_win_kv.shape[2]
    assert wb == WINDOW
    kv_shape = (2, N_KV_A, HEAD_DIM_A)

    row = lambda a: a.reshape(1, -1)
    two = lambda a: jnp.tile(a, 2).reshape(1, KV_W)
    w1 = (row(g_ffn1[0]), w1_gate[0].astype(BF16), w1_up[0].astype(BF16), w1_down[0].astype(BF16))
    w2 = (row(g_ffn2[0]), w2_gate[0].astype(BF16), w2_up[0].astype(BF16), w2_down[0].astype(BF16))
    wq, wkv, wm, ws = _prep_in_proj(w_in[0])
    qg = two(qn_g[0]) * F32(HEAD_DIM_A ** -0.5)
    kg = jnp.concatenate([two(kn_sel_g[0]), two(kn_win_g[0])], axis=0)
    w_big, b_big = _prep_compress(w_cmp_k[0], b_cmp_k[0], w_cmp_v[0], b_cmp_v[0])
    g_cmp = two(kn_cmp_g[0])
    woa, wom = _prep_out_proj(w_out[0])
    n_gate = 3 * N_HEADS_A
    bif = jnp.zeros((1, LANES), F32).at[0, n_gate:n_gate + 2 * N_HEADS_M].set(b_if[0])
    conv_b2 = row(conv_b[0])
    mn = row(mn_g[0])

    nqb = t // Q_TILE
    tiles = _bias_tiles(rel_bias, jnp.asarray(_bucket_tiles(t, past, wb)))
    stack_heads = lambda a: a.reshape(a.shape[0], N_HEADS_A * LANES, LANES)
    bnear = stack_heads(tiles[0:2])
    bcmp = stack_heads(tiles[2:2 + nqb])
    t_cmp, t_win, t_sel = tiles[2 + nqb], tiles[3 + nqb], tiles[4 + nqb]

    def dense_in(x):
        x1 = _ffn(x, *w1)
        return (x1,) + tuple(_inproj(x1, row(g_mix[0]), wq, wkv, wm, ws, qg, kg))

    (x1, q_pad, cmp_kv, sel_kv, win_kv, ksb, vsb, kwb, vwb, qk_m, v_m, o_m, small, gates) = dense_in(
        x_prompt.reshape(nb * t, D_MODEL))
    seq = lambda a: a.reshape(nb, t, a.shape[-1])
    n16 = t // CMP_STRIDE
    kc, vc = _compress(cmp_kv.reshape(nb, n16, CHUNK_ROW), w_big, b_big, g_cmp)
    n_sel = -(-t // SEL_BLOCK)
    cov_t = jnp.asarray(np.pad(_cover_np(n16 - 1, n_sel), ((0, 1), (0, 0))).T, BF16)
    expand = jnp.asarray(np.repeat(np.eye(n_sel, dtype=np.float32), SEL_BLOCK, axis=1)[:, :t], BF16)
    a_pad = _nsa_prompt(seq(q_pad), seq(gates), kc, vc, seq(ksb), seq(vsb), seq(kwb), seq(vwb), bnear, bcmp,
                        cov_t, expand, n16 - 1)
    chunk = 256 if t % 256 == 0 else Q_TILE
    m_out, c_p, n_p, m_p = _mlstm_prompt(seq(qk_m), seq(v_m), seq(o_m), seq(small), conv_w[0], conv_b2, bif, mn,
                                         chunk)
    y_prompt = _post(x1, a_pad.reshape(nb * t, -1), m_out.reshape(nb * t, -1), woa, wom, *w2)
    kv6 = lambda a, n: a.reshape((1, n, -1) + kv_shape)
    prompt_states = (kv6(cmp_kv, nb), kv6(sel_kv, nb), kv6(seq(win_kv)[:, t - wb:], nb), c_p[None], n_p[None],
                     m_p[None, :, 0, :N_HEADS_M], seq(qk_m)[None, :, t - (CONV_W - 1):])

    (x1s, q_s, cmp_s, sel_s, win_s, _, _, _, _, qk_s, v_s, o_s, small_s, gates_s) = dense_in(
        x_sample.reshape(ns, D_MODEL))
    q3 = q_s.reshape(ns, N_HEADS_A, LANES)
    n16s = past // CMP_STRIDE
    n_sel_s = -(-(past + 1) // SEL_BLOCK)
    n_sel_pad = -(-n_sel_s // LANES) * LANES
    cover_s = jnp.asarray(np.pad(_cover_np(n16s - 1, n_sel_s), ((0, 1), (0, n_sel_pad - n_sel_s))), BF16)
    bias_c = t_cmp[:, :n16s // LANES, :].reshape(N_HEADS_A, n16s)
    o_c, idx = _sample_cmp(page_table, cache_cmp_kv[0], w_big, b_big, g_cmp, q3, bias_c, cover_s, n_sel_s, past)
    n_past_blocks = past // SEL_BLOCK
    two_lanes = lambda a: jnp.concatenate([a[:, :SEL_BLOCK], a[:, :SEL_BLOCK]], axis=-1)
    bsel = jnp.stack([two_lanes(t_sel[:, n_past_blocks - 2, :]), two_lanes(t_sel[:, n_past_blocks - 1, :])])
    bwin = t_win[:, :wb // LANES, :].reshape(N_HEADS_A, wb)
    b0 = jnp.broadcast_to(t_win[:, wb // LANES, 0:1], (N_HEADS_A, LANES))
    gate_b = jnp.broadcast_to(gates_s[:, :n_gate, None], (ns, n_gate, LANES))
    a_s, win_out = _sample_attn(idx[:, :, 0].reshape(-1), page_table, cache_sel_kv[0],
                                cache_win_kv[0].reshape(ns, wb, KV_ROW), q3, sel_s[:, None, :], win_s[:, None, :],
                                o_c, gate_b, bsel, bwin, b0)
    m0 = jnp.pad(state_m[0], ((0, 0), (0, LANES - N_HEADS_M)))[:, None, :]
    m_s, c_s, n_s, m_new = _mlstm_step(qk_s[:, None, :], state_conv[0], v_s[:, None, :], o_s[:, None, :],
                                       small_s[:, None, :], state_C[0], state_n[0], m0, conv_w[0], conv_b2, bif, mn)
    y_sample = _post(x1s, a_s.reshape(ns, -1), m_s.reshape(ns, -1), woa, wom, *w2)
    conv_new = jnp.concatenate([state_conv[0][:, 1:], qk_s[:, None, :]], axis=1)
    sample_states = (kv6(cmp_s, ns), kv6(sel_s, ns), kv6(win_out, ns), c_s[None], n_s[None],
                     m_new[None, :, 0, :N_HEADS_M], conv_new[None])

    return (y_prompt.reshape(nb, t, D_MODEL), y_sample.reshape(ns, 1, D_MODEL)) + prompt_states + sample_states
```

```python
import functools
import math

import jax
import jax.numpy as jnp
import numpy as np
from jax import lax
from jax.experimental import pallas as pl
from jax.experimental.pallas import tpu as pltpu

F32 = jnp.float32
BF16 = jnp.bfloat16

D_MODEL = 1024
PAGE_SIZE = 128
N_HEADS_A = 8
HEAD_DIM_A = 64
N_KV_A = 2
GROUP_R = N_HEADS_A // N_KV_A
D_A = N_HEADS_A * HEAD_DIM_A
KV_W = N_KV_A * HEAD_DIM_A
CMP_STRIDE = 16
CMP_LEN = 2 * CMP_STRIDE
SEL_BLOCK = 64
TOP_K_BLOCKS = 16
WINDOW = 512
N_HEADS_M = 4
HEAD_DIM_M = 128
D_M = N_HEADS_M * HEAD_DIM_M
CONV_W = 4
N_BUCKETS = 32
MAX_DISTANCE = 128
EPS = 1e-6

LANES = 128
Q_TILE = 128
KV_ROW = 2 * KV_W
CHUNK_ROW = CMP_STRIDE * KV_ROW
NEG = -1e30
VMEM_LIMIT = 56 * 1024 * 1024

NT_DIMS = (((1,), (1,)), ((), ()))


def _dot(a, b):
    return jnp.dot(a, b, preferred_element_type=F32)


def _dot_nt(a, b):
    return lax.dot_general(a, b, NT_DIMS, preferred_element_type=F32)


def _split3(x):
    x1 = x.astype(BF16)
    r1 = x - x1.astype(F32)
    x2 = r1.astype(BF16)
    x3 = (r1 - x2.astype(F32)).astype(BF16)
    return x1, x2, x3


def _dot_exact_rhs(x, m):
    x1, x2, x3 = _split3(x)
    return _dot(x1, m) + _dot(x2, m) + _dot(x3, m)


def _dot_exact_lhs(m, x):
    x1, x2, x3 = _split3(x)
    return _dot(m, x1) + _dot(m, x2) + _dot(m, x3)


def _rms_rows(x, g):
    ms = jnp.mean(x * x, axis=-1, keepdims=True)
    return x * lax.rsqrt(ms + EPS) * g


def _rms_two_groups(k, g):
    sq = k * k
    lane = lax.broadcasted_iota(jnp.int32, sq.shape, 1)
    lo = lane < HEAD_DIM_A
    s0 = jnp.sum(jnp.where(lo, sq, 0.0), axis=-1, keepdims=True)
    s1 = jnp.sum(jnp.where(lo, 0.0, sq), axis=-1, keepdims=True)
    ms = jnp.where(lo, s0, s1) * (1.0 / HEAD_DIM_A)
    return k * lax.rsqrt(ms + EPS) * g


def _sigmoid(x):
    return jax.nn.sigmoid(x)


def _log_sigmoid(x):
    return jnp.minimum(x, 0.0) - jnp.log1p(jnp.exp(-jnp.abs(x)))


def _const_spec(shape):
    nd = len(shape)
    return pl.BlockSpec(shape, lambda *_: (0,) * nd, pipeline_mode=pl.Buffered(1))


def _params(*sem):
    return pltpu.CompilerParams(dimension_semantics=sem, vmem_limit_bytes=VMEM_LIMIT)


def _swiglu_residual(x, g_ref, wg_ref, wu_ref, wd_ref):
    xn = _rms_rows(x, g_ref[...]).astype(BF16)
    d_ff = wg_ref.shape[1]
    n_split = 2 if d_ff % (2 * LANES) == 0 else 1
    step = d_ff // n_split
    acc = jnp.zeros_like(x)
    for c in range(n_split):
        hg = _dot(xn, wg_ref[:, c * step:(c + 1) * step])
        hu = _dot(xn, wu_ref[:, c * step:(c + 1) * step])
        h = (hg * _sigmoid(hg)) * hu
        acc = acc + _dot(h.astype(BF16), wd_ref[c * step:(c + 1) * step, :])
    return x + 0.5 * acc


def _ffn_kernel(x_ref, g_ref, wg_ref, wu_ref, wd_ref, y_ref):
    y_ref[...] = _swiglu_residual(x_ref[...], g_ref, wg_ref, wu_ref, wd_ref)


def _post_kernel(x_ref, a_ref, m_ref, woa_ref, wom_ref, g_ref, wg_ref, wu_ref, wd_ref, y_ref):
    x = x_ref[...] + (_dot(a_ref[...], woa_ref[...]) + _dot(m_ref[...], wom_ref[...]))
    y_ref[...] = _swiglu_residual(x, g_ref, wg_ref, wu_ref, wd_ref)


def _token_tile(n):
    return 512 if n % 512 == 0 else n


def _ffn(x, g, wg, wu, wd):
    n = x.shape[0]
    tm = _token_tile(n)
    row = lambda w: pl.BlockSpec((tm, w), lambda i: (i, 0))
    return pl.pallas_call(
        _ffn_kernel,
        grid=(n // tm,),
        in_specs=[row(D_MODEL), _const_spec(g.shape), _const_spec(wg.shape), _const_spec(wu.shape),
                  _const_spec(wd.shape)],
        out_specs=row(D_MODEL),
        out_shape=jax.ShapeDtypeStruct((n, D_MODEL), F32),
        compiler_params=_params("parallel"),
        name="ffn1",
    )(x, g, wg, wu, wd)


def _post(x, a_pad, m_out, woa, wom, g, wg, wu, wd):
    n = x.shape[0]
    tm = _token_tile(n)
    row = lambda w: pl.BlockSpec((tm, w), lambda i: (i, 0))
    return pl.pallas_call(
        _post_kernel,
        grid=(n // tm,),
        in_specs=[row(D_MODEL), row(a_pad.shape[1]), row(m_out.shape[1]), _const_spec(woa.shape),
                  _const_spec(wom.shape), _const_spec(g.shape), _const_spec(wg.shape), _const_spec(wu.shape),
                  _const_spec(wd.shape)],
        out_specs=row(D_MODEL),
        out_shape=jax.ShapeDtypeStruct((n, D_MODEL), F32),
        compiler_params=_params("parallel"),
        name="post_mix_ffn2",
    )(x, a_pad, m_out, woa, wom, g, wg, wu, wd)


def _rms_two_groups_t(k, g_col):
    sq = k * k
    half = k.shape[0] // 2
    s0 = jnp.sum(sq[0:half], axis=0, keepdims=True)
    s1 = jnp.sum(sq[half:], axis=0, keepdims=True)
    ms = jnp.concatenate([jnp.broadcast_to(s0, (half, k.shape[1])), jnp.broadcast_to(s1, (half, k.shape[1]))],
                         axis=0) * (1.0 / HEAD_DIM_A)
    return k * lax.rsqrt(ms + EPS) * g_col


def _inproj_kernel(x_ref, g_ref, wq_ref, wc_ref, wkvt_ref, wm_ref, ws_ref, qg_ref, kgt_ref,
                   q_out, cmp_rows, cmpt_out, selt_out, wint_out, kst, vst, kwt, vwt, qk_out, v_out, o_out,
                   small_out, gate_out):
    xn = _rms_rows(x_ref[...], g_ref[...]).astype(BF16)
    q = _dot(xn, wq_ref[...])
    for h in range(N_HEADS_A):
        qh = q[:, h * LANES:(h + 1) * LANES]
        ms = jnp.sum(qh * qh, axis=-1, keepdims=True) * (1.0 / HEAD_DIM_A)
        q_out[:, h * LANES:(h + 1) * LANES] = (qh * lax.rsqrt(ms + EPS) * qg_ref[...]).astype(BF16)
    c = _dot(xn, wc_ref[...])
    cmp_rows[0] = c[:, 0:KV_W]
    cmp_rows[1] = c[:, KV_W:KV_ROW]
    kvt = _dot_nt(wkvt_ref[...], xn)
    cmpt_out[0] = kvt[0:KV_ROW]
    ks = _rms_two_groups_t(kvt[2 * KV_W:3 * KV_W], kgt_ref[0])
    vs = kvt[3 * KV_W:4 * KV_W]
    kw = _rms_two_groups_t(kvt[4 * KV_W:5 * KV_W], kgt_ref[1])
    vw = kvt[5 * KV_W:6 * KV_W]
    selt_out[0, 0:KV_W] = ks
    selt_out[0, KV_W:KV_ROW] = vs
    wint_out[0, 0:KV_W] = kw
    wint_out[0, KV_W:KV_ROW] = vw
    for j in range(kst.shape[1]):
        cols = slice(j * LANES, (j + 1) * LANES)
        kst[0, j] = ks[:, cols].astype(BF16)
        vst[0, j] = vs[:, cols].astype(BF16)
        kwt[0, j] = kw[:, cols].astype(BF16)
        vwt[0, j] = vw[:, cols].astype(BF16)
    m = _dot(xn, wm_ref[...])
    qk_out[...] = m[:, 0:2 * D_M]
    v_out[...] = m[:, 2 * D_M:3 * D_M]
    o_out[...] = m[:, 3 * D_M:4 * D_M]
    small = _dot(xn, ws_ref[...])
    small_out[...] = small
    gate_out[...] = _sigmoid(small)


def _inproj(x, seq_len, g, wq, wc, wkvt, wm, ws, qg, kgt):
    n = x.shape[0]
    nb = n // seq_len
    tm = _token_tile(seq_len)
    tpb = seq_len // tm
    row = lambda w: pl.BlockSpec((tm, w), lambda i: (i, 0))
    rows_out = lambda w, dt: (row(w), jax.ShapeDtypeStruct((n, w), dt))
    chan_out = (pl.BlockSpec((1, KV_ROW, tm), lambda i: (i // tpb, 0, i % tpb)),
                jax.ShapeDtypeStruct((nb, KV_ROW, seq_len), F32))
    tile_out = (pl.BlockSpec((1, tm // LANES, KV_W, LANES), lambda i: (i // tpb, i % tpb, 0, 0)),
                jax.ShapeDtypeStruct((nb, seq_len // LANES, KV_W, LANES), BF16))
    halves_out = (pl.BlockSpec((2, tm, KV_W), lambda i: (0, i, 0)), jax.ShapeDtypeStruct((2, n, KV_W), F32))
    outs = [rows_out(N_HEADS_A * LANES, BF16), halves_out, chan_out, chan_out, chan_out,
            tile_out, tile_out, tile_out, tile_out,
            rows_out(2 * D_M, F32), rows_out(D_M, F32), rows_out(D_M, F32), rows_out(LANES, F32),
            rows_out(LANES, F32)]
    return pl.pallas_call(
        _inproj_kernel,
        grid=(n // tm,),
        in_specs=[row(D_MODEL)] + [_const_spec(a.shape) for a in (g, wq, wc, wkvt, wm, ws, qg, kgt)],
        out_specs=[o[0] for o in outs],
        out_shape=[o[1] for o in outs],
        compiler_params=_params("parallel"),
        name="in_proj",
    )(x, g, wq, wc, wkvt, wm, ws, qg, kgt)


def _t5_bucket_np(dist):
    n = np.maximum(dist, 0)
    exact = N_BUCKETS // 2
    nf = np.maximum(n, 1).astype(np.float32)
    ratio = np.log(nf / np.float32(exact)) / np.float32(math.log(MAX_DISTANCE / exact))
    large = exact + (ratio * np.float32(N_BUCKETS - exact)).astype(np.int32)
    return np.where(n < exact, n, np.minimum(large, N_BUCKETS - 1)).astype(np.int32)


def _bias_kernel(tbl_ref, bkt_ref, out_ref):
    bkt = bkt_ref[0]
    for h in range(N_HEADS_A):
        far = tbl_ref[N_BUCKETS - 1, h]
        acc = jnp.zeros(bkt.shape, F32)
        for b in range(N_BUCKETS - 1):
            acc = jnp.where(bkt == b, tbl_ref[b, h] - far, acc)
        out_ref[0, h] = acc


def _bias_tiles(rel_bias, buckets):
    n = buckets.shape[0]
    return pl.pallas_call(
        _bias_kernel,
        grid=(n,),
        in_specs=[pl.BlockSpec(memory_space=pltpu.SMEM),
                  pl.BlockSpec((1, LANES, LANES), lambda i: (i, 0, 0))],
        out_specs=pl.BlockSpec((1, N_HEADS_A, LANES, LANES), lambda i: (i, 0, 0, 0)),
        out_shape=jax.ShapeDtypeStruct((n, N_HEADS_A, LANES, LANES), F32),
        compiler_params=_params("parallel"),
        name="bias_tiles",
    )(rel_bias, buckets)


def _compress_rows(x_ref, nc, w_ref, b_ref, g_ref):
    fs = None
    for t in range(CMP_STRIDE):
        rows = pl.ds(t, nc, stride=CMP_STRIDE)
        x_t = jnp.concatenate([x_ref[0, rows, :], x_ref[1, rows, :]], axis=1)
        part = _dot(x_t.astype(BF16), w_ref[t])
        fs = part if fs is None else fs + part
    first = fs[:, 0:KV_ROW]
    nxt = pltpu.roll(fs[:, KV_ROW:2 * KV_ROW], nc - 1, 0)
    blk = first + nxt + b_ref[...]
    kc = _rms_two_groups(blk[:, 0:KV_W], g_ref[...])
    vc = blk[:, KV_W:KV_ROW]
    complete = lax.broadcasted_iota(jnp.int32, kc.shape, 0) < nc - 1
    return jnp.where(complete, kc, 0.0), jnp.where(complete, vc, 0.0)


def _compress_kernel(x_ref, w_ref, b_ref, g_ref, kc_ref, vc_ref):
    nc = kc_ref.shape[1]
    kc, vc = _compress_rows(x_ref, nc, w_ref, b_ref, g_ref)
    kc_ref[0] = kc.astype(BF16)
    vc_ref[0] = vc.astype(BF16)


def _compress(cmp_rows, t, w_big, b_big, g_k):
    nb = cmp_rows.shape[1] // t
    nc = t // CMP_STRIDE
    blk = pl.BlockSpec((1, nc, KV_W), lambda b: (b, 0, 0))
    return pl.pallas_call(
        _compress_kernel,
        grid=(nb,),
        in_specs=[pl.BlockSpec((2, t, KV_W), lambda b: (0, b, 0)), _const_spec(w_big.shape),
                  _const_spec(b_big.shape), _const_spec(g_k.shape)],
        out_specs=[blk, blk],
        out_shape=[jax.ShapeDtypeStruct((nb, nc, KV_W), BF16)] * 2,
        compiler_params=_params("parallel"),
        name="compress_prompt",
    )(cmp_rows, w_big, b_big, g_k)


def _online_update(carry, qg, k_tile, v_tile, addmask):
    m, l, acc = carry
    s = _dot(qg, k_tile).reshape(GROUP_R, Q_TILE, -1) + addmask
    m_new = jnp.maximum(m, jnp.max(s, axis=-1, keepdims=True))
    alpha = jnp.exp(m - m_new)
    p = jnp.exp(s - m_new)
    l = alpha * l + jnp.sum(p, axis=-1, keepdims=True)
    pv = _dot_nt(p.reshape(GROUP_R * Q_TILE, -1).astype(BF16), v_tile)
    acc = alpha * acc + pv.reshape(GROUP_R, Q_TILE, LANES)
    return m_new, l, acc


def _rank_rows(score):
    n_blk = score.shape[0]
    blk = lax.broadcasted_iota(jnp.int32, score.shape, 0)
    rank = jnp.zeros(score.shape, F32)
    for b in range(n_blk):
        row = score[b:b + 1, :]
        rank = rank + jnp.where(blk > b, jnp.where(row >= score, 1.0, 0.0), jnp.where(row > score, 1.0, 0.0))
    return rank


def _nsa_prompt_kernel(q_ref, gate_ref, kc_ref, vc_ref, ks_ref, vs_ref, kw_ref, vw_ref, bnear_ref, bcmp_ref,
                       covt_ref, exp_ref, a_ref, mb_ref, *, n_cmp):
    i = pl.program_id(1)
    n_tiles = mb_ref.shape[0]
    n_sel = covt_ref.shape[0]
    q0 = i * Q_TILE
    qi_col = lax.broadcasted_iota(jnp.int32, (Q_TILE, LANES), 0)
    ki_row = lax.broadcasted_iota(jnp.int32, (Q_TILE, LANES), 1)
    causal_add = jnp.where(ki_row <= qi_col, 0.0, NEG)
    upper_add = jnp.where(ki_row > qi_col, 0.0, NEG)
    eye = jnp.where(ki_row == qi_col, 1.0, 0.0).astype(BF16)
    gates = gate_ref[0]
    lane = lax.broadcasted_iota(jnp.int32, (Q_TILE, LANES), 1)

    for g in range(N_KV_A):
        heads = range(g * GROUP_R, (g + 1) * GROUP_R)
        qg = jnp.concatenate([q_ref[0, :, h * LANES:(h + 1) * LANES] for h in heads], axis=0)
        rows = slice(g * GROUP_R * Q_TILE, (g + 1) * GROUP_R * Q_TILE)

        ncp = kc_ref.shape[1]
        j_row = lax.broadcasted_iota(jnp.int32, (Q_TILE, ncp), 1)
        q_col = q0 + lax.broadcasted_iota(jnp.int32, (Q_TILE, ncp), 0)
        usable = (j_row * CMP_STRIDE + (CMP_LEN - 1) <= q_col) & (j_row < n_cmp)
        s = _dot_nt(qg, kc_ref[0]).reshape(GROUP_R, Q_TILE, ncp)
        s = s + bcmp_ref[0, rows, :].reshape(GROUP_R, Q_TILE, ncp) + jnp.where(usable, 0.0, NEG)
        any_usable = jnp.where(q_col[:, 0:1] >= CMP_LEN - 1, 1.0, 0.0)
        e = jnp.exp(s - jnp.max(s, axis=-1, keepdims=True)) * any_usable
        p_c = e / jnp.maximum(jnp.sum(e, axis=-1, keepdims=True), 1e-30)
        o_c = _dot(p_c.reshape(GROUP_R * Q_TILE, ncp).astype(BF16), vc_ref[0]).reshape(GROUP_R, Q_TILE, LANES)

        p_sum = jnp.sum(p_c, axis=0)
        p1, p2, p3 = _split3(p_sum)
        cov_t = covt_ref[...]
        imp_t = _dot_nt(cov_t, p1) + _dot_nt(cov_t, p2) + _dot_nt(cov_t, p3)
        blk = lax.broadcasted_iota(jnp.int32, (n_sel, Q_TILE), 0)
        qpos = q0 + lax.broadcasted_iota(jnp.int32, (n_sel, Q_TILE), 1)
        cur = qpos // SEL_BLOCK
        valid = blk * SEL_BLOCK <= qpos
        forced = valid & ((blk == 0) | (blk == cur) | (blk == cur - 1))
        score = jnp.where(forced, -NEG, jnp.where(valid, imp_t, NEG))
        sel_t = jnp.where(_rank_rows(score) < float(min(TOP_K_BLOCKS, n_sel)), 1.0, 0.0).astype(BF16)
        sel = _dot_nt(eye, sel_t).astype(BF16)
        mask_all = _dot(sel, exp_ref[...])
        for t in range(n_tiles):
            mb_ref[t] = (mask_all[:, t * LANES:(t + 1) * LANES] - 1.0) * (-NEG)

        bnear = [bnear_ref[d, rows, :].reshape(GROUP_R, Q_TILE, LANES) for d in range(2)]
        init = (jnp.full((GROUP_R, Q_TILE, 1), NEG, F32), jnp.zeros((GROUP_R, Q_TILE, 1), F32),
                jnp.zeros((GROUP_R, Q_TILE, LANES), F32))

        def tile(ref, t):
            return ref[0, t]

        def far_body(t, carry):
            return _online_update(carry, qg, tile(ks_ref, t), tile(vs_ref, t), mb_ref[t])

        carry = lax.fori_loop(0, jnp.maximum(i - 1, 0), far_body, init)
        t1 = jnp.maximum(i - 1, 0)
        sub_add = mb_ref[t1] + jnp.where(i >= 1, 0.0, NEG)
        carry = _online_update(carry, qg, tile(ks_ref, t1), tile(vs_ref, t1), bnear[1] + sub_add)
        carry = _online_update(carry, qg, tile(ks_ref, i), tile(vs_ref, i), bnear[0] + (mb_ref[i] + causal_add))
        o_s = carry[2] / carry[1]

        carry = init
        n_win_tiles = WINDOW // LANES
        for d in range(n_win_tiles, -1, -1):
            t = jnp.maximum(i - d, 0)
            add = jnp.where(i >= d, 0.0, NEG)
            if d == n_win_tiles:
                add = add + upper_add
            elif d == 0:
                add = bnear[0] + causal_add
            elif d == 1:
                add = bnear[1] + add
            else:
                add = jnp.broadcast_to(add, (Q_TILE, LANES))
            carry = _online_update(carry, qg, tile(kw_ref, t), tile(vw_ref, t), add)
        o_w = carry[2] / carry[1]

        own_half = (lane >= g * HEAD_DIM_A) & (lane < (g + 1) * HEAD_DIM_A)
        for r, h in enumerate(heads):
            out = (gates[:, h:h + 1] * o_c[r] + gates[:, N_HEADS_A + h:N_HEADS_A + h + 1] * o_s[r]
                   + gates[:, 2 * N_HEADS_A + h:2 * N_HEADS_A + h + 1] * o_w[r])
            a_ref[0, :, h * LANES:(h + 1) * LANES] = jnp.where(own_half, out, 0.0).astype(BF16)


def _nsa_prompt(q_pad, gates, kc, vc, ksb, vsb, kwb, vwb, bnear, bcmp, cov_t, expand, n_cmp):
    nb, t, _ = q_pad.shape
    nqb = t // Q_TILE
    seq = lambda a: pl.BlockSpec((1,) + a.shape[1:], lambda b, i: (b,) + (0,) * (a.ndim - 1))
    return pl.pallas_call(
        functools.partial(_nsa_prompt_kernel, n_cmp=n_cmp),
        grid=(nb, nqb),
        in_specs=[pl.BlockSpec((1, Q_TILE, N_HEADS_A * LANES), lambda b, i: (b, i, 0)),
                  pl.BlockSpec((1, Q_TILE, LANES), lambda b, i: (b, i, 0)),
                  seq(kc), seq(vc), seq(ksb), seq(vsb), seq(kwb), seq(vwb),
                  _const_spec(bnear.shape),
                  pl.BlockSpec((1,) + bcmp.shape[1:], lambda b, i: (i, 0, 0)),
                  _const_spec(cov_t.shape), _const_spec(expand.shape)],
        out_specs=pl.BlockSpec((1, Q_TILE, N_HEADS_A * LANES), lambda b, i: (b, i, 0)),
        out_shape=jax.ShapeDtypeStruct((nb, t, N_HEADS_A * LANES), BF16),
        scratch_shapes=[pltpu.VMEM((t // LANES, Q_TILE, LANES), F32)],
        compiler_params=_params("parallel", "arbitrary"),
        name="nsa_prompt",
    )(q_pad, gates, kc, vc, ksb, vsb, kwb, vwb, bnear, bcmp, cov_t, expand)


def _head_norm_gate(h_all, mn_ref, o):
    outs = []
    for h in range(N_HEADS_M):
        hs = slice(h * HEAD_DIM_M, (h + 1) * HEAD_DIM_M)
        outs.append(_rms_rows(h_all[h], mn_ref[:, hs]))
    return _sigmoid(o) * jnp.concatenate(outs, axis=-1)


def _mlstm_prompt_kernel(qk_ref, v_ref, o_ref, small_ref, cw_ref, cb_ref, bif_ref, mn_ref, tri_ref, triu_ref,
                         y_ref, c_ref, n_ref, m_ref, xbuf):
    chunk = qk_ref.shape[1]
    pad = 8

    @pl.when(pl.program_id(1) == 0)
    def _():
        xbuf[0:pad, :] = jnp.zeros((pad, 2 * D_M), F32)
        c_ref[...] = jnp.zeros_like(c_ref)
        n_ref[...] = jnp.zeros_like(n_ref)
        m_ref[...] = jnp.zeros_like(m_ref)

    x = qk_ref[0]
    xbuf[pad:pad + chunk, :] = x
    y = xbuf[pad - 3:pad - 3 + chunk, :] * cw_ref[0:1, :]
    y = y + xbuf[pad - 2:pad - 2 + chunk, :] * cw_ref[1:2, :]
    y = y + xbuf[pad - 1:pad - 1 + chunk, :] * cw_ref[2:3, :]
    y = y + x * cw_ref[3:4, :] + cb_ref[...]
    qkc = y * _sigmoid(y)
    xbuf[0:pad, :] = xbuf[chunk:chunk + pad, :]

    gi = small_ref[0] + bif_ref[...]
    gi_t = gi.T
    b_col = _dot_exact_lhs(tri_ref[...], _log_sigmoid(gi))
    b_row = _dot_exact_rhs(_log_sigmoid(gi_t), triu_ref[...])
    t_col = lax.broadcasted_iota(jnp.int32, (chunk, chunk), 0)
    s_row = lax.broadcasted_iota(jnp.int32, (chunk, chunk), 1)
    causal = s_row <= t_col
    m_all = m_ref[0]
    lane_row = lax.broadcasted_iota(jnp.int32, m_all.shape, 1)

    h_all = []
    for h in range(N_HEADS_M):
        hs = slice(h * HEAD_DIM_M, (h + 1) * HEAD_DIM_M)
        ks = slice(D_M + h * HEAD_DIM_M, D_M + (h + 1) * HEAD_DIM_M)
        q = qkc[:, hs]
        k = qkc[:, ks] * F32(HEAD_DIM_M ** -0.5)
        v = v_ref[0, :, hs]
        qb, kb = q.astype(BF16), k.astype(BF16)
        ci, fi = 3 * N_HEADS_A + h, 3 * N_HEADS_A + N_HEADS_M + h
        bt = b_col[:, fi:fi + 1]
        bs = b_row[fi:fi + 1, :]
        ig_row = gi_t[ci:ci + 1, :]
        ig_col = gi[:, ci:ci + 1]
        m_prev = m_all[:, h:h + 1]
        c_prev = c_ref[0, h]
        n_prev = n_ref[0, h:h + 1, :]

        dlog = jnp.where(causal, bt - bs + ig_row, NEG)
        inter = bt + m_prev
        m_t = jnp.maximum(inter, jnp.max(dlog, axis=-1, keepdims=True))
        s_qk = _dot_nt(qb, kb) * jnp.exp(dlog - m_t)
        dec = jnp.exp(inter - m_t)
        num = _dot(s_qk.astype(BF16), v.astype(BF16)) + dec * _dot_nt(qb, c_prev.astype(BF16))
        den = jnp.sum(s_qk, axis=-1, keepdims=True) + dec * jnp.sum(q * n_prev, axis=-1, keepdims=True)
        h_all.append(num / jnp.maximum(jnp.abs(den), jnp.exp(-m_t)))

        m_new = m_t[chunk - 1:chunk, :]
        b_end = bt[chunk - 1:chunk, :]
        w_end = jnp.exp(b_end - bt + ig_col - m_new)
        dec_end = jnp.exp(b_end + m_prev - m_new)
        wv_t = (w_end * v).T.astype(BF16)
        c_ref[0, h] = dec_end * c_prev + _dot(wv_t, kb)
        n_ref[0, h:h + 1, :] = dec_end * n_prev + jnp.sum(w_end * k, axis=0, keepdims=True)
        m_all = jnp.where(lane_row == h, m_new, m_all)

    m_ref[0] = m_all
    y_ref[0] = _head_norm_gate(h_all, mn_ref, o_ref[0]).astype(BF16)


def _mlstm_prompt(qk, v, o, small, conv_w, conv_b, bif, mn_g, chunk):
    nb, t, _ = qk.shape
    tri = jnp.asarray(np.tril(np.ones((chunk, chunk), np.float32)), BF16)
    triu = jnp.asarray(np.triu(np.ones((chunk, chunk), np.float32)), BF16)
    tok = lambda w: pl.BlockSpec((1, chunk, w), lambda b, c: (b, c, 0))
    return pl.pallas_call(
        _mlstm_prompt_kernel,
        grid=(nb, t // chunk),
        in_specs=[tok(2 * D_M), tok(D_M), tok(D_M), tok(LANES)]
        + [_const_spec(a.shape) for a in (conv_w, conv_b, bif, mn_g, tri, triu)],
        out_specs=[tok(D_M),
                   pl.BlockSpec((1, N_HEADS_M, HEAD_DIM_M, HEAD_DIM_M), lambda b, c: (b, 0, 0, 0)),
                   pl.BlockSpec((1, N_HEADS_M, HEAD_DIM_M), lambda b, c: (b, 0, 0)),
                   pl.BlockSpec((1, 1, LANES), lambda b, c: (b, 0, 0))],
        out_shape=[jax.ShapeDtypeStruct((nb, t, D_M), BF16),
                   jax.ShapeDtypeStruct((nb, N_HEADS_M, HEAD_DIM_M, HEAD_DIM_M), F32),
                   jax.ShapeDtypeStruct((nb, N_HEADS_M, HEAD_DIM_M), F32),
                   jax.ShapeDtypeStruct((nb, 1, LANES), F32)],
        scratch_shapes=[pltpu.VMEM((chunk + 8, 2 * D_M), F32)],
        compiler_params=_params("parallel", "arbitrary"),
        name="mlstm_prompt",
    )(qk, v, o, small, conv_w, conv_b, bif, mn_g, tri, triu)


def _softmax_rows(s):
    e = jnp.exp(s - jnp.max(s, axis=-1, keepdims=True))
    return e, jnp.sum(e, axis=-1, keepdims=True)


def _sample_cmp_kernel(pt_ref, *refs, n_pages, n_sel, qpos):
    pages = refs[:n_pages]
    w_ref, b_ref, g_ref, q_ref, bias_ref, cov_ref, oc_ref, idx_ref, xs_ref = refs[n_pages:]
    for u in range(n_pages):
        for half in range(2):
            xs_ref[half, u * PAGE_SIZE:(u + 1) * PAGE_SIZE, :] = pages[u][0, half * KV_W:(half + 1) * KV_W, :].T
    nc = n_pages * PAGE_SIZE // CMP_STRIDE
    kc, vc = _compress_rows(xs_ref, nc, w_ref, b_ref, g_ref)
    q = q_ref[0]
    j_row = lax.broadcasted_iota(jnp.int32, (N_HEADS_A, nc), 1)
    s = _dot_nt(q, kc.astype(BF16)) + bias_ref[...] + jnp.where(j_row < nc - 1, 0.0, NEG)
    e, l = _softmax_rows(s)
    p = e / jnp.maximum(l, 1e-30)
    oc_ref[0] = _dot(p.astype(BF16), vc.astype(BF16))

    nsp = cov_ref.shape[1]
    blk_row = lax.broadcasted_iota(jnp.int32, (1, nsp), 1)
    sub = lax.broadcasted_iota(jnp.int32, (nsp, nsp), 0)
    lan = lax.broadcasted_iota(jnp.int32, (nsp, nsp), 1)
    cur = qpos // SEL_BLOCK
    valid = (blk_row * SEL_BLOCK <= qpos) & (blk_row < n_sel)
    forced = valid & ((blk_row == 0) | (blk_row == cur) | (blk_row == cur - 1))
    k_col = lax.broadcasted_iota(jnp.int32, (TOP_K_BLOCKS, nsp), 0).astype(F32)
    blk_f = lax.broadcasted_iota(jnp.int32, (TOP_K_BLOCKS, nsp), 1).astype(F32)
    for g in range(N_KV_A):
        p_sum = jnp.sum(p[g * GROUP_R:(g + 1) * GROUP_R, :], axis=0, keepdims=True)
        imp = _dot_exact_rhs(jnp.broadcast_to(p_sum, (8, nc)), cov_ref[...])[0:1, :]
        score = jnp.where(forced, -NEG, jnp.where(valid, imp, NEG))
        score_col = jnp.sum(jnp.where(sub == lan, score, 0.0), axis=-1, keepdims=True)
        ahead = jnp.where(sub < lan, jnp.where(score_col >= score, 1.0, 0.0), jnp.where(score_col > score, 1.0, 0.0))
        rank = jnp.sum(ahead, axis=0, keepdims=True)
        chosen = jnp.sum(jnp.where(rank == k_col, blk_f, 0.0), axis=-1, keepdims=True)
        idx_ref[0, g * TOP_K_BLOCKS:(g + 1) * TOP_K_BLOCKS, :] = jnp.broadcast_to(
            chosen, (TOP_K_BLOCKS, LANES)).astype(jnp.int32)


def _sample_cmp(page_table, pool, w_big, b_big, g_k, q3, bias_c, cover, n_sel, qpos):
    ns, n_pages = page_table.shape
    page_spec = lambda u: pl.BlockSpec((1, KV_ROW, PAGE_SIZE), lambda s, pt: (pt[s, u], 0, 0))
    const = lambda a: pl.BlockSpec(a.shape, lambda s, pt: (0,) * a.ndim, pipeline_mode=pl.Buffered(1))
    grid_spec = pltpu.PrefetchScalarGridSpec(
        num_scalar_prefetch=1,
        grid=(ns,),
        in_specs=[page_spec(u) for u in range(n_pages)]
        + [const(w_big), const(b_big), const(g_k),
           pl.BlockSpec((1, N_HEADS_A, LANES), lambda s, pt: (s, 0, 0)), const(bias_c), const(cover)],
        out_specs=[pl.BlockSpec((1, N_HEADS_A, LANES), lambda s, pt: (s, 0, 0)),
                   pl.BlockSpec((1, N_KV_A * TOP_K_BLOCKS, LANES), lambda s, pt: (s, 0, 0))],
        scratch_shapes=[pltpu.VMEM((2, n_pages * PAGE_SIZE, KV_W), F32)],
    )
    return pl.pallas_call(
        functools.partial(_sample_cmp_kernel, n_pages=n_pages, n_sel=n_sel, qpos=qpos),
        grid_spec=grid_spec,
        out_shape=[jax.ShapeDtypeStruct((ns, N_HEADS_A, LANES), F32),
                   jax.ShapeDtypeStruct((ns, N_KV_A * TOP_K_BLOCKS, LANES), jnp.int32)],
        compiler_params=_params("arbitrary"),
        name="sample_cmp_topk",
    )(page_table, *([pool] * n_pages), w_big, b_big, g_k, q3, bias_c, cover)


def _sample_attn_kernel(idx_ref, pt_ref, *refs, n_past_blocks):
    n_slots = N_KV_A * TOP_K_BLOCKS
    blocks = refs[:n_slots]
    (win_ref, q_ref, selnew_ref, winnew_ref, wint_ref, oc_ref, gate_ref, bsel_ref, bwin_ref, b0_ref,
     a_ref, winout_ref) = refs[n_slots:]
    s_id = pl.program_id(0)
    q = q_ref[0]
    qf = q.astype(F32)
    lane = lax.broadcasted_iota(jnp.int32, (N_HEADS_A, LANES), 1)
    row = lax.broadcasted_iota(jnp.int32, (N_HEADS_A, LANES), 0)
    lo = lane < SEL_BLOCK

    def with_new_key(s, vt_mat, new_row):
        s_new = jnp.sum(qf * new_row[:, 0:KV_W], axis=-1, keepdims=True) + b0_ref[:, 0:1]
        m = jnp.maximum(jnp.max(s, axis=-1, keepdims=True), s_new)
        p = jnp.exp(s - m)
        p_new = jnp.exp(s_new - m)
        l = jnp.sum(p, axis=-1, keepdims=True) + p_new
        return (_dot_nt(p.astype(BF16), vt_mat) + p_new * new_row[:, KV_W:KV_ROW]) / l

    sel_new = selnew_ref[0]
    o_s = []
    for g in range(N_KV_A):
        slots = range(g * TOP_K_BLOCKS, (g + 1) * TOP_K_BLOCKS)
        kt_cat = jnp.concatenate([blocks[k][0, 0:KV_W, :] for k in slots], axis=1).astype(BF16)
        vt_cat = jnp.concatenate([blocks[k][0, KV_W:KV_ROW, :] for k in slots], axis=1).astype(BF16)
        adds = []
        for k in slots:
            b = idx_ref[s_id * n_slots + k]
            add = jnp.where(b == n_past_blocks - 2, bsel_ref[0], jnp.where(b == n_past_blocks - 1, bsel_ref[1], 0.0))
            other_half = jnp.where(b % 2 == 0, jnp.where(lo, 0.0, NEG), jnp.where(lo, NEG, 0.0))
            adds.append(add + other_half + jnp.where(b >= n_past_blocks, NEG, 0.0))
        o_s.append(with_new_key(_dot(q, kt_cat) + jnp.concatenate(adds, axis=-1), vt_cat, sel_new))
    o_sel = jnp.where(row < GROUP_R, o_s[0], o_s[1])

    win = win_ref[0]
    wb = win.shape[1]
    w_lane = lax.broadcasted_iota(jnp.int32, (N_HEADS_A, wb), 1)
    s_w = _dot(q, win[0:KV_W, :].astype(BF16)) + bwin_ref[...] + jnp.where(w_lane == 0, NEG, 0.0)
    o_win = with_new_key(s_w, win[KV_W:KV_ROW, :].astype(BF16), winnew_ref[0])

    na = N_HEADS_A
    out = gate_ref[0, 0:na] * oc_ref[0] + gate_ref[0, na:2 * na] * o_sel + gate_ref[0, 2 * na:3 * na] * o_win
    a_ref[0] = jnp.where(lo == (row < GROUP_R), out, 0.0).astype(BF16)

    wint = wint_ref[...]
    seq_lane = lax.broadcasted_iota(jnp.int32, wint.shape, 1)
    new_col = jnp.sum(jnp.where(seq_lane == s_id, wint, 0.0), axis=-1, keepdims=True)
    buf_lane = lax.broadcasted_iota(jnp.int32, win.shape, 1)
    winout_ref[0] = jnp.where(buf_lane == wb - 1, new_col, pltpu.roll(win, wb - 1, 1))


def _sample_attn(idx, page_table, pool, win_t, q3, sel_new, win_new, win_new_t, o_c, gate_b, bsel, bwin, b0):
    ns, n_pages = page_table.shape
    n_slots = N_KV_A * TOP_K_BLOCKS
    per_page = PAGE_SIZE // SEL_BLOCK
    n_past_blocks = n_pages * per_page
    wb = win_t.shape[2]

    def slot_spec(k):
        def index(s, idx_ref, pt_ref):
            b = jnp.minimum(idx_ref[s * n_slots + k], n_past_blocks - 1)
            return (pt_ref[s, b // per_page], 0, 0)
        return pl.BlockSpec((1, KV_ROW, PAGE_SIZE), index)

    per_seq = lambda a: pl.BlockSpec((1,) + a.shape[1:], lambda s, *_: (s,) + (0,) * (a.ndim - 1))
    const = lambda a: pl.BlockSpec(a.shape, lambda s, *_: (0,) * a.ndim, pipeline_mode=pl.Buffered(1))
    grid_spec = pltpu.PrefetchScalarGridSpec(
        num_scalar_prefetch=2,
        grid=(ns,),
        in_specs=[slot_spec(k) for k in range(n_slots)]
        + [per_seq(win_t), per_seq(q3), per_seq(sel_new), per_seq(win_new), const(win_new_t), per_seq(o_c),
           per_seq(gate_b), const(bsel), const(bwin), const(b0)],
        out_specs=[pl.BlockSpec((1, N_HEADS_A, LANES), lambda s, *_: (s, 0, 0)),
                   pl.BlockSpec((1, KV_ROW, wb), lambda s, *_: (s, 0, 0))],
    )
    return pl.pallas_call(
        functools.partial(_sample_attn_kernel, n_past_blocks=n_past_blocks),
        grid_spec=grid_spec,
        out_shape=[jax.ShapeDtypeStruct((ns, N_HEADS_A, LANES), BF16),
                   jax.ShapeDtypeStruct((ns, KV_ROW, wb), F32)],
        compiler_params=_params("arbitrary"),
        name="sample_sel_win",
    )(idx, page_table, *([pool] * n_slots), win_t, q3, sel_new, win_new, win_new_t, o_c, gate_b, bsel, bwin, b0)


def _mlstm_step_kernel(qk_ref, conv_ref, v_ref, o_ref, small_ref, c_ref, n_ref, m_ref, cw_ref, cb_ref, bif_ref,
                       mn_ref, y_ref, c_out, n_out, m_out):
    seqs = qk_ref.shape[0]
    sub = lax.broadcasted_iota(jnp.int32, (HEAD_DIM_M, HEAD_DIM_M), 0)
    lan = lax.broadcasted_iota(jnp.int32, (HEAD_DIM_M, HEAD_DIM_M), 1)
    eye = sub == lan
    lane_row = lax.broadcasted_iota(jnp.int32, (1, LANES), 1)
    for s in range(seqs):
        hist = conv_ref[s]
        y = hist[0:1, :] * cw_ref[0:1, :]
        y = y + hist[1:2, :] * cw_ref[1:2, :]
        y = y + hist[2:3, :] * cw_ref[2:3, :]
        y = y + qk_ref[s] * cw_ref[3:4, :] + cb_ref[...]
        qkc = y * _sigmoid(y)
        gi = small_ref[s] + bif_ref[...]
        lf_all = _log_sigmoid(gi)
        m_all = m_ref[s]
        m_new_all = m_all
        h_all = []
        for h in range(N_HEADS_M):
            hs = slice(h * HEAD_DIM_M, (h + 1) * HEAD_DIM_M)
            ks = slice(D_M + h * HEAD_DIM_M, D_M + (h + 1) * HEAD_DIM_M)
            q = qkc[:, hs]
            k = qkc[:, ks] * F32(HEAD_DIM_M ** -0.5)
            v = v_ref[s][:, hs]
            ci, fi = 3 * N_HEADS_A + h, 3 * N_HEADS_A + N_HEADS_M + h
            ig = gi[:, ci:ci + 1]
            lf = lf_all[:, fi:fi + 1]
            m_prev = m_all[:, h:h + 1]
            c_prev = c_ref[s, h]
            n_prev = n_ref[s, h:h + 1, :]
            inter = lf + m_prev
            m_t = jnp.maximum(inter, ig)
            w_in = jnp.exp(ig - m_t)
            dec = jnp.exp(inter - m_t)
            s_qk = jnp.sum(q * k, axis=-1, keepdims=True) * w_in
            cq = _dot_nt(jnp.broadcast_to(q, (8, HEAD_DIM_M)).astype(BF16), c_prev.astype(BF16))[0:1, :]
            num = s_qk * v + dec * cq
            den = s_qk + dec * jnp.sum(n_prev * q, axis=-1, keepdims=True)
            h_all.append(num / jnp.maximum(jnp.abs(den), jnp.exp(-m_t)))
            v_col = jnp.sum(jnp.where(eye, v, 0.0), axis=-1, keepdims=True)
            c_out[s, h] = dec * c_prev + (w_in * v_col) * k
            n_out[s, h:h + 1, :] = dec * n_prev + w_in * k
            m_new_all = jnp.where(lane_row == h, m_t, m_new_all)
        m_out[s] = m_new_all
        y_ref[s] = _head_norm_gate(h_all, mn_ref, o_ref[s]).astype(BF16)


def _mlstm_step(qk, conv0, v, o, small, c0, n0, m0, conv_w, conv_b, bif, mn_g):
    ns = qk.shape[0]
    sb = 8 if ns % 8 == 0 else 1
    per = lambda a: pl.BlockSpec((sb,) + a.shape[1:], lambda i: (i,) + (0,) * (a.ndim - 1))
    ins = (qk, conv0, v, o, small, c0, n0, m0)
    outs = [jax.ShapeDtypeStruct((ns, 1, D_M), BF16), jax.ShapeDtypeStruct(c0.shape, F32),
            jax.ShapeDtypeStruct(n0.shape, F32), jax.ShapeDtypeStruct(m0.shape, F32)]
    return pl.pallas_call(
        _mlstm_step_kernel,
        grid=(ns // sb,),
        in_specs=[per(a) for a in ins] + [_const_spec(a.shape) for a in (conv_w, conv_b, bif, mn_g)],
        out_specs=[per(a) for a in outs],
        out_shape=outs,
        compiler_params=_params("parallel"),
        name="mlstm_step",
    )(*ins, conv_w, conv_b, bif, mn_g)


def _cover_np(n_cmp, n_sel):
    cs = np.arange(n_cmp)[:, None] * CMP_STRIDE
    bs = np.arange(n_sel)[None, :] * SEL_BLOCK
    shared = np.clip(np.minimum(cs + CMP_LEN, bs + SEL_BLOCK) - np.maximum(cs, bs), 0, None)
    return (shared / CMP_LEN).astype(np.float32)


def _prep_in_proj(w_in):
    d = w_in.shape[0]
    wq = w_in[:, :D_A].reshape(d, N_HEADS_A, HEAD_DIM_A)
    z = jnp.zeros_like(wq)
    g0 = jnp.concatenate([wq[:, :GROUP_R], z[:, :GROUP_R]], axis=-1)
    g1 = jnp.concatenate([z[:, GROUP_R:], wq[:, GROUP_R:]], axis=-1)
    wq_pad = jnp.concatenate([g0, g1], axis=1).reshape(d, N_HEADS_A * LANES)
    o_kv = D_A
    o_gate = o_kv + 6 * KV_W
    o_m = o_gate + 3 * N_HEADS_A
    o_if = o_m + 4 * D_M
    w_cmp = w_in[:, o_kv:o_kv + KV_ROW]
    w_kv_t = w_in[:, o_kv:o_gate].T
    w_m = w_in[:, o_m:o_if]
    w_small = jnp.concatenate([w_in[:, o_gate:o_m], w_in[:, o_if:],
                               jnp.zeros((d, LANES - 3 * N_HEADS_A - 2 * N_HEADS_M), w_in.dtype)], axis=1)
    return [w.astype(BF16) for w in (wq_pad, w_cmp, w_kv_t, w_m, w_small)]


def _prep_compress(w_k, b_k, w_v, b_v):
    per_c = jnp.stack([w_k, w_k, w_v, w_v]).reshape(4, 2, CMP_STRIDE, HEAD_DIM_A, HEAD_DIM_A)
    w_big = jnp.einsum('chtde,cf->tcdhfe', per_c, jnp.eye(4, dtype=w_k.dtype))
    w_big = w_big.reshape(CMP_STRIDE, KV_ROW, 2 * KV_ROW).astype(BF16)
    b_big = jnp.concatenate([b_k, b_k, b_v, b_v]).reshape(1, KV_ROW)
    return w_big, b_big


def _prep_out_proj(w_out):
    wa = w_out[:D_A].reshape(N_HEADS_A, HEAD_DIM_A, -1)
    z = jnp.zeros_like(wa)
    g0 = jnp.concatenate([wa[:GROUP_R], z[:GROUP_R]], axis=1)
    g1 = jnp.concatenate([z[GROUP_R:], wa[GROUP_R:]], axis=1)
    woa = jnp.concatenate([g0, g1], axis=0).reshape(N_HEADS_A * LANES, -1)
    return woa.astype(BF16), w_out[D_A:].astype(BF16)


def _bucket_tiles(t, past, wb):
    r = np.arange(LANES)
    tiles = [_t5_bucket_np(d * LANES + r[:, None] - r[None, :]) for d in range(2)]
    for i in range(t // Q_TILE):
        tiles.append(_t5_bucket_np(i * Q_TILE + r[:, None] - (CMP_STRIDE * r[None, :] + CMP_LEN - 1)))
    flat = np.arange(LANES * LANES).reshape(LANES, LANES)
    tiles.append(_t5_bucket_np(past - (CMP_STRIDE * flat + CMP_LEN - 1)))
    tiles.append(_t5_bucket_np(wb - flat))
    tiles.append(_t5_bucket_np(past - (r[:, None] * SEL_BLOCK + r[None, :] % SEL_BLOCK)))
    return np.stack(tiles).astype(np.int32)


def kernel(x_prompt, x_sample, cache_cmp_kv, cache_sel_kv, cache_win_kv, state_C, state_n, state_m, state_conv, page_table, rel_bias, g_ffn1, w1_gate, w1_up, w1_down, g_mix, w_in, qn_g, kn_cmp_g, kn_sel_g, kn_win_g, w_cmp_k, b_cmp_k, w_cmp_v, b_cmp_v, conv_w, conv_b, b_if, mn_g, w_out, g_ffn2, w2_gate, w2_up, w2_down):
    assert x_prompt.shape[2] == D_MODEL and g_ffn1.shape[0] == 1
    nb, t, _ = x_prompt.shape
    ns, ds, _ = x_sample.shape
    assert ds == 1 and t % Q_TILE == 0
    n_pages = page_table.shape[1]
    past = n_pages * PAGE_SIZE
    wb = cache_win_kv.shape[2]
    assert wb == WINDOW
    kv_shape = (2, N_KV_A, HEAD_DIM_A)

    row = lambda a: a.reshape(1, -1)
    two = lambda a: jnp.tile(a, 2).reshape(1, KV_W)
    w1 = (row(g_ffn1[0]), w1_gate[0].astype(BF16), w1_up[0].astype(BF16), w1_down[0].astype(BF16))
    w2 = (row(g_ffn2[0]), w2_gate[0].astype(BF16), w2_up[0].astype(BF16), w2_down[0].astype(BF16))
    wq, wc, wkvt, wm, ws = _prep_in_proj(w_in[0])
    qg = two(qn_g[0]) * F32(HEAD_DIM_A ** -0.5)
    kgt = jnp.stack([jnp.tile(kn_sel_g[0], 2), jnp.tile(kn_win_g[0], 2)])[:, :, None]
    w_big, b_big = _prep_compress(w_cmp_k[0], b_cmp_k[0], w_cmp_v[0], b_cmp_v[0])
    g_cmp = two(kn_cmp_g[0])
    woa, wom = _prep_out_proj(w_out[0])
    n_gate = 3 * N_HEADS_A
    bif = jnp.zeros((1, LANES), F32).at[0, n_gate:n_gate + 2 * N_HEADS_M].set(b_if[0])
    conv_b2 = row(conv_b[0])
    mn = row(mn_g[0])

    nqb = t // Q_TILE
    tiles = _bias_tiles(rel_bias, jnp.asarray(_bucket_tiles(t, past, wb)))
    stack_heads = lambda a: a.reshape(a.shape[0], N_HEADS_A * LANES, LANES)
    bnear = stack_heads(tiles[0:2])
    bcmp = stack_heads(tiles[2:2 + nqb])
    t_cmp, t_win, t_sel = tiles[2 + nqb], tiles[3 + nqb], tiles[4 + nqb]

    def dense_in(x, seq_len):
        x1 = _ffn(x, *w1)
        return (x1,) + tuple(_inproj(x1, seq_len, row(g_mix[0]), wq, wc, wkvt, wm, ws, qg, kgt))

    from_chan = lambda a: a.reshape((1, a.shape[0]) + kv_shape + (a.shape[2],)).transpose(0, 1, 5, 2, 3, 4)
    to_chan = lambda a: a.transpose(0, 2, 3, 4, 1).reshape(a.shape[0], KV_ROW, a.shape[1])

    (x1, q_pad, cmp_rows, cmp_t, sel_t, win_t, kst, vst, kwt, vwt, qk_m, v_m, o_m, small, gates) = dense_in(
        x_prompt.reshape(nb * t, D_MODEL), t)
    seq = lambda a: a.reshape(nb, t, a.shape[-1])
    n16 = t // CMP_STRIDE
    kc, vc = _compress(cmp_rows, t, w_big, b_big, g_cmp)
    n_sel = -(-t // SEL_BLOCK)
    cov_t = jnp.asarray(np.pad(_cover_np(n16 - 1, n_sel), ((0, 1), (0, 0))).T, BF16)
    expand = jnp.asarray(np.repeat(np.eye(n_sel, dtype=np.float32), SEL_BLOCK, axis=1)[:, :t], BF16)
    a_pad = _nsa_prompt(seq(q_pad), seq(gates), kc, vc, kst, vst, kwt, vwt, bnear, bcmp, cov_t, expand, n16 - 1)
    chunk = 256 if t % 256 == 0 else Q_TILE
    m_out, c_p, n_p, m_p = _mlstm_prompt(seq(qk_m), seq(v_m), seq(o_m), seq(small), conv_w[0], conv_b2, bif, mn,
                                         chunk)
    y_prompt = _post(x1, a_pad.reshape(nb * t, -1), m_out.reshape(nb * t, -1), woa, wom, *w2)
    kv6 = lambda a, n: a.reshape((1, n, -1) + kv_shape)
    prompt_states = (from_chan(cmp_t), from_chan(sel_t), from_chan(win_t[:, :, t - wb:]), c_p[None], n_p[None],
                     m_p[None, :, 0, :N_HEADS_M], seq(qk_m)[None, :, t - (CONV_W - 1):])

    (x1s, q_s, cmp_halves, _, sel_st, win_st, _, _, _, _, qk_s, v_s, o_s, small_s, gates_s) = dense_in(
        x_sample.reshape(ns, D_MODEL), ns)
    cmp_s = jnp.concatenate([cmp_halves[0], cmp_halves[1]], axis=1)
    sel_s, win_s = sel_st[0].T, win_st[0].T
    q3 = q_s.reshape(ns, N_HEADS_A, LANES)
    n16s = past // CMP_STRIDE
    n_sel_s = -(-(past + 1) // SEL_BLOCK)
    n_sel_pad = -(-n_sel_s // LANES) * LANES
    cover_s = jnp.asarray(np.pad(_cover_np(n16s - 1, n_sel_s), ((0, 1), (0, n_sel_pad - n_sel_s))), BF16)
    bias_c = t_cmp[:, :n16s // LANES, :].reshape(N_HEADS_A, n16s)
    o_c, idx = _sample_cmp(page_table, to_chan(cache_cmp_kv[0]), w_big, b_big, g_cmp, q3, bias_c, cover_s, n_sel_s,
                           past)
    n_past_blocks = past // SEL_BLOCK
    two_lanes = lambda a: jnp.concatenate([a[:, :SEL_BLOCK], a[:, :SEL_BLOCK]], axis=-1)
    bsel = jnp.stack([two_lanes(t_sel[:, n_past_blocks - 2, :]), two_lanes(t_sel[:, n_past_blocks - 1, :])])
    bwin = t_win[:, :wb // LANES, :].reshape(N_HEADS_A, wb)
    b0 = jnp.broadcast_to(t_win[:, wb // LANES, 0:1], (N_HEADS_A, LANES))
    gate_b = jnp.broadcast_to(gates_s[:, :n_gate, None], (ns, n_gate, LANES))
    a_s, win_out = _sample_attn(idx[:, :, 0].reshape(-1), page_table, to_chan(cache_sel_kv[0]),
                                to_chan(cache_win_kv[0]), q3, sel_s[:, None, :], win_s[:, None, :], win_st[0],
                                o_c, gate_b, bsel, bwin, b0)
    m0 = jnp.pad(state_m[0], ((0, 0), (0, LANES - N_HEADS_M)))[:, None, :]
    m_s, c_s, n_s, m_new = _mlstm_step(qk_s[:, None, :], state_conv[0], v_s[:, None, :], o_s[:, None, :],
                                       small_s[:, None, :], state_C[0], state_n[0], m0, conv_w[0], conv_b2, bif, mn)
    y_sample = _post(x1s, a_s.reshape(ns, -1), m_s.reshape(ns, -1), woa, wom, *w2)
    conv_new = jnp.concatenate([state_conv[0][:, 1:], qk_s[:, None, :]], axis=1)
    sample_states = (kv6(cmp_s, ns), kv6(sel_s, ns), from_chan(win_out), c_s[None], n_s[None],
                     m_new[None, :, 0, :N_HEADS_M], conv_new[None])

    return (y_prompt.reshape(nb, t, D_MODEL), y_sample.reshape(ns, 1, D_MODEL)) + prompt_states + sample_states
```

```python
import functools
import math

import jax
import jax.numpy as jnp
import numpy as np
from jax import lax
from jax.experimental import pallas as pl
from jax.experimental.pallas import tpu as pltpu

F32 = jnp.float32
BF16 = jnp.bfloat16

D_MODEL = 1024
PAGE_SIZE = 128
N_HEADS_A = 8
HEAD_DIM_A = 64
N_KV_A = 2
GROUP_R = N_HEADS_A // N_KV_A
D_A = N_HEADS_A * HEAD_DIM_A
KV_W = N_KV_A * HEAD_DIM_A
CMP_STRIDE = 16
CMP_LEN = 2 * CMP_STRIDE
SEL_BLOCK = 64
TOP_K_BLOCKS = 16
WINDOW = 512
N_HEADS_M = 4
HEAD_DIM_M = 128
D_M = N_HEADS_M * HEAD_DIM_M
CONV_W = 4
N_BUCKETS = 32
MAX_DISTANCE = 128
EPS = 1e-6

LANES = 128
Q_TILE = 128
KV_ROW = 2 * KV_W
CHUNK_ROW = CMP_STRIDE * KV_ROW
NEG = -1e30
VMEM_LIMIT = 56 * 1024 * 1024

NT_DIMS = (((1,), (1,)), ((), ()))


def _dot(a, b):
    return jnp.dot(a, b, preferred_element_type=F32)


def _dot_nt(a, b):
    return lax.dot_general(a, b, NT_DIMS, preferred_element_type=F32)


def _split3(x):
    x1 = x.astype(BF16)
    r1 = x - x1.astype(F32)
    x2 = r1.astype(BF16)
    x3 = (r1 - x2.astype(F32)).astype(BF16)
    return x1, x2, x3


def _dot_exact_rhs(x, m):
    x1, x2, x3 = _split3(x)
    return _dot(x1, m) + _dot(x2, m) + _dot(x3, m)


def _dot_exact_lhs(m, x):
    x1, x2, x3 = _split3(x)
    return _dot(m, x1) + _dot(m, x2) + _dot(m, x3)


def _rms_rows(x, g):
    ms = jnp.mean(x * x, axis=-1, keepdims=True)
    return x * lax.rsqrt(ms + EPS) * g


def _rms_two_groups(k, g):
    sq = k * k
    lane = lax.broadcasted_iota(jnp.int32, sq.shape, 1)
    lo = lane < HEAD_DIM_A
    s0 = jnp.sum(jnp.where(lo, sq, 0.0), axis=-1, keepdims=True)
    s1 = jnp.sum(jnp.where(lo, 0.0, sq), axis=-1, keepdims=True)
    ms = jnp.where(lo, s0, s1) * (1.0 / HEAD_DIM_A)
    return k * lax.rsqrt(ms + EPS) * g


def _sigmoid(x):
    return jax.nn.sigmoid(x)


def _log_sigmoid(x):
    return jnp.minimum(x, 0.0) - jnp.log1p(jnp.exp(-jnp.abs(x)))


def _const_spec(shape):
    nd = len(shape)
    return pl.BlockSpec(shape, lambda *_: (0,) * nd, pipeline_mode=pl.Buffered(1))


def _params(*sem):
    return pltpu.CompilerParams(dimension_semantics=sem, vmem_limit_bytes=VMEM_LIMIT)


def _swiglu_residual(x, g_ref, wg_ref, wu_ref, wd_ref):
    xn = _rms_rows(x, g_ref[...]).astype(BF16)
    d_ff = wg_ref.shape[1]
    n_split = 2 if d_ff % (2 * LANES) == 0 else 1
    step = d_ff // n_split
    acc = jnp.zeros_like(x)
    for c in range(n_split):
        hg = _dot(xn, wg_ref[:, c * step:(c + 1) * step])
        hu = _dot(xn, wu_ref[:, c * step:(c + 1) * step])
        h = (hg * _sigmoid(hg)) * hu
        acc = acc + _dot(h.astype(BF16), wd_ref[c * step:(c + 1) * step, :])
    return x + 0.5 * acc


def _ffn_kernel(x_ref, g_ref, wg_ref, wu_ref, wd_ref, y_ref):
    y_ref[...] = _swiglu_residual(x_ref[...], g_ref, wg_ref, wu_ref, wd_ref)


def _post_kernel(x_ref, a_ref, m_ref, woa_ref, wom_ref, g_ref, wg_ref, wu_ref, wd_ref, y_ref):
    x = x_ref[...] + (_dot(a_ref[...], woa_ref[...]) + _dot(m_ref[...], wom_ref[...]))
    y_ref[...] = _swiglu_residual(x, g_ref, wg_ref, wu_ref, wd_ref)


def _token_tile(n):
    return 512 if n % 512 == 0 else n


def _ffn(x, g, wg, wu, wd):
    n = x.shape[0]
    tm = _token_tile(n)
    row = lambda w: pl.BlockSpec((tm, w), lambda i: (i, 0))
    return pl.pallas_call(
        _ffn_kernel,
        grid=(n // tm,),
        in_specs=[row(D_MODEL), _const_spec(g.shape), _const_spec(wg.shape), _const_spec(wu.shape),
                  _const_spec(wd.shape)],
        out_specs=row(D_MODEL),
        out_shape=jax.ShapeDtypeStruct((n, D_MODEL), F32),
        compiler_params=_params("parallel"),
        name="ffn1",
    )(x, g, wg, wu, wd)


def _post(x, a_pad, m_out, woa, wom, g, wg, wu, wd):
    n = x.shape[0]
    tm = _token_tile(n)
    row = lambda w: pl.BlockSpec((tm, w), lambda i: (i, 0))
    return pl.pallas_call(
        _post_kernel,
        grid=(n // tm,),
        in_specs=[row(D_MODEL), row(a_pad.shape[1]), row(m_out.shape[1]), _const_spec(woa.shape),
                  _const_spec(wom.shape), _const_spec(g.shape), _const_spec(wg.shape), _const_spec(wu.shape),
                  _const_spec(wd.shape)],
        out_specs=row(D_MODEL),
        out_shape=jax.ShapeDtypeStruct((n, D_MODEL), F32),
        compiler_params=_params("parallel"),
        name="post_mix_ffn2",
    )(x, a_pad, m_out, woa, wom, g, wg, wu, wd)


def _rms_two_groups_t(k, g_col):
    sq = k * k
    half = k.shape[0] // 2
    s0 = jnp.sum(sq[0:half], axis=0, keepdims=True)
    s1 = jnp.sum(sq[half:], axis=0, keepdims=True)
    ms = jnp.concatenate([jnp.broadcast_to(s0, (half, k.shape[1])), jnp.broadcast_to(s1, (half, k.shape[1]))],
                         axis=0) * (1.0 / HEAD_DIM_A)
    return k * lax.rsqrt(ms + EPS) * g_col


def _inproj_kernel(x_ref, g_ref, wqt_ref, wc_ref, wkvt_ref, wm_ref, ws_ref, wst_ref, qg_ref, kgt_ref,
                   q_out, cmp_rows, cmpt_out, selt_out, wint_out, kst, vst, kwt, vwt, qk_out, v_out, o_out,
                   small_out, gate_out):
    xn = _rms_rows(x_ref[...], g_ref[...]).astype(BF16)
    qt = _dot_nt(wqt_ref[...], xn)
    for h in range(N_HEADS_A):
        qh = qt[h * LANES:(h + 1) * LANES, :]
        ms = jnp.sum(qh * qh, axis=0, keepdims=True) * (1.0 / HEAD_DIM_A)
        q_out[0, h * LANES:(h + 1) * LANES, :] = (qh * lax.rsqrt(ms + EPS) * qg_ref[...]).astype(BF16)
    c = _dot(xn, wc_ref[...])
    cmp_rows[0] = c[:, 0:KV_W]
    cmp_rows[1] = c[:, KV_W:KV_ROW]
    kvt = _dot_nt(wkvt_ref[...], xn)
    cmpt_out[0] = kvt[0:KV_ROW]
    ks = _rms_two_groups_t(kvt[2 * KV_W:3 * KV_W], kgt_ref[0])
    vs = kvt[3 * KV_W:4 * KV_W]
    kw = _rms_two_groups_t(kvt[4 * KV_W:5 * KV_W], kgt_ref[1])
    vw = kvt[5 * KV_W:6 * KV_W]
    selt_out[0, 0:KV_W] = ks
    selt_out[0, KV_W:KV_ROW] = vs
    wint_out[0, 0:KV_W] = kw
    wint_out[0, KV_W:KV_ROW] = vw
    ones = jnp.ones((HEAD_DIM_A, LANES), F32)
    for j in range(kst.shape[1]):
        cols = slice(j * LANES, (j + 1) * LANES)
        kst[0, j] = ks[:, cols].T.astype(BF16)
        kwt[0, j] = kw[:, cols].T.astype(BF16)
        for v, out in ((vs, vst), (vw, vwt)):
            out[0, j, 0] = jnp.concatenate([v[0:HEAD_DIM_A, cols], ones], axis=0).astype(BF16)
            out[0, j, 1] = jnp.concatenate([ones, v[HEAD_DIM_A:KV_W, cols]], axis=0).astype(BF16)
    m = _dot(xn, wm_ref[...])
    qk_out[...] = m[:, 0:2 * D_M]
    v_out[...] = m[:, 2 * D_M:3 * D_M]
    o_out[...] = m[:, 3 * D_M:4 * D_M]
    small_out[...] = _dot(xn, ws_ref[...])
    gate_out[0] = _sigmoid(_dot_nt(wst_ref[...], xn))


def _inproj(x, seq_len, g, wqt, wc, wkvt, wm, ws, wst, qg, kgt):
    n = x.shape[0]
    nb = n // seq_len
    tm = _token_tile(seq_len)
    tpb = seq_len // tm
    row = lambda w: pl.BlockSpec((tm, w), lambda i: (i, 0))
    rows_out = lambda w, dt: (row(w), jax.ShapeDtypeStruct((n, w), dt))
    chan_out = lambda c, dt: (pl.BlockSpec((1, c, tm), lambda i: (i // tpb, 0, i % tpb)),
                              jax.ShapeDtypeStruct((nb, c, seq_len), dt))
    k_tiles = (pl.BlockSpec((1, tm // LANES, LANES, KV_W), lambda i: (i // tpb, i % tpb, 0, 0)),
               jax.ShapeDtypeStruct((nb, seq_len // LANES, LANES, KV_W), BF16))
    v_tiles = (pl.BlockSpec((1, tm // LANES, N_KV_A, KV_W, LANES), lambda i: (i // tpb, i % tpb, 0, 0, 0)),
               jax.ShapeDtypeStruct((nb, seq_len // LANES, N_KV_A, KV_W, LANES), BF16))
    halves_out = (pl.BlockSpec((2, tm, KV_W), lambda i: (0, i, 0)), jax.ShapeDtypeStruct((2, n, KV_W), F32))
    outs = [chan_out(N_HEADS_A * LANES, BF16), halves_out, chan_out(KV_ROW, F32), chan_out(KV_ROW, F32),
            chan_out(KV_ROW, F32), k_tiles, v_tiles, k_tiles, v_tiles,
            rows_out(2 * D_M, F32), rows_out(D_M, F32), rows_out(D_M, F32), rows_out(LANES, F32),
            chan_out(LANES, F32)]
    consts = (g, wqt, wc, wkvt, wm, ws, wst, qg, kgt)
    return pl.pallas_call(
        _inproj_kernel,
        grid=(n // tm,),
        in_specs=[row(D_MODEL)] + [_const_spec(a.shape) for a in consts],
        out_specs=[o[0] for o in outs],
        out_shape=[o[1] for o in outs],
        compiler_params=_params("parallel"),
        name="in_proj",
    )(x, *consts)


def _t5_bucket_np(dist):
    n = np.maximum(dist, 0)
    exact = N_BUCKETS // 2
    nf = np.maximum(n, 1).astype(np.float32)
    ratio = np.log(nf / np.float32(exact)) / np.float32(math.log(MAX_DISTANCE / exact))
    large = exact + (ratio * np.float32(N_BUCKETS - exact)).astype(np.int32)
    return np.where(n < exact, n, np.minimum(large, N_BUCKETS - 1)).astype(np.int32)


def _bias_kernel(tbl_ref, bkt_ref, out_ref):
    bkt = bkt_ref[0]
    for h in range(N_HEADS_A):
        far = tbl_ref[N_BUCKETS - 1, h]
        acc = jnp.zeros(bkt.shape, F32)
        for b in range(N_BUCKETS - 1):
            acc = jnp.where(bkt == b, tbl_ref[b, h] - far, acc)
        out_ref[0, h] = acc


def _bias_tiles(rel_bias, buckets):
    n = buckets.shape[0]
    return pl.pallas_call(
        _bias_kernel,
        grid=(n,),
        in_specs=[pl.BlockSpec(memory_space=pltpu.SMEM),
                  pl.BlockSpec((1, LANES, LANES), lambda i: (i, 0, 0))],
        out_specs=pl.BlockSpec((1, N_HEADS_A, LANES, LANES), lambda i: (i, 0, 0, 0)),
        out_shape=jax.ShapeDtypeStruct((n, N_HEADS_A, LANES, LANES), F32),
        compiler_params=_params("parallel"),
        name="bias_tiles",
    )(rel_bias, buckets)


def _compress_rows(x_ref, nc, w_ref, b_ref, g_ref):
    fs = None
    for t in range(CMP_STRIDE):
        rows = pl.ds(t, nc, stride=CMP_STRIDE)
        x_t = jnp.concatenate([x_ref[0, rows, :], x_ref[1, rows, :]], axis=1)
        part = _dot(x_t.astype(BF16), w_ref[t])
        fs = part if fs is None else fs + part
    first = fs[:, 0:KV_ROW]
    nxt = pltpu.roll(fs[:, KV_ROW:2 * KV_ROW], nc - 1, 0)
    blk = first + nxt + b_ref[...]
    kc = _rms_two_groups(blk[:, 0:KV_W], g_ref[...])
    vc = blk[:, KV_W:KV_ROW]
    complete = lax.broadcasted_iota(jnp.int32, kc.shape, 0) < nc - 1
    return jnp.where(complete, kc, 0.0), jnp.where(complete, vc, 0.0)


def _compress_kernel(x_ref, w_ref, b_ref, g_ref, kc_ref, vc_ref):
    nc = kc_ref.shape[1]
    kc, vc = _compress_rows(x_ref, nc, w_ref, b_ref, g_ref)
    kc_ref[0] = kc.astype(BF16)
    vc_ref[0] = vc.T.astype(BF16)


def _compress(cmp_rows, t, w_big, b_big, g_k):
    nb = cmp_rows.shape[1] // t
    nc = t // CMP_STRIDE
    return pl.pallas_call(
        _compress_kernel,
        grid=(nb,),
        in_specs=[pl.BlockSpec((2, t, KV_W), lambda b: (0, b, 0)), _const_spec(w_big.shape),
                  _const_spec(b_big.shape), _const_spec(g_k.shape)],
        out_specs=[pl.BlockSpec((1, nc, KV_W), lambda b: (b, 0, 0)),
                   pl.BlockSpec((1, KV_W, nc), lambda b: (b, 0, 0))],
        out_shape=[jax.ShapeDtypeStruct((nb, nc, KV_W), BF16), jax.ShapeDtypeStruct((nb, KV_W, nc), BF16)],
        compiler_params=_params("parallel"),
        name="compress_prompt",
    )(cmp_rows, w_big, b_big, g_k)


def _tile_heads(x):
    return jnp.concatenate([x] * GROUP_R, axis=1)


def _online_update(carry, s, vt):
    m, acc = carry
    m_new = jnp.maximum(m, jnp.max(s, axis=0, keepdims=True))
    alpha = jnp.exp(m - m_new)
    p = jnp.exp(s - m_new)
    return m_new, alpha * acc + _dot(vt, p.astype(BF16))


def _key_rows(ref, t, n):
    return ref[0, pl.ds(t, n)].reshape(n * LANES, ref.shape[-1])


def _value_cols(ref, t, n, g):
    return jnp.concatenate([ref[0, t + k, g] for k in range(n)], axis=1)


def _rank_rows(score):
    n_blk = score.shape[0]
    blk = lax.broadcasted_iota(jnp.int32, score.shape, 0)
    rank = jnp.zeros(score.shape, F32)
    for b in range(n_blk):
        row = score[b:b + 1, :]
        rank = rank + jnp.where(blk > b, jnp.where(row >= score, 1.0, 0.0), jnp.where(row > score, 1.0, 0.0))
    return rank


def _nsa_prompt_kernel(q_ref, gate_ref, kc_ref, vct_ref, ks_ref, vs_ref, kw_ref, vw_ref, sel_add_ref, win_add_ref,
                       bcmp_ref, covt_ref, expt_ref, a_ref, mb_ref, *, n_cmp):
    i = pl.program_id(1)
    n_tiles = mb_ref.shape[1]
    n_sel = covt_ref.shape[0]
    q0 = i * Q_TILE
    wide = GROUP_R * Q_TILE
    gate_t = gate_ref[0]
    chan = lax.broadcasted_iota(jnp.int32, (LANES, wide), 0)
    groups = range(N_KV_A)
    heads_of = lambda g: range(g * GROUP_R, (g + 1) * GROUP_R)
    qg_t = [jnp.concatenate([q_ref[0, h * LANES:(h + 1) * LANES, :] for h in heads_of(g)], axis=1) for g in groups]

    oc_t = []
    for g in groups:

        ncp = kc_ref.shape[1]
        j_sub = lax.broadcasted_iota(jnp.int32, (ncp, wide), 0)
        q_lane = q0 + (lax.broadcasted_iota(jnp.int32, (ncp, wide), 1) & (Q_TILE - 1))
        usable = (j_sub * CMP_STRIDE + (CMP_LEN - 1) <= q_lane) & (j_sub < n_cmp)
        s = _dot(kc_ref[0], qg_t[g]) + bcmp_ref[0, g] + jnp.where(usable, 0.0, NEG)
        any_usable = jnp.where(q_lane[0:1, :] >= CMP_LEN - 1, 1.0, 0.0)
        e = jnp.exp(s - jnp.max(s, axis=0, keepdims=True)) * any_usable
        p_c = e / jnp.maximum(jnp.sum(e, axis=0, keepdims=True), 1e-30)
        oc_t.append(_dot(vct_ref[0], p_c.astype(BF16)))

        p_sum = p_c[:, 0:Q_TILE]
        for r in range(1, GROUP_R):
            p_sum = p_sum + p_c[:, r * Q_TILE:(r + 1) * Q_TILE]
        imp_t = _dot_exact_lhs(covt_ref[...], p_sum)
        blk = lax.broadcasted_iota(jnp.int32, (n_sel, Q_TILE), 0)
        qpos = q0 + lax.broadcasted_iota(jnp.int32, (n_sel, Q_TILE), 1)
        cur = qpos // SEL_BLOCK
        valid = blk * SEL_BLOCK <= qpos
        forced = valid & ((blk == 0) | (blk == cur) | (blk == cur - 1))
        score = jnp.where(forced, -NEG, jnp.where(valid, imp_t, NEG))
        sel_t = jnp.where(_rank_rows(score) < float(min(TOP_K_BLOCKS, n_sel)), 1.0, 0.0).astype(BF16)
        mask_all = _dot(expt_ref[...], sel_t)
        for t in range(n_tiles):
            mb_ref[g, t] = (mask_all[t * LANES:(t + 1) * LANES, :] - 1.0) * (-NEG)

    init = (jnp.full((1, wide), NEG, F32), jnp.zeros((LANES, wide), F32))

    pair_row = lax.broadcasted_iota(jnp.int32, (2 * LANES, Q_TILE), 0)

    def far_body(p, carry):
        t = 2 * p
        second = jnp.where(pair_row >= LANES, jnp.where(t + 1 <= i - 2, 0.0, NEG), 0.0)
        k2 = _key_rows(ks_ref, t, 2)
        out = []
        for g in groups:
            add = _tile_heads(mb_ref[g, pl.ds(t, 2)].reshape(2 * LANES, Q_TILE) + second)
            out.append(_online_update(carry[g], _dot(k2, qg_t[g]) + add, _value_cols(vs_ref, t, 2, g)))
        return tuple(out)

    carry = lax.fori_loop(0, i // 2, far_body, (init, init))

    t1 = jnp.maximum(i - 1, 0)
    k_near = _key_rows(ks_ref, t1, 2)
    near_off = pl.multiple_of(jnp.where(i == 0, LANES, 0), LANES)
    n_win = WINDOW // LANES + 1
    t0 = jnp.maximum(i - (n_win - 1), 0)
    k_win = _key_rows(kw_ref, t0, n_win)
    win_off = pl.multiple_of(jnp.maximum(n_win - 1 - i, 0) * LANES, LANES)
    for g in groups:
        denom = (1 - g) * HEAD_DIM_A
        add = sel_add_ref[g, pl.ds(near_off, 2 * LANES), :] + _tile_heads(
            mb_ref[g, pl.ds(t1, 2)].reshape(2 * LANES, Q_TILE))
        _, acc = _online_update(carry[g], _dot(k_near, qg_t[g]) + add, _value_cols(vs_ref, t1, 2, g))
        os_t = acc / acc[denom:denom + 1, :]

        s = _dot(k_win, qg_t[g]) + win_add_ref[g, pl.ds(win_off, n_win * LANES), :]
        p = jnp.exp(s - jnp.max(s, axis=0, keepdims=True))
        acc = _dot(_value_cols(vw_ref, t0, n_win, g), p.astype(BF16))
        ow_t = acc / acc[denom:denom + 1, :]

        heads = heads_of(g)
        gate_row = lambda br: jnp.concatenate(
            [gate_t[br * N_HEADS_A + h:br * N_HEADS_A + h + 1, :] for h in heads], axis=1)
        out_t = gate_row(0) * oc_t[g] + gate_row(1) * os_t + gate_row(2) * ow_t
        own_rows = (chan >= g * HEAD_DIM_A) & (chan < (g + 1) * HEAD_DIM_A)
        out_t = jnp.where(own_rows, out_t, 0.0)
        for r, h in enumerate(heads):
            a_ref[0, :, h * LANES:(h + 1) * LANES] = out_t[:, r * Q_TILE:(r + 1) * Q_TILE].T.astype(BF16)


def _nsa_prompt(q_t, gate_t, kc, vc_t, ks, vs, kw, vw, sel_add, win_add, bcmp, cov_t, expand_t, n_cmp):
    nb, _, t = q_t.shape
    nqb = t // Q_TILE
    assert nqb > WINDOW // LANES
    seq = lambda a: pl.BlockSpec((1,) + a.shape[1:], lambda b, i: (b,) + (0,) * (a.ndim - 1))
    return pl.pallas_call(
        functools.partial(_nsa_prompt_kernel, n_cmp=n_cmp),
        grid=(nb, nqb),
        in_specs=[pl.BlockSpec((1, N_HEADS_A * LANES, Q_TILE), lambda b, i: (b, 0, i)),
                  pl.BlockSpec((1, LANES, Q_TILE), lambda b, i: (b, 0, i)),
                  seq(kc), seq(vc_t), seq(ks), seq(vs), seq(kw), seq(vw),
                  _const_spec(sel_add.shape), _const_spec(win_add.shape),
                  pl.BlockSpec((1,) + bcmp.shape[1:], lambda b, i: (i, 0, 0, 0)),
                  _const_spec(cov_t.shape), _const_spec(expand_t.shape)],
        out_specs=pl.BlockSpec((1, Q_TILE, N_HEADS_A * LANES), lambda b, i: (b, i, 0)),
        out_shape=jax.ShapeDtypeStruct((nb, t, N_HEADS_A * LANES), BF16),
        scratch_shapes=[pltpu.VMEM((N_KV_A, t // LANES, LANES, Q_TILE), F32)],
        compiler_params=_params("parallel", "arbitrary"),
        name="nsa_prompt",
    )(q_t, gate_t, kc, vc_t, ks, vs, kw, vw, sel_add, win_add, bcmp, cov_t, expand_t)


def _head_norm_gate(h_all, mn_ref, o):
    outs = []
    for h in range(N_HEADS_M):
        hs = slice(h * HEAD_DIM_M, (h + 1) * HEAD_DIM_M)
        outs.append(_rms_rows(h_all[h], mn_ref[:, hs]))
    return _sigmoid(o) * jnp.concatenate(outs, axis=-1)


def _mlstm_prompt_kernel(qk_ref, v_ref, o_ref, small_ref, cw_ref, cb_ref, bif_ref, mn_ref, tri_ref, triu_ref,
                         y_ref, c_ref, n_ref, m_ref, xbuf):
    chunk = qk_ref.shape[1]
    pad = 8

    @pl.when(pl.program_id(1) == 0)
    def _():
        xbuf[0:pad, :] = jnp.zeros((pad, 2 * D_M), F32)
        c_ref[...] = jnp.zeros_like(c_ref)
        n_ref[...] = jnp.zeros_like(n_ref)
        m_ref[...] = jnp.zeros_like(m_ref)

    x = qk_ref[0]
    xbuf[pad:pad + chunk, :] = x
    y = xbuf[pad - 3:pad - 3 + chunk, :] * cw_ref[0:1, :]
    y = y + xbuf[pad - 2:pad - 2 + chunk, :] * cw_ref[1:2, :]
    y = y + xbuf[pad - 1:pad - 1 + chunk, :] * cw_ref[2:3, :]
    y = y + x * cw_ref[3:4, :] + cb_ref[...]
    qkc = y * _sigmoid(y)
    xbuf[0:pad, :] = xbuf[chunk:chunk + pad, :]

    gi = small_ref[0] + bif_ref[...]
    gi_t = gi.T
    b_col = _dot_exact_lhs(tri_ref[...], _log_sigmoid(gi))
    b_row = _dot_exact_rhs(_log_sigmoid(gi_t), triu_ref[...])
    t_col = lax.broadcasted_iota(jnp.int32, (chunk, chunk), 0)
    s_row = lax.broadcasted_iota(jnp.int32, (chunk, chunk), 1)
    causal = s_row <= t_col
    m_all = m_ref[0]
    lane_row = lax.broadcasted_iota(jnp.int32, m_all.shape, 1)

    h_all = []
    for h in range(N_HEADS_M):
        hs = slice(h * HEAD_DIM_M, (h + 1) * HEAD_DIM_M)
        ks = slice(D_M + h * HEAD_DIM_M, D_M + (h + 1) * HEAD_DIM_M)
        q = qkc[:, hs]
        k = qkc[:, ks] * F32(HEAD_DIM_M ** -0.5)
        v = v_ref[0, :, hs]
        qb, kb = q.astype(BF16), k.astype(BF16)
        ci, fi = 3 * N_HEADS_A + h, 3 * N_HEADS_A + N_HEADS_M + h
        bt = b_col[:, fi:fi + 1]
        bs = b_row[fi:fi + 1, :]
        ig_row = gi_t[ci:ci + 1, :]
        ig_col = gi[:, ci:ci + 1]
        m_prev = m_all[:, h:h + 1]
        c_prev = c_ref[0, h]
        n_prev = n_ref[0, h:h + 1, :]

        dlog = jnp.where(causal, bt - bs + ig_row, NEG)
        inter = bt + m_prev
        m_t = jnp.maximum(inter, jnp.max(dlog, axis=-1, keepdims=True))
        s_qk = _dot_nt(qb, kb) * jnp.exp(dlog - m_t)
        dec = jnp.exp(inter - m_t)
        num = _dot(s_qk.astype(BF16), v.astype(BF16)) + dec * _dot_nt(qb, c_prev.astype(BF16))
        den = jnp.sum(s_qk, axis=-1, keepdims=True) + dec * jnp.sum(q * n_prev, axis=-1, keepdims=True)
        h_all.append(num / jnp.maximum(jnp.abs(den), jnp.exp(-m_t)))

        m_new = m_t[chunk - 1:chunk, :]
        b_end = bt[chunk - 1:chunk, :]
        w_end = jnp.exp(b_end - bt + ig_col - m_new)
        dec_end = jnp.exp(b_end + m_prev - m_new)
        wv_t = (w_end * v).T.astype(BF16)
        c_ref[0, h] = dec_end * c_prev + _dot(wv_t, kb)
        n_ref[0, h:h + 1, :] = dec_end * n_prev + jnp.sum(w_end * k, axis=0, keepdims=True)
        m_all = jnp.where(lane_row == h, m_new, m_all)

    m_ref[0] = m_all
    y_ref[0] = _head_norm_gate(h_all, mn_ref, o_ref[0]).astype(BF16)


def _mlstm_prompt(qk, v, o, small, conv_w, conv_b, bif, mn_g, chunk):
    nb, t, _ = qk.shape
    tri = jnp.asarray(np.tril(np.ones((chunk, chunk), np.float32)), BF16)
    triu = jnp.asarray(np.triu(np.ones((chunk, chunk), np.float32)), BF16)
    tok = lambda w: pl.BlockSpec((1, chunk, w), lambda b, c: (b, c, 0))
    return pl.pallas_call(
        _mlstm_prompt_kernel,
        grid=(nb, t // chunk),
        in_specs=[tok(2 * D_M), tok(D_M), tok(D_M), tok(LANES)]
        + [_const_spec(a.shape) for a in (conv_w, conv_b, bif, mn_g, tri, triu)],
        out_specs=[tok(D_M),
                   pl.BlockSpec((1, N_HEADS_M, HEAD_DIM_M, HEAD_DIM_M), lambda b, c: (b, 0, 0, 0)),
                   pl.BlockSpec((1, N_HEADS_M, HEAD_DIM_M), lambda b, c: (b, 0, 0)),
                   pl.BlockSpec((1, 1, LANES), lambda b, c: (b, 0, 0))],
        out_shape=[jax.ShapeDtypeStruct((nb, t, D_M), BF16),
                   jax.ShapeDtypeStruct((nb, N_HEADS_M, HEAD_DIM_M, HEAD_DIM_M), F32),
                   jax.ShapeDtypeStruct((nb, N_HEADS_M, HEAD_DIM_M), F32),
                   jax.ShapeDtypeStruct((nb, 1, LANES), F32)],
        scratch_shapes=[pltpu.VMEM((chunk + 8, 2 * D_M), F32)],
        compiler_params=_params("parallel", "arbitrary"),
        name="mlstm_prompt",
    )(qk, v, o, small, conv_w, conv_b, bif, mn_g, tri, triu)


def _softmax_rows(s):
    e = jnp.exp(s - jnp.max(s, axis=-1, keepdims=True))
    return e, jnp.sum(e, axis=-1, keepdims=True)


def _sample_cmp_kernel(pt_ref, *refs, n_pages, n_sel, qpos):
    pages = refs[:n_pages]
    w_ref, b_ref, g_ref, q_ref, bias_ref, cov_ref, oc_ref, idx_ref, xs_ref = refs[n_pages:]
    for u in range(n_pages):
        for half in range(2):
            xs_ref[half, u * PAGE_SIZE:(u + 1) * PAGE_SIZE, :] = pages[u][0, half * KV_W:(half + 1) * KV_W, :].T
    nc = n_pages * PAGE_SIZE // CMP_STRIDE
    kc, vc = _compress_rows(xs_ref, nc, w_ref, b_ref, g_ref)
    q = q_ref[0]
    j_row = lax.broadcasted_iota(jnp.int32, (N_HEADS_A, nc), 1)
    s = _dot_nt(q, kc.astype(BF16)) + bias_ref[...] + jnp.where(j_row < nc - 1, 0.0, NEG)
    e, l = _softmax_rows(s)
    p = e / jnp.maximum(l, 1e-30)
    oc_ref[0] = _dot(p.astype(BF16), vc.astype(BF16))

    nsp = cov_ref.shape[1]
    blk_row = lax.broadcasted_iota(jnp.int32, (1, nsp), 1)
    sub = lax.broadcasted_iota(jnp.int32, (nsp, nsp), 0)
    lan = lax.broadcasted_iota(jnp.int32, (nsp, nsp), 1)
    cur = qpos // SEL_BLOCK
    valid = (blk_row * SEL_BLOCK <= qpos) & (blk_row < n_sel)
    forced = valid & ((blk_row == 0) | (blk_row == cur) | (blk_row == cur - 1))
    k_col = lax.broadcasted_iota(jnp.int32, (TOP_K_BLOCKS, nsp), 0).astype(F32)
    blk_f = lax.broadcasted_iota(jnp.int32, (TOP_K_BLOCKS, nsp), 1).astype(F32)
    for g in range(N_KV_A):
        p_sum = jnp.sum(p[g * GROUP_R:(g + 1) * GROUP_R, :], axis=0, keepdims=True)
        imp = _dot_exact_rhs(jnp.broadcast_to(p_sum, (8, nc)), cov_ref[...])[0:1, :]
        score = jnp.where(forced, -NEG, jnp.where(valid, imp, NEG))
        score_col = jnp.sum(jnp.where(sub == lan, score, 0.0), axis=-1, keepdims=True)
        ahead = jnp.where(sub < lan, jnp.where(score_col >= score, 1.0, 0.0), jnp.where(score_col > score, 1.0, 0.0))
        rank = jnp.sum(ahead, axis=0, keepdims=True)
        chosen = jnp.sum(jnp.where(rank == k_col, blk_f, 0.0), axis=-1, keepdims=True)
        idx_ref[0, g * TOP_K_BLOCKS:(g + 1) * TOP_K_BLOCKS, :] = jnp.broadcast_to(
            chosen, (TOP_K_BLOCKS, LANES)).astype(jnp.int32)


def _sample_cmp(page_table, pool, w_big, b_big, g_k, q3, bias_c, cover, n_sel, qpos):
    ns, n_pages = page_table.shape
    page_spec = lambda u: pl.BlockSpec((1, KV_ROW, PAGE_SIZE), lambda s, pt: (pt[s, u], 0, 0))
    const = lambda a: pl.BlockSpec(a.shape, lambda s, pt: (0,) * a.ndim, pipeline_mode=pl.Buffered(1))
    grid_spec = pltpu.PrefetchScalarGridSpec(
        num_scalar_prefetch=1,
        grid=(ns,),
        in_specs=[page_spec(u) for u in range(n_pages)]
        + [const(w_big), const(b_big), const(g_k),
           pl.BlockSpec((1, N_HEADS_A, LANES), lambda s, pt: (s, 0, 0)), const(bias_c), const(cover)],
        out_specs=[pl.BlockSpec((1, N_HEADS_A, LANES), lambda s, pt: (s, 0, 0)),
                   pl.BlockSpec((1, N_KV_A * TOP_K_BLOCKS, LANES), lambda s, pt: (s, 0, 0))],
        scratch_shapes=[pltpu.VMEM((2, n_pages * PAGE_SIZE, KV_W), F32)],
    )
    return pl.pallas_call(
        functools.partial(_sample_cmp_kernel, n_pages=n_pages, n_sel=n_sel, qpos=qpos),
        grid_spec=grid_spec,
        out_shape=[jax.ShapeDtypeStruct((ns, N_HEADS_A, LANES), F32),
                   jax.ShapeDtypeStruct((ns, N_KV_A * TOP_K_BLOCKS, LANES), jnp.int32)],
        compiler_params=_params("arbitrary"),
        name="sample_cmp_topk",
    )(page_table, *([pool] * n_pages), w_big, b_big, g_k, q3, bias_c, cover)


def _sample_attn_kernel(idx_ref, pt_ref, *refs, n_past_blocks):
    n_slots = N_KV_A * TOP_K_BLOCKS
    blocks = refs[:n_slots]
    (win_ref, q_ref, selnew_ref, winnew_ref, wint_ref, oc_ref, gate_ref, bsel_ref, bwin_ref, b0_ref,
     a_ref, winout_ref) = refs[n_slots:]
    s_id = pl.program_id(0)
    q = q_ref[0]
    qf = q.astype(F32)
    lane = lax.broadcasted_iota(jnp.int32, (N_HEADS_A, LANES), 1)
    row = lax.broadcasted_iota(jnp.int32, (N_HEADS_A, LANES), 0)
    lo = lane < SEL_BLOCK

    def with_new_key(s, vt_mat, new_row):
        s_new = jnp.sum(qf * new_row[:, 0:KV_W], axis=-1, keepdims=True) + b0_ref[:, 0:1]
        m = jnp.maximum(jnp.max(s, axis=-1, keepdims=True), s_new)
        p = jnp.exp(s - m)
        p_new = jnp.exp(s_new - m)
        l = jnp.sum(p, axis=-1, keepdims=True) + p_new
        return (_dot_nt(p.astype(BF16), vt_mat) + p_new * new_row[:, KV_W:KV_ROW]) / l

    sel_new = selnew_ref[0]
    o_s = []
    for g in range(N_KV_A):
        slots = range(g * TOP_K_BLOCKS, (g + 1) * TOP_K_BLOCKS)
        kt_cat = jnp.concatenate([blocks[k][0, 0:KV_W, :] for k in slots], axis=1).astype(BF16)
        vt_cat = jnp.concatenate([blocks[k][0, KV_W:KV_ROW, :] for k in slots], axis=1).astype(BF16)
        adds = []
        for k in slots:
            b = idx_ref[s_id * n_slots + k]
            add = jnp.where(b == n_past_blocks - 2, bsel_ref[0], jnp.where(b == n_past_blocks - 1, bsel_ref[1], 0.0))
            other_half = jnp.where(b % 2 == 0, jnp.where(lo, 0.0, NEG), jnp.where(lo, NEG, 0.0))
            adds.append(add + other_half + jnp.where(b >= n_past_blocks, NEG, 0.0))
        o_s.append(with_new_key(_dot(q, kt_cat) + jnp.concatenate(adds, axis=-1), vt_cat, sel_new))
    o_sel = jnp.where(row < GROUP_R, o_s[0], o_s[1])

    win = win_ref[0]
    wb = win.shape[1]
    w_lane = lax.broadcasted_iota(jnp.int32, (N_HEADS_A, wb), 1)
    s_w = _dot(q, win[0:KV_W, :].astype(BF16)) + bwin_ref[...] + jnp.where(w_lane == 0, NEG, 0.0)
    o_win = with_new_key(s_w, win[KV_W:KV_ROW, :].astype(BF16), winnew_ref[0])

    na = N_HEADS_A
    out = gate_ref[0, 0:na] * oc_ref[0] + gate_ref[0, na:2 * na] * o_sel + gate_ref[0, 2 * na:3 * na] * o_win
    a_ref[0] = jnp.where(lo == (row < GROUP_R), out, 0.0).astype(BF16)

    wint = wint_ref[...]
    seq_lane = lax.broadcasted_iota(jnp.int32, wint.shape, 1)
    new_col = jnp.sum(jnp.where(seq_lane == s_id, wint, 0.0), axis=-1, keepdims=True)
    buf_lane = lax.broadcasted_iota(jnp.int32, win.shape, 1)
    winout_ref[0] = jnp.where(buf_lane == wb - 1, new_col, pltpu.roll(win, wb - 1, 1))


def _sample_attn(idx, page_table, pool, win_t, q3, sel_new, win_new, win_new_t, o_c, gate_b, bsel, bwin, b0):
    ns, n_pages = page_table.shape
    n_slots = N_KV_A * TOP_K_BLOCKS
    per_page = PAGE_SIZE // SEL_BLOCK
    n_past_blocks = n_pages * per_page
    wb = win_t.shape[2]

    def slot_spec(k):
        def index(s, idx_ref, pt_ref):
            b = jnp.minimum(idx_ref[s * n_slots + k], n_past_blocks - 1)
            return (pt_ref[s, b // per_page], 0, 0)
        return pl.BlockSpec((1, KV_ROW, PAGE_SIZE), index)

    per_seq = lambda a: pl.BlockSpec((1,) + a.shape[1:], lambda s, *_: (s,) + (0,) * (a.ndim - 1))
    const = lambda a: pl.BlockSpec(a.shape, lambda s, *_: (0,) * a.ndim, pipeline_mode=pl.Buffered(1))
    grid_spec = pltpu.PrefetchScalarGridSpec(
        num_scalar_prefetch=2,
        grid=(ns,),
        in_specs=[slot_spec(k) for k in range(n_slots)]
        + [per_seq(win_t), per_seq(q3), per_seq(sel_new), per_seq(win_new), const(win_new_t), per_seq(o_c),
           per_seq(gate_b), const(bsel), const(bwin), const(b0)],
        out_specs=[pl.BlockSpec((1, N_HEADS_A, LANES), lambda s, *_: (s, 0, 0)),
                   pl.BlockSpec((1, KV_ROW, wb), lambda s, *_: (s, 0, 0))],
    )
    return pl.pallas_call(
        functools.partial(_sample_attn_kernel, n_past_blocks=n_past_blocks),
        grid_spec=grid_spec,
        out_shape=[jax.ShapeDtypeStruct((ns, N_HEADS_A, LANES), BF16),
                   jax.ShapeDtypeStruct((ns, KV_ROW, wb), F32)],
        compiler_params=_params("arbitrary"),
        name="sample_sel_win",
    )(idx, page_table, *([pool] * n_slots), win_t, q3, sel_new, win_new, win_new_t, o_c, gate_b, bsel, bwin, b0)


def _mlstm_step_kernel(qk_ref, conv_ref, v_ref, o_ref, small_ref, c_ref, n_ref, m_ref, cw_ref, cb_ref, bif_ref,
                       mn_ref, y_ref, c_out, n_out, m_out):
    seqs = qk_ref.shape[0]
    sub = lax.broadcasted_iota(jnp.int32, (HEAD_DIM_M, HEAD_DIM_M), 0)
    lan = lax.broadcasted_iota(jnp.int32, (HEAD_DIM_M, HEAD_DIM_M), 1)
    eye = sub == lan
    lane_row = lax.broadcasted_iota(jnp.int32, (1, LANES), 1)
    for s in range(seqs):
        hist = conv_ref[s]
        y = hist[0:1, :] * cw_ref[0:1, :]
        y = y + hist[1:2, :] * cw_ref[1:2, :]
        y = y + hist[2:3, :] * cw_ref[2:3, :]
        y = y + qk_ref[s] * cw_ref[3:4, :] + cb_ref[...]
        qkc = y * _sigmoid(y)
        gi = small_ref[s] + bif_ref[...]
        lf_all = _log_sigmoid(gi)
        m_all = m_ref[s]
        m_new_all = m_all
        h_all = []
        for h in range(N_HEADS_M):
            hs = slice(h * HEAD_DIM_M, (h + 1) * HEAD_DIM_M)
            ks = slice(D_M + h * HEAD_DIM_M, D_M + (h + 1) * HEAD_DIM_M)
            q = qkc[:, hs]
            k = qkc[:, ks] * F32(HEAD_DIM_M ** -0.5)
            v = v_ref[s][:, hs]
            ci, fi = 3 * N_HEADS_A + h, 3 * N_HEADS_A + N_HEADS_M + h
            ig = gi[:, ci:ci + 1]
            lf = lf_all[:, fi:fi + 1]
            m_prev = m_all[:, h:h + 1]
            c_prev = c_ref[s, h]
            n_prev = n_ref[s, h:h + 1, :]
            inter = lf + m_prev
            m_t = jnp.maximum(inter, ig)
            w_in = jnp.exp(ig - m_t)
            dec = jnp.exp(inter - m_t)
            s_qk = jnp.sum(q * k, axis=-1, keepdims=True) * w_in
            cq = _dot_nt(jnp.broadcast_to(q, (8, HEAD_DIM_M)).astype(BF16), c_prev.astype(BF16))[0:1, :]
            num = s_qk * v + dec * cq
            den = s_qk + dec * jnp.sum(n_prev * q, axis=-1, keepdims=True)
            h_all.append(num / jnp.maximum(jnp.abs(den), jnp.exp(-m_t)))
            v_col = jnp.sum(jnp.where(eye, v, 0.0), axis=-1, keepdims=True)
            c_out[s, h] = dec * c_prev + (w_in * v_col) * k
            n_out[s, h:h + 1, :] = dec * n_prev + w_in * k
            m_new_all = jnp.where(lane_row == h, m_t, m_new_all)
        m_out[s] = m_new_all
        y_ref[s] = _head_norm_gate(h_all, mn_ref, o_ref[s]).astype(BF16)


def _mlstm_step(qk, conv0, v, o, small, c0, n0, m0, conv_w, conv_b, bif, mn_g):
    ns = qk.shape[0]
    sb = 8 if ns % 8 == 0 else 1
    per = lambda a: pl.BlockSpec((sb,) + a.shape[1:], lambda i: (i,) + (0,) * (a.ndim - 1))
    ins = (qk, conv0, v, o, small, c0, n0, m0)
    outs = [jax.ShapeDtypeStruct((ns, 1, D_M), BF16), jax.ShapeDtypeStruct(c0.shape, F32),
            jax.ShapeDtypeStruct(n0.shape, F32), jax.ShapeDtypeStruct(m0.shape, F32)]
    return pl.pallas_call(
        _mlstm_step_kernel,
        grid=(ns // sb,),
        in_specs=[per(a) for a in ins] + [_const_spec(a.shape) for a in (conv_w, conv_b, bif, mn_g)],
        out_specs=[per(a) for a in outs],
        out_shape=outs,
        compiler_params=_params("parallel"),
        name="mlstm_step",
    )(*ins, conv_w, conv_b, bif, mn_g)


def _cover_np(n_cmp, n_sel):
    cs = np.arange(n_cmp)[:, None] * CMP_STRIDE
    bs = np.arange(n_sel)[None, :] * SEL_BLOCK
    shared = np.clip(np.minimum(cs + CMP_LEN, bs + SEL_BLOCK) - np.maximum(cs, bs), 0, None)
    return (shared / CMP_LEN).astype(np.float32)


def _prep_in_proj(w_in):
    d = w_in.shape[0]
    wq = w_in[:, :D_A].reshape(d, N_HEADS_A, HEAD_DIM_A)
    z = jnp.zeros_like(wq)
    g0 = jnp.concatenate([wq[:, :GROUP_R], z[:, :GROUP_R]], axis=-1)
    g1 = jnp.concatenate([z[:, GROUP_R:], wq[:, GROUP_R:]], axis=-1)
    wq_pad = jnp.concatenate([g0, g1], axis=1).reshape(d, N_HEADS_A * LANES)
    o_kv = D_A
    o_gate = o_kv + 6 * KV_W
    o_m = o_gate + 3 * N_HEADS_A
    o_if = o_m + 4 * D_M
    w_cmp = w_in[:, o_kv:o_kv + KV_ROW]
    w_kv_t = w_in[:, o_kv:o_gate].T
    w_m = w_in[:, o_m:o_if]
    w_small = jnp.concatenate([w_in[:, o_gate:o_m], w_in[:, o_if:],
                               jnp.zeros((d, LANES - 3 * N_HEADS_A - 2 * N_HEADS_M), w_in.dtype)], axis=1)
    return [w.astype(BF16) for w in (wq_pad.T, w_cmp, w_kv_t, w_m, w_small, w_small.T)]


def _prep_compress(w_k, b_k, w_v, b_v):
    per_c = jnp.stack([w_k, w_k, w_v, w_v]).reshape(4, 2, CMP_STRIDE, HEAD_DIM_A, HEAD_DIM_A)
    w_big = jnp.einsum('chtde,cf->tcdhfe', per_c, jnp.eye(4, dtype=w_k.dtype))
    w_big = w_big.reshape(CMP_STRIDE, KV_ROW, 2 * KV_ROW).astype(BF16)
    b_big = jnp.concatenate([b_k, b_k, b_v, b_v]).reshape(1, KV_ROW)
    return w_big, b_big


def _prep_out_proj(w_out):
    wa = w_out[:D_A].reshape(N_HEADS_A, HEAD_DIM_A, -1)
    z = jnp.zeros_like(wa)
    g0 = jnp.concatenate([wa[:GROUP_R], z[:GROUP_R]], axis=1)
    g1 = jnp.concatenate([z[GROUP_R:], wa[GROUP_R:]], axis=1)
    woa = jnp.concatenate([g0, g1], axis=0).reshape(N_HEADS_A * LANES, -1)
    return woa.astype(BF16), w_out[D_A:].astype(BF16)


def _bucket_tiles(t, past, wb):
    r = np.arange(LANES)
    tiles = [_t5_bucket_np(d * LANES + r[None, :] - r[:, None]) for d in range(2)]
    for i in range(t // Q_TILE):
        tiles.append(_t5_bucket_np(i * Q_TILE + r[None, :] - (CMP_STRIDE * r[:, None] + CMP_LEN - 1)))
    flat = np.arange(LANES * LANES).reshape(LANES, LANES)
    tiles.append(_t5_bucket_np(past - (CMP_STRIDE * flat + CMP_LEN - 1)))
    tiles.append(_t5_bucket_np(wb - flat))
    tiles.append(_t5_bucket_np(past - (r[:, None] * SEL_BLOCK + r[None, :] % SEL_BLOCK)))
    return np.stack(tiles).astype(np.int32)


def _near_tables(bnear):
    r = np.arange(LANES)
    tile4 = lambda a: jnp.asarray(np.tile(a, (1, GROUP_R)), F32)
    causal = tile4(np.where(r[:, None] <= r[None, :], 0.0, NEG))
    oldest = tile4(np.where(r[:, None] > r[None, :], 0.0, NEG))
    zero = jnp.zeros_like(causal)
    masked = jnp.full_like(causal, NEG)
    n_mid = WINDOW // LANES - 2
    sel_add, win_add = [], []
    for g in range(N_KV_A):
        near = [bnear[1, g], bnear[0, g] + causal]
        sel_add.append(jnp.concatenate(near + [masked], axis=0))
        win_add.append(jnp.concatenate([oldest] + [zero] * n_mid + near + [masked] * (WINDOW // LANES), axis=0))
    return jnp.stack(sel_add), jnp.stack(win_add)


def kernel(x_prompt, x_sample, cache_cmp_kv, cache_sel_kv, cache_win_kv, state_C, state_n, state_m, state_conv, page_table, rel_bias, g_ffn1, w1_gate, w1_up, w1_down, g_mix, w_in, qn_g, kn_cmp_g, kn_sel_g, kn_win_g, w_cmp_k, b_cmp_k, w_cmp_v, b_cmp_v, conv_w, conv_b, b_if, mn_g, w_out, g_ffn2, w2_gate, w2_up, w2_down):
    assert x_prompt.shape[2] == D_MODEL and g_ffn1.shape[0] == 1
    nb, t, _ = x_prompt.shape
    ns, ds, _ = x_sample.shape
    assert ds == 1 and t % Q_TILE == 0
    n_pages = page_table.shape[1]
    past = n_pages * PAGE_SIZE
    wb = cache_win_kv.shape[2]
    assert wb == WINDOW
    kv_shape = (2, N_KV_A, HEAD_DIM_A)

    row = lambda a: a.reshape(1, -1)
    two = lambda a: jnp.tile(a, 2).reshape(1, KV_W)
    w1 = (row(g_ffn1[0]), w1_gate[0].astype(BF16), w1_up[0].astype(BF16), w1_down[0].astype(BF16))
    w2 = (row(g_ffn2[0]), w2_gate[0].astype(BF16), w2_up[0].astype(BF16), w2_down[0].astype(BF16))
    wqt, wc, wkvt, wm, ws, wst = _prep_in_proj(w_in[0])
    qg = (jnp.tile(qn_g[0], 2) * F32(HEAD_DIM_A ** -0.5))[:, None]
    kgt = jnp.stack([jnp.tile(kn_sel_g[0], 2), jnp.tile(kn_win_g[0], 2)])[:, :, None]
    w_big, b_big = _prep_compress(w_cmp_k[0], b_cmp_k[0], w_cmp_v[0], b_cmp_v[0])
    g_cmp = two(kn_cmp_g[0])
    woa, wom = _prep_out_proj(w_out[0])
    n_gate = 3 * N_HEADS_A
    bif = jnp.zeros((1, LANES), F32).at[0, n_gate:n_gate + 2 * N_HEADS_M].set(b_if[0])
    conv_b2 = row(conv_b[0])
    mn = row(mn_g[0])

    nqb = t // Q_TILE
    tiles = _bias_tiles(rel_bias, jnp.asarray(_bucket_tiles(t, past, wb)))
    group_lanes = lambda a: a.reshape(a.shape[0], N_KV_A, GROUP_R, LANES, LANES).transpose(0, 1, 3, 2, 4).reshape(
        a.shape[0], N_KV_A, LANES, GROUP_R * LANES)
    bnear = group_lanes(tiles[0:2])
    sel_add, win_add = _near_tables(bnear)
    bcmp = group_lanes(tiles[2:2 + nqb])
    t_cmp, t_win, t_sel = tiles[2 + nqb], tiles[3 + nqb], tiles[4 + nqb]

    def dense_in(x, seq_len):
        x1 = _ffn(x, *w1)
        return (x1,) + tuple(_inproj(x1, seq_len, row(g_mix[0]), wqt, wc, wkvt, wm, ws, wst, qg, kgt))

    from_chan = lambda a: a.reshape((1, a.shape[0]) + kv_shape + (a.shape[2],)).transpose(0, 1, 5, 2, 3, 4)
    to_chan = lambda a: a.transpose(0, 2, 3, 4, 1).reshape(a.shape[0], KV_ROW, a.shape[1])

    (x1, q_t, cmp_rows, cmp_t, sel_t, win_t, ks, vs, kw, vw, qk_m, v_m, o_m, small, gate_t) = dense_in(
        x_prompt.reshape(nb * t, D_MODEL), t)
    seq = lambda a: a.reshape(nb, t, a.shape[-1])
    n16 = t // CMP_STRIDE
    assert n16 == LANES
    kc, vc_t = _compress(cmp_rows, t, w_big, b_big, g_cmp)
    n_sel = -(-t // SEL_BLOCK)
    cov_t = jnp.asarray(np.pad(_cover_np(n16 - 1, n_sel), ((0, 1), (0, 0))).T, BF16)
    expand_t = jnp.asarray(np.repeat(np.eye(n_sel, dtype=np.float32), SEL_BLOCK, axis=0)[:t], BF16)
    a_pad = _nsa_prompt(q_t, gate_t, kc, vc_t, ks, vs, kw, vw, sel_add, win_add, bcmp, cov_t, expand_t, n16 - 1)
    chunk = 256 if t % 256 == 0 else Q_TILE
    m_out, c_p, n_p, m_p = _mlstm_prompt(seq(qk_m), seq(v_m), seq(o_m), seq(small), conv_w[0], conv_b2, bif, mn,
                                         chunk)
    y_prompt = _post(x1, a_pad.reshape(nb * t, -1), m_out.reshape(nb * t, -1), woa, wom, *w2)
    kv6 = lambda a, n: a.reshape((1, n, -1) + kv_shape)
    prompt_states = (from_chan(cmp_t), from_chan(sel_t), from_chan(win_t[:, :, t - wb:]), c_p[None], n_p[None],
                     m_p[None, :, 0, :N_HEADS_M], seq(qk_m)[None, :, t - (CONV_W - 1):])

    (x1s, q_st, cmp_halves, _, sel_st, win_st, _, _, _, _, qk_s, v_s, o_s, small_s, gate_st) = dense_in(
        x_sample.reshape(ns, D_MODEL), ns)
    cmp_s = jnp.concatenate([cmp_halves[0], cmp_halves[1]], axis=1)
    sel_s, win_s, gates_s = sel_st[0].T, win_st[0].T, gate_st[0].T
    q3 = q_st[0].T.reshape(ns, N_HEADS_A, LANES)
    n16s = past // CMP_STRIDE
    n_sel_s = -(-(past + 1) // SEL_BLOCK)
    n_sel_pad = -(-n_sel_s // LANES) * LANES
    cover_s = jnp.asarray(np.pad(_cover_np(n16s - 1, n_sel_s), ((0, 1), (0, n_sel_pad - n_sel_s))), BF16)
    bias_c = t_cmp[:, :n16s // LANES, :].reshape(N_HEADS_A, n16s)
    o_c, idx = _sample_cmp(page_table, to_chan(cache_cmp_kv[0]), w_big, b_big, g_cmp, q3, bias_c, cover_s, n_sel_s,
                           past)
    n_past_blocks = past // SEL_BLOCK
    two_lanes = lambda a: jnp.concatenate([a[:, :SEL_BLOCK], a[:, :SEL_BLOCK]], axis=-1)
    bsel = jnp.stack([two_lanes(t_sel[:, n_past_blocks - 2, :]), two_lanes(t_sel[:, n_past_blocks - 1, :])])
    bwin = t_win[:, :wb // LANES, :].reshape(N_HEADS_A, wb)
    b0 = jnp.broadcast_to(t_win[:, wb // LANES, 0:1], (N_HEADS_A, LANES))
    gate_b = jnp.broadcast_to(gates_s[:, :n_gate, None], (ns, n_gate, LANES))
    a_s, win_out = _sample_attn(idx[:, :, 0].reshape(-1), page_table, to_chan(cache_sel_kv[0]),
                                to_chan(cache_win_kv[0]), q3, sel_s[:, None, :], win_s[:, None, :], win_st[0],
                                o_c, gate_b, bsel, bwin, b0)
    m0 = jnp.pad(state_m[0], ((0, 0), (0, LANES - N_HEADS_M)))[:, None, :]
    m_s, c_s, n_s, m_new = _mlstm_step(qk_s[:, None, :], state_conv[0], v_s[:, None, :], o_s[:, None, :],
                                       small_s[:, None, :], state_C[0], state_n[0], m0, conv_w[0], conv_b2, bif, mn)
    y_sample = _post(x1s, a_s.reshape(ns, -1), m_s.reshape(ns, -1), woa, wom, *w2)
    conv_new = jnp.concatenate([state_conv[0][:, 1:], qk_s[:, None, :]], axis=1)
    sample_states = (kv6(cmp_s, ns), kv6(sel_s, ns), from_chan(win_out), c_s[None], n_s[None],
                     m_new[None, :, 0, :N_HEADS_M], conv_new[None])

    return (y_prompt.reshape(nb, t, D_MODEL), y_sample.reshape(ns, 1, D_MODEL)) + prompt_states + sample_states
```

```python
import functools
import math

import jax
import jax.numpy as jnp
import numpy as np
from jax import lax
from jax.experimental import pallas as pl
from jax.experimental.pallas import tpu as pltpu

F32 = jnp.float32
BF16 = jnp.bfloat16

D_MODEL = 1024
PAGE_SIZE = 128
N_HEADS_A = 8
HEAD_DIM_A = 64
N_KV_A = 2
GROUP_R = N_HEADS_A // N_KV_A
D_A = N_HEADS_A * HEAD_DIM_A
KV_W = N_KV_A * HEAD_DIM_A
CMP_STRIDE = 16
CMP_LEN = 2 * CMP_STRIDE
SEL_BLOCK = 64
TOP_K_BLOCKS = 16
WINDOW = 512
N_HEADS_M = 4
HEAD_DIM_M = 128
D_M = N_HEADS_M * HEAD_DIM_M
CONV_W = 4
N_BUCKETS = 32
MAX_DISTANCE = 128
EPS = 1e-6

LANES = 128
Q_TILE = 128
KV_ROW = 2 * KV_W
CHUNK_ROW = CMP_STRIDE * KV_ROW
NEG = -1e30
VMEM_LIMIT = 56 * 1024 * 1024

NT_DIMS = (((1,), (1,)), ((), ()))


def _dot(a, b):
    return jnp.dot(a, b, preferred_element_type=F32)


def _dot_nt(a, b):
    return lax.dot_general(a, b, NT_DIMS, preferred_element_type=F32)


def _split3(x):
    x1 = x.astype(BF16)
    r1 = x - x1.astype(F32)
    x2 = r1.astype(BF16)
    x3 = (r1 - x2.astype(F32)).astype(BF16)
    return x1, x2, x3


def _dot_exact_rhs(x, m):
    x1, x2, x3 = _split3(x)
    return _dot(x1, m) + _dot(x2, m) + _dot(x3, m)


def _dot_exact_lhs(m, x):
    x1, x2, x3 = _split3(x)
    return _dot(m, x1) + _dot(m, x2) + _dot(m, x3)


def _rms_rows(x, g):
    ms = jnp.mean(x * x, axis=-1, keepdims=True)
    return x * lax.rsqrt(ms + EPS) * g


def _rms_two_groups(k, g):
    sq = k * k
    lane = lax.broadcasted_iota(jnp.int32, sq.shape, 1)
    lo = lane < HEAD_DIM_A
    s0 = jnp.sum(jnp.where(lo, sq, 0.0), axis=-1, keepdims=True)
    s1 = jnp.sum(jnp.where(lo, 0.0, sq), axis=-1, keepdims=True)
    ms = jnp.where(lo, s0, s1) * (1.0 / HEAD_DIM_A)
    return k * lax.rsqrt(ms + EPS) * g


def _sigmoid(x):
    return jax.nn.sigmoid(x)


def _log_sigmoid(x):
    return jnp.minimum(x, 0.0) - jnp.log1p(jnp.exp(-jnp.abs(x)))


def _const_spec(shape):
    nd = len(shape)
    return pl.BlockSpec(shape, lambda *_: (0,) * nd, pipeline_mode=pl.Buffered(1))


def _params(*sem):
    return pltpu.CompilerParams(dimension_semantics=sem, vmem_limit_bytes=VMEM_LIMIT)


def _swiglu_residual(x, g_ref, wg_ref, wu_ref, wd_ref):
    xn = _rms_rows(x, g_ref[...]).astype(BF16)
    d_ff = wg_ref.shape[1]
    n_split = 2 if d_ff % (2 * LANES) == 0 else 1
    step = d_ff // n_split
    acc = jnp.zeros_like(x)
    for c in range(n_split):
        hg = _dot(xn, wg_ref[:, c * step:(c + 1) * step])
        hu = _dot(xn, wu_ref[:, c * step:(c + 1) * step])
        h = (hg * _sigmoid(hg)) * hu
        acc = acc + _dot(h.astype(BF16), wd_ref[c * step:(c + 1) * step, :])
    return x + 0.5 * acc


def _ffn_kernel(x_ref, g_ref, wg_ref, wu_ref, wd_ref, y_ref):
    y_ref[...] = _swiglu_residual(x_ref[...], g_ref, wg_ref, wu_ref, wd_ref)


def _post_kernel(x_ref, a_ref, m_ref, woa_ref, wom_ref, g_ref, wg_ref, wu_ref, wd_ref, y_ref):
    x = x_ref[...] + (_dot(a_ref[...], woa_ref[...]) + _dot(m_ref[...], wom_ref[...]))
    y_ref[...] = _swiglu_residual(x, g_ref, wg_ref, wu_ref, wd_ref)


def _token_tile(n):
    return 512 if n % 512 == 0 else n


def _ffn(x, g, wg, wu, wd):
    n = x.shape[0]
    tm = _token_tile(n)
    row = lambda w: pl.BlockSpec((tm, w), lambda i: (i, 0))
    return pl.pallas_call(
        _ffn_kernel,
        grid=(n // tm,),
        in_specs=[row(D_MODEL), _const_spec(g.shape), _const_spec(wg.shape), _const_spec(wu.shape),
                  _const_spec(wd.shape)],
        out_specs=row(D_MODEL),
        out_shape=jax.ShapeDtypeStruct((n, D_MODEL), F32),
        compiler_params=_params("parallel"),
        name="ffn1",
    )(x, g, wg, wu, wd)


def _post(x, a_pad, m_out, woa, wom, g, wg, wu, wd):
    n = x.shape[0]
    tm = _token_tile(n)
    row = lambda w: pl.BlockSpec((tm, w), lambda i: (i, 0))
    return pl.pallas_call(
        _post_kernel,
        grid=(n // tm,),
        in_specs=[row(D_MODEL), row(a_pad.shape[1]), row(m_out.shape[1]), _const_spec(woa.shape),
                  _const_spec(wom.shape), _const_spec(g.shape), _const_spec(wg.shape), _const_spec(wu.shape),
                  _const_spec(wd.shape)],
        out_specs=row(D_MODEL),
        out_shape=jax.ShapeDtypeStruct((n, D_MODEL), F32),
        compiler_params=_params("parallel"),
        name="post_mix_ffn2",
    )(x, a_pad, m_out, woa, wom, g, wg, wu, wd)


def _rms_two_groups_t(k, g_col):
    sq = k * k
    half = k.shape[0] // 2
    s0 = jnp.sum(sq[0:half], axis=0, keepdims=True)
    s1 = jnp.sum(sq[half:], axis=0, keepdims=True)
    ms = jnp.concatenate([jnp.broadcast_to(s0, (half, k.shape[1])), jnp.broadcast_to(s1, (half, k.shape[1]))],
                         axis=0) * (1.0 / HEAD_DIM_A)
    return k * lax.rsqrt(ms + EPS) * g_col


def _inproj_kernel(x_ref, g_ref, wqt_ref, wc_ref, wkvt_ref, wm_ref, ws_ref, wst_ref, qg_ref, kgt_ref,
                   q_out, cmp_rows, cmpt_out, selt_out, wint_out, kst, vst, kwt, vwt, qk_out, v_out, o_out,
                   small_out, gate_out):
    xn = _rms_rows(x_ref[...], g_ref[...]).astype(BF16)
    qt = _dot_nt(wqt_ref[...], xn)
    for h in range(N_HEADS_A):
        qh = qt[h * LANES:(h + 1) * LANES, :]
        ms = jnp.sum(qh * qh, axis=0, keepdims=True) * (1.0 / HEAD_DIM_A)
        q_out[0, h * LANES:(h + 1) * LANES, :] = (qh * lax.rsqrt(ms + EPS) * qg_ref[...]).astype(BF16)
    c = _dot(xn, wc_ref[...])
    cmp_rows[0] = c[:, 0:KV_W]
    cmp_rows[1] = c[:, KV_W:KV_ROW]
    kvt = _dot_nt(wkvt_ref[...], xn)
    cmpt_out[0] = kvt[0:KV_ROW]
    ks = _rms_two_groups_t(kvt[2 * KV_W:3 * KV_W], kgt_ref[0])
    vs = kvt[3 * KV_W:4 * KV_W]
    kw = _rms_two_groups_t(kvt[4 * KV_W:5 * KV_W], kgt_ref[1])
    vw = kvt[5 * KV_W:6 * KV_W]
    selt_out[0, 0:KV_W] = ks
    selt_out[0, KV_W:KV_ROW] = vs
    wint_out[0, 0:KV_W] = kw
    wint_out[0, KV_W:KV_ROW] = vw
    ones = jnp.ones((HEAD_DIM_A, LANES), F32)
    for j in range(kst.shape[1]):
        cols = slice(j * LANES, (j + 1) * LANES)
        kst[0, j] = ks[:, cols].T.astype(BF16)
        kwt[0, j] = kw[:, cols].T.astype(BF16)
        for v, out in ((vs, vst), (vw, vwt)):
            out[0, j, 0] = jnp.concatenate([v[0:HEAD_DIM_A, cols], ones], axis=0).astype(BF16)
            out[0, j, 1] = jnp.concatenate([ones, v[HEAD_DIM_A:KV_W, cols]], axis=0).astype(BF16)
    m = _dot(xn, wm_ref[...])
    qk_out[...] = m[:, 0:2 * D_M]
    v_out[...] = m[:, 2 * D_M:3 * D_M]
    o_out[...] = m[:, 3 * D_M:4 * D_M]
    small_out[...] = _dot(xn, ws_ref[...])
    gate_out[0] = _sigmoid(_dot_nt(wst_ref[...], xn))


def _inproj(x, seq_len, g, wqt, wc, wkvt, wm, ws, wst, qg, kgt):
    n = x.shape[0]
    nb = n // seq_len
    tm = _token_tile(seq_len)
    tpb = seq_len // tm
    row = lambda w: pl.BlockSpec((tm, w), lambda i: (i, 0))
    rows_out = lambda w, dt: (row(w), jax.ShapeDtypeStruct((n, w), dt))
    chan_out = lambda c, dt: (pl.BlockSpec((1, c, tm), lambda i: (i // tpb, 0, i % tpb)),
                              jax.ShapeDtypeStruct((nb, c, seq_len), dt))
    k_tiles = (pl.BlockSpec((1, tm // LANES, LANES, KV_W), lambda i: (i // tpb, i % tpb, 0, 0)),
               jax.ShapeDtypeStruct((nb, seq_len // LANES, LANES, KV_W), BF16))
    v_tiles = (pl.BlockSpec((1, tm // LANES, N_KV_A, KV_W, LANES), lambda i: (i // tpb, i % tpb, 0, 0, 0)),
               jax.ShapeDtypeStruct((nb, seq_len // LANES, N_KV_A, KV_W, LANES), BF16))
    halves_out = (pl.BlockSpec((2, tm, KV_W), lambda i: (0, i, 0)), jax.ShapeDtypeStruct((2, n, KV_W), F32))
    outs = [chan_out(N_HEADS_A * LANES, BF16), halves_out, chan_out(KV_ROW, F32), chan_out(KV_ROW, F32),
            chan_out(KV_ROW, F32), k_tiles, v_tiles, k_tiles, v_tiles,
            rows_out(2 * D_M, F32), rows_out(D_M, F32), rows_out(D_M, F32), rows_out(LANES, F32),
            chan_out(LANES, F32)]
    consts = (g, wqt, wc, wkvt, wm, ws, wst, qg, kgt)
    return pl.pallas_call(
        _inproj_kernel,
        grid=(n // tm,),
        in_specs=[row(D_MODEL)] + [_const_spec(a.shape) for a in consts],
        out_specs=[o[0] for o in outs],
        out_shape=[o[1] for o in outs],
        compiler_params=_params("parallel"),
        name="in_proj",
    )(x, *consts)


def _t5_bucket_np(dist):
    n = np.maximum(dist, 0)
    exact = N_BUCKETS // 2
    nf = np.maximum(n, 1).astype(np.float32)
    ratio = np.log(nf / np.float32(exact)) / np.float32(math.log(MAX_DISTANCE / exact))
    large = exact + (ratio * np.float32(N_BUCKETS - exact)).astype(np.int32)
    return np.where(n < exact, n, np.minimum(large, N_BUCKETS - 1)).astype(np.int32)


def _bias_kernel(tbl_ref, bkt_ref, out_ref):
    bkt = bkt_ref[0]
    for h in range(N_HEADS_A):
        far = tbl_ref[N_BUCKETS - 1, h]
        acc = jnp.zeros(bkt.shape, F32)
        for b in range(N_BUCKETS - 1):
            acc = jnp.where(bkt == b, tbl_ref[b, h] - far, acc)
        out_ref[0, h] = acc


def _bias_tiles(rel_bias, buckets):
    n = buckets.shape[0]
    return pl.pallas_call(
        _bias_kernel,
        grid=(n,),
        in_specs=[pl.BlockSpec(memory_space=pltpu.SMEM),
                  pl.BlockSpec((1, LANES, LANES), lambda i: (i, 0, 0))],
        out_specs=pl.BlockSpec((1, N_HEADS_A, LANES, LANES), lambda i: (i, 0, 0, 0)),
        out_shape=jax.ShapeDtypeStruct((n, N_HEADS_A, LANES, LANES), F32),
        compiler_params=_params("parallel"),
        name="bias_tiles",
    )(rel_bias, buckets)


def _compress_half(rows_ref, nc, w_ref, half, bias):
    fs = None
    for t in range(CMP_STRIDE):
        part = _dot(rows_ref[pl.ds(t, nc, stride=CMP_STRIDE), :].astype(BF16), w_ref[half, t])
        fs = part if fs is None else fs + part
    return fs[:, 0:KV_W] + pltpu.roll(fs[:, KV_W:KV_ROW], nc - 1, 0) + bias


def _compress_rows(k_rows_ref, v_rows_ref, nc, w_ref, b_ref, g_ref):
    kc = _rms_two_groups(_compress_half(k_rows_ref, nc, w_ref, 0, b_ref[:, 0:KV_W]), g_ref[...])
    vc = _compress_half(v_rows_ref, nc, w_ref, 1, b_ref[:, KV_W:KV_ROW])
    complete = lax.broadcasted_iota(jnp.int32, kc.shape, 0) < nc - 1
    return jnp.where(complete, kc, 0.0), jnp.where(complete, vc, 0.0)


def _compress_kernel(x_ref, w_ref, b_ref, g_ref, kc_ref, vc_ref):
    nc = kc_ref.shape[1]
    kc, vc = _compress_rows(x_ref.at[0], x_ref.at[1], nc, w_ref, b_ref, g_ref)
    kc_ref[0] = kc.astype(BF16)
    vc_ref[0] = vc.T.astype(BF16)


def _compress(cmp_rows, t, w_big, b_big, g_k):
    nb = cmp_rows.shape[1] // t
    nc = t // CMP_STRIDE
    return pl.pallas_call(
        _compress_kernel,
        grid=(nb,),
        in_specs=[pl.BlockSpec((2, t, KV_W), lambda b: (0, b, 0)), _const_spec(w_big.shape),
                  _const_spec(b_big.shape), _const_spec(g_k.shape)],
        out_specs=[pl.BlockSpec((1, nc, KV_W), lambda b: (b, 0, 0)),
                   pl.BlockSpec((1, KV_W, nc), lambda b: (b, 0, 0))],
        out_shape=[jax.ShapeDtypeStruct((nb, nc, KV_W), BF16), jax.ShapeDtypeStruct((nb, KV_W, nc), BF16)],
        compiler_params=_params("parallel"),
        name="compress_prompt",
    )(cmp_rows, w_big, b_big, g_k)


def _tile_heads(x):
    return jnp.concatenate([x] * GROUP_R, axis=1)


def _softmax_part(m, s):
    m_new = jnp.maximum(m, jnp.max(s, axis=0, keepdims=True))
    return m_new, jnp.exp(m - m_new), jnp.exp(s - m_new).astype(BF16)


def _accumulate(acc, soft, vt):
    m_new, alpha, p = soft
    return m_new, alpha * acc + _dot(vt, p)


def _key_rows(ref, t, n):
    return ref[0, pl.ds(t, n)].reshape(n * LANES, ref.shape[-1])


def _value_cols(ref, t, n, g):
    return jnp.concatenate([ref[0, t + k, g] for k in range(n)], axis=1)


def _rank_rows(score):
    n_blk = score.shape[0]
    blk = lax.broadcasted_iota(jnp.int32, score.shape, 0)
    rank = jnp.zeros(score.shape, F32)
    for b in range(n_blk):
        row = score[b:b + 1, :]
        rank = rank + jnp.where(blk > b, jnp.where(row >= score, 1.0, 0.0), jnp.where(row > score, 1.0, 0.0))
    return rank


def _nsa_prompt_kernel(q_ref, gate_ref, kc_ref, vct_ref, ks_ref, vs_ref, kw_ref, vw_ref, sel_add_ref, win_add_ref,
                       bcmp_ref, covt_ref, expt_ref, a_ref, mb_ref, *, n_cmp):
    i = pl.program_id(1)
    n_tiles = mb_ref.shape[1]
    n_sel = covt_ref.shape[0]
    q0 = i * Q_TILE
    wide = GROUP_R * Q_TILE
    gate_t = gate_ref[0]
    chan = lax.broadcasted_iota(jnp.int32, (LANES, wide), 0)
    groups = range(N_KV_A)
    heads_of = lambda g: range(g * GROUP_R, (g + 1) * GROUP_R)
    qg_t = [jnp.concatenate([q_ref[0, h * LANES:(h + 1) * LANES, :] for h in heads_of(g)], axis=1) for g in groups]

    oc_t = []
    for g in groups:

        ncp = kc_ref.shape[1]
        j_sub = lax.broadcasted_iota(jnp.int32, (ncp, wide), 0)
        q_lane = q0 + (lax.broadcasted_iota(jnp.int32, (ncp, wide), 1) & (Q_TILE - 1))
        usable = (j_sub * CMP_STRIDE + (CMP_LEN - 1) <= q_lane) & (j_sub < n_cmp)
        s = _dot(kc_ref[0], qg_t[g]) + bcmp_ref[0, g] + jnp.where(usable, 0.0, NEG)
        any_usable = jnp.where(q_lane[0:1, :] >= CMP_LEN - 1, 1.0, 0.0)
        e = jnp.exp(s - jnp.max(s, axis=0, keepdims=True)) * any_usable
        p_c = e / jnp.maximum(jnp.sum(e, axis=0, keepdims=True), 1e-30)
        oc_t.append(_dot(vct_ref[0], p_c.astype(BF16)))

        p_sum = p_c[:, 0:Q_TILE]
        for r in range(1, GROUP_R):
            p_sum = p_sum + p_c[:, r * Q_TILE:(r + 1) * Q_TILE]
        imp_t = _dot_exact_lhs(covt_ref[...], p_sum)
        blk = lax.broadcasted_iota(jnp.int32, (n_sel, Q_TILE), 0)
        qpos = q0 + lax.broadcasted_iota(jnp.int32, (n_sel, Q_TILE), 1)
        cur = qpos // SEL_BLOCK
        valid = blk * SEL_BLOCK <= qpos
        forced = valid & ((blk == 0) | (blk == cur) | (blk == cur - 1))
        score = jnp.where(forced, -NEG, jnp.where(valid, imp_t, NEG))
        sel_t = jnp.where(_rank_rows(score) < float(min(TOP_K_BLOCKS, n_sel)), 1.0, 0.0).astype(BF16)
        mask_all = _dot(expt_ref[...], sel_t)
        for t in range(n_tiles):
            mb_ref[g, t] = (mask_all[t * LANES:(t + 1) * LANES, :] - 1.0) * (-NEG)

    init = (jnp.full((1, wide), NEG, F32), jnp.zeros((LANES, wide), F32))

    pair_row = lax.broadcasted_iota(jnp.int32, (2 * LANES, Q_TILE), 0)

    def far_body(p, carry):
        t = 2 * p
        second = jnp.where(pair_row >= LANES, jnp.where(t + 1 <= i - 2, 0.0, NEG), 0.0)
        k2 = _key_rows(ks_ref, t, 2)
        s = [_dot(k2, qg_t[g]) + _tile_heads(mb_ref[g, pl.ds(t, 2)].reshape(2 * LANES, Q_TILE) + second)
             for g in groups]
        soft = [_softmax_part(carry[g][0], s[g]) for g in groups]
        return tuple(_accumulate(carry[g][1], soft[g], _value_cols(vs_ref, t, 2, g)) for g in groups)

    carry = lax.fori_loop(0, i // 2, far_body, (init, init))

    t1 = jnp.maximum(i - 1, 0)
    k_near = _key_rows(ks_ref, t1, 2)
    near_off = pl.multiple_of(jnp.where(i == 0, LANES, 0), LANES)
    n_win = WINDOW // LANES + 1
    t0 = jnp.maximum(i - (n_win - 1), 0)
    k_win = _key_rows(kw_ref, t0, n_win)
    win_off = pl.multiple_of(jnp.maximum(n_win - 1 - i, 0) * LANES, LANES)
    s_near = [_dot(k_near, qg_t[g]) + sel_add_ref[g, pl.ds(near_off, 2 * LANES), :] + _tile_heads(
        mb_ref[g, pl.ds(t1, 2)].reshape(2 * LANES, Q_TILE)) for g in groups]
    s_win = [_dot(k_win, qg_t[g]) + win_add_ref[g, pl.ds(win_off, n_win * LANES), :] for g in groups]
    soft_near = [_softmax_part(carry[g][0], s_near[g]) for g in groups]
    soft_win = [_softmax_part(init[0], s_win[g]) for g in groups]
    for g in groups:
        denom = (1 - g) * HEAD_DIM_A
        _, acc = _accumulate(carry[g][1], soft_near[g], _value_cols(vs_ref, t1, 2, g))
        os_t = acc / acc[denom:denom + 1, :]
        acc = _dot(_value_cols(vw_ref, t0, n_win, g), soft_win[g][2])
        ow_t = acc / acc[denom:denom + 1, :]

        heads = heads_of(g)
        gate_row = lambda br: jnp.concatenate(
            [gate_t[br * N_HEADS_A + h:br * N_HEADS_A + h + 1, :] for h in heads], axis=1)
        out_t = gate_row(0) * oc_t[g] + gate_row(1) * os_t + gate_row(2) * ow_t
        own_rows = (chan >= g * HEAD_DIM_A) & (chan < (g + 1) * HEAD_DIM_A)
        out_t = jnp.where(own_rows, out_t, 0.0)
        for r, h in enumerate(heads):
            a_ref[0, :, h * LANES:(h + 1) * LANES] = out_t[:, r * Q_TILE:(r + 1) * Q_TILE].T.astype(BF16)


def _nsa_prompt(q_t, gate_t, kc, vc_t, ks, vs, kw, vw, sel_add, win_add, bcmp, cov_t, expand_t, n_cmp):
    nb, _, t = q_t.shape
    nqb = t // Q_TILE
    assert nqb > WINDOW // LANES
    seq = lambda a: pl.BlockSpec((1,) + a.shape[1:], lambda b, i: (b,) + (0,) * (a.ndim - 1))
    return pl.pallas_call(
        functools.partial(_nsa_prompt_kernel, n_cmp=n_cmp),
        grid=(nb, nqb),
        in_specs=[pl.BlockSpec((1, N_HEADS_A * LANES, Q_TILE), lambda b, i: (b, 0, i)),
                  pl.BlockSpec((1, LANES, Q_TILE), lambda b, i: (b, 0, i)),
                  seq(kc), seq(vc_t), seq(ks), seq(vs), seq(kw), seq(vw),
                  _const_spec(sel_add.shape), _const_spec(win_add.shape),
                  pl.BlockSpec((1,) + bcmp.shape[1:], lambda b, i: (i, 0, 0, 0)),
                  _const_spec(cov_t.shape), _const_spec(expand_t.shape)],
        out_specs=pl.BlockSpec((1, Q_TILE, N_HEADS_A * LANES), lambda b, i: (b, i, 0)),
        out_shape=jax.ShapeDtypeStruct((nb, t, N_HEADS_A * LANES), BF16),
        scratch_shapes=[pltpu.VMEM((N_KV_A, t // LANES, LANES, Q_TILE), F32)],
        compiler_params=_params("parallel", "arbitrary"),
        name="nsa_prompt",
    )(q_t, gate_t, kc, vc_t, ks, vs, kw, vw, sel_add, win_add, bcmp, cov_t, expand_t)


def _head_norm_gate(h_all, mn_ref, o):
    outs = []
    for h in range(N_HEADS_M):
        hs = slice(h * HEAD_DIM_M, (h + 1) * HEAD_DIM_M)
        outs.append(_rms_rows(h_all[h], mn_ref[:, hs]))
    return _sigmoid(o) * jnp.concatenate(outs, axis=-1)


def _mlstm_prompt_kernel(qk_ref, v_ref, o_ref, small_ref, cw_ref, cb_ref, bif_ref, mn_ref, tri_ref, triu_ref,
                         y_ref, c_ref, n_ref, m_ref, xbuf):
    chunk = qk_ref.shape[1]
    pad = 8

    @pl.when(pl.program_id(1) == 0)
    def _():
        xbuf[0:pad, :] = jnp.zeros((pad, 2 * D_M), F32)
        c_ref[...] = jnp.zeros_like(c_ref)
        n_ref[...] = jnp.zeros_like(n_ref)
        m_ref[...] = jnp.zeros_like(m_ref)

    x = qk_ref[0]
    xbuf[pad:pad + chunk, :] = x
    y = xbuf[pad - 3:pad - 3 + chunk, :] * cw_ref[0:1, :]
    y = y + xbuf[pad - 2:pad - 2 + chunk, :] * cw_ref[1:2, :]
    y = y + xbuf[pad - 1:pad - 1 + chunk, :] * cw_ref[2:3, :]
    y = y + x * cw_ref[3:4, :] + cb_ref[...]
    qkc = y * _sigmoid(y)
    xbuf[0:pad, :] = xbuf[chunk:chunk + pad, :]

    gi = small_ref[0] + bif_ref[...]
    gi_t = gi.T
    b_col = _dot_exact_lhs(tri_ref[...], _log_sigmoid(gi))
    b_row = _dot_exact_rhs(_log_sigmoid(gi_t), triu_ref[...])
    t_col = lax.broadcasted_iota(jnp.int32, (chunk, chunk), 0)
    s_row = lax.broadcasted_iota(jnp.int32, (chunk, chunk), 1)
    causal = s_row <= t_col
    m_all = m_ref[0]
    lane_row = lax.broadcasted_iota(jnp.int32, m_all.shape, 1)

    h_all = []
    for h in range(N_HEADS_M):
        hs = slice(h * HEAD_DIM_M, (h + 1) * HEAD_DIM_M)
        ks = slice(D_M + h * HEAD_DIM_M, D_M + (h + 1) * HEAD_DIM_M)
        q = qkc[:, hs]
        k = qkc[:, ks] * F32(HEAD_DIM_M ** -0.5)
        v = v_ref[0, :, hs]
        qb, kb = q.astype(BF16), k.astype(BF16)
        ci, fi = 3 * N_HEADS_A + h, 3 * N_HEADS_A + N_HEADS_M + h
        bt = b_col[:, fi:fi + 1]
        bs = b_row[fi:fi + 1, :]
        ig_row = gi_t[ci:ci + 1, :]
        ig_col = gi[:, ci:ci + 1]
        m_prev = m_all[:, h:h + 1]
        c_prev = c_ref[0, h]
        n_prev = n_ref[0, h:h + 1, :]

        dlog = jnp.where(causal, bt - bs + ig_row, NEG)
        inter = bt + m_prev
        m_t = jnp.maximum(inter, jnp.max(dlog, axis=-1, keepdims=True))
        s_qk = _dot_nt(qb, kb) * jnp.exp(dlog - m_t)
        dec = jnp.exp(inter - m_t)
        num = _dot(s_qk.astype(BF16), v.astype(BF16)) + dec * _dot_nt(qb, c_prev.astype(BF16))
        den = jnp.sum(s_qk, axis=-1, keepdims=True) + dec * jnp.sum(q * n_prev, axis=-1, keepdims=True)
        h_all.append(num / jnp.maximum(jnp.abs(den), jnp.exp(-m_t)))

        m_new = m_t[chunk - 1:chunk, :]
        b_end = bt[chunk - 1:chunk, :]
        w_end = jnp.exp(b_end - bt + ig_col - m_new)
        dec_end = jnp.exp(b_end + m_prev - m_new)
        wv_t = (w_end * v).T.astype(BF16)
        c_ref[0, h] = dec_end * c_prev + _dot(wv_t, kb)
        n_ref[0, h:h + 1, :] = dec_end * n_prev + jnp.sum(w_end * k, axis=0, keepdims=True)
        m_all = jnp.where(lane_row == h, m_new, m_all)

    m_ref[0] = m_all
    y_ref[0] = _head_norm_gate(h_all, mn_ref, o_ref[0]).astype(BF16)


def _mlstm_prompt(qk, v, o, small, conv_w, conv_b, bif, mn_g, chunk):
    nb, t, _ = qk.shape
    tri = jnp.asarray(np.tril(np.ones((chunk, chunk), np.float32)), BF16)
    triu = jnp.asarray(np.triu(np.ones((chunk, chunk), np.float32)), BF16)
    tok = lambda w: pl.BlockSpec((1, chunk, w), lambda b, c: (b, c, 0))
    return pl.pallas_call(
        _mlstm_prompt_kernel,
        grid=(nb, t // chunk),
        in_specs=[tok(2 * D_M), tok(D_M), tok(D_M), tok(LANES)]
        + [_const_spec(a.shape) for a in (conv_w, conv_b, bif, mn_g, tri, triu)],
        out_specs=[tok(D_M),
                   pl.BlockSpec((1, N_HEADS_M, HEAD_DIM_M, HEAD_DIM_M), lambda b, c: (b, 0, 0, 0)),
                   pl.BlockSpec((1, N_HEADS_M, HEAD_DIM_M), lambda b, c: (b, 0, 0)),
                   pl.BlockSpec((1, 1, LANES), lambda b, c: (b, 0, 0))],
        out_shape=[jax.ShapeDtypeStruct((nb, t, D_M), BF16),
                   jax.ShapeDtypeStruct((nb, N_HEADS_M, HEAD_DIM_M, HEAD_DIM_M), F32),
                   jax.ShapeDtypeStruct((nb, N_HEADS_M, HEAD_DIM_M), F32),
                   jax.ShapeDtypeStruct((nb, 1, LANES), F32)],
        scratch_shapes=[pltpu.VMEM((chunk + 8, 2 * D_M), F32)],
        compiler_params=_params("parallel", "arbitrary"),
        name="mlstm_prompt",
    )(qk, v, o, small, conv_w, conv_b, bif, mn_g, tri, triu)


def _softmax_rows(s):
    e = jnp.exp(s - jnp.max(s, axis=-1, keepdims=True))
    return e, jnp.sum(e, axis=-1, keepdims=True)


def _sample_cmp_kernel(pt_ref, *refs, n_pages, n_sel, qpos):
    pages = refs[:n_pages]
    w_ref, b_ref, g_ref, q_ref, bias_ref, cov_ref, oc_ref, idx_ref, xk_ref, xv_ref = refs[n_pages:]
    for half, rows_ref in ((0, xk_ref), (1, xv_ref)):
        for u in range(n_pages):
            rows_ref[u * PAGE_SIZE:(u + 1) * PAGE_SIZE, :] = pages[u][0, half * KV_W:(half + 1) * KV_W, :].T
    nc = n_pages * PAGE_SIZE // CMP_STRIDE
    kc, vc = _compress_rows(xk_ref, xv_ref, nc, w_ref, b_ref, g_ref)
    q = q_ref[0]
    j_row = lax.broadcasted_iota(jnp.int32, (N_HEADS_A, nc), 1)
    s = _dot_nt(q, kc.astype(BF16)) + bias_ref[...] + jnp.where(j_row < nc - 1, 0.0, NEG)
    e, l = _softmax_rows(s)
    p = e / jnp.maximum(l, 1e-30)
    oc_ref[0] = _dot(p.astype(BF16), vc.astype(BF16))

    nsp = cov_ref.shape[1]
    blk_row = lax.broadcasted_iota(jnp.int32, (1, nsp), 1)
    sub = lax.broadcasted_iota(jnp.int32, (nsp, nsp), 0)
    lan = lax.broadcasted_iota(jnp.int32, (nsp, nsp), 1)
    cur = qpos // SEL_BLOCK
    valid = (blk_row * SEL_BLOCK <= qpos) & (blk_row < n_sel)
    forced = valid & ((blk_row == 0) | (blk_row == cur) | (blk_row == cur - 1))
    k_col = lax.broadcasted_iota(jnp.int32, (TOP_K_BLOCKS, nsp), 0).astype(F32)
    blk_f = lax.broadcasted_iota(jnp.int32, (TOP_K_BLOCKS, nsp), 1).astype(F32)
    for g in range(N_KV_A):
        p_sum = jnp.sum(p[g * GROUP_R:(g + 1) * GROUP_R, :], axis=0, keepdims=True)
        imp = _dot_exact_rhs(jnp.broadcast_to(p_sum, (8, nc)), cov_ref[...])[0:1, :]
        score = jnp.where(forced, -NEG, jnp.where(valid, imp, NEG))
        score_col = jnp.sum(jnp.where(sub == lan, score, 0.0), axis=-1, keepdims=True)
        ahead = jnp.where(sub < lan, jnp.where(score_col >= score, 1.0, 0.0), jnp.where(score_col > score, 1.0, 0.0))
        rank = jnp.sum(ahead, axis=0, keepdims=True)
        chosen = jnp.sum(jnp.where(rank == k_col, blk_f, 0.0), axis=-1, keepdims=True)
        idx_ref[0, g * TOP_K_BLOCKS:(g + 1) * TOP_K_BLOCKS, :] = jnp.broadcast_to(
            chosen, (TOP_K_BLOCKS, LANES)).astype(jnp.int32)


def _sample_cmp(page_table, pool, w_big, b_big, g_k, q3, bias_c, cover, n_sel, qpos):
    ns, n_pages = page_table.shape
    page_spec = lambda u: pl.BlockSpec((1, KV_ROW, PAGE_SIZE), lambda s, pt: (pt[s, u], 0, 0))
    const = lambda a: pl.BlockSpec(a.shape, lambda s, pt: (0,) * a.ndim, pipeline_mode=pl.Buffered(1))
    grid_spec = pltpu.PrefetchScalarGridSpec(
        num_scalar_prefetch=1,
        grid=(ns,),
        in_specs=[page_spec(u) for u in range(n_pages)]
        + [const(w_big), const(b_big), const(g_k),
           pl.BlockSpec((1, N_HEADS_A, LANES), lambda s, pt: (s, 0, 0)), const(bias_c), const(cover)],
        out_specs=[pl.BlockSpec((1, N_HEADS_A, LANES), lambda s, pt: (s, 0, 0)),
                   pl.BlockSpec((1, N_KV_A * TOP_K_BLOCKS, LANES), lambda s, pt: (s, 0, 0))],
        scratch_shapes=[pltpu.VMEM((n_pages * PAGE_SIZE, KV_W), F32)] * 2,
    )
    return pl.pallas_call(
        functools.partial(_sample_cmp_kernel, n_pages=n_pages, n_sel=n_sel, qpos=qpos),
        grid_spec=grid_spec,
        out_shape=[jax.ShapeDtypeStruct((ns, N_HEADS_A, LANES), F32),
                   jax.ShapeDtypeStruct((ns, N_KV_A * TOP_K_BLOCKS, LANES), jnp.int32)],
        compiler_params=_params("arbitrary"),
        name="sample_cmp_topk",
    )(page_table, *([pool] * n_pages), w_big, b_big, g_k, q3, bias_c, cover)


def _sample_attn_kernel(idx_ref, pt_ref, *refs, n_past_blocks):
    n_slots = N_KV_A * TOP_K_BLOCKS
    blocks = refs[:n_slots]
    (win_ref, q_ref, selnew_ref, winnew_ref, wint_ref, oc_ref, gate_ref, bsel_ref, bwin_ref, b0_ref,
     a_ref, winout_ref) = refs[n_slots:]
    s_id = pl.program_id(0)
    q = q_ref[0]
    qf = q.astype(F32)
    lane = lax.broadcasted_iota(jnp.int32, (N_HEADS_A, LANES), 1)
    row = lax.broadcasted_iota(jnp.int32, (N_HEADS_A, LANES), 0)
    lo = lane < SEL_BLOCK

    def with_new_key(s, vt_mat, new_row):
        s_new = jnp.sum(qf * new_row[:, 0:KV_W], axis=-1, keepdims=True) + b0_ref[:, 0:1]
        m = jnp.maximum(jnp.max(s, axis=-1, keepdims=True), s_new)
        p = jnp.exp(s - m)
        p_new = jnp.exp(s_new - m)
        l = jnp.sum(p, axis=-1, keepdims=True) + p_new
        return (_dot_nt(p.astype(BF16), vt_mat) + p_new * new_row[:, KV_W:KV_ROW]) / l

    sel_new = selnew_ref[0]
    o_s = []
    for g in range(N_KV_A):
        slots = range(g * TOP_K_BLOCKS, (g + 1) * TOP_K_BLOCKS)
        kt_cat = jnp.concatenate([blocks[k][0, 0:KV_W, :] for k in slots], axis=1).astype(BF16)
        vt_cat = jnp.concatenate([blocks[k][0, KV_W:KV_ROW, :] for k in slots], axis=1).astype(BF16)
        adds = []
        for k in slots:
            b = idx_ref[s_id * n_slots + k]
            add = jnp.where(b == n_past_blocks - 2, bsel_ref[0], jnp.where(b == n_past_blocks - 1, bsel_ref[1], 0.0))
            other_half = jnp.where(b % 2 == 0, jnp.where(lo, 0.0, NEG), jnp.where(lo, NEG, 0.0))
            adds.append(add + other_half + jnp.where(b >= n_past_blocks, NEG, 0.0))
        o_s.append(with_new_key(_dot(q, kt_cat) + jnp.concatenate(adds, axis=-1), vt_cat, sel_new))
    o_sel = jnp.where(row < GROUP_R, o_s[0], o_s[1])

    win = win_ref[0]
    wb = win.shape[1]
    w_lane = lax.broadcasted_iota(jnp.int32, (N_HEADS_A, wb), 1)
    s_w = _dot(q, win[0:KV_W, :].astype(BF16)) + bwin_ref[...] + jnp.where(w_lane == 0, NEG, 0.0)
    o_win = with_new_key(s_w, win[KV_W:KV_ROW, :].astype(BF16), winnew_ref[0])

    na = N_HEADS_A
    out = gate_ref[0, 0:na] * oc_ref[0] + gate_ref[0, na:2 * na] * o_sel + gate_ref[0, 2 * na:3 * na] * o_win
    a_ref[0] = jnp.where(lo == (row < GROUP_R), out, 0.0).astype(BF16)

    wint = wint_ref[...]
    seq_lane = lax.broadcasted_iota(jnp.int32, wint.shape, 1)
    new_col = jnp.sum(jnp.where(seq_lane == s_id, wint, 0.0), axis=-1, keepdims=True)
    buf_lane = lax.broadcasted_iota(jnp.int32, win.shape, 1)
    winout_ref[0] = jnp.where(buf_lane == wb - 1, new_col, pltpu.roll(win, wb - 1, 1))


def _sample_attn(idx, page_table, pool, win_t, q3, sel_new, win_new, win_new_t, o_c, gate_b, bsel, bwin, b0):
    ns, n_pages = page_table.shape
    n_slots = N_KV_A * TOP_K_BLOCKS
    per_page = PAGE_SIZE // SEL_BLOCK
    n_past_blocks = n_pages * per_page
    wb = win_t.shape[2]

    def slot_spec(k):
        def index(s, idx_ref, pt_ref):
            b = jnp.minimum(idx_ref[s * n_slots + k], n_past_blocks - 1)
            return (pt_ref[s, b // per_page], 0, 0)
        return pl.BlockSpec((1, KV_ROW, PAGE_SIZE), index)

    per_seq = lambda a: pl.BlockSpec((1,) + a.shape[1:], lambda s, *_: (s,) + (0,) * (a.ndim - 1))
    const = lambda a: pl.BlockSpec(a.shape, lambda s, *_: (0,) * a.ndim, pipeline_mode=pl.Buffered(1))
    grid_spec = pltpu.PrefetchScalarGridSpec(
        num_scalar_prefetch=2,
        grid=(ns,),
        in_specs=[slot_spec(k) for k in range(n_slots)]
        + [per_seq(win_t), per_seq(q3), per_seq(sel_new), per_seq(win_new), const(win_new_t), per_seq(o_c),
           per_seq(gate_b), const(bsel), const(bwin), const(b0)],
        out_specs=[pl.BlockSpec((1, N_HEADS_A, LANES), lambda s, *_: (s, 0, 0)),
                   pl.BlockSpec((1, KV_ROW, wb), lambda s, *_: (s, 0, 0))],
    )
    return pl.pallas_call(
        functools.partial(_sample_attn_kernel, n_past_blocks=n_past_blocks),
        grid_spec=grid_spec,
        out_shape=[jax.ShapeDtypeStruct((ns, N_HEADS_A, LANES), BF16),
                   jax.ShapeDtypeStruct((ns, KV_ROW, wb), F32)],
        compiler_params=_params("arbitrary"),
        name="sample_sel_win",
    )(idx, page_table, *([pool] * n_slots), win_t, q3, sel_new, win_new, win_new_t, o_c, gate_b, bsel, bwin, b0)


def _mlstm_step_kernel(qk_ref, conv_ref, v_ref, o_ref, small_ref, c_ref, n_ref, m_ref, cw_ref, cb_ref, bif_ref,
                       mn_ref, y_ref, c_out, n_out, m_out):
    seqs = qk_ref.shape[0]
    sub = lax.broadcasted_iota(jnp.int32, (HEAD_DIM_M, HEAD_DIM_M), 0)
    lan = lax.broadcasted_iota(jnp.int32, (HEAD_DIM_M, HEAD_DIM_M), 1)
    eye = sub == lan
    lane_row = lax.broadcasted_iota(jnp.int32, (1, LANES), 1)
    for s in range(seqs):
        hist = conv_ref[s]
        y = hist[0:1, :] * cw_ref[0:1, :]
        y = y + hist[1:2, :] * cw_ref[1:2, :]
        y = y + hist[2:3, :] * cw_ref[2:3, :]
        y = y + qk_ref[s] * cw_ref[3:4, :] + cb_ref[...]
        qkc = y * _sigmoid(y)
        gi = small_ref[s] + bif_ref[...]
        lf_all = _log_sigmoid(gi)
        m_all = m_ref[s]
        m_new_all = m_all
        h_all = []
        for h in range(N_HEADS_M):
            hs = slice(h * HEAD_DIM_M, (h + 1) * HEAD_DIM_M)
            ks = slice(D_M + h * HEAD_DIM_M, D_M + (h + 1) * HEAD_DIM_M)
            q = qkc[:, hs]
            k = qkc[:, ks] * F32(HEAD_DIM_M ** -0.5)
            v = v_ref[s][:, hs]
            ci, fi = 3 * N_HEADS_A + h, 3 * N_HEADS_A + N_HEADS_M + h
            ig = gi[:, ci:ci + 1]
            lf = lf_all[:, fi:fi + 1]
            m_prev = m_all[:, h:h + 1]
            c_prev = c_ref[s, h]
            n_prev = n_ref[s, h:h + 1, :]
            inter = lf + m_prev
            m_t = jnp.maximum(inter, ig)
            w_in = jnp.exp(ig - m_t)
            dec = jnp.exp(inter - m_t)
            s_qk = jnp.sum(q * k, axis=-1, keepdims=True) * w_in
            cq = _dot_nt(jnp.broadcast_to(q, (8, HEAD_DIM_M)).astype(BF16), c_prev.astype(BF16))[0:1, :]
            num = s_qk * v + dec * cq
            den = s_qk + dec * jnp.sum(n_prev * q, axis=-1, keepdims=True)
            h_all.append(num / jnp.maximum(jnp.abs(den), jnp.exp(-m_t)))
            v_col = jnp.sum(jnp.where(eye, v, 0.0), axis=-1, keepdims=True)
            c_out[s, h] = dec * c_prev + (w_in * v_col) * k
            n_out[s, h:h + 1, :] = dec * n_prev + w_in * k
            m_new_all = jnp.where(lane_row == h, m_t, m_new_all)
        m_out[s] = m_new_all
        y_ref[s] = _head_norm_gate(h_all, mn_ref, o_ref[s]).astype(BF16)


def _mlstm_step(qk, conv0, v, o, small, c0, n0, m0, conv_w, conv_b, bif, mn_g):
    ns = qk.shape[0]
    sb = 8 if ns % 8 == 0 else 1
    per = lambda a: pl.BlockSpec((sb,) + a.shape[1:], lambda i: (i,) + (0,) * (a.ndim - 1))
    ins = (qk, conv0, v, o, small, c0, n0, m0)
    outs = [jax.ShapeDtypeStruct((ns, 1, D_M), BF16), jax.ShapeDtypeStruct(c0.shape, F32),
            jax.ShapeDtypeStruct(n0.shape, F32), jax.ShapeDtypeStruct(m0.shape, F32)]
    return pl.pallas_call(
        _mlstm_step_kernel,
        grid=(ns // sb,),
        in_specs=[per(a) for a in ins] + [_const_spec(a.shape) for a in (conv_w, conv_b, bif, mn_g)],
        out_specs=[per(a) for a in outs],
        out_shape=outs,
        compiler_params=_params("parallel"),
        name="mlstm_step",
    )(*ins, conv_w, conv_b, bif, mn_g)


def _cover_np(n_cmp, n_sel):
    cs = np.arange(n_cmp)[:, None] * CMP_STRIDE
    bs = np.arange(n_sel)[None, :] * SEL_BLOCK
    shared = np.clip(np.minimum(cs + CMP_LEN, bs + SEL_BLOCK) - np.maximum(cs, bs), 0, None)
    return (shared / CMP_LEN).astype(np.float32)


def _prep_in_proj(w_in):
    d = w_in.shape[0]
    wq = w_in[:, :D_A].reshape(d, N_HEADS_A, HEAD_DIM_A)
    z = jnp.zeros_like(wq)
    g0 = jnp.concatenate([wq[:, :GROUP_R], z[:, :GROUP_R]], axis=-1)
    g1 = jnp.concatenate([z[:, GROUP_R:], wq[:, GROUP_R:]], axis=-1)
    wq_pad = jnp.concatenate([g0, g1], axis=1).reshape(d, N_HEADS_A * LANES)
    o_kv = D_A
    o_gate = o_kv + 6 * KV_W
    o_m = o_gate + 3 * N_HEADS_A
    o_if = o_m + 4 * D_M
    w_cmp = w_in[:, o_kv:o_kv + KV_ROW]
    w_kv_t = w_in[:, o_kv:o_gate].T
    w_m = w_in[:, o_m:o_if]
    w_small = jnp.concatenate([w_in[:, o_gate:o_m], w_in[:, o_if:],
                               jnp.zeros((d, LANES - 3 * N_HEADS_A - 2 * N_HEADS_M), w_in.dtype)], axis=1)
    return [w.astype(BF16) for w in (wq_pad.T, w_cmp, w_kv_t, w_m, w_small, w_small.T)]


def _prep_compress(w_k, b_k, w_v, b_v):
    per_c = jnp.stack([w_k, w_k, w_v, w_v]).reshape(4, 2, CMP_STRIDE, HEAD_DIM_A, HEAD_DIM_A)
    w_big = jnp.einsum('chtde,cf->tcdhfe', per_c, jnp.eye(4, dtype=w_k.dtype))
    w_big = w_big.reshape(CMP_STRIDE, KV_ROW, 2 * KV_ROW).astype(BF16)
    pick = lambda h: jnp.concatenate([w_big[:, h * KV_W:(h + 1) * KV_W, h * KV_W:(h + 1) * KV_W],
                                      w_big[:, h * KV_W:(h + 1) * KV_W, KV_ROW + h * KV_W:KV_ROW + (h + 1) * KV_W]],
                                     axis=-1)
    w_big = jnp.stack([pick(0), pick(1)])
    b_big = jnp.concatenate([b_k, b_k, b_v, b_v]).reshape(1, KV_ROW)
    return w_big, b_big


def _prep_out_proj(w_out):
    wa = w_out[:D_A].reshape(N_HEADS_A, HEAD_DIM_A, -1)
    z = jnp.zeros_like(wa)
    g0 = jnp.concatenate([wa[:GROUP_R], z[:GROUP_R]], axis=1)
    g1 = jnp.concatenate([z[GROUP_R:], wa[GROUP_R:]], axis=1)
    woa = jnp.concatenate([g0, g1], axis=0).reshape(N_HEADS_A * LANES, -1)
    return woa.astype(BF16), w_out[D_A:].astype(BF16)


def _bucket_tiles(t, past, wb):
    r = np.arange(LANES)
    tiles = [_t5_bucket_np(d * LANES + r[None, :] - r[:, None]) for d in range(2)]
    for i in range(t // Q_TILE):
        tiles.append(_t5_bucket_np(i * Q_TILE + r[None, :] - (CMP_STRIDE * r[:, None] + CMP_LEN - 1)))
    flat = np.arange(LANES * LANES).reshape(LANES, LANES)
    tiles.append(_t5_bucket_np(past - (CMP_STRIDE * flat + CMP_LEN - 1)))
    tiles.append(_t5_bucket_np(wb - flat))
    tiles.append(_t5_bucket_np(past - (r[:, None] * SEL_BLOCK + r[None, :] % SEL_BLOCK)))
    return np.stack(tiles).astype(np.int32)


def _near_tables(bnear):
    r = np.arange(LANES)
    tile4 = lambda a: jnp.asarray(np.tile(a, (1, GROUP_R)), F32)
    causal = tile4(np.where(r[:, None] <= r[None, :], 0.0, NEG))
    oldest = tile4(np.where(r[:, None] > r[None, :], 0.0, NEG))
    zero = jnp.zeros_like(causal)
    masked = jnp.full_like(causal, NEG)
    n_mid = WINDOW // LANES - 2
    sel_add, win_add = [], []
    for g in range(N_KV_A):
        near = [bnear[1, g], bnear[0, g] + causal]
        sel_add.append(jnp.concatenate(near + [masked], axis=0))
        win_add.append(jnp.concatenate([oldest] + [zero] * n_mid + near + [masked] * (WINDOW // LANES), axis=0))
    return jnp.stack(sel_add), jnp.stack(win_add)


def kernel(x_prompt, x_sample, cache_cmp_kv, cache_sel_kv, cache_win_kv, state_C, state_n, state_m, state_conv, page_table, rel_bias, g_ffn1, w1_gate, w1_up, w1_down, g_mix, w_in, qn_g, kn_cmp_g, kn_sel_g, kn_win_g, w_cmp_k, b_cmp_k, w_cmp_v, b_cmp_v, conv_w, conv_b, b_if, mn_g, w_out, g_ffn2, w2_gate, w2_up, w2_down):
    assert x_prompt.shape[2] == D_MODEL and g_ffn1.shape[0] == 1
    nb, t, _ = x_prompt.shape
    ns, ds, _ = x_sample.shape
    assert ds == 1 and t % Q_TILE == 0
    n_pages = page_table.shape[1]
    past = n_pages * PAGE_SIZE
    wb = cache_win_kv.shape[2]
    assert wb == WINDOW
    kv_shape = (2, N_KV_A, HEAD_DIM_A)

    row = lambda a: a.reshape(1, -1)
    two = lambda a: jnp.tile(a, 2).reshape(1, KV_W)
    w1 = (row(g_ffn1[0]), w1_gate[0].astype(BF16), w1_up[0].astype(BF16), w1_down[0].astype(BF16))
    w2 = (row(g_ffn2[0]), w2_gate[0].astype(BF16), w2_up[0].astype(BF16), w2_down[0].astype(BF16))
    wqt, wc, wkvt, wm, ws, wst = _prep_in_proj(w_in[0])
    qg = (jnp.tile(qn_g[0], 2) * F32(HEAD_DIM_A ** -0.5))[:, None]
    kgt = jnp.stack([jnp.tile(kn_sel_g[0], 2), jnp.tile(kn_win_g[0], 2)])[:, :, None]
    w_big, b_big = _prep_compress(w_cmp_k[0], b_cmp_k[0], w_cmp_v[0], b_cmp_v[0])
    g_cmp = two(kn_cmp_g[0])
    woa, wom = _prep_out_proj(w_out[0])
    n_gate = 3 * N_HEADS_A
    bif = jnp.zeros((1, LANES), F32).at[0, n_gate:n_gate + 2 * N_HEADS_M].set(b_if[0])
    conv_b2 = row(conv_b[0])
    mn = row(mn_g[0])

    nqb = t // Q_TILE
    tiles = _bias_tiles(rel_bias, jnp.asarray(_bucket_tiles(t, past, wb)))
    group_lanes = lambda a: a.reshape(a.shape[0], N_KV_A, GROUP_R, LANES, LANES).transpose(0, 1, 3, 2, 4).reshape(
        a.shape[0], N_KV_A, LANES, GROUP_R * LANES)
    bnear = group_lanes(tiles[0:2])
    sel_add, win_add = _near_tables(bnear)
    bcmp = group_lanes(tiles[2:2 + nqb])
    t_cmp, t_win, t_sel = tiles[2 + nqb], tiles[3 + nqb], tiles[4 + nqb]

    def dense_in(x, seq_len):
        x1 = _ffn(x, *w1)
        return (x1,) + tuple(_inproj(x1, seq_len, row(g_mix[0]), wqt, wc, wkvt, wm, ws, wst, qg, kgt))

    from_chan = lambda a: a.reshape((1, a.shape[0]) + kv_shape + (a.shape[2],)).transpose(0, 1, 5, 2, 3, 4)
    to_chan = lambda a: a.transpose(0, 2, 3, 4, 1).reshape(a.shape[0], KV_ROW, a.shape[1])

    (x1, q_t, cmp_rows, cmp_t, sel_t, win_t, ks, vs, kw, vw, qk_m, v_m, o_m, small, gate_t) = dense_in(
        x_prompt.reshape(nb * t, D_MODEL), t)
    seq = lambda a: a.reshape(nb, t, a.shape[-1])
    n16 = t // CMP_STRIDE
    assert n16 == LANES
    kc, vc_t = _compress(cmp_rows, t, w_big, b_big, g_cmp)
    n_sel = -(-t // SEL_BLOCK)
    cov_t = jnp.asarray(np.pad(_cover_np(n16 - 1, n_sel), ((0, 1), (0, 0))).T, BF16)
    expand_t = jnp.asarray(np.repeat(np.eye(n_sel, dtype=np.float32), SEL_BLOCK, axis=0)[:t], BF16)
    a_pad = _nsa_prompt(q_t, gate_t, kc, vc_t, ks, vs, kw, vw, sel_add, win_add, bcmp, cov_t, expand_t, n16 - 1)
    chunk = 256 if t % 256 == 0 else Q_TILE
    m_out, c_p, n_p, m_p = _mlstm_prompt(seq(qk_m), seq(v_m), seq(o_m), seq(small), conv_w[0], conv_b2, bif, mn,
                                         chunk)
    y_prompt = _post(x1, a_pad.reshape(nb * t, -1), m_out.reshape(nb * t, -1), woa, wom, *w2)
    kv6 = lambda a, n: a.reshape((1, n, -1) + kv_shape)
    prompt_states = (from_chan(cmp_t), from_chan(sel_t), from_chan(win_t[:, :, t - wb:]), c_p[None], n_p[None],
                     m_p[None, :, 0, :N_HEADS_M], seq(qk_m)[None, :, t - (CONV_W - 1):])

    (x1s, q_st, cmp_halves, _, sel_st, win_st, _, _, _, _, qk_s, v_s, o_s, small_s, gate_st) = dense_in(
        x_sample.reshape(ns, D_MODEL), ns)
    cmp_s = jnp.concatenate([cmp_halves[0], cmp_halves[1]], axis=1)
    sel_s, win_s, gates_s = sel_st[0].T, win_st[0].T, gate_st[0].T
    q3 = q_st[0].T.reshape(ns, N_HEADS_A, LANES)
    n16s = past // CMP_STRIDE
    n_sel_s = -(-(past + 1) // SEL_BLOCK)
    n_sel_pad = -(-n_sel_s // LANES) * LANES
    cover_s = jnp.asarray(np.pad(_cover_np(n16s - 1, n_sel_s), ((0, 1), (0, n_sel_pad - n_sel_s))), BF16)
    bias_c = t_cmp[:, :n16s // LANES, :].reshape(N_HEADS_A, n16s)
    o_c, idx = _sample_cmp(page_table, to_chan(cache_cmp_kv[0]), w_big, b_big, g_cmp, q3, bias_c, cover_s, n_sel_s,
                           past)
    n_past_blocks = past // SEL_BLOCK
    two_lanes = lambda a: jnp.concatenate([a[:, :SEL_BLOCK], a[:, :SEL_BLOCK]], axis=-1)
    bsel = jnp.stack([two_lanes(t_sel[:, n_past_blocks - 2, :]), two_lanes(t_sel[:, n_past_blocks - 1, :])])
    bwin = t_win[:, :wb // LANES, :].reshape(N_HEADS_A, wb)
    b0 = jnp.broadcast_to(t_win[:, wb // LANES, 0:1], (N_HEADS_A, LANES))
    gate_b = jnp.broadcast_to(gates_s[:, :n_gate, None], (ns, n_gate, LANES))
    a_s, win_out = _sample_attn(idx[:, :, 0].reshape(-1), page_table, to_chan(cache_sel_kv[0]),
                                to_chan(cache_win_kv[0]), q3, sel_s[:, None, :], win_s[:, None, :], win_st[0],
                                o_c, gate_b, bsel, bwin, b0)
    m0 = jnp.pad(state_m[0], ((0, 0), (0, LANES - N_HEADS_M)))[:, None, :]
    m_s, c_s, n_s, m_new = _mlstm_step(qk_s[:, None, :], state_conv[0], v_s[:, None, :], o_s[:, None, :],
                                       small_s[:, None, :], state_C[0], state_n[0], m0, conv_w[0], conv_b2, bif, mn)
    y_sample = _post(x1s, a_s.reshape(ns, -1), m_s.reshape(ns, -1), woa, wom, *w2)
    conv_new = jnp.concatenate([state_conv[0][:, 1:], qk_s[:, None, :]], axis=1)
    sample_states = (kv6(cmp_s, ns), kv6(sel_s, ns), from_chan(win_out), c_s[None], n_s[None],
                     m_new[None, :, 0, :N_HEADS_M], conv_new[None])

    return (y_prompt.reshape(nb, t, D_MODEL), y_sample.reshape(ns, 1, D_MODEL)) + prompt_states + sample_states
```

```python
import functools
import math

import jax
import jax.numpy as jnp
import numpy as np
from jax import lax
from jax.experimental import pallas as pl
from jax.experimental.pallas import tpu as pltpu

F32 = jnp.float32
BF16 = jnp.bfloat16

D_MODEL = 1024
PAGE_SIZE = 128
N_HEADS_A = 8
HEAD_DIM_A = 64
N_KV_A = 2
GROUP_R = N_HEADS_A // N_KV_A
D_A = N_HEADS_A * HEAD_DIM_A
KV_W = N_KV_A * HEAD_DIM_A
CMP_STRIDE = 16
CMP_LEN = 2 * CMP_STRIDE
SEL_BLOCK = 64
TOP_K_BLOCKS = 16
WINDOW = 512
N_HEADS_M = 4
HEAD_DIM_M = 128
D_M = N_HEADS_M * HEAD_DIM_M
CONV_W = 4
N_BUCKETS = 32
MAX_DISTANCE = 128
EPS = 1e-6

LANES = 128
Q_TILE = 128
KV_ROW = 2 * KV_W
CHUNK_ROW = CMP_STRIDE * KV_ROW
NEG = -1e30
MASK_BIG = 2.0 ** 100
MASK_PAD = 16
VMEM_LIMIT = 56 * 1024 * 1024

NT_DIMS = (((1,), (1,)), ((), ()))


def _dot(a, b):
    return jnp.dot(a, b, preferred_element_type=F32)


def _dot_nt(a, b):
    return lax.dot_general(a, b, NT_DIMS, preferred_element_type=F32)


def _split3(x):
    x1 = x.astype(BF16)
    r1 = x - x1.astype(F32)
    x2 = r1.astype(BF16)
    x3 = (r1 - x2.astype(F32)).astype(BF16)
    return x1, x2, x3


def _dot_exact_rhs(x, m):
    x1, x2, x3 = _split3(x)
    return _dot(x1, m) + _dot(x2, m) + _dot(x3, m)


def _dot_exact_lhs(m, x):
    x1, x2, x3 = _split3(x)
    return _dot(m, x1) + _dot(m, x2) + _dot(m, x3)


def _rms_rows(x, g):
    ms = jnp.mean(x * x, axis=-1, keepdims=True)
    return x * lax.rsqrt(ms + EPS) * g


def _rms_two_groups(k, g):
    sq = k * k
    lane = lax.broadcasted_iota(jnp.int32, sq.shape, 1)
    lo = lane < HEAD_DIM_A
    s0 = jnp.sum(jnp.where(lo, sq, 0.0), axis=-1, keepdims=True)
    s1 = jnp.sum(jnp.where(lo, 0.0, sq), axis=-1, keepdims=True)
    ms = jnp.where(lo, s0, s1) * (1.0 / HEAD_DIM_A)
    return k * lax.rsqrt(ms + EPS) * g


def _sigmoid(x):
    return jax.nn.sigmoid(x)


def _log_sigmoid(x):
    return jnp.minimum(x, 0.0) - jnp.log1p(jnp.exp(-jnp.abs(x)))


def _const_spec(shape):
    nd = len(shape)
    return pl.BlockSpec(shape, lambda *_: (0,) * nd, pipeline_mode=pl.Buffered(1))


def _params(*sem):
    return pltpu.CompilerParams(dimension_semantics=sem, vmem_limit_bytes=VMEM_LIMIT)


def _swiglu_residual(x, g_ref, wg_ref, wu_ref, wd_ref):
    xn = _rms_rows(x, g_ref[...]).astype(BF16)
    d_ff = wg_ref.shape[1]
    n_split = 2 if d_ff % (2 * LANES) == 0 else 1
    step = d_ff // n_split
    acc = jnp.zeros_like(x)
    for c in range(n_split):
        hg = _dot(xn, wg_ref[:, c * step:(c + 1) * step])
        hu = _dot(xn, wu_ref[:, c * step:(c + 1) * step])
        h = (hg * _sigmoid(hg)) * hu
        acc = acc + _dot(h.astype(BF16), wd_ref[c * step:(c + 1) * step, :])
    return x + 0.5 * acc


def _ffn_kernel(x_ref, g_ref, wg_ref, wu_ref, wd_ref, y_ref):
    y_ref[...] = _swiglu_residual(x_ref[...], g_ref, wg_ref, wu_ref, wd_ref)


def _post_kernel(x_ref, a_ref, m_ref, woa_ref, wom_ref, g_ref, wg_ref, wu_ref, wd_ref, y_ref):
    x = x_ref[...] + (_dot(a_ref[...], woa_ref[...]) + _dot(m_ref[...], wom_ref[...]))
    y_ref[...] = _swiglu_residual(x, g_ref, wg_ref, wu_ref, wd_ref)


def _token_tile(n):
    return 512 if n % 512 == 0 else n


def _ffn(x, g, wg, wu, wd):
    n = x.shape[0]
    tm = _token_tile(n)
    row = lambda w: pl.BlockSpec((tm, w), lambda i: (i, 0))
    return pl.pallas_call(
        _ffn_kernel,
        grid=(n // tm,),
        in_specs=[row(D_MODEL), _const_spec(g.shape), _const_spec(wg.shape), _const_spec(wu.shape),
                  _const_spec(wd.shape)],
        out_specs=row(D_MODEL),
        out_shape=jax.ShapeDtypeStruct((n, D_MODEL), F32),
        compiler_params=_params("parallel"),
        name="ffn1",
    )(x, g, wg, wu, wd)


def _post(x, a_pad, m_out, woa, wom, g, wg, wu, wd):
    n = x.shape[0]
    tm = _token_tile(n)
    row = lambda w: pl.BlockSpec((tm, w), lambda i: (i, 0))
    return pl.pallas_call(
        _post_kernel,
        grid=(n // tm,),
        in_specs=[row(D_MODEL), row(a_pad.shape[1]), row(m_out.shape[1]), _const_spec(woa.shape),
                  _const_spec(wom.shape), _const_spec(g.shape), _const_spec(wg.shape), _const_spec(wu.shape),
                  _const_spec(wd.shape)],
        out_specs=row(D_MODEL),
        out_shape=jax.ShapeDtypeStruct((n, D_MODEL), F32),
        compiler_params=_params("parallel"),
        name="post_mix_ffn2",
    )(x, a_pad, m_out, woa, wom, g, wg, wu, wd)


def _rms_two_groups_t(k, g_col):
    sq = k * k
    half = k.shape[0] // 2
    s0 = jnp.sum(sq[0:half], axis=0, keepdims=True)
    s1 = jnp.sum(sq[half:], axis=0, keepdims=True)
    ms = jnp.concatenate([jnp.broadcast_to(s0, (half, k.shape[1])), jnp.broadcast_to(s1, (half, k.shape[1]))],
                         axis=0) * (1.0 / HEAD_DIM_A)
    return k * lax.rsqrt(ms + EPS) * g_col


def _inproj_kernel(x_ref, g_ref, wqt_ref, wc_ref, wkvt_ref, wm_ref, ws_ref, wst_ref, qg_ref, kgt_ref,
                   q_out, cmp_rows, cmpt_out, selt_out, wint_out, kst, vst, kwt, vwt, qk_out, v_out, o_out,
                   small_out, gate_out):
    xn = _rms_rows(x_ref[...], g_ref[...]).astype(BF16)
    qt = _dot_nt(wqt_ref[...], xn)
    for h in range(N_HEADS_A):
        qh = qt[h * LANES:(h + 1) * LANES, :]
        ms = jnp.sum(qh * qh, axis=0, keepdims=True) * (1.0 / HEAD_DIM_A)
        q_out[0, h * LANES:(h + 1) * LANES, :] = (qh * lax.rsqrt(ms + EPS) * qg_ref[...]).astype(BF16)
    c = _dot(xn, wc_ref[...])
    cmp_rows[0] = c[:, 0:KV_W]
    cmp_rows[1] = c[:, KV_W:KV_ROW]
    kvt = _dot_nt(wkvt_ref[...], xn)
    cmpt_out[0] = kvt[0:KV_ROW]
    ks = _rms_two_groups_t(kvt[2 * KV_W:3 * KV_W], kgt_ref[0])
    vs = kvt[3 * KV_W:4 * KV_W]
    kw = _rms_two_groups_t(kvt[4 * KV_W:5 * KV_W], kgt_ref[1])
    vw = kvt[5 * KV_W:6 * KV_W]
    selt_out[0, 0:KV_W] = ks
    selt_out[0, KV_W:KV_ROW] = vs
    wint_out[0, 0:KV_W] = kw
    wint_out[0, KV_W:KV_ROW] = vw
    ones = jnp.ones((HEAD_DIM_A, LANES), F32)
    for j in range(kst.shape[1]):
        cols = slice(j * LANES, (j + 1) * LANES)
        kst[0, j] = ks[:, cols].T.astype(BF16)
        kwt[0, j] = kw[:, cols].T.astype(BF16)
        for v, out in ((vs, vst), (vw, vwt)):
            out[0, j, 0] = jnp.concatenate([v[0:HEAD_DIM_A, cols], ones], axis=0).astype(BF16)
            out[0, j, 1] = jnp.concatenate([ones, v[HEAD_DIM_A:KV_W, cols]], axis=0).astype(BF16)
    m = _dot(xn, wm_ref[...])
    qk_out[...] = m[:, 0:2 * D_M]
    v_out[...] = m[:, 2 * D_M:3 * D_M]
    o_out[...] = m[:, 3 * D_M:4 * D_M]
    small_out[...] = _dot(xn, ws_ref[...])
    gate_out[0] = _sigmoid(_dot_nt(wst_ref[...], xn))


def _inproj(x, seq_len, g, wqt, wc, wkvt, wm, ws, wst, qg, kgt):
    n = x.shape[0]
    nb = n // seq_len
    tm = _token_tile(seq_len)
    tpb = seq_len // tm
    row = lambda w: pl.BlockSpec((tm, w), lambda i: (i, 0))
    rows_out = lambda w, dt: (row(w), jax.ShapeDtypeStruct((n, w), dt))
    chan_out = lambda c, dt: (pl.BlockSpec((1, c, tm), lambda i: (i // tpb, 0, i % tpb)),
                              jax.ShapeDtypeStruct((nb, c, seq_len), dt))
    k_tiles = (pl.BlockSpec((1, tm // LANES, LANES, KV_W), lambda i: (i // tpb, i % tpb, 0, 0)),
               jax.ShapeDtypeStruct((nb, seq_len // LANES, LANES, KV_W), BF16))
    v_tiles = (pl.BlockSpec((1, tm // LANES, N_KV_A, KV_W, LANES), lambda i: (i // tpb, i % tpb, 0, 0, 0)),
               jax.ShapeDtypeStruct((nb, seq_len // LANES, N_KV_A, KV_W, LANES), BF16))
    halves_out = (pl.BlockSpec((2, tm, KV_W), lambda i: (0, i, 0)), jax.ShapeDtypeStruct((2, n, KV_W), F32))
    outs = [chan_out(N_HEADS_A * LANES, BF16), halves_out, chan_out(KV_ROW, F32), chan_out(KV_ROW, F32),
            chan_out(KV_ROW, F32), k_tiles, v_tiles, k_tiles, v_tiles,
            rows_out(2 * D_M, F32), rows_out(D_M, F32), rows_out(D_M, F32), rows_out(LANES, F32),
            chan_out(LANES, F32)]
    consts = (g, wqt, wc, wkvt, wm, ws, wst, qg, kgt)
    return pl.pallas_call(
        _inproj_kernel,
        grid=(n // tm,),
        in_specs=[row(D_MODEL)] + [_const_spec(a.shape) for a in consts],
        out_specs=[o[0] for o in outs],
        out_shape=[o[1] for o in outs],
        compiler_params=_params("parallel"),
        name="in_proj",
    )(x, *consts)


def _t5_bucket_np(dist):
    n = np.maximum(dist, 0)
    exact = N_BUCKETS // 2
    nf = np.maximum(n, 1).astype(np.float32)
    ratio = np.log(nf / np.float32(exact)) / np.float32(math.log(MAX_DISTANCE / exact))
    large = exact + (ratio * np.float32(N_BUCKETS - exact)).astype(np.int32)
    return np.where(n < exact, n, np.minimum(large, N_BUCKETS - 1)).astype(np.int32)


def _bias_kernel(tbl_ref, bkt_ref, out_ref):
    bkt = bkt_ref[0]
    for h in range(N_HEADS_A):
        far = tbl_ref[N_BUCKETS - 1, h]
        acc = jnp.zeros(bkt.shape, F32)
        for b in range(N_BUCKETS - 1):
            acc = jnp.where(bkt == b, tbl_ref[b, h] - far, acc)
        out_ref[0, h] = acc


def _bias_tiles(rel_bias, buckets):
    n = buckets.shape[0]
    return pl.pallas_call(
        _bias_kernel,
        grid=(n,),
        in_specs=[pl.BlockSpec(memory_space=pltpu.SMEM),
                  pl.BlockSpec((1, LANES, LANES), lambda i: (i, 0, 0))],
        out_specs=pl.BlockSpec((1, N_HEADS_A, LANES, LANES), lambda i: (i, 0, 0, 0)),
        out_shape=jax.ShapeDtypeStruct((n, N_HEADS_A, LANES, LANES), F32),
        compiler_params=_params("parallel"),
        name="bias_tiles",
    )(rel_bias, buckets)


def _compress_half(rows_ref, nc, w_ref, half, bias):
    fs = None
    for t in range(CMP_STRIDE):
        part = _dot(rows_ref[pl.ds(t, nc, stride=CMP_STRIDE), :].astype(BF16), w_ref[half, t])
        fs = part if fs is None else fs + part
    return fs[:, 0:KV_W] + pltpu.roll(fs[:, KV_W:KV_ROW], nc - 1, 0) + bias


def _compress_rows(k_rows_ref, v_rows_ref, nc, w_ref, b_ref, g_ref):
    kc = _rms_two_groups(_compress_half(k_rows_ref, nc, w_ref, 0, b_ref[:, 0:KV_W]), g_ref[...])
    vc = _compress_half(v_rows_ref, nc, w_ref, 1, b_ref[:, KV_W:KV_ROW])
    complete = lax.broadcasted_iota(jnp.int32, kc.shape, 0) < nc - 1
    return jnp.where(complete, kc, 0.0), jnp.where(complete, vc, 0.0)


def _compress_kernel(x_ref, w_ref, b_ref, g_ref, kc_ref, vc_ref):
    nc = kc_ref.shape[1]
    kc, vc = _compress_rows(x_ref.at[0], x_ref.at[1], nc, w_ref, b_ref, g_ref)
    kc_ref[0] = kc.astype(BF16)
    vc_ref[0] = vc.T.astype(BF16)


def _compress(cmp_rows, t, w_big, b_big, g_k):
    nb = cmp_rows.shape[1] // t
    nc = t // CMP_STRIDE
    return pl.pallas_call(
        _compress_kernel,
        grid=(nb,),
        in_specs=[pl.BlockSpec((2, t, KV_W), lambda b: (0, b, 0)), _const_spec(w_big.shape),
                  _const_spec(b_big.shape), _const_spec(g_k.shape)],
        out_specs=[pl.BlockSpec((1, nc, KV_W), lambda b: (b, 0, 0)),
                   pl.BlockSpec((1, KV_W, nc), lambda b: (b, 0, 0))],
        out_shape=[jax.ShapeDtypeStruct((nb, nc, KV_W), BF16), jax.ShapeDtypeStruct((nb, KV_W, nc), BF16)],
        compiler_params=_params("parallel"),
        name="compress_prompt",
    )(cmp_rows, w_big, b_big, g_k)


def _tile_heads(x):
    return jnp.concatenate([x] * GROUP_R, axis=1)


def _softmax_part(m, s):
    m_new = jnp.maximum(m, jnp.max(s, axis=0, keepdims=True))
    return m_new, jnp.exp(m - m_new), jnp.exp(s - m_new).astype(BF16)


def _accumulate(acc, soft, vt):
    m_new, alpha, p = soft
    return m_new, alpha * acc + _dot(vt, p)


def _key_rows(ref, t, n):
    return ref[0, pl.ds(t, n)].reshape(n * LANES, ref.shape[-1])


def _value_cols(ref, t, n, g):
    return jnp.concatenate([ref[0, t + k, g] for k in range(n)], axis=1)


def _rank_rows(score):
    n_blk = score.shape[0]
    blk = lax.broadcasted_iota(jnp.int32, score.shape, 0)
    rank = jnp.zeros(score.shape, F32)
    for b in range(n_blk):
        row = score[b:b + 1, :]
        rank = rank + jnp.where(blk > b, jnp.where(row >= score, 1.0, 0.0), jnp.where(row > score, 1.0, 0.0))
    return rank


def _nsa_prompt_kernel(q_ref, gate_ref, kc_ref, vct_ref, ks_ref, vs_ref, kw_ref, vw_ref, sel_add_ref, win_add_ref,
                       bcmp_ref, covt_ref, expt_ref, a_ref, mb_ref, *, n_cmp):
    i = pl.program_id(1)
    n_tiles = mb_ref.shape[1]
    n_sel = covt_ref.shape[0]
    q0 = i * Q_TILE
    wide = GROUP_R * Q_TILE
    gate_t = gate_ref[0]
    chan = lax.broadcasted_iota(jnp.int32, (LANES, wide), 0)
    groups = range(N_KV_A)
    heads_of = lambda g: range(g * GROUP_R, (g + 1) * GROUP_R)
    qg_t = [jnp.concatenate([q_ref[0, h * LANES:(h + 1) * LANES, :] for h in heads_of(g)], axis=1) for g in groups]

    init = (jnp.full((1, wide), NEG, F32), jnp.zeros((LANES, wide), F32))

    n_win = WINDOW // LANES + 1
    t0 = jnp.maximum(i - (n_win - 1), 0)
    win_off = pl.multiple_of(jnp.maximum(n_win - 1 - i, 0) * LANES, LANES)
    k_win = _key_rows(kw_ref, t0, n_win)
    s_win = [_dot(k_win, qg_t[g]) + win_add_ref[g, pl.ds(win_off, n_win * LANES), :] for g in groups]

    ncp = kc_ref.shape[1]
    j_sub = lax.broadcasted_iota(jnp.int32, (ncp, wide), 0)
    q_lane = q0 + (lax.broadcasted_iota(jnp.int32, (ncp, wide), 1) & (Q_TILE - 1))
    usable_add = jnp.where((j_sub * CMP_STRIDE + (CMP_LEN - 1) <= q_lane) & (j_sub < n_cmp), 0.0, NEG)
    any_usable = jnp.where(q_lane[0:1, :] >= CMP_LEN - 1, 1.0, 0.0)
    s_cmp = [_dot(kc_ref[0], qg_t[g]) + bcmp_ref[0, g] + usable_add for g in groups]
    p_c = []
    for g in groups:
        e = jnp.exp(s_cmp[g] - jnp.max(s_cmp[g], axis=0, keepdims=True)) * any_usable
        p_c.append(e / jnp.maximum(jnp.sum(e, axis=0, keepdims=True), 1e-30))
    oc_t = [_dot(vct_ref[0], p_c[g].astype(BF16)) for g in groups]

    blk = lax.broadcasted_iota(jnp.int32, (n_sel, Q_TILE), 0)
    qpos = q0 + lax.broadcasted_iota(jnp.int32, (n_sel, Q_TILE), 1)
    cur = qpos // SEL_BLOCK
    valid = blk * SEL_BLOCK <= qpos
    forced = valid & ((blk == 0) | (blk == cur) | (blk == cur - 1))
    ones_row = jnp.where(lax.broadcasted_iota(jnp.int32, (MASK_PAD, Q_TILE), 0) == 0, 1.0, 0.0)
    for g in groups:
        p_sum = p_c[g][:, 0:Q_TILE]
        for r in range(1, GROUP_R):
            p_sum = p_sum + p_c[g][:, r * Q_TILE:(r + 1) * Q_TILE]
        imp_t = _dot_exact_lhs(covt_ref[...], p_sum)
        score = jnp.where(forced, -NEG, jnp.where(valid, imp_t, NEG))
        chosen = jnp.where(_rank_rows(score) < float(min(TOP_K_BLOCKS, n_sel)), 1.0, 0.0)
        mask_all = _dot(expt_ref[...], jnp.concatenate([chosen, ones_row], axis=0).astype(BF16))
        for t in range(n_tiles):
            mb_ref[g, t] = mask_all[t * LANES:(t + 1) * LANES, :]

    soft_win = [_softmax_part(init[0], s_win[g]) for g in groups]
    ow_acc = [_dot(_value_cols(vw_ref, t0, n_win, g), soft_win[g][2]) for g in groups]

    pair_row = lax.broadcasted_iota(jnp.int32, (2 * LANES, Q_TILE), 0)

    def far_body(pp, carry):
        s = []
        for half in range(2):
            t = 4 * pp + 2 * half
            past_far = jnp.where(pair_row >= LANES, jnp.where(t + 1 <= i - 2, 0.0, NEG),
                                 jnp.where(t <= i - 2, 0.0, NEG))
            k2 = _key_rows(ks_ref, t, 2)
            s.append([_dot(k2, qg_t[g]) + _tile_heads(mb_ref[g, pl.ds(t, 2)].reshape(2 * LANES, Q_TILE) + past_far)
                      for g in groups])
        soft0 = [_softmax_part(carry[g][0], s[0][g]) for g in groups]
        soft1 = [_softmax_part(soft0[g][0], s[1][g]) for g in groups]
        mid = [_accumulate(carry[g][1], soft0[g], _value_cols(vs_ref, 4 * pp, 2, g)) for g in groups]
        return tuple(_accumulate(mid[g][1], soft1[g], _value_cols(vs_ref, 4 * pp + 2, 2, g)) for g in groups)

    carry = lax.fori_loop(0, (i // 2 + 1) // 2, far_body, (init, init))

    t1 = jnp.maximum(i - 1, 0)
    k_near = _key_rows(ks_ref, t1, 2)
    near_off = pl.multiple_of(jnp.where(i == 0, LANES, 0), LANES)
    s_near = [_dot(k_near, qg_t[g]) + sel_add_ref[g, pl.ds(near_off, 2 * LANES), :] + _tile_heads(
        mb_ref[g, pl.ds(t1, 2)].reshape(2 * LANES, Q_TILE)) for g in groups]
    soft_near = [_softmax_part(carry[g][0], s_near[g]) for g in groups]
    for g in groups:
        denom = (1 - g) * HEAD_DIM_A
        _, acc = _accumulate(carry[g][1], soft_near[g], _value_cols(vs_ref, t1, 2, g))
        os_t = acc / acc[denom:denom + 1, :]
        ow_t = ow_acc[g] / ow_acc[g][denom:denom + 1, :]

        heads = heads_of(g)
        gate_row = lambda br: jnp.concatenate(
            [gate_t[br * N_HEADS_A + h:br * N_HEADS_A + h + 1, :] for h in heads], axis=1)
        out_t = gate_row(0) * oc_t[g] + gate_row(1) * os_t + gate_row(2) * ow_t
        own_rows = (chan >= g * HEAD_DIM_A) & (chan < (g + 1) * HEAD_DIM_A)
        out_t = jnp.where(own_rows, out_t, 0.0)
        for r, h in enumerate(heads):
            a_ref[0, :, h * LANES:(h + 1) * LANES] = out_t[:, r * Q_TILE:(r + 1) * Q_TILE].T.astype(BF16)


def _nsa_prompt(q_t, gate_t, kc, vc_t, ks, vs, kw, vw, sel_add, win_add, bcmp, cov_t, expand_t, n_cmp):
    nb, _, t = q_t.shape
    nqb = t // Q_TILE
    assert nqb > WINDOW // LANES
    seq = lambda a: pl.BlockSpec((1,) + a.shape[1:], lambda b, i: (b,) + (0,) * (a.ndim - 1))
    return pl.pallas_call(
        functools.partial(_nsa_prompt_kernel, n_cmp=n_cmp),
        grid=(nb, nqb),
        in_specs=[pl.BlockSpec((1, N_HEADS_A * LANES, Q_TILE), lambda b, i: (b, 0, i)),
                  pl.BlockSpec((1, LANES, Q_TILE), lambda b, i: (b, 0, i)),
                  seq(kc), seq(vc_t), seq(ks), seq(vs), seq(kw), seq(vw),
                  _const_spec(sel_add.shape), _const_spec(win_add.shape),
                  pl.BlockSpec((1,) + bcmp.shape[1:], lambda b, i: (i, 0, 0, 0)),
                  _const_spec(cov_t.shape), _const_spec(expand_t.shape)],
        out_specs=pl.BlockSpec((1, Q_TILE, N_HEADS_A * LANES), lambda b, i: (b, i, 0)),
        out_shape=jax.ShapeDtypeStruct((nb, t, N_HEADS_A * LANES), BF16),
        scratch_shapes=[pltpu.VMEM((N_KV_A, t // LANES, LANES, Q_TILE), F32)],
        compiler_params=_params("parallel", "arbitrary"),
        name="nsa_prompt",
    )(q_t, gate_t, kc, vc_t, ks, vs, kw, vw, sel_add, win_add, bcmp, cov_t, expand_t)


def _head_norm_gate(h_all, mn_ref, o):
    outs = []
    for h in range(N_HEADS_M):
        hs = slice(h * HEAD_DIM_M, (h + 1) * HEAD_DIM_M)
        outs.append(_rms_rows(h_all[h], mn_ref[:, hs]))
    return _sigmoid(o) * jnp.concatenate(outs, axis=-1)


def _mlstm_prompt_kernel(qk_ref, v_ref, o_ref, small_ref, cw_ref, cb_ref, bif_ref, mn_ref, tri_ref, triu_ref,
                         y_ref, c_ref, n_ref, m_ref, xbuf):
    chunk = qk_ref.shape[1]
    pad = 8

    @pl.when(pl.program_id(1) == 0)
    def _():
        xbuf[0:pad, :] = jnp.zeros((pad, 2 * D_M), F32)
        c_ref[...] = jnp.zeros_like(c_ref)
        n_ref[...] = jnp.zeros_like(n_ref)
        m_ref[...] = jnp.zeros_like(m_ref)

    x = qk_ref[0]
    xbuf[pad:pad + chunk, :] = x
    y = xbuf[pad - 3:pad - 3 + chunk, :] * cw_ref[0:1, :]
    y = y + xbuf[pad - 2:pad - 2 + chunk, :] * cw_ref[1:2, :]
    y = y + xbuf[pad - 1:pad - 1 + chunk, :] * cw_ref[2:3, :]
    y = y + x * cw_ref[3:4, :] + cb_ref[...]
    qkc = y * _sigmoid(y)
    xbuf[0:pad, :] = xbuf[chunk:chunk + pad, :]

    gi = small_ref[0] + bif_ref[...]
    gi_t = gi.T
    b_col = _dot_exact_lhs(tri_ref[...], _log_sigmoid(gi))
    b_row = _dot_exact_rhs(_log_sigmoid(gi_t), triu_ref[...])
    t_col = lax.broadcasted_iota(jnp.int32, (chunk, chunk), 0)
    s_row = lax.broadcasted_iota(jnp.int32, (chunk, chunk), 1)
    causal = s_row <= t_col
    m_all = m_ref[0]
    lane_row = lax.broadcasted_iota(jnp.int32, m_all.shape, 1)

    h_all = []
    for h in range(N_HEADS_M):
        hs = slice(h * HEAD_DIM_M, (h + 1) * HEAD_DIM_M)
        ks = slice(D_M + h * HEAD_DIM_M, D_M + (h + 1) * HEAD_DIM_M)
        q = qkc[:, hs]
        k = qkc[:, ks] * F32(HEAD_DIM_M ** -0.5)
        v = v_ref[0, :, hs]
        qb, kb = q.astype(BF16), k.astype(BF16)
        ci, fi = 3 * N_HEADS_A + h, 3 * N_HEADS_A + N_HEADS_M + h
        bt = b_col[:, fi:fi + 1]
        bs = b_row[fi:fi + 1, :]
        ig_row = gi_t[ci:ci + 1, :]
        ig_col = gi[:, ci:ci + 1]
        m_prev = m_all[:, h:h + 1]
        c_prev = c_ref[0, h]
        n_prev = n_ref[0, h:h + 1, :]

        dlog = jnp.where(causal, bt - bs + ig_row, NEG)
        inter = bt + m_prev
        m_t = jnp.maximum(inter, jnp.max(dlog, axis=-1, keepdims=True))
        s_qk = _dot_nt(qb, kb) * jnp.exp(dlog - m_t)
        dec = jnp.exp(inter - m_t)
        num = _dot(s_qk.astype(BF16), v.astype(BF16)) + dec * _dot_nt(qb, c_prev.astype(BF16))
        den = jnp.sum(s_qk, axis=-1, keepdims=True) + dec * jnp.sum(q * n_prev, axis=-1, keepdims=True)
        h_all.append(num / jnp.maximum(jnp.abs(den), jnp.exp(-m_t)))

        m_new = m_t[chunk - 1:chunk, :]
        b_end = bt[chunk - 1:chunk, :]
        w_end = jnp.exp(b_end - bt + ig_col - m_new)
        dec_end = jnp.exp(b_end + m_prev - m_new)
        wv_t = (w_end * v).T.astype(BF16)
        c_ref[0, h] = dec_end * c_prev + _dot(wv_t, kb)
        n_ref[0, h:h + 1, :] = dec_end * n_prev + jnp.sum(w_end * k, axis=0, keepdims=True)
        m_all = jnp.where(lane_row == h, m_new, m_all)

    m_ref[0] = m_all
    y_ref[0] = _head_norm_gate(h_all, mn_ref, o_ref[0]).astype(BF16)


def _mlstm_prompt(qk, v, o, small, conv_w, conv_b, bif, mn_g, chunk):
    nb, t, _ = qk.shape
    tri = jnp.asarray(np.tril(np.ones((chunk, chunk), np.float32)), BF16)
    triu = jnp.asarray(np.triu(np.ones((chunk, chunk), np.float32)), BF16)
    tok = lambda w: pl.BlockSpec((1, chunk, w), lambda b, c: (b, c, 0))
    return pl.pallas_call(
        _mlstm_prompt_kernel,
        grid=(nb, t // chunk),
        in_specs=[tok(2 * D_M), tok(D_M), tok(D_M), tok(LANES)]
        + [_const_spec(a.shape) for a in (conv_w, conv_b, bif, mn_g, tri, triu)],
        out_specs=[tok(D_M),
                   pl.BlockSpec((1, N_HEADS_M, HEAD_DIM_M, HEAD_DIM_M), lambda b, c: (b, 0, 0, 0)),
                   pl.BlockSpec((1, N_HEADS_M, HEAD_DIM_M), lambda b, c: (b, 0, 0)),
                   pl.BlockSpec((1, 1, LANES), lambda b, c: (b, 0, 0))],
        out_shape=[jax.ShapeDtypeStruct((nb, t, D_M), BF16),
                   jax.ShapeDtypeStruct((nb, N_HEADS_M, HEAD_DIM_M, HEAD_DIM_M), F32),
                   jax.ShapeDtypeStruct((nb, N_HEADS_M, HEAD_DIM_M), F32),
                   jax.ShapeDtypeStruct((nb, 1, LANES), F32)],
        scratch_shapes=[pltpu.VMEM((chunk + 8, 2 * D_M), F32)],
        compiler_params=_params("parallel", "arbitrary"),
        name="mlstm_prompt",
    )(qk, v, o, small, conv_w, conv_b, bif, mn_g, tri, triu)


def _softmax_rows(s):
    e = jnp.exp(s - jnp.max(s, axis=-1, keepdims=True))
    return e, jnp.sum(e, axis=-1, keepdims=True)


def _sample_cmp_kernel(pt_ref, *refs, n_pages, n_sel, qpos):
    pages = refs[:n_pages]
    w_ref, b_ref, g_ref, q_ref, bias_ref, cov_ref, oc_ref, idx_ref, xk_ref, xv_ref = refs[n_pages:]
    for half, rows_ref in ((0, xk_ref), (1, xv_ref)):
        for u in range(n_pages):
            rows_ref[u * PAGE_SIZE:(u + 1) * PAGE_SIZE, :] = pages[u][0, half * KV_W:(half + 1) * KV_W, :].T
    nc = n_pages * PAGE_SIZE // CMP_STRIDE
    kc, vc = _compress_rows(xk_ref, xv_ref, nc, w_ref, b_ref, g_ref)
    q = q_ref[0]
    j_row = lax.broadcasted_iota(jnp.int32, (N_HEADS_A, nc), 1)
    s = _dot_nt(q, kc.astype(BF16)) + bias_ref[...] + jnp.where(j_row < nc - 1, 0.0, NEG)
    e, l = _softmax_rows(s)
    p = e / jnp.maximum(l, 1e-30)
    oc_ref[0] = _dot(p.astype(BF16), vc.astype(BF16))

    nsp = cov_ref.shape[1]
    blk_row = lax.broadcasted_iota(jnp.int32, (1, nsp), 1)
    sub = lax.broadcasted_iota(jnp.int32, (nsp, nsp), 0)
    lan = lax.broadcasted_iota(jnp.int32, (nsp, nsp), 1)
    cur = qpos // SEL_BLOCK
    valid = (blk_row * SEL_BLOCK <= qpos) & (blk_row < n_sel)
    forced = valid & ((blk_row == 0) | (blk_row == cur) | (blk_row == cur - 1))
    k_col = lax.broadcasted_iota(jnp.int32, (TOP_K_BLOCKS, nsp), 0).astype(F32)
    blk_f = lax.broadcasted_iota(jnp.int32, (TOP_K_BLOCKS, nsp), 1).astype(F32)
    for g in range(N_KV_A):
        p_sum = jnp.sum(p[g * GROUP_R:(g + 1) * GROUP_R, :], axis=0, keepdims=True)
        imp = _dot_exact_rhs(jnp.broadcast_to(p_sum, (8, nc)), cov_ref[...])[0:1, :]
        score = jnp.where(forced, -NEG, jnp.where(valid, imp, NEG))
        score_col = jnp.sum(jnp.where(sub == lan, score, 0.0), axis=-1, keepdims=True)
        ahead = jnp.where(sub < lan, jnp.where(score_col >= score, 1.0, 0.0), jnp.where(score_col > score, 1.0, 0.0))
        rank = jnp.sum(ahead, axis=0, keepdims=True)
        chosen = jnp.sum(jnp.where(rank == k_col, blk_f, 0.0), axis=-1, keepdims=True)
        idx_ref[0, g * TOP_K_BLOCKS:(g + 1) * TOP_K_BLOCKS, :] = jnp.broadcast_to(
            chosen, (TOP_K_BLOCKS, LANES)).astype(jnp.int32)


def _sample_cmp(page_table, pool, w_big, b_big, g_k, q3, bias_c, cover, n_sel, qpos):
    ns, n_pages = page_table.shape
    page_spec = lambda u: pl.BlockSpec((1, KV_ROW, PAGE_SIZE), lambda s, pt: (pt[s, u], 0, 0))
    const = lambda a: pl.BlockSpec(a.shape, lambda s, pt: (0,) * a.ndim, pipeline_mode=pl.Buffered(1))
    grid_spec = pltpu.PrefetchScalarGridSpec(
        num_scalar_prefetch=1,
        grid=(ns,),
        in_specs=[page_spec(u) for u in range(n_pages)]
        + [const(w_big), const(b_big), const(g_k),
           pl.BlockSpec((1, N_HEADS_A, LANES), lambda s, pt: (s, 0, 0)), const(bias_c), const(cover)],
        out_specs=[pl.BlockSpec((1, N_HEADS_A, LANES), lambda s, pt: (s, 0, 0)),
                   pl.BlockSpec((1, N_KV_A * TOP_K_BLOCKS, LANES), lambda s, pt: (s, 0, 0))],
        scratch_shapes=[pltpu.VMEM((n_pages * PAGE_SIZE, KV_W), F32)] * 2,
    )
    return pl.pallas_call(
        functools.partial(_sample_cmp_kernel, n_pages=n_pages, n_sel=n_sel, qpos=qpos),
        grid_spec=grid_spec,
        out_shape=[jax.ShapeDtypeStruct((ns, N_HEADS_A, LANES), F32),
                   jax.ShapeDtypeStruct((ns, N_KV_A * TOP_K_BLOCKS, LANES), jnp.int32)],
        compiler_params=_params("arbitrary"),
        name="sample_cmp_topk",
    )(page_table, *([pool] * n_pages), w_big, b_big, g_k, q3, bias_c, cover)


def _sample_attn_kernel(idx_ref, pt_ref, *refs, n_past_blocks):
    n_slots = N_KV_A * TOP_K_BLOCKS
    blocks = refs[:n_slots]
    (win_ref, q_ref, selnew_ref, winnew_ref, wint_ref, oc_ref, gate_ref, bsel_ref, bwin_ref, b0_ref,
     a_ref, winout_ref) = refs[n_slots:]
    s_id = pl.program_id(0)
    q = q_ref[0]
    qf = q.astype(F32)
    lane = lax.broadcasted_iota(jnp.int32, (N_HEADS_A, LANES), 1)
    row = lax.broadcasted_iota(jnp.int32, (N_HEADS_A, LANES), 0)
    lo = lane < SEL_BLOCK

    def with_new_key(s, vt_mat, new_row):
        s_new = jnp.sum(qf * new_row[:, 0:KV_W], axis=-1, keepdims=True) + b0_ref[:, 0:1]
        m = jnp.maximum(jnp.max(s, axis=-1, keepdims=True), s_new)
        p = jnp.exp(s - m)
        p_new = jnp.exp(s_new - m)
        l = jnp.sum(p, axis=-1, keepdims=True) + p_new
        return (_dot_nt(p.astype(BF16), vt_mat) + p_new * new_row[:, KV_W:KV_ROW]) / l

    sel_new = selnew_ref[0]
    o_s = []
    for g in range(N_KV_A):
        slots = range(g * TOP_K_BLOCKS, (g + 1) * TOP_K_BLOCKS)
        kt_cat = jnp.concatenate([blocks[k][0, 0:KV_W, :] for k in slots], axis=1).astype(BF16)
        vt_cat = jnp.concatenate([blocks[k][0, KV_W:KV_ROW, :] for k in slots], axis=1).astype(BF16)
        adds = []
        for k in slots:
            b = idx_ref[s_id * n_slots + k]
            add = jnp.where(b == n_past_blocks - 2, bsel_ref[0], jnp.where(b == n_past_blocks - 1, bsel_ref[1], 0.0))
            other_half = jnp.where(b % 2 == 0, jnp.where(lo, 0.0, NEG), jnp.where(lo, NEG, 0.0))
            adds.append(add + other_half + jnp.where(b >= n_past_blocks, NEG, 0.0))
        o_s.append(with_new_key(_dot(q, kt_cat) + jnp.concatenate(adds, axis=-1), vt_cat, sel_new))
    o_sel = jnp.where(row < GROUP_R, o_s[0], o_s[1])

    win = win_ref[0]
    wb = win.shape[1]
    w_lane = lax.broadcasted_iota(jnp.int32, (N_HEADS_A, wb), 1)
    s_w = _dot(q, win[0:KV_W, :].astype(BF16)) + bwin_ref[...] + jnp.where(w_lane == 0, NEG, 0.0)
    o_win = with_new_key(s_w, win[KV_W:KV_ROW, :].astype(BF16), winnew_ref[0])

    na = N_HEADS_A
    out = gate_ref[0, 0:na] * oc_ref[0] + gate_ref[0, na:2 * na] * o_sel + gate_ref[0, 2 * na:3 * na] * o_win
    a_ref[0] = jnp.where(lo == (row < GROUP_R), out, 0.0).astype(BF16)

    wint = wint_ref[...]
    seq_lane = lax.broadcasted_iota(jnp.int32, wint.shape, 1)
    new_col = jnp.sum(jnp.where(seq_lane == s_id, wint, 0.0), axis=-1, keepdims=True)
    buf_lane = lax.broadcasted_iota(jnp.int32, win.shape, 1)
    winout_ref[0] = jnp.where(buf_lane == wb - 1, new_col, pltpu.roll(win, wb - 1, 1))


def _sample_attn(idx, page_table, pool, win_t, q3, sel_new, win_new, win_new_t, o_c, gate_b, bsel, bwin, b0):
    ns, n_pages = page_table.shape
    n_slots = N_KV_A * TOP_K_BLOCKS
    per_page = PAGE_SIZE // SEL_BLOCK
    n_past_blocks = n_pages * per_page
    wb = win_t.shape[2]

    def slot_spec(k):
        def index(s, idx_ref, pt_ref):
            b = jnp.minimum(idx_ref[s * n_slots + k], n_past_blocks - 1)
            return (pt_ref[s, b // per_page], 0, 0)
        return pl.BlockSpec((1, KV_ROW, PAGE_SIZE), index)

    per_seq = lambda a: pl.BlockSpec((1,) + a.shape[1:], lambda s, *_: (s,) + (0,) * (a.ndim - 1))
    const = lambda a: pl.BlockSpec(a.shape, lambda s, *_: (0,) * a.ndim, pipeline_mode=pl.Buffered(1))
    grid_spec = pltpu.PrefetchScalarGridSpec(
        num_scalar_prefetch=2,
        grid=(ns,),
        in_specs=[slot_spec(k) for k in range(n_slots)]
        + [per_seq(win_t), per_seq(q3), per_seq(sel_new), per_seq(win_new), const(win_new_t), per_seq(o_c),
           per_seq(gate_b), const(bsel), const(bwin), const(b0)],
        out_specs=[pl.BlockSpec((1, N_HEADS_A, LANES), lambda s, *_: (s, 0, 0)),
                   pl.BlockSpec((1, KV_ROW, wb), lambda s, *_: (s, 0, 0))],
    )
    return pl.pallas_call(
        functools.partial(_sample_attn_kernel, n_past_blocks=n_past_blocks),
        grid_spec=grid_spec,
        out_shape=[jax.ShapeDtypeStruct((ns, N_HEADS_A, LANES), BF16),
                   jax.ShapeDtypeStruct((ns, KV_ROW, wb), F32)],
        compiler_params=_params("arbitrary"),
        name="sample_sel_win",
    )(idx, page_table, *([pool] * n_slots), win_t, q3, sel_new, win_new, win_new_t, o_c, gate_b, bsel, bwin, b0)


def _mlstm_step_kernel(qk_ref, conv_ref, v_ref, o_ref, small_ref, c_ref, n_ref, m_ref, cw_ref, cb_ref, bif_ref,
                       mn_ref, y_ref, c_out, n_out, m_out):
    seqs = qk_ref.shape[0]
    sub = lax.broadcasted_iota(jnp.int32, (HEAD_DIM_M, HEAD_DIM_M), 0)
    lan = lax.broadcasted_iota(jnp.int32, (HEAD_DIM_M, HEAD_DIM_M), 1)
    eye = sub == lan
    lane_row = lax.broadcasted_iota(jnp.int32, (1, LANES), 1)
    for s in range(seqs):
        hist = conv_ref[s]
        y = hist[0:1, :] * cw_ref[0:1, :]
        y = y + hist[1:2, :] * cw_ref[1:2, :]
        y = y + hist[2:3, :] * cw_ref[2:3, :]
        y = y + qk_ref[s] * cw_ref[3:4, :] + cb_ref[...]
        qkc = y * _sigmoid(y)
        gi = small_ref[s] + bif_ref[...]
        lf_all = _log_sigmoid(gi)
        m_all = m_ref[s]
        m_new_all = m_all
        h_all = []
        for h in range(N_HEADS_M):
            hs = slice(h * HEAD_DIM_M, (h + 1) * HEAD_DIM_M)
            ks = slice(D_M + h * HEAD_DIM_M, D_M + (h + 1) * HEAD_DIM_M)
            q = qkc[:, hs]
            k = qkc[:, ks] * F32(HEAD_DIM_M ** -0.5)
            v = v_ref[s][:, hs]
            ci, fi = 3 * N_HEADS_A + h, 3 * N_HEADS_A + N_HEADS_M + h
            ig = gi[:, ci:ci + 1]
            lf = lf_all[:, fi:fi + 1]
            m_prev = m_all[:, h:h + 1]
            c_prev = c_ref[s, h]
            n_prev = n_ref[s, h:h + 1, :]
            inter = lf + m_prev
            m_t = jnp.maximum(inter, ig)
            w_in = jnp.exp(ig - m_t)
            dec = jnp.exp(inter - m_t)
            s_qk = jnp.sum(q * k, axis=-1, keepdims=True) * w_in
            cq = _dot_nt(jnp.broadcast_to(q, (8, HEAD_DIM_M)).astype(BF16), c_prev.astype(BF16))[0:1, :]
            num = s_qk * v + dec * cq
            den = s_qk + dec * jnp.sum(n_prev * q, axis=-1, keepdims=True)
            h_all.append(num / jnp.maximum(jnp.abs(den), jnp.exp(-m_t)))
            v_col = jnp.sum(jnp.where(eye, v, 0.0), axis=-1, keepdims=True)
            c_out[s, h] = dec * c_prev + (w_in * v_col) * k
            n_out[s, h:h + 1, :] = dec * n_prev + w_in * k
            m_new_all = jnp.where(lane_row == h, m_t, m_new_all)
        m_out[s] = m_new_all
        y_ref[s] = _head_norm_gate(h_all, mn_ref, o_ref[s]).astype(BF16)


def _mlstm_step(qk, conv0, v, o, small, c0, n0, m0, conv_w, conv_b, bif, mn_g):
    ns = qk.shape[0]
    sb = 8 if ns % 8 == 0 else 1
    per = lambda a: pl.BlockSpec((sb,) + a.shape[1:], lambda i: (i,) + (0,) * (a.ndim - 1))
    ins = (qk, conv0, v, o, small, c0, n0, m0)
    outs = [jax.ShapeDtypeStruct((ns, 1, D_M), BF16), jax.ShapeDtypeStruct(c0.shape, F32),
            jax.ShapeDtypeStruct(n0.shape, F32), jax.ShapeDtypeStruct(m0.shape, F32)]
    return pl.pallas_call(
        _mlstm_step_kernel,
        grid=(ns // sb,),
        in_specs=[per(a) for a in ins] + [_const_spec(a.shape) for a in (conv_w, conv_b, bif, mn_g)],
        out_specs=[per(a) for a in outs],
        out_shape=outs,
        compiler_params=_params("parallel"),
        name="mlstm_step",
    )(*ins, conv_w, conv_b, bif, mn_g)


def _cover_np(n_cmp, n_sel):
    cs = np.arange(n_cmp)[:, None] * CMP_STRIDE
    bs = np.arange(n_sel)[None, :] * SEL_BLOCK
    shared = np.clip(np.minimum(cs + CMP_LEN, bs + SEL_BLOCK) - np.maximum(cs, bs), 0, None)
    return (shared / CMP_LEN).astype(np.float32)


def _prep_in_proj(w_in):
    d = w_in.shape[0]
    wq = w_in[:, :D_A].reshape(d, N_HEADS_A, HEAD_DIM_A)
    z = jnp.zeros_like(wq)
    g0 = jnp.concatenate([wq[:, :GROUP_R], z[:, :GROUP_R]], axis=-1)
    g1 = jnp.concatenate([z[:, GROUP_R:], wq[:, GROUP_R:]], axis=-1)
    wq_pad = jnp.concatenate([g0, g1], axis=1).reshape(d, N_HEADS_A * LANES)
    o_kv = D_A
    o_gate = o_kv + 6 * KV_W
    o_m = o_gate + 3 * N_HEADS_A
    o_if = o_m + 4 * D_M
    w_cmp = w_in[:, o_kv:o_kv + KV_ROW]
    w_kv_t = w_in[:, o_kv:o_gate].T
    w_m = w_in[:, o_m:o_if]
    w_small = jnp.concatenate([w_in[:, o_gate:o_m], w_in[:, o_if:],
                               jnp.zeros((d, LANES - 3 * N_HEADS_A - 2 * N_HEADS_M), w_in.dtype)], axis=1)
    return [w.astype(BF16) for w in (wq_pad.T, w_cmp, w_kv_t, w_m, w_small, w_small.T)]


def _prep_compress(w_k, b_k, w_v, b_v):
    per_c = jnp.stack([w_k, w_k, w_v, w_v]).reshape(4, 2, CMP_STRIDE, HEAD_DIM_A, HEAD_DIM_A)
    w_big = jnp.einsum('chtde,cf->tcdhfe', per_c, jnp.eye(4, dtype=w_k.dtype))
    w_big = w_big.reshape(CMP_STRIDE, KV_ROW, 2 * KV_ROW).astype(BF16)
    pick = lambda h: jnp.concatenate([w_big[:, h * KV_W:(h + 1) * KV_W, h * KV_W:(h + 1) * KV_W],
                                      w_big[:, h * KV_W:(h + 1) * KV_W, KV_ROW + h * KV_W:KV_ROW + (h + 1) * KV_W]],
                                     axis=-1)
    w_big = jnp.stack([pick(0), pick(1)])
    b_big = jnp.concatenate([b_k, b_k, b_v, b_v]).reshape(1, KV_ROW)
    return w_big, b_big


def _prep_out_proj(w_out):
    wa = w_out[:D_A].reshape(N_HEADS_A, HEAD_DIM_A, -1)
    z = jnp.zeros_like(wa)
    g0 = jnp.concatenate([wa[:GROUP_R], z[:GROUP_R]], axis=1)
    g1 = jnp.concatenate([z[GROUP_R:], wa[GROUP_R:]], axis=1)
    woa = jnp.concatenate([g0, g1], axis=0).reshape(N_HEADS_A * LANES, -1)
    return woa.astype(BF16), w_out[D_A:].astype(BF16)


def _bucket_tiles(t, past, wb):
    r = np.arange(LANES)
    tiles = [_t5_bucket_np(d * LANES + r[None, :] - r[:, None]) for d in range(2)]
    for i in range(t // Q_TILE):
        tiles.append(_t5_bucket_np(i * Q_TILE + r[None, :] - (CMP_STRIDE * r[:, None] + CMP_LEN - 1)))
    flat = np.arange(LANES * LANES).reshape(LANES, LANES)
    tiles.append(_t5_bucket_np(past - (CMP_STRIDE * flat + CMP_LEN - 1)))
    tiles.append(_t5_bucket_np(wb - flat))
    tiles.append(_t5_bucket_np(past - (r[:, None] * SEL_BLOCK + r[None, :] % SEL_BLOCK)))
    return np.stack(tiles).astype(np.int32)


def _near_tables(bnear):
    r = np.arange(LANES)
    tile4 = lambda a: jnp.asarray(np.tile(a, (1, GROUP_R)), F32)
    causal = tile4(np.where(r[:, None] <= r[None, :], 0.0, NEG))
    oldest = tile4(np.where(r[:, None] > r[None, :], 0.0, NEG))
    zero = jnp.zeros_like(causal)
    masked = jnp.full_like(causal, NEG)
    n_mid = WINDOW // LANES - 2
    sel_add, win_add = [], []
    for g in range(N_KV_A):
        near = [bnear[1, g], bnear[0, g] + causal]
        sel_add.append(jnp.concatenate(near + [masked], axis=0))
        win_add.append(jnp.concatenate([oldest] + [zero] * n_mid + near + [masked] * (WINDOW // LANES), axis=0))
    return jnp.stack(sel_add), jnp.stack(win_add)


def kernel(x_prompt, x_sample, cache_cmp_kv, cache_sel_kv, cache_win_kv, state_C, state_n, state_m, state_conv, page_table, rel_bias, g_ffn1, w1_gate, w1_up, w1_down, g_mix, w_in, qn_g, kn_cmp_g, kn_sel_g, kn_win_g, w_cmp_k, b_cmp_k, w_cmp_v, b_cmp_v, conv_w, conv_b, b_if, mn_g, w_out, g_ffn2, w2_gate, w2_up, w2_down):
    assert x_prompt.shape[2] == D_MODEL and g_ffn1.shape[0] == 1
    nb, t, _ = x_prompt.shape
    ns, ds, _ = x_sample.shape
    assert ds == 1 and t % Q_TILE == 0
    n_pages = page_table.shape[1]
    past = n_pages * PAGE_SIZE
    wb = cache_win_kv.shape[2]
    assert wb == WINDOW
    kv_shape = (2, N_KV_A, HEAD_DIM_A)

    row = lambda a: a.reshape(1, -1)
    two = lambda a: jnp.tile(a, 2).reshape(1, KV_W)
    w1 = (row(g_ffn1[0]), w1_gate[0].astype(BF16), w1_up[0].astype(BF16), w1_down[0].astype(BF16))
    w2 = (row(g_ffn2[0]), w2_gate[0].astype(BF16), w2_up[0].astype(BF16), w2_down[0].astype(BF16))
    wqt, wc, wkvt, wm, ws, wst = _prep_in_proj(w_in[0])
    qg = (jnp.tile(qn_g[0], 2) * F32(HEAD_DIM_A ** -0.5))[:, None]
    kgt = jnp.stack([jnp.tile(kn_sel_g[0], 2), jnp.tile(kn_win_g[0], 2)])[:, :, None]
    w_big, b_big = _prep_compress(w_cmp_k[0], b_cmp_k[0], w_cmp_v[0], b_cmp_v[0])
    g_cmp = two(kn_cmp_g[0])
    woa, wom = _prep_out_proj(w_out[0])
    n_gate = 3 * N_HEADS_A
    bif = jnp.zeros((1, LANES), F32).at[0, n_gate:n_gate + 2 * N_HEADS_M].set(b_if[0])
    conv_b2 = row(conv_b[0])
    mn = row(mn_g[0])

    nqb = t // Q_TILE
    tiles = _bias_tiles(rel_bias, jnp.asarray(_bucket_tiles(t, past, wb)))
    group_lanes = lambda a: a.reshape(a.shape[0], N_KV_A, GROUP_R, LANES, LANES).transpose(0, 1, 3, 2, 4).reshape(
        a.shape[0], N_KV_A, LANES, GROUP_R * LANES)
    bnear = group_lanes(tiles[0:2])
    sel_add, win_add = _near_tables(bnear)
    bcmp = group_lanes(tiles[2:2 + nqb])
    t_cmp, t_win, t_sel = tiles[2 + nqb], tiles[3 + nqb], tiles[4 + nqb]

    def dense_in(x, seq_len):
        x1 = _ffn(x, *w1)
        return (x1,) + tuple(_inproj(x1, seq_len, row(g_mix[0]), wqt, wc, wkvt, wm, ws, wst, qg, kgt))

    from_chan = lambda a: a.reshape((1, a.shape[0]) + kv_shape + (a.shape[2],)).transpose(0, 1, 5, 2, 3, 4)
    to_chan = lambda a: a.transpose(0, 2, 3, 4, 1).reshape(a.shape[0], KV_ROW, a.shape[1])

    (x1, q_t, cmp_rows, cmp_t, sel_t, win_t, ks, vs, kw, vw, qk_m, v_m, o_m, small, gate_t) = dense_in(
        x_prompt.reshape(nb * t, D_MODEL), t)
    seq = lambda a: a.reshape(nb, t, a.shape[-1])
    n16 = t // CMP_STRIDE
    assert n16 == LANES
    kc, vc_t = _compress(cmp_rows, t, w_big, b_big, g_cmp)
    n_sel = -(-t // SEL_BLOCK)
    cov_t = jnp.asarray(np.pad(_cover_np(n16 - 1, n_sel), ((0, 1), (0, 0))).T, BF16)
    expand_np = np.zeros((t, n_sel + MASK_PAD), np.float32)
    expand_np[:, :n_sel] = MASK_BIG * np.repeat(np.eye(n_sel, dtype=np.float32), SEL_BLOCK, axis=0)[:t]
    expand_np[:, n_sel] = -MASK_BIG
    expand_t = jnp.asarray(expand_np, BF16)
    a_pad = _nsa_prompt(q_t, gate_t, kc, vc_t, ks, vs, kw, vw, sel_add, win_add, bcmp, cov_t, expand_t, n16 - 1)
    chunk = 256 if t % 256 == 0 else Q_TILE
    m_out, c_p, n_p, m_p = _mlstm_prompt(seq(qk_m), seq(v_m), seq(o_m), seq(small), conv_w[0], conv_b2, bif, mn,
                                         chunk)
    y_prompt = _post(x1, a_pad.reshape(nb * t, -1), m_out.reshape(nb * t, -1), woa, wom, *w2)
    kv6 = lambda a, n: a.reshape((1, n, -1) + kv_shape)
    prompt_states = (from_chan(cmp_t), from_chan(sel_t), from_chan(win_t[:, :, t - wb:]), c_p[None], n_p[None],
                     m_p[None, :, 0, :N_HEADS_M], seq(qk_m)[None, :, t - (CONV_W - 1):])

    (x1s, q_st, cmp_halves, _, sel_st, win_st, _, _, _, _, qk_s, v_s, o_s, small_s, gate_st) = dense_in(
        x_sample.reshape(ns, D_MODEL), ns)
    cmp_s = jnp.concatenate([cmp_halves[0], cmp_halves[1]], axis=1)
    sel_s, win_s, gates_s = sel_st[0].T, win_st[0].T, gate_st[0].T
    q3 = q_st[0].T.reshape(ns, N_HEADS_A, LANES)
    n16s = past // CMP_STRIDE
    n_sel_s = -(-(past + 1) // SEL_BLOCK)
    n_sel_pad = -(-n_sel_s // LANES) * LANES
    cover_s = jnp.asarray(np.pad(_cover_np(n16s - 1, n_sel_s), ((0, 1), (0, n_sel_pad - n_sel_s))), BF16)
    bias_c = t_cmp[:, :n16s // LANES, :].reshape(N_HEADS_A, n16s)
    o_c, idx = _sample_cmp(page_table, to_chan(cache_cmp_kv[0]), w_big, b_big, g_cmp, q3, bias_c, cover_s, n_sel_s,
                           past)
    n_past_blocks = past // SEL_BLOCK
    two_lanes = lambda a: jnp.concatenate([a[:, :SEL_BLOCK], a[:, :SEL_BLOCK]], axis=-1)
    bsel = jnp.stack([two_lanes(t_sel[:, n_past_blocks - 2, :]), two_lanes(t_sel[:, n_past_blocks - 1, :])])
    bwin = t_win[:, :wb // LANES, :].reshape(N_HEADS_A, wb)
    b0 = jnp.broadcast_to(t_win[:, wb // LANES, 0:1], (N_HEADS_A, LANES))
    gate_b = jnp.broadcast_to(gates_s[:, :n_gate, None], (ns, n_gate, LANES))
    a_s, win_out = _sample_attn(idx[:, :, 0].reshape(-1), page_table, to_chan(cache_sel_kv[0]),
                                to_chan(cache_win_kv[0]), q3, sel_s[:, None, :], win_s[:, None, :], win_st[0],
                                o_c, gate_b, bsel, bwin, b0)
    m0 = jnp.pad(state_m[0], ((0, 0), (0, LANES - N_HEADS_M)))[:, None, :]
    m_s, c_s, n_s, m_new = _mlstm_step(qk_s[:, None, :], state_conv[0], v_s[:, None, :], o_s[:, None, :],
                                       small_s[:, None, :], state_C[0], state_n[0], m0, conv_w[0], conv_b2, bif, mn)
    y_sample = _post(x1s, a_s.reshape(ns, -1), m_s.reshape(ns, -1), woa, wom, *w2)
    conv_new = jnp.concatenate([state_conv[0][:, 1:], qk_s[:, None, :]], axis=1)
    sample_states = (kv6(cmp_s, ns), kv6(sel_s, ns), from_chan(win_out), c_s[None], n_s[None],
                     m_new[None, :, 0, :N_HEADS_M], conv_new[None])

    return (y_prompt.reshape(nb, t, D_MODEL), y_sample.reshape(ns, 1, D_MODEL)) + prompt_states + sample_states
```

```python
import functools
import math

import jax
import jax.numpy as jnp
import numpy as np
from jax import lax
from jax.experimental import pallas as pl
from jax.experimental.pallas import tpu as pltpu

F32 = jnp.float32
BF16 = jnp.bfloat16

D_MODEL = 1024
PAGE_SIZE = 128
N_HEADS_A = 8
HEAD_DIM_A = 64
N_KV_A = 2
GROUP_R = N_HEADS_A // N_KV_A
D_A = N_HEADS_A * HEAD_DIM_A
KV_W = N_KV_A * HEAD_DIM_A
CMP_STRIDE = 16
CMP_LEN = 2 * CMP_STRIDE
SEL_BLOCK = 64
TOP_K_BLOCKS = 16
WINDOW = 512
N_HEADS_M = 4
HEAD_DIM_M = 128
D_M = N_HEADS_M * HEAD_DIM_M
CONV_W = 4
N_BUCKETS = 32
MAX_DISTANCE = 128
EPS = 1e-6

LANES = 128
Q_TILE = 128
Q_BLOCKS = 2
FAR_TILES = 4
KV_ROW = 2 * KV_W
CHUNK_ROW = CMP_STRIDE * KV_ROW
NEG = -1e30
MASK_BIG = 2.0 ** 100
MASK_PAD = 16
VMEM_LIMIT = 56 * 1024 * 1024

NT_DIMS = (((1,), (1,)), ((), ()))


def _dot(a, b):
    return jnp.dot(a, b, preferred_element_type=F32)


def _dot_nt(a, b):
    return lax.dot_general(a, b, NT_DIMS, preferred_element_type=F32)


def _split3(x):
    x1 = x.astype(BF16)
    r1 = x - x1.astype(F32)
    x2 = r1.astype(BF16)
    x3 = (r1 - x2.astype(F32)).astype(BF16)
    return x1, x2, x3


def _dot_exact_rhs(x, m):
    x1, x2, x3 = _split3(x)
    return _dot(x1, m) + _dot(x2, m) + _dot(x3, m)


def _dot_exact_lhs(m, x):
    x1, x2, x3 = _split3(x)
    return _dot(m, x1) + _dot(m, x2) + _dot(m, x3)


def _rms_rows(x, g):
    ms = jnp.mean(x * x, axis=-1, keepdims=True)
    return x * lax.rsqrt(ms + EPS) * g


def _rms_two_groups(k, g):
    sq = k * k
    lane = lax.broadcasted_iota(jnp.int32, sq.shape, 1)
    lo = lane < HEAD_DIM_A
    s0 = jnp.sum(jnp.where(lo, sq, 0.0), axis=-1, keepdims=True)
    s1 = jnp.sum(jnp.where(lo, 0.0, sq), axis=-1, keepdims=True)
    ms = jnp.where(lo, s0, s1) * (1.0 / HEAD_DIM_A)
    return k * lax.rsqrt(ms + EPS) * g


def _sigmoid(x):
    return jax.nn.sigmoid(x)


def _log_sigmoid(x):
    return jnp.minimum(x, 0.0) - jnp.log1p(jnp.exp(-jnp.abs(x)))


def _const_spec(shape):
    nd = len(shape)
    return pl.BlockSpec(shape, lambda *_: (0,) * nd, pipeline_mode=pl.Buffered(1))


def _params(*sem):
    return pltpu.CompilerParams(dimension_semantics=sem, vmem_limit_bytes=VMEM_LIMIT)


def _swiglu_residual(x, g_ref, wg_ref, wu_ref, wd_ref):
    xn = _rms_rows(x, g_ref[...]).astype(BF16)
    d_ff = wg_ref.shape[1]
    n_split = 2 if d_ff % (2 * LANES) == 0 else 1
    step = d_ff // n_split
    acc = jnp.zeros_like(x)
    for c in range(n_split):
        hg = _dot(xn, wg_ref[:, c * step:(c + 1) * step])
        hu = _dot(xn, wu_ref[:, c * step:(c + 1) * step])
        h = (hg * _sigmoid(hg)) * hu
        acc = acc + _dot(h.astype(BF16), wd_ref[c * step:(c + 1) * step, :])
    return x + 0.5 * acc


def _ffn_kernel(x_ref, g_ref, wg_ref, wu_ref, wd_ref, y_ref):
    y_ref[...] = _swiglu_residual(x_ref[...], g_ref, wg_ref, wu_ref, wd_ref)


def _post_kernel(x_ref, a_ref, m_ref, woa_ref, wom_ref, g_ref, wg_ref, wu_ref, wd_ref, y_ref):
    x = x_ref[...] + (_dot(a_ref[...], woa_ref[...]) + _dot(m_ref[...], wom_ref[...]))
    y_ref[...] = _swiglu_residual(x, g_ref, wg_ref, wu_ref, wd_ref)


def _token_tile(n):
    return 512 if n % 512 == 0 else n


def _ffn(x, g, wg, wu, wd):
    n = x.shape[0]
    tm = _token_tile(n)
    row = lambda w: pl.BlockSpec((tm, w), lambda i: (i, 0))
    return pl.pallas_call(
        _ffn_kernel,
        grid=(n // tm,),
        in_specs=[row(D_MODEL), _const_spec(g.shape), _const_spec(wg.shape), _const_spec(wu.shape),
                  _const_spec(wd.shape)],
        out_specs=row(D_MODEL),
        out_shape=jax.ShapeDtypeStruct((n, D_MODEL), F32),
        compiler_params=_params("parallel"),
        name="ffn1",
    )(x, g, wg, wu, wd)


def _post(x, a_pad, m_out, woa, wom, g, wg, wu, wd):
    n = x.shape[0]
    tm = _token_tile(n)
    row = lambda w: pl.BlockSpec((tm, w), lambda i: (i, 0))
    return pl.pallas_call(
        _post_kernel,
        grid=(n // tm,),
        in_specs=[row(D_MODEL), row(a_pad.shape[1]), row(m_out.shape[1]), _const_spec(woa.shape),
                  _const_spec(wom.shape), _const_spec(g.shape), _const_spec(wg.shape), _const_spec(wu.shape),
                  _const_spec(wd.shape)],
        out_specs=row(D_MODEL),
        out_shape=jax.ShapeDtypeStruct((n, D_MODEL), F32),
        compiler_params=_params("parallel"),
        name="post_mix_ffn2",
    )(x, a_pad, m_out, woa, wom, g, wg, wu, wd)


def _rms_two_groups_t(k, g_col):
    sq = k * k
    half = k.shape[0] // 2
    s0 = jnp.sum(sq[0:half], axis=0, keepdims=True)
    s1 = jnp.sum(sq[half:], axis=0, keepdims=True)
    ms = jnp.concatenate([jnp.broadcast_to(s0, (half, k.shape[1])), jnp.broadcast_to(s1, (half, k.shape[1]))],
                         axis=0) * (1.0 / HEAD_DIM_A)
    return k * lax.rsqrt(ms + EPS) * g_col


def _inproj_kernel(x_ref, g_ref, wqt_ref, wc_ref, wkvt_ref, wm_ref, ws_ref, wst_ref, qg_ref, kgt_ref,
                   q_out, cmp_rows, cmpt_out, selt_out, wint_out, kst, vst, kwt, vwt, qk_out, v_out, o_out,
                   small_out, gate_out):
    xn = _rms_rows(x_ref[...], g_ref[...]).astype(BF16)
    qt = _dot_nt(wqt_ref[...], xn)
    for h in range(N_HEADS_A):
        qh = qt[h * LANES:(h + 1) * LANES, :]
        ms = jnp.sum(qh * qh, axis=0, keepdims=True) * (1.0 / HEAD_DIM_A)
        q_out[0, h * LANES:(h + 1) * LANES, :] = (qh * lax.rsqrt(ms + EPS) * qg_ref[...]).astype(BF16)
    c = _dot(xn, wc_ref[...])
    cmp_rows[0] = c[:, 0:KV_W]
    cmp_rows[1] = c[:, KV_W:KV_ROW]
    kvt = _dot_nt(wkvt_ref[...], xn)
    cmpt_out[0] = kvt[0:KV_ROW]
    ks = _rms_two_groups_t(kvt[2 * KV_W:3 * KV_W], kgt_ref[0])
    vs = kvt[3 * KV_W:4 * KV_W]
    kw = _rms_two_groups_t(kvt[4 * KV_W:5 * KV_W], kgt_ref[1])
    vw = kvt[5 * KV_W:6 * KV_W]
    selt_out[0, 0:KV_W] = ks
    selt_out[0, KV_W:KV_ROW] = vs
    wint_out[0, 0:KV_W] = kw
    wint_out[0, KV_W:KV_ROW] = vw
    ones = jnp.ones((HEAD_DIM_A, LANES), F32)
    for j in range(kst.shape[1]):
        cols = slice(j * LANES, (j + 1) * LANES)
        kst[0, j] = ks[:, cols].T.astype(BF16)
        kwt[0, j] = kw[:, cols].T.astype(BF16)
        for v, out in ((vs, vst), (vw, vwt)):
            out[0, j, 0] = jnp.concatenate([v[0:HEAD_DIM_A, cols], ones], axis=0).astype(BF16)
            out[0, j, 1] = jnp.concatenate([ones, v[HEAD_DIM_A:KV_W, cols]], axis=0).astype(BF16)
    m = _dot(xn, wm_ref[...])
    qk_out[...] = m[:, 0:2 * D_M]
    v_out[...] = m[:, 2 * D_M:3 * D_M]
    o_out[...] = m[:, 3 * D_M:4 * D_M]
    small_out[...] = _dot(xn, ws_ref[...])
    gate_out[0] = _sigmoid(_dot_nt(wst_ref[...], xn))


def _inproj(x, seq_len, g, wqt, wc, wkvt, wm, ws, wst, qg, kgt):
    n = x.shape[0]
    nb = n // seq_len
    tm = _token_tile(seq_len)
    tpb = seq_len // tm
    row = lambda w: pl.BlockSpec((tm, w), lambda i: (i, 0))
    rows_out = lambda w, dt: (row(w), jax.ShapeDtypeStruct((n, w), dt))
    chan_out = lambda c, dt: (pl.BlockSpec((1, c, tm), lambda i: (i // tpb, 0, i % tpb)),
                              jax.ShapeDtypeStruct((nb, c, seq_len), dt))
    k_tiles = (pl.BlockSpec((1, tm // LANES, LANES, KV_W), lambda i: (i // tpb, i % tpb, 0, 0)),
               jax.ShapeDtypeStruct((nb, seq_len // LANES, LANES, KV_W), BF16))
    v_tiles = (pl.BlockSpec((1, tm // LANES, N_KV_A, KV_W, LANES), lambda i: (i // tpb, i % tpb, 0, 0, 0)),
               jax.ShapeDtypeStruct((nb, seq_len // LANES, N_KV_A, KV_W, LANES), BF16))
    halves_out = (pl.BlockSpec((2, tm, KV_W), lambda i: (0, i, 0)), jax.ShapeDtypeStruct((2, n, KV_W), F32))
    outs = [chan_out(N_HEADS_A * LANES, BF16), halves_out, chan_out(KV_ROW, F32), chan_out(KV_ROW, F32),
            chan_out(KV_ROW, F32), k_tiles, v_tiles, k_tiles, v_tiles,
            rows_out(2 * D_M, F32), rows_out(D_M, F32), rows_out(D_M, F32), rows_out(LANES, F32),
            chan_out(LANES, F32)]
    consts = (g, wqt, wc, wkvt, wm, ws, wst, qg, kgt)
    return pl.pallas_call(
        _inproj_kernel,
        grid=(n // tm,),
        in_specs=[row(D_MODEL)] + [_const_spec(a.shape) for a in consts],
        out_specs=[o[0] for o in outs],
        out_shape=[o[1] for o in outs],
        compiler_params=_params("parallel"),
        name="in_proj",
    )(x, *consts)


def _t5_bucket_np(dist):
    n = np.maximum(dist, 0)
    exact = N_BUCKETS // 2
    nf = np.maximum(n, 1).astype(np.float32)
    ratio = np.log(nf / np.float32(exact)) / np.float32(math.log(MAX_DISTANCE / exact))
    large = exact + (ratio * np.float32(N_BUCKETS - exact)).astype(np.int32)
    return np.where(n < exact, n, np.minimum(large, N_BUCKETS - 1)).astype(np.int32)


def _bias_kernel(tbl_ref, bkt_ref, out_ref):
    bkt = bkt_ref[0]
    for h in range(N_HEADS_A):
        far = tbl_ref[N_BUCKETS - 1, h]
        acc = jnp.zeros(bkt.shape, F32)
        for b in range(N_BUCKETS - 1):
            acc = jnp.where(bkt == b, tbl_ref[b, h] - far, acc)
        out_ref[0, h] = acc


def _bias_tiles(rel_bias, buckets):
    n = buckets.shape[0]
    return pl.pallas_call(
        _bias_kernel,
        grid=(n,),
        in_specs=[pl.BlockSpec(memory_space=pltpu.SMEM),
                  pl.BlockSpec((1, LANES, LANES), lambda i: (i, 0, 0))],
        out_specs=pl.BlockSpec((1, N_HEADS_A, LANES, LANES), lambda i: (i, 0, 0, 0)),
        out_shape=jax.ShapeDtypeStruct((n, N_HEADS_A, LANES, LANES), F32),
        compiler_params=_params("parallel"),
        name="bias_tiles",
    )(rel_bias, buckets)


def _compress_half(rows_ref, nc, w_ref, half, bias):
    fs = None
    for t in range(CMP_STRIDE):
        part = _dot(rows_ref[pl.ds(t, nc, stride=CMP_STRIDE), :].astype(BF16), w_ref[half, t])
        fs = part if fs is None else fs + part
    return fs[:, 0:KV_W] + pltpu.roll(fs[:, KV_W:KV_ROW], nc - 1, 0) + bias


def _compress_rows(k_rows_ref, v_rows_ref, nc, w_ref, b_ref, g_ref):
    kc = _rms_two_groups(_compress_half(k_rows_ref, nc, w_ref, 0, b_ref[:, 0:KV_W]), g_ref[...])
    vc = _compress_half(v_rows_ref, nc, w_ref, 1, b_ref[:, KV_W:KV_ROW])
    complete = lax.broadcasted_iota(jnp.int32, kc.shape, 0) < nc - 1
    return jnp.where(complete, kc, 0.0), jnp.where(complete, vc, 0.0)


def _compress_kernel(x_ref, w_ref, b_ref, g_ref, kc_ref, vc_ref):
    nc = kc_ref.shape[1]
    kc, vc = _compress_rows(x_ref.at[0], x_ref.at[1], nc, w_ref, b_ref, g_ref)
    kc_ref[0] = kc.astype(BF16)
    vc_ref[0] = vc.T.astype(BF16)


def _compress(cmp_rows, t, w_big, b_big, g_k):
    nb = cmp_rows.shape[1] // t
    nc = t // CMP_STRIDE
    return pl.pallas_call(
        _compress_kernel,
        grid=(nb,),
        in_specs=[pl.BlockSpec((2, t, KV_W), lambda b: (0, b, 0)), _const_spec(w_big.shape),
                  _const_spec(b_big.shape), _const_spec(g_k.shape)],
        out_specs=[pl.BlockSpec((1, nc, KV_W), lambda b: (b, 0, 0)),
                   pl.BlockSpec((1, KV_W, nc), lambda b: (b, 0, 0))],
        out_shape=[jax.ShapeDtypeStruct((nb, nc, KV_W), BF16), jax.ShapeDtypeStruct((nb, KV_W, nc), BF16)],
        compiler_params=_params("parallel"),
        name="compress_prompt",
    )(cmp_rows, w_big, b_big, g_k)


def _tile_heads(x):
    return jnp.concatenate([x] * GROUP_R, axis=1)


def _softmax_part(m, s):
    m_new = jnp.maximum(m, jnp.max(s, axis=0, keepdims=True))
    return m_new, jnp.exp(m - m_new), jnp.exp(s - m_new).astype(BF16)


def _accumulate(acc, soft, vt):
    m_new, alpha, p = soft
    return m_new, alpha * acc + _dot(vt, p)


def _key_rows(ref, t, n):
    return ref[0, pl.ds(t, n)].reshape(n * LANES, ref.shape[-1])


def _value_cols(ref, t, n, g):
    return jnp.concatenate([ref[0, t + k, g] for k in range(n)], axis=1)


def _rank_rows(score):
    n_blk = score.shape[0]
    blk = lax.broadcasted_iota(jnp.int32, score.shape, 0)
    rank = jnp.zeros(score.shape, F32)
    for b in range(n_blk):
        row = score[b:b + 1, :]
        rank = rank + jnp.where(blk > b, jnp.where(row >= score, 1.0, 0.0), jnp.where(row > score, 1.0, 0.0))
    return rank


def _nsa_prompt_kernel(q_ref, gate_ref, kc_ref, vct_ref, ks_ref, vs_ref, kw_ref, vw_ref, sel_add_ref, win_add_ref,
                       bcmp_ref, covt_ref, expt_ref, a_ref, mb_ref, *, n_cmp):
    ii = pl.program_id(1)
    n_tiles = mb_ref.shape[2]
    n_sel = covt_ref.shape[0]
    wide = GROUP_R * Q_TILE
    chan = lax.broadcasted_iota(jnp.int32, (LANES, wide), 0)
    heads_of = lambda g: range(g * GROUP_R, (g + 1) * GROUP_R)
    chains = [(sb, g) for sb in range(Q_BLOCKS) for g in range(N_KV_A)]
    blk_i = [Q_BLOCKS * ii + sb for sb in range(Q_BLOCKS)]
    cols = lambda sb: slice(sb * Q_TILE, (sb + 1) * Q_TILE)
    qg_t = {(sb, g): jnp.concatenate([q_ref[0, h * LANES:(h + 1) * LANES, cols(sb)] for h in heads_of(g)], axis=1)
            for sb, g in chains}
    init = (jnp.full((1, wide), NEG, F32), jnp.zeros((LANES, wide), F32))

    n_win = WINDOW // LANES + 1
    t0 = [jnp.maximum(i - (n_win - 1), 0) for i in blk_i]
    s_win = {}
    for sb, i in enumerate(blk_i):
        win_off = pl.multiple_of(jnp.maximum(n_win - 1 - i, 0) * LANES, LANES)
        k_win = _key_rows(kw_ref, t0[sb], n_win)
        for g in range(N_KV_A):
            s_win[sb, g] = _dot(k_win, qg_t[sb, g]) + win_add_ref[g, pl.ds(win_off, n_win * LANES), :]

    ncp = kc_ref.shape[1]
    j_sub = lax.broadcasted_iota(jnp.int32, (ncp, wide), 0)
    q_in_tile = lax.broadcasted_iota(jnp.int32, (ncp, wide), 1) & (Q_TILE - 1)
    s_cmp, any_usable = {}, []
    for sb, i in enumerate(blk_i):
        q_lane = i * Q_TILE + q_in_tile
        usable_add = jnp.where((j_sub * CMP_STRIDE + (CMP_LEN - 1) <= q_lane) & (j_sub < n_cmp), 0.0, NEG)
        any_usable.append(jnp.where(q_lane[0:1, :] >= CMP_LEN - 1, 1.0, 0.0))
        for g in range(N_KV_A):
            s_cmp[sb, g] = _dot(kc_ref[0], qg_t[sb, g]) + bcmp_ref[sb, g] + usable_add
    p_c = {}
    for sb, g in chains:
        e = jnp.exp(s_cmp[sb, g] - jnp.max(s_cmp[sb, g], axis=0, keepdims=True)) * any_usable[sb]
        p_c[sb, g] = e / jnp.maximum(jnp.sum(e, axis=0, keepdims=True), 1e-30)
    oc_t = {c: _dot(vct_ref[0], p_c[c].astype(BF16)) for c in chains}

    blk = lax.broadcasted_iota(jnp.int32, (n_sel, Q_TILE), 0)
    q_in_blk = lax.broadcasted_iota(jnp.int32, (n_sel, Q_TILE), 1)
    ones_row = jnp.where(lax.broadcasted_iota(jnp.int32, (MASK_PAD, Q_TILE), 0) == 0, 1.0, 0.0)
    for sb, g in chains:
        qpos = blk_i[sb] * Q_TILE + q_in_blk
        cur = qpos // SEL_BLOCK
        valid = blk * SEL_BLOCK <= qpos
        forced = valid & ((blk == 0) | (blk == cur) | (blk == cur - 1))
        p_sum = p_c[sb, g][:, 0:Q_TILE]
        for r in range(1, GROUP_R):
            p_sum = p_sum + p_c[sb, g][:, r * Q_TILE:(r + 1) * Q_TILE]
        imp_t = _dot_exact_lhs(covt_ref[...], p_sum)
        score = jnp.where(forced, -NEG, jnp.where(valid, imp_t, NEG))
        chosen = jnp.where(_rank_rows(score) < float(min(TOP_K_BLOCKS, n_sel)), 1.0, 0.0)
        mask_all = _dot(expt_ref[...], jnp.concatenate([chosen, ones_row], axis=0).astype(BF16))
        for t in range(n_tiles):
            mb_ref[sb, g, t] = mask_all[t * LANES:(t + 1) * LANES, :]

    soft_win = {c: _softmax_part(init[0], s_win[c]) for c in chains}
    ow_acc = {(sb, g): _dot(_value_cols(vw_ref, t0[sb], n_win, g), soft_win[sb, g][2]) for sb, g in chains}

    tile_in_iter = lax.broadcasted_iota(jnp.int32, (FAR_TILES * LANES, Q_TILE), 0) // LANES

    def far_body(it, carry):
        t = FAR_TILES * it
        k_far = _key_rows(ks_ref, t, FAR_TILES)
        s = {}
        for sb, i in enumerate(blk_i):
            past_far = jnp.where(t + tile_in_iter <= i - 2, 0.0, NEG)
            for g in range(N_KV_A):
                add = _tile_heads(mb_ref[sb, g, pl.ds(t, FAR_TILES)].reshape(FAR_TILES * LANES, Q_TILE) + past_far)
                s[sb, g] = _dot(k_far, qg_t[sb, g]) + add
        soft = [_softmax_part(carry[n][0], s[c]) for n, c in enumerate(chains)]
        return tuple(_accumulate(carry[n][1], soft[n], _value_cols(vs_ref, t, FAR_TILES, c[1]))
                     for n, c in enumerate(chains))

    n_far = blk_i[-1] - 1
    carry = lax.fori_loop(0, (n_far + FAR_TILES - 1) // FAR_TILES, far_body, (init,) * len(chains))

    t1 = [jnp.maximum(i - 1, 0) for i in blk_i]
    s_near = {}
    for sb, i in enumerate(blk_i):
        k_near = _key_rows(ks_ref, t1[sb], 2)
        near_off = pl.multiple_of(jnp.where(i == 0, LANES, 0), LANES)
        for g in range(N_KV_A):
            s_near[sb, g] = _dot(k_near, qg_t[sb, g]) + sel_add_ref[g, pl.ds(near_off, 2 * LANES), :] + _tile_heads(
                mb_ref[sb, g, pl.ds(t1[sb], 2)].reshape(2 * LANES, Q_TILE))
    soft_near = [_softmax_part(carry[n][0], s_near[c]) for n, c in enumerate(chains)]
    for n, (sb, g) in enumerate(chains):
        denom = (1 - g) * HEAD_DIM_A
        _, acc = _accumulate(carry[n][1], soft_near[n], _value_cols(vs_ref, t1[sb], 2, g))
        os_t = acc / acc[denom:denom + 1, :]
        ow_t = ow_acc[sb, g] / ow_acc[sb, g][denom:denom + 1, :]

        heads = heads_of(g)
        gate_row = lambda br: jnp.concatenate(
            [gate_ref[0, br * N_HEADS_A + h:br * N_HEADS_A + h + 1, cols(sb)] for h in heads], axis=1)
        out_t = gate_row(0) * oc_t[sb, g] + gate_row(1) * os_t + gate_row(2) * ow_t
        own_rows = (chan >= g * HEAD_DIM_A) & (chan < (g + 1) * HEAD_DIM_A)
        out_t = jnp.where(own_rows, out_t, 0.0)
        for r, h in enumerate(heads):
            a_ref[0, cols(sb), h * LANES:(h + 1) * LANES] = out_t[:, r * Q_TILE:(r + 1) * Q_TILE].T.astype(BF16)


def _nsa_prompt(q_t, gate_t, kc, vc_t, ks, vs, kw, vw, sel_add, win_add, bcmp, cov_t, expand_t, n_cmp):
    nb, _, t = q_t.shape
    nqb = t // Q_TILE
    assert nqb > WINDOW // LANES and nqb % Q_BLOCKS == 0
    step = Q_BLOCKS * Q_TILE
    seq = lambda a: pl.BlockSpec((1,) + a.shape[1:], lambda b, i: (b,) + (0,) * (a.ndim - 1))
    return pl.pallas_call(
        functools.partial(_nsa_prompt_kernel, n_cmp=n_cmp),
        grid=(nb, nqb // Q_BLOCKS),
        in_specs=[pl.BlockSpec((1, N_HEADS_A * LANES, step), lambda b, i: (b, 0, i)),
                  pl.BlockSpec((1, LANES, step), lambda b, i: (b, 0, i)),
                  seq(kc), seq(vc_t), seq(ks), seq(vs), seq(kw), seq(vw),
                  _const_spec(sel_add.shape), _const_spec(win_add.shape),
                  pl.BlockSpec((Q_BLOCKS,) + bcmp.shape[1:], lambda b, i: (i, 0, 0, 0)),
                  _const_spec(cov_t.shape), _const_spec(expand_t.shape)],
        out_specs=pl.BlockSpec((1, step, N_HEADS_A * LANES), lambda b, i: (b, i, 0)),
        out_shape=jax.ShapeDtypeStruct((nb, t, N_HEADS_A * LANES), BF16),
        scratch_shapes=[pltpu.VMEM((Q_BLOCKS, N_KV_A, t // LANES, LANES, Q_TILE), F32)],
        compiler_params=_params("parallel", "arbitrary"),
        name="nsa_prompt",
    )(q_t, gate_t, kc, vc_t, ks, vs, kw, vw, sel_add, win_add, bcmp, cov_t, expand_t)


def _head_norm_gate(h_all, mn_ref, o):
    outs = []
    for h in range(N_HEADS_M):
        hs = slice(h * HEAD_DIM_M, (h + 1) * HEAD_DIM_M)
        outs.append(_rms_rows(h_all[h], mn_ref[:, hs]))
    return _sigmoid(o) * jnp.concatenate(outs, axis=-1)


def _mlstm_prompt_kernel(qk_ref, v_ref, o_ref, small_ref, cw_ref, cb_ref, bif_ref, mn_ref, tri_ref, triu_ref,
                         y_ref, c_ref, n_ref, m_ref, xbuf):
    chunk = qk_ref.shape[1]
    pad = 8

    @pl.when(pl.program_id(1) == 0)
    def _():
        xbuf[0:pad, :] = jnp.zeros((pad, 2 * D_M), F32)
        c_ref[...] = jnp.zeros_like(c_ref)
        n_ref[...] = jnp.zeros_like(n_ref)
        m_ref[...] = jnp.zeros_like(m_ref)

    x = qk_ref[0]
    xbuf[pad:pad + chunk, :] = x
    y = xbuf[pad - 3:pad - 3 + chunk, :] * cw_ref[0:1, :]
    y = y + xbuf[pad - 2:pad - 2 + chunk, :] * cw_ref[1:2, :]
    y = y + xbuf[pad - 1:pad - 1 + chunk, :] * cw_ref[2:3, :]
    y = y + x * cw_ref[3:4, :] + cb_ref[...]
    qkc = y * _sigmoid(y)
    xbuf[0:pad, :] = xbuf[chunk:chunk + pad, :]

    gi = small_ref[0] + bif_ref[...]
    gi_t = gi.T
    b_col = _dot_exact_lhs(tri_ref[...], _log_sigmoid(gi))
    b_row = _dot_exact_rhs(_log_sigmoid(gi_t), triu_ref[...])
    t_col = lax.broadcasted_iota(jnp.int32, (chunk, chunk), 0)
    s_row = lax.broadcasted_iota(jnp.int32, (chunk, chunk), 1)
    causal = s_row <= t_col
    m_all = m_ref[0]
    lane_row = lax.broadcasted_iota(jnp.int32, m_all.shape, 1)

    h_all = []
    for h in range(N_HEADS_M):
        hs = slice(h * HEAD_DIM_M, (h + 1) * HEAD_DIM_M)
        ks = slice(D_M + h * HEAD_DIM_M, D_M + (h + 1) * HEAD_DIM_M)
        q = qkc[:, hs]
        k = qkc[:, ks] * F32(HEAD_DIM_M ** -0.5)
        v = v_ref[0, :, hs]
        qb, kb = q.astype(BF16), k.astype(BF16)
        ci, fi = 3 * N_HEADS_A + h, 3 * N_HEADS_A + N_HEADS_M + h
        bt = b_col[:, fi:fi + 1]
        bs = b_row[fi:fi + 1, :]
        ig_row = gi_t[ci:ci + 1, :]
        ig_col = gi[:, ci:ci + 1]
        m_prev = m_all[:, h:h + 1]
        c_prev = c_ref[0, h]
        n_prev = n_ref[0, h:h + 1, :]

        dlog = jnp.where(causal, bt - bs + ig_row, NEG)
        inter = bt + m_prev
        m_t = jnp.maximum(inter, jnp.max(dlog, axis=-1, keepdims=True))
        s_qk = _dot_nt(qb, kb) * jnp.exp(dlog - m_t)
        dec = jnp.exp(inter - m_t)
        num = _dot(s_qk.astype(BF16), v.astype(BF16)) + dec * _dot_nt(qb, c_prev.astype(BF16))
        den = jnp.sum(s_qk, axis=-1, keepdims=True) + dec * jnp.sum(q * n_prev, axis=-1, keepdims=True)
        h_all.append(num / jnp.maximum(jnp.abs(den), jnp.exp(-m_t)))

        m_new = m_t[chunk - 1:chunk, :]
        b_end = bt[chunk - 1:chunk, :]
        w_end = jnp.exp(b_end - bt + ig_col - m_new)
        dec_end = jnp.exp(b_end + m_prev - m_new)
        wv_t = (w_end * v).T.astype(BF16)
        c_ref[0, h] = dec_end * c_prev + _dot(wv_t, kb)
        n_ref[0, h:h + 1, :] = dec_end * n_prev + jnp.sum(w_end * k, axis=0, keepdims=True)
        m_all = jnp.where(lane_row == h, m_new, m_all)

    m_ref[0] = m_all
    y_ref[0] = _head_norm_gate(h_all, mn_ref, o_ref[0]).astype(BF16)


def _mlstm_prompt(qk, v, o, small, conv_w, conv_b, bif, mn_g, chunk):
    nb, t, _ = qk.shape
    tri = jnp.asarray(np.tril(np.ones((chunk, chunk), np.float32)), BF16)
    triu = jnp.asarray(np.triu(np.ones((chunk, chunk), np.float32)), BF16)
    tok = lambda w: pl.BlockSpec((1, chunk, w), lambda b, c: (b, c, 0))
    return pl.pallas_call(
        _mlstm_prompt_kernel,
        grid=(nb, t // chunk),
        in_specs=[tok(2 * D_M), tok(D_M), tok(D_M), tok(LANES)]
        + [_const_spec(a.shape) for a in (conv_w, conv_b, bif, mn_g, tri, triu)],
        out_specs=[tok(D_M),
                   pl.BlockSpec((1, N_HEADS_M, HEAD_DIM_M, HEAD_DIM_M), lambda b, c: (b, 0, 0, 0)),
                   pl.BlockSpec((1, N_HEADS_M, HEAD_DIM_M), lambda b, c: (b, 0, 0)),
                   pl.BlockSpec((1, 1, LANES), lambda b, c: (b, 0, 0))],
        out_shape=[jax.ShapeDtypeStruct((nb, t, D_M), BF16),
                   jax.ShapeDtypeStruct((nb, N_HEADS_M, HEAD_DIM_M, HEAD_DIM_M), F32),
                   jax.ShapeDtypeStruct((nb, N_HEADS_M, HEAD_DIM_M), F32),
                   jax.ShapeDtypeStruct((nb, 1, LANES), F32)],
        scratch_shapes=[pltpu.VMEM((chunk + 8, 2 * D_M), F32)],
        compiler_params=_params("parallel", "arbitrary"),
        name="mlstm_prompt",
    )(qk, v, o, small, conv_w, conv_b, bif, mn_g, tri, triu)


def _softmax_rows(s):
    e = jnp.exp(s - jnp.max(s, axis=-1, keepdims=True))
    return e, jnp.sum(e, axis=-1, keepdims=True)


def _sample_cmp_kernel(pt_ref, *refs, n_pages, n_sel, qpos):
    pages = refs[:n_pages]
    w_ref, b_ref, g_ref, q_ref, bias_ref, cov_ref, oc_ref, idx_ref, xk_ref, xv_ref = refs[n_pages:]
    for half, rows_ref in ((0, xk_ref), (1, xv_ref)):
        for u in range(n_pages):
            rows_ref[u * PAGE_SIZE:(u + 1) * PAGE_SIZE, :] = pages[u][0, half * KV_W:(half + 1) * KV_W, :].T
    nc = n_pages * PAGE_SIZE // CMP_STRIDE
    kc, vc = _compress_rows(xk_ref, xv_ref, nc, w_ref, b_ref, g_ref)
    q = q_ref[0]
    j_row = lax.broadcasted_iota(jnp.int32, (N_HEADS_A, nc), 1)
    s = _dot_nt(q, kc.astype(BF16)) + bias_ref[...] + jnp.where(j_row < nc - 1, 0.0, NEG)
    e, l = _softmax_rows(s)
    p = e / jnp.maximum(l, 1e-30)
    oc_ref[0] = _dot(p.astype(BF16), vc.astype(BF16))

    nsp = cov_ref.shape[1]
    blk_row = lax.broadcasted_iota(jnp.int32, (1, nsp), 1)
    sub = lax.broadcasted_iota(jnp.int32, (nsp, nsp), 0)
    lan = lax.broadcasted_iota(jnp.int32, (nsp, nsp), 1)
    cur = qpos // SEL_BLOCK
    valid = (blk_row * SEL_BLOCK <= qpos) & (blk_row < n_sel)
    forced = valid & ((blk_row == 0) | (blk_row == cur) | (blk_row == cur - 1))
    k_col = lax.broadcasted_iota(jnp.int32, (TOP_K_BLOCKS, nsp), 0).astype(F32)
    blk_f = lax.broadcasted_iota(jnp.int32, (TOP_K_BLOCKS, nsp), 1).astype(F32)
    for g in range(N_KV_A):
        p_sum = jnp.sum(p[g * GROUP_R:(g + 1) * GROUP_R, :], axis=0, keepdims=True)
        imp = _dot_exact_rhs(jnp.broadcast_to(p_sum, (8, nc)), cov_ref[...])[0:1, :]
        score = jnp.where(forced, -NEG, jnp.where(valid, imp, NEG))
        score_col = jnp.sum(jnp.where(sub == lan, score, 0.0), axis=-1, keepdims=True)
        ahead = jnp.where(sub < lan, jnp.where(score_col >= score, 1.0, 0.0), jnp.where(score_col > score, 1.0, 0.0))
        rank = jnp.sum(ahead, axis=0, keepdims=True)
        chosen = jnp.sum(jnp.where(rank == k_col, blk_f, 0.0), axis=-1, keepdims=True)
        idx_ref[0, g * TOP_K_BLOCKS:(g + 1) * TOP_K_BLOCKS, :] = jnp.broadcast_to(
            chosen, (TOP_K_BLOCKS, LANES)).astype(jnp.int32)


def _sample_cmp(page_table, pool, w_big, b_big, g_k, q3, bias_c, cover, n_sel, qpos):
    ns, n_pages = page_table.shape
    page_spec = lambda u: pl.BlockSpec((1, KV_ROW, PAGE_SIZE), lambda s, pt: (pt[s, u], 0, 0))
    const = lambda a: pl.BlockSpec(a.shape, lambda s, pt: (0,) * a.ndim, pipeline_mode=pl.Buffered(1))
    grid_spec = pltpu.PrefetchScalarGridSpec(
        num_scalar_prefetch=1,
        grid=(ns,),
        in_specs=[page_spec(u) for u in range(n_pages)]
        + [const(w_big), const(b_big), const(g_k),
           pl.BlockSpec((1, N_HEADS_A, LANES), lambda s, pt: (s, 0, 0)), const(bias_c), const(cover)],
        out_specs=[pl.BlockSpec((1, N_HEADS_A, LANES), lambda s, pt: (s, 0, 0)),
                   pl.BlockSpec((1, N_KV_A * TOP_K_BLOCKS, LANES), lambda s, pt: (s, 0, 0))],
        scratch_shapes=[pltpu.VMEM((n_pages * PAGE_SIZE, KV_W), F32)] * 2,
    )
    return pl.pallas_call(
        functools.partial(_sample_cmp_kernel, n_pages=n_pages, n_sel=n_sel, qpos=qpos),
        grid_spec=grid_spec,
        out_shape=[jax.ShapeDtypeStruct((ns, N_HEADS_A, LANES), F32),
                   jax.ShapeDtypeStruct((ns, N_KV_A * TOP_K_BLOCKS, LANES), jnp.int32)],
        compiler_params=_params("arbitrary"),
        name="sample_cmp_topk",
    )(page_table, *([pool] * n_pages), w_big, b_big, g_k, q3, bias_c, cover)


def _sample_attn_kernel(idx_ref, pt_ref, *refs, n_past_blocks):
    n_slots = N_KV_A * TOP_K_BLOCKS
    blocks = refs[:n_slots]
    (win_ref, q_ref, selnew_ref, winnew_ref, wint_ref, oc_ref, gate_ref, bsel_ref, bwin_ref, b0_ref,
     a_ref, winout_ref) = refs[n_slots:]
    s_id = pl.program_id(0)
    q = q_ref[0]
    qf = q.astype(F32)
    lane = lax.broadcasted_iota(jnp.int32, (N_HEADS_A, LANES), 1)
    row = lax.broadcasted_iota(jnp.int32, (N_HEADS_A, LANES), 0)
    lo = lane < SEL_BLOCK

    def with_new_key(s, vt_mat, new_row):
        s_new = jnp.sum(qf * new_row[:, 0:KV_W], axis=-1, keepdims=True) + b0_ref[:, 0:1]
        m = jnp.maximum(jnp.max(s, axis=-1, keepdims=True), s_new)
        p = jnp.exp(s - m)
        p_new = jnp.exp(s_new - m)
        l = jnp.sum(p, axis=-1, keepdims=True) + p_new
        return (_dot_nt(p.astype(BF16), vt_mat) + p_new * new_row[:, KV_W:KV_ROW]) / l

    sel_new = selnew_ref[0]
    o_s = []
    for g in range(N_KV_A):
        slots = range(g * TOP_K_BLOCKS, (g + 1) * TOP_K_BLOCKS)
        kt_cat = jnp.concatenate([blocks[k][0, 0:KV_W, :] for k in slots], axis=1).astype(BF16)
        vt_cat = jnp.concatenate([blocks[k][0, KV_W:KV_ROW, :] for k in slots], axis=1).astype(BF16)
        adds = []
        for k in slots:
            b = idx_ref[s_id * n_slots + k]
            add = jnp.where(b == n_past_blocks - 2, bsel_ref[0], jnp.where(b == n_past_blocks - 1, bsel_ref[1], 0.0))
            other_half = jnp.where(b % 2 == 0, jnp.where(lo, 0.0, NEG), jnp.where(lo, NEG, 0.0))
            adds.append(add + other_half + jnp.where(b >= n_past_blocks, NEG, 0.0))
        o_s.append(with_new_key(_dot(q, kt_cat) + jnp.concatenate(adds, axis=-1), vt_cat, sel_new))
    o_sel = jnp.where(row < GROUP_R, o_s[0], o_s[1])

    win = win_ref[0]
    wb = win.shape[1]
    w_lane = lax.broadcasted_iota(jnp.int32, (N_HEADS_A, wb), 1)
    s_w = _dot(q, win[0:KV_W, :].astype(BF16)) + bwin_ref[...] + jnp.where(w_lane == 0, NEG, 0.0)
    o_win = with_new_key(s_w, win[KV_W:KV_ROW, :].astype(BF16), winnew_ref[0])

    na = N_HEADS_A
    out = gate_ref[0, 0:na] * oc_ref[0] + gate_ref[0, na:2 * na] * o_sel + gate_ref[0, 2 * na:3 * na] * o_win
    a_ref[0] = jnp.where(lo == (row < GROUP_R), out, 0.0).astype(BF16)

    wint = wint_ref[...]
    seq_lane = lax.broadcasted_iota(jnp.int32, wint.shape, 1)
    new_col = jnp.sum(jnp.where(seq_lane == s_id, wint, 0.0), axis=-1, keepdims=True)
    buf_lane = lax.broadcasted_iota(jnp.int32, win.shape, 1)
    winout_ref[0] = jnp.where(buf_lane == wb - 1, new_col, pltpu.roll(win, wb - 1, 1))


def _sample_attn(idx, page_table, pool, win_t, q3, sel_new, win_new, win_new_t, o_c, gate_b, bsel, bwin, b0):
    ns, n_pages = page_table.shape
    n_slots = N_KV_A * TOP_K_BLOCKS
    per_page = PAGE_SIZE // SEL_BLOCK
    n_past_blocks = n_pages * per_page
    wb = win_t.shape[2]

    def slot_spec(k):
        def index(s, idx_ref, pt_ref):
            b = jnp.minimum(idx_ref[s * n_slots + k], n_past_blocks - 1)
            return (pt_ref[s, b // per_page], 0, 0)
        return pl.BlockSpec((1, KV_ROW, PAGE_SIZE), index)

    per_seq = lambda a: pl.BlockSpec((1,) + a.shape[1:], lambda s, *_: (s,) + (0,) * (a.ndim - 1))
    const = lambda a: pl.BlockSpec(a.shape, lambda s, *_: (0,) * a.ndim, pipeline_mode=pl.Buffered(1))
    grid_spec = pltpu.PrefetchScalarGridSpec(
        num_scalar_prefetch=2,
        grid=(ns,),
        in_specs=[slot_spec(k) for k in range(n_slots)]
        + [per_seq(win_t), per_seq(q3), per_seq(sel_new), per_seq(win_new), const(win_new_t), per_seq(o_c),
           per_seq(gate_b), const(bsel), const(bwin), const(b0)],
        out_specs=[pl.BlockSpec((1, N_HEADS_A, LANES), lambda s, *_: (s, 0, 0)),
                   pl.BlockSpec((1, KV_ROW, wb), lambda s, *_: (s, 0, 0))],
    )
    return pl.pallas_call(
        functools.partial(_sample_attn_kernel, n_past_blocks=n_past_blocks),
        grid_spec=grid_spec,
        out_shape=[jax.ShapeDtypeStruct((ns, N_HEADS_A, LANES), BF16),
                   jax.ShapeDtypeStruct((ns, KV_ROW, wb), F32)],
        compiler_params=_params("arbitrary"),
        name="sample_sel_win",
    )(idx, page_table, *([pool] * n_slots), win_t, q3, sel_new, win_new, win_new_t, o_c, gate_b, bsel, bwin, b0)


def _mlstm_step_kernel(qk_ref, conv_ref, v_ref, o_ref, small_ref, c_ref, n_ref, m_ref, cw_ref, cb_ref, bif_ref,
                       mn_ref, y_ref, c_out, n_out, m_out):
    seqs = qk_ref.shape[0]
    sub = lax.broadcasted_iota(jnp.int32, (HEAD_DIM_M, HEAD_DIM_M), 0)
    lan = lax.broadcasted_iota(jnp.int32, (HEAD_DIM_M, HEAD_DIM_M), 1)
    eye = sub == lan
    lane_row = lax.broadcasted_iota(jnp.int32, (1, LANES), 1)
    for s in range(seqs):
        hist = conv_ref[s]
        y = hist[0:1, :] * cw_ref[0:1, :]
        y = y + hist[1:2, :] * cw_ref[1:2, :]
        y = y + hist[2:3, :] * cw_ref[2:3, :]
        y = y + qk_ref[s] * cw_ref[3:4, :] + cb_ref[...]
        qkc = y * _sigmoid(y)
        gi = small_ref[s] + bif_ref[...]
        lf_all = _log_sigmoid(gi)
        m_all = m_ref[s]
        m_new_all = m_all
        h_all = []
        for h in range(N_HEADS_M):
            hs = slice(h * HEAD_DIM_M, (h + 1) * HEAD_DIM_M)
            ks = slice(D_M + h * HEAD_DIM_M, D_M + (h + 1) * HEAD_DIM_M)
            q = qkc[:, hs]
            k = qkc[:, ks] * F32(HEAD_DIM_M ** -0.5)
            v = v_ref[s][:, hs]
            ci, fi = 3 * N_HEADS_A + h, 3 * N_HEADS_A + N_HEADS_M + h
            ig = gi[:, ci:ci + 1]
            lf = lf_all[:, fi:fi + 1]
            m_prev = m_all[:, h:h + 1]
            c_prev = c_ref[s, h]
            n_prev = n_ref[s, h:h + 1, :]
            inter = lf + m_prev
            m_t = jnp.maximum(inter, ig)
            w_in = jnp.exp(ig - m_t)
            dec = jnp.exp(inter - m_t)
            s_qk = jnp.sum(q * k, axis=-1, keepdims=True) * w_in
            cq = _dot_nt(jnp.broadcast_to(q, (8, HEAD_DIM_M)).astype(BF16), c_prev.astype(BF16))[0:1, :]
            num = s_qk * v + dec * cq
            den = s_qk + dec * jnp.sum(n_prev * q, axis=-1, keepdims=True)
            h_all.append(num / jnp.maximum(jnp.abs(den), jnp.exp(-m_t)))
            v_col = jnp.sum(jnp.where(eye, v, 0.0), axis=-1, keepdims=True)
            c_out[s, h] = dec * c_prev + (w_in * v_col) * k
            n_out[s, h:h + 1, :] = dec * n_prev + w_in * k
            m_new_all = jnp.where(lane_row == h, m_t, m_new_all)
        m_out[s] = m_new_all
        y_ref[s] = _head_norm_gate(h_all, mn_ref, o_ref[s]).astype(BF16)


def _mlstm_step(qk, conv0, v, o, small, c0, n0, m0, conv_w, conv_b, bif, mn_g):
    ns = qk.shape[0]
    sb = 8 if ns % 8 == 0 else 1
    per = lambda a: pl.BlockSpec((sb,) + a.shape[1:], lambda i: (i,) + (0,) * (a.ndim - 1))
    ins = (qk, conv0, v, o, small, c0, n0, m0)
    outs = [jax.ShapeDtypeStruct((ns, 1, D_M), BF16), jax.ShapeDtypeStruct(c0.shape, F32),
            jax.ShapeDtypeStruct(n0.shape, F32), jax.ShapeDtypeStruct(m0.shape, F32)]
    return pl.pallas_call(
        _mlstm_step_kernel,
        grid=(ns // sb,),
        in_specs=[per(a) for a in ins] + [_const_spec(a.shape) for a in (conv_w, conv_b, bif, mn_g)],
        out_specs=[per(a) for a in outs],
        out_shape=outs,
        compiler_params=_params("parallel"),
        name="mlstm_step",
    )(*ins, conv_w, conv_b, bif, mn_g)


def _cover_np(n_cmp, n_sel):
    cs = np.arange(n_cmp)[:, None] * CMP_STRIDE
    bs = np.arange(n_sel)[None, :] * SEL_BLOCK
    shared = np.clip(np.minimum(cs + CMP_LEN, bs + SEL_BLOCK) - np.maximum(cs, bs), 0, None)
    return (shared / CMP_LEN).astype(np.float32)


def _prep_in_proj(w_in):
    d = w_in.shape[0]
    wq = w_in[:, :D_A].reshape(d, N_HEADS_A, HEAD_DIM_A)
    z = jnp.zeros_like(wq)
    g0 = jnp.concatenate([wq[:, :GROUP_R], z[:, :GROUP_R]], axis=-1)
    g1 = jnp.concatenate([z[:, GROUP_R:], wq[:, GROUP_R:]], axis=-1)
    wq_pad = jnp.concatenate([g0, g1], axis=1).reshape(d, N_HEADS_A * LANES)
    o_kv = D_A
    o_gate = o_kv + 6 * KV_W
    o_m = o_gate + 3 * N_HEADS_A
    o_if = o_m + 4 * D_M
    w_cmp = w_in[:, o_kv:o_kv + KV_ROW]
    w_kv_t = w_in[:, o_kv:o_gate].T
    w_m = w_in[:, o_m:o_if]
    w_small = jnp.concatenate([w_in[:, o_gate:o_m], w_in[:, o_if:],
                               jnp.zeros((d, LANES - 3 * N_HEADS_A - 2 * N_HEADS_M), w_in.dtype)], axis=1)
    return [w.astype(BF16) for w in (wq_pad.T, w_cmp, w_kv_t, w_m, w_small, w_small.T)]


def _prep_compress(w_k, b_k, w_v, b_v):
    per_c = jnp.stack([w_k, w_k, w_v, w_v]).reshape(4, 2, CMP_STRIDE, HEAD_DIM_A, HEAD_DIM_A)
    w_big = jnp.einsum('chtde,cf->tcdhfe', per_c, jnp.eye(4, dtype=w_k.dtype))
    w_big = w_big.reshape(CMP_STRIDE, KV_ROW, 2 * KV_ROW).astype(BF16)
    pick = lambda h: jnp.concatenate([w_big[:, h * KV_W:(h + 1) * KV_W, h * KV_W:(h + 1) * KV_W],
                                      w_big[:, h * KV_W:(h + 1) * KV_W, KV_ROW + h * KV_W:KV_ROW + (h + 1) * KV_W]],
                                     axis=-1)
    w_big = jnp.stack([pick(0), pick(1)])
    b_big = jnp.concatenate([b_k, b_k, b_v, b_v]).reshape(1, KV_ROW)
    return w_big, b_big


def _prep_out_proj(w_out):
    wa = w_out[:D_A].reshape(N_HEADS_A, HEAD_DIM_A, -1)
    z = jnp.zeros_like(wa)
    g0 = jnp.concatenate([wa[:GROUP_R], z[:GROUP_R]], axis=1)
    g1 = jnp.concatenate([z[GROUP_R:], wa[GROUP_R:]], axis=1)
    woa = jnp.concatenate([g0, g1], axis=0).reshape(N_HEADS_A * LANES, -1)
    return woa.astype(BF16), w_out[D_A:].astype(BF16)


def _bucket_tiles(t, past, wb):
    r = np.arange(LANES)
    tiles = [_t5_bucket_np(d * LANES + r[None, :] - r[:, None]) for d in range(2)]
    for i in range(t // Q_TILE):
        tiles.append(_t5_bucket_np(i * Q_TILE + r[None, :] - (CMP_STRIDE * r[:, None] + CMP_LEN - 1)))
    flat = np.arange(LANES * LANES).reshape(LANES, LANES)
    tiles.append(_t5_bucket_np(past - (CMP_STRIDE * flat + CMP_LEN - 1)))
    tiles.append(_t5_bucket_np(wb - flat))
    tiles.append(_t5_bucket_np(past - (r[:, None] * SEL_BLOCK + r[None, :] % SEL_BLOCK)))
    return np.stack(tiles).astype(np.int32)


def _near_tables(bnear):
    r = np.arange(LANES)
    tile4 = lambda a: jnp.asarray(np.tile(a, (1, GROUP_R)), F32)
    causal = tile4(np.where(r[:, None] <= r[None, :], 0.0, NEG))
    oldest = tile4(np.where(r[:, None] > r[None, :], 0.0, NEG))
    zero = jnp.zeros_like(causal)
    masked = jnp.full_like(causal, NEG)
    n_mid = WINDOW // LANES - 2
    sel_add, win_add = [], []
    for g in range(N_KV_A):
        near = [bnear[1, g], bnear[0, g] + causal]
        sel_add.append(jnp.concatenate(near + [masked], axis=0))
        win_add.append(jnp.concatenate([oldest] + [zero] * n_mid + near + [masked] * (WINDOW // LANES), axis=0))
    return jnp.stack(sel_add), jnp.stack(win_add)


def kernel(x_prompt, x_sample, cache_cmp_kv, cache_sel_kv, cache_win_kv, state_C, state_n, state_m, state_conv, page_table, rel_bias, g_ffn1, w1_gate, w1_up, w1_down, g_mix, w_in, qn_g, kn_cmp_g, kn_sel_g, kn_win_g, w_cmp_k, b_cmp_k, w_cmp_v, b_cmp_v, conv_w, conv_b, b_if, mn_g, w_out, g_ffn2, w2_gate, w2_up, w2_down):
    assert x_prompt.shape[2] == D_MODEL and g_ffn1.shape[0] == 1
    nb, t, _ = x_prompt.shape
    ns, ds, _ = x_sample.shape
    assert ds == 1 and t % Q_TILE == 0
    n_pages = page_table.shape[1]
    past = n_pages * PAGE_SIZE
    wb = cache_win_kv.shape[2]
    assert wb == WINDOW
    kv_shape = (2, N_KV_A, HEAD_DIM_A)

    row = lambda a: a.reshape(1, -1)
    two = lambda a: jnp.tile(a, 2).reshape(1, KV_W)
    w1 = (row(g_ffn1[0]), w1_gate[0].astype(BF16), w1_up[0].astype(BF16), w1_down[0].astype(BF16))
    w2 = (row(g_ffn2[0]), w2_gate[0].astype(BF16), w2_up[0].astype(BF16), w2_down[0].astype(BF16))
    wqt, wc, wkvt, wm, ws, wst = _prep_in_proj(w_in[0])
    qg = (jnp.tile(qn_g[0], 2) * F32(HEAD_DIM_A ** -0.5))[:, None]
    kgt = jnp.stack([jnp.tile(kn_sel_g[0], 2), jnp.tile(kn_win_g[0], 2)])[:, :, None]
    w_big, b_big = _prep_compress(w_cmp_k[0], b_cmp_k[0], w_cmp_v[0], b_cmp_v[0])
    g_cmp = two(kn_cmp_g[0])
    woa, wom = _prep_out_proj(w_out[0])
    n_gate = 3 * N_HEADS_A
    bif = jnp.zeros((1, LANES), F32).at[0, n_gate:n_gate + 2 * N_HEADS_M].set(b_if[0])
    conv_b2 = row(conv_b[0])
    mn = row(mn_g[0])

    nqb = t // Q_TILE
    tiles = _bias_tiles(rel_bias, jnp.asarray(_bucket_tiles(t, past, wb)))
    group_lanes = lambda a: a.reshape(a.shape[0], N_KV_A, GROUP_R, LANES, LANES).transpose(0, 1, 3, 2, 4).reshape(
        a.shape[0], N_KV_A, LANES, GROUP_R * LANES)
    bnear = group_lanes(tiles[0:2])
    sel_add, win_add = _near_tables(bnear)
    bcmp = group_lanes(tiles[2:2 + nqb])
    t_cmp, t_win, t_sel = tiles[2 + nqb], tiles[3 + nqb], tiles[4 + nqb]

    def dense_in(x, seq_len):
        x1 = _ffn(x, *w1)
        return (x1,) + tuple(_inproj(x1, seq_len, row(g_mix[0]), wqt, wc, wkvt, wm, ws, wst, qg, kgt))

    from_chan = lambda a: a.reshape((1, a.shape[0]) + kv_shape + (a.shape[2],)).transpose(0, 1, 5, 2, 3, 4)
    to_chan = lambda a: a.transpose(0, 2, 3, 4, 1).reshape(a.shape[0], KV_ROW, a.shape[1])

    (x1, q_t, cmp_rows, cmp_t, sel_t, win_t, ks, vs, kw, vw, qk_m, v_m, o_m, small, gate_t) = dense_in(
        x_prompt.reshape(nb * t, D_MODEL), t)
    seq = lambda a: a.reshape(nb, t, a.shape[-1])
    n16 = t // CMP_STRIDE
    assert n16 == LANES
    kc, vc_t = _compress(cmp_rows, t, w_big, b_big, g_cmp)
    n_sel = -(-t // SEL_BLOCK)
    cov_t = jnp.asarray(np.pad(_cover_np(n16 - 1, n_sel), ((0, 1), (0, 0))).T, BF16)
    expand_np = np.zeros((t, n_sel + MASK_PAD), np.float32)
    expand_np[:, :n_sel] = MASK_BIG * np.repeat(np.eye(n_sel, dtype=np.float32), SEL_BLOCK, axis=0)[:t]
    expand_np[:, n_sel] = -MASK_BIG
    expand_t = jnp.asarray(expand_np, BF16)
    a_pad = _nsa_prompt(q_t, gate_t, kc, vc_t, ks, vs, kw, vw, sel_add, win_add, bcmp, cov_t, expand_t, n16 - 1)
    chunk = 256 if t % 256 == 0 else Q_TILE
    m_out, c_p, n_p, m_p = _mlstm_prompt(seq(qk_m), seq(v_m), seq(o_m), seq(small), conv_w[0], conv_b2, bif, mn,
                                         chunk)
    y_prompt = _post(x1, a_pad.reshape(nb * t, -1), m_out.reshape(nb * t, -1), woa, wom, *w2)
    kv6 = lambda a, n: a.reshape((1, n, -1) + kv_shape)
    prompt_states = (from_chan(cmp_t), from_chan(sel_t), from_chan(win_t[:, :, t - wb:]), c_p[None], n_p[None],
                     m_p[None, :, 0, :N_HEADS_M], seq(qk_m)[None, :, t - (CONV_W - 1):])

    (x1s, q_st, cmp_halves, _, sel_st, win_st, _, _, _, _, qk_s, v_s, o_s, small_s, gate_st) = dense_in(
        x_sample.reshape(ns, D_MODEL), ns)
    cmp_s = jnp.concatenate([cmp_halves[0], cmp_halves[1]], axis=1)
    sel_s, win_s, gates_s = sel_st[0].T, win_st[0].T, gate_st[0].T
    q3 = q_st[0].T.reshape(ns, N_HEADS_A, LANES)
    n16s = past // CMP_STRIDE
    n_sel_s = -(-(past + 1) // SEL_BLOCK)
    n_sel_pad = -(-n_sel_s // LANES) * LANES
    cover_s = jnp.asarray(np.pad(_cover_np(n16s - 1, n_sel_s), ((0, 1), (0, n_sel_pad - n_sel_s))), BF16)
    bias_c = t_cmp[:, :n16s // LANES, :].reshape(N_HEADS_A, n16s)
    o_c, idx = _sample_cmp(page_table, to_chan(cache_cmp_kv[0]), w_big, b_big, g_cmp, q3, bias_c, cover_s, n_sel_s,
                           past)
    n_past_blocks = past // SEL_BLOCK
    two_lanes = lambda a: jnp.concatenate([a[:, :SEL_BLOCK], a[:, :SEL_BLOCK]], axis=-1)
    bsel = jnp.stack([two_lanes(t_sel[:, n_past_blocks - 2, :]), two_lanes(t_sel[:, n_past_blocks - 1, :])])
    bwin = t_win[:, :wb // LANES, :].reshape(N_HEADS_A, wb)
    b0 = jnp.broadcast_to(t_win[:, wb // LANES, 0:1], (N_HEADS_A, LANES))
    gate_b = jnp.broadcast_to(gates_s[:, :n_gate, None], (ns, n_gate, LANES))
    a_s, win_out = _sample_attn(idx[:, :, 0].reshape(-1), page_table, to_chan(cache_sel_kv[0]),
                                to_chan(cache_win_kv[0]), q3, sel_s[:, None, :], win_s[:, None, :], win_st[0],
                                o_c, gate_b, bsel, bwin, b0)
    m0 = jnp.pad(state_m[0], ((0, 0), (0, LANES - N_HEADS_M)))[:, None, :]
    m_s, c_s, n_s, m_new = _mlstm_step(qk_s[:, None, :], state_conv[0], v_s[:, None, :], o_s[:, None, :],
                                       small_s[:, None, :], state_C[0], state_n[0], m0, conv_w[0], conv_b2, bif, mn)
    y_sample = _post(x1s, a_s.reshape(ns, -1), m_s.reshape(ns, -1), woa, wom, *w2)
    conv_new = jnp.concatenate([state_conv[0][:, 1:], qk_s[:, None, :]], axis=1)
    sample_states = (kv6(cmp_s, ns), kv6(sel_s, ns), from_chan(win_out), c_s[None], n_s[None],
                     m_new[None, :, 0, :N_HEADS_M], conv_new[None])

    return (y_prompt.reshape(nb, t, D_MODEL), y_sample.reshape(ns, 1, D_MODEL)) + prompt_states + sample_states
```

```python
import functools
import math

import jax
import jax.numpy as jnp
import numpy as np
from jax import lax
from jax.experimental import pallas as pl
from jax.experimental.pallas import tpu as pltpu

F32 = jnp.float32
BF16 = jnp.bfloat16

D_MODEL = 1024
PAGE_SIZE = 128
N_HEADS_A = 8
HEAD_DIM_A = 64
N_KV_A = 2
GROUP_R = N_HEADS_A // N_KV_A
D_A = N_HEADS_A * HEAD_DIM_A
KV_W = N_KV_A * HEAD_DIM_A
CMP_STRIDE = 16
CMP_LEN = 2 * CMP_STRIDE
SEL_BLOCK = 64
TOP_K_BLOCKS = 16
WINDOW = 512
N_HEADS_M = 4
HEAD_DIM_M = 128
D_M = N_HEADS_M * HEAD_DIM_M
CONV_W = 4
N_BUCKETS = 32
MAX_DISTANCE = 128
EPS = 1e-6

LANES = 128
Q_TILE = 128
Q_BLOCKS = 2
FAR_TILES = 4
KV_ROW = 2 * KV_W
CHUNK_ROW = CMP_STRIDE * KV_ROW
NEG = -1e30
MASK_BIG = 2.0 ** 100
MASK_PAD = 16
VMEM_LIMIT = 56 * 1024 * 1024

NT_DIMS = (((1,), (1,)), ((), ()))


def _dot(a, b):
    return jnp.dot(a, b, preferred_element_type=F32)


def _dot_nt(a, b):
    return lax.dot_general(a, b, NT_DIMS, preferred_element_type=F32)


def _split3(x):
    x1 = x.astype(BF16)
    r1 = x - x1.astype(F32)
    x2 = r1.astype(BF16)
    x3 = (r1 - x2.astype(F32)).astype(BF16)
    return x1, x2, x3


def _dot_exact_rhs(x, m):
    x1, x2, x3 = _split3(x)
    return _dot(x1, m) + _dot(x2, m) + _dot(x3, m)


def _dot_exact_lhs(m, x):
    x1, x2, x3 = _split3(x)
    return _dot(m, x1) + _dot(m, x2) + _dot(m, x3)


def _rms_rows(x, g):
    ms = jnp.mean(x * x, axis=-1, keepdims=True)
    return x * lax.rsqrt(ms + EPS) * g


def _rms_two_groups(k, g):
    sq = k * k
    lane = lax.broadcasted_iota(jnp.int32, sq.shape, 1)
    lo = lane < HEAD_DIM_A
    s0 = jnp.sum(jnp.where(lo, sq, 0.0), axis=-1, keepdims=True)
    s1 = jnp.sum(jnp.where(lo, 0.0, sq), axis=-1, keepdims=True)
    ms = jnp.where(lo, s0, s1) * (1.0 / HEAD_DIM_A)
    return k * lax.rsqrt(ms + EPS) * g


def _sigmoid(x):
    return jax.nn.sigmoid(x)


def _log_sigmoid(x):
    return jnp.minimum(x, 0.0) - jnp.log1p(jnp.exp(-jnp.abs(x)))


def _const_spec(shape):
    nd = len(shape)
    return pl.BlockSpec(shape, lambda *_: (0,) * nd, pipeline_mode=pl.Buffered(1))


def _params(*sem):
    return pltpu.CompilerParams(dimension_semantics=sem, vmem_limit_bytes=VMEM_LIMIT)


def _swiglu_residual(x, g_ref, wg_ref, wu_ref, wd_ref):
    xn = _rms_rows(x, g_ref[...]).astype(BF16)
    d_ff = wg_ref.shape[1]
    n_split = 2 if d_ff % (2 * LANES) == 0 else 1
    step = d_ff // n_split
    acc = jnp.zeros_like(x)
    for c in range(n_split):
        hg = _dot(xn, wg_ref[:, c * step:(c + 1) * step])
        hu = _dot(xn, wu_ref[:, c * step:(c + 1) * step])
        h = (hg * _sigmoid(hg)) * hu
        acc = acc + _dot(h.astype(BF16), wd_ref[c * step:(c + 1) * step, :])
    return x + 0.5 * acc


def _ffn_kernel(x_ref, g_ref, wg_ref, wu_ref, wd_ref, y_ref):
    y_ref[...] = _swiglu_residual(x_ref[...], g_ref, wg_ref, wu_ref, wd_ref)


def _post_kernel(x_ref, a_ref, m_ref, woa_ref, wom_ref, g_ref, wg_ref, wu_ref, wd_ref, y_ref):
    x = x_ref[...] + (_dot(a_ref[...], woa_ref[...]) + _dot(m_ref[...], wom_ref[...]))
    y_ref[...] = _swiglu_residual(x, g_ref, wg_ref, wu_ref, wd_ref)


def _token_tile(n):
    return 512 if n % 512 == 0 else n


def _ffn(x, g, wg, wu, wd):
    n = x.shape[0]
    tm = _token_tile(n)
    row = lambda w: pl.BlockSpec((tm, w), lambda i: (i, 0))
    return pl.pallas_call(
        _ffn_kernel,
        grid=(n // tm,),
        in_specs=[row(D_MODEL), _const_spec(g.shape), _const_spec(wg.shape), _const_spec(wu.shape),
                  _const_spec(wd.shape)],
        out_specs=row(D_MODEL),
        out_shape=jax.ShapeDtypeStruct((n, D_MODEL), F32),
        compiler_params=_params("parallel"),
        name="ffn1",
    )(x, g, wg, wu, wd)


def _post(x, a_pad, m_out, woa, wom, g, wg, wu, wd):
    n = x.shape[0]
    tm = _token_tile(n)
    row = lambda w: pl.BlockSpec((tm, w), lambda i: (i, 0))
    return pl.pallas_call(
        _post_kernel,
        grid=(n // tm,),
        in_specs=[row(D_MODEL), row(a_pad.shape[1]), row(m_out.shape[1]), _const_spec(woa.shape),
                  _const_spec(wom.shape), _const_spec(g.shape), _const_spec(wg.shape), _const_spec(wu.shape),
                  _const_spec(wd.shape)],
        out_specs=row(D_MODEL),
        out_shape=jax.ShapeDtypeStruct((n, D_MODEL), F32),
        compiler_params=_params("parallel"),
        name="post_mix_ffn2",
    )(x, a_pad, m_out, woa, wom, g, wg, wu, wd)


def _rms_two_groups_t(k, g_col):
    sq = k * k
    half = k.shape[0] // 2
    s0 = jnp.sum(sq[0:half], axis=0, keepdims=True)
    s1 = jnp.sum(sq[half:], axis=0, keepdims=True)
    ms = jnp.concatenate([jnp.broadcast_to(s0, (half, k.shape[1])), jnp.broadcast_to(s1, (half, k.shape[1]))],
                         axis=0) * (1.0 / HEAD_DIM_A)
    return k * lax.rsqrt(ms + EPS) * g_col


def _inproj_kernel(x_ref, g_ref, wqt_ref, wkvt_ref, wm_ref, ws_ref, wst_ref, qg_ref, kgt_ref,
                   q_out, cmp_rows, cmpt_out, selt_out, wint_out, kst, vst, kwt, vwt, qk_out, v_out, o_out,
                   small_out, gate_out):
    xn = _rms_rows(x_ref[...], g_ref[...]).astype(BF16)
    qt = _dot_nt(wqt_ref[...], xn)
    zeros = jnp.zeros((HEAD_DIM_A, qt.shape[1]), BF16)
    for h in range(N_HEADS_A):
        qh = qt[h * HEAD_DIM_A:(h + 1) * HEAD_DIM_A, :]
        ms = jnp.sum(qh * qh, axis=0, keepdims=True) * (1.0 / HEAD_DIM_A)
        own = h * LANES + (h // GROUP_R) * HEAD_DIM_A
        other = h * LANES + (1 - h // GROUP_R) * HEAD_DIM_A
        q_out[0, own:own + HEAD_DIM_A, :] = (qh * lax.rsqrt(ms + EPS) * qg_ref[...]).astype(BF16)
        q_out[0, other:other + HEAD_DIM_A, :] = zeros
    kvt = _dot_nt(wkvt_ref[...], xn)
    cmpt_out[0] = kvt[0:KV_ROW]
    cmp_rows[0] = kvt[0:KV_W].T
    cmp_rows[1] = kvt[KV_W:KV_ROW].T
    ks = _rms_two_groups_t(kvt[2 * KV_W:3 * KV_W], kgt_ref[0])
    vs = kvt[3 * KV_W:4 * KV_W]
    kw = _rms_two_groups_t(kvt[4 * KV_W:5 * KV_W], kgt_ref[1])
    vw = kvt[5 * KV_W:6 * KV_W]
    selt_out[0, 0:KV_W] = ks
    selt_out[0, KV_W:KV_ROW] = vs
    wint_out[0, 0:KV_W] = kw
    wint_out[0, KV_W:KV_ROW] = vw
    ones = jnp.ones((HEAD_DIM_A, LANES), F32)
    for j in range(kst.shape[1]):
        cols = slice(j * LANES, (j + 1) * LANES)
        kst[0, j] = ks[:, cols].T.astype(BF16)
        kwt[0, j] = kw[:, cols].T.astype(BF16)
        for v, out in ((vs, vst), (vw, vwt)):
            out[0, j, 0] = jnp.concatenate([v[0:HEAD_DIM_A, cols], ones], axis=0).astype(BF16)
            out[0, j, 1] = jnp.concatenate([ones, v[HEAD_DIM_A:KV_W, cols]], axis=0).astype(BF16)
    m = _dot(xn, wm_ref[...])
    qk_out[...] = m[:, 0:2 * D_M]
    v_out[...] = m[:, 2 * D_M:3 * D_M]
    o_out[...] = m[:, 3 * D_M:4 * D_M]
    small_out[...] = _dot(xn, ws_ref[...])
    gate_out[0] = _sigmoid(_dot_nt(wst_ref[...], xn))


def _inproj(x, seq_len, g, wqt, wkvt, wm, ws, wst, qg, kgt):
    n = x.shape[0]
    nb = n // seq_len
    tm = _token_tile(seq_len)
    tpb = seq_len // tm
    row = lambda w: pl.BlockSpec((tm, w), lambda i: (i, 0))
    rows_out = lambda w, dt: (row(w), jax.ShapeDtypeStruct((n, w), dt))
    chan_out = lambda c, dt: (pl.BlockSpec((1, c, tm), lambda i: (i // tpb, 0, i % tpb)),
                              jax.ShapeDtypeStruct((nb, c, seq_len), dt))
    k_tiles = (pl.BlockSpec((1, tm // LANES, LANES, KV_W), lambda i: (i // tpb, i % tpb, 0, 0)),
               jax.ShapeDtypeStruct((nb, seq_len // LANES, LANES, KV_W), BF16))
    v_tiles = (pl.BlockSpec((1, tm // LANES, N_KV_A, KV_W, LANES), lambda i: (i // tpb, i % tpb, 0, 0, 0)),
               jax.ShapeDtypeStruct((nb, seq_len // LANES, N_KV_A, KV_W, LANES), BF16))
    halves_out = (pl.BlockSpec((2, tm, KV_W), lambda i: (0, i, 0)), jax.ShapeDtypeStruct((2, n, KV_W), F32))
    outs = [chan_out(N_HEADS_A * LANES, BF16), halves_out, chan_out(KV_ROW, F32), chan_out(KV_ROW, F32),
            chan_out(KV_ROW, F32), k_tiles, v_tiles, k_tiles, v_tiles,
            rows_out(2 * D_M, F32), rows_out(D_M, F32), rows_out(D_M, F32), rows_out(LANES, F32),
            chan_out(LANES, F32)]
    consts = (g, wqt, wkvt, wm, ws, wst, qg, kgt)
    return pl.pallas_call(
        _inproj_kernel,
        grid=(n // tm,),
        in_specs=[row(D_MODEL)] + [_const_spec(a.shape) for a in consts],
        out_specs=[o[0] for o in outs],
        out_shape=[o[1] for o in outs],
        compiler_params=_params("parallel"),
        name="in_proj",
    )(x, *consts)


def _t5_bucket_np(dist):
    n = np.maximum(dist, 0)
    exact = N_BUCKETS // 2
    nf = np.maximum(n, 1).astype(np.float32)
    ratio = np.log(nf / np.float32(exact)) / np.float32(math.log(MAX_DISTANCE / exact))
    large = exact + (ratio * np.float32(N_BUCKETS - exact)).astype(np.int32)
    return np.where(n < exact, n, np.minimum(large, N_BUCKETS - 1)).astype(np.int32)


def _bias_kernel(tbl_ref, bkt_ref, out_ref):
    bkt = bkt_ref[0]
    for h in range(N_HEADS_A):
        far = tbl_ref[N_BUCKETS - 1, h]
        acc = jnp.zeros(bkt.shape, F32)
        for b in range(N_BUCKETS - 1):
            acc = jnp.where(bkt == b, tbl_ref[b, h] - far, acc)
        out_ref[0, h] = acc


def _bias_tiles(rel_bias, buckets):
    n = buckets.shape[0]
    return pl.pallas_call(
        _bias_kernel,
        grid=(n,),
        in_specs=[pl.BlockSpec(memory_space=pltpu.SMEM),
                  pl.BlockSpec((1, LANES, LANES), lambda i: (i, 0, 0))],
        out_specs=pl.BlockSpec((1, N_HEADS_A, LANES, LANES), lambda i: (i, 0, 0, 0)),
        out_shape=jax.ShapeDtypeStruct((n, N_HEADS_A, LANES, LANES), F32),
        compiler_params=_params("parallel"),
        name="bias_tiles",
    )(rel_bias, buckets)


def _compress_half(rows_ref, nc, w_ref, half, bias):
    fs = None
    for t in range(CMP_STRIDE):
        part = _dot(rows_ref[pl.ds(t, nc, stride=CMP_STRIDE), :].astype(BF16), w_ref[half, t])
        fs = part if fs is None else fs + part
    return fs[:, 0:KV_W] + pltpu.roll(fs[:, KV_W:KV_ROW], nc - 1, 0) + bias


def _compress_rows(k_rows_ref, v_rows_ref, nc, w_ref, b_ref, g_ref):
    kc = _rms_two_groups(_compress_half(k_rows_ref, nc, w_ref, 0, b_ref[:, 0:KV_W]), g_ref[...])
    vc = _compress_half(v_rows_ref, nc, w_ref, 1, b_ref[:, KV_W:KV_ROW])
    complete = lax.broadcasted_iota(jnp.int32, kc.shape, 0) < nc - 1
    return jnp.where(complete, kc, 0.0), jnp.where(complete, vc, 0.0)


def _compress_kernel(x_ref, w_ref, b_ref, g_ref, kc_ref, vc_ref):
    nc = kc_ref.shape[1]
    kc, vc = _compress_rows(x_ref.at[0], x_ref.at[1], nc, w_ref, b_ref, g_ref)
    kc_ref[0] = kc.astype(BF16)
    vc_ref[0] = vc.T.astype(BF16)


def _compress(cmp_rows, t, w_big, b_big, g_k):
    nb = cmp_rows.shape[1] // t
    nc = t // CMP_STRIDE
    return pl.pallas_call(
        _compress_kernel,
        grid=(nb,),
        in_specs=[pl.BlockSpec((2, t, KV_W), lambda b: (0, b, 0)), _const_spec(w_big.shape),
                  _const_spec(b_big.shape), _const_spec(g_k.shape)],
        out_specs=[pl.BlockSpec((1, nc, KV_W), lambda b: (b, 0, 0)),
                   pl.BlockSpec((1, KV_W, nc), lambda b: (b, 0, 0))],
        out_shape=[jax.ShapeDtypeStruct((nb, nc, KV_W), BF16), jax.ShapeDtypeStruct((nb, KV_W, nc), BF16)],
        compiler_params=_params("parallel"),
        name="compress_prompt",
    )(cmp_rows, w_big, b_big, g_k)


def _tile_heads(x):
    return jnp.concatenate([x] * GROUP_R, axis=1)


def _softmax_part(m, s):
    m_new = jnp.maximum(m, jnp.max(s, axis=0, keepdims=True))
    return m_new, jnp.exp(m - m_new), jnp.exp(s - m_new).astype(BF16)


def _accumulate(acc, soft, vt):
    m_new, alpha, p = soft
    return m_new, alpha * acc + _dot(vt, p)


def _key_rows(ref, t, n):
    return ref[0, pl.ds(t, n)].reshape(n * LANES, ref.shape[-1])


def _value_cols(ref, t, n, g):
    return jnp.concatenate([ref[0, t + k, g] for k in range(n)], axis=1)


def _rank_rows(score):
    n_blk = score.shape[0]
    blk = lax.broadcasted_iota(jnp.int32, score.shape, 0)
    rank = jnp.zeros(score.shape, F32)
    for b in range(n_blk):
        row = score[b:b + 1, :]
        rank = rank + jnp.where(blk > b, jnp.where(row >= score, 1.0, 0.0), jnp.where(row > score, 1.0, 0.0))
    return rank


def _nsa_prompt_kernel(q_ref, gate_ref, kc_ref, vct_ref, ks_ref, vs_ref, kw_ref, vw_ref, sel_add_ref, win_add_ref,
                       bcmp_ref, covt_ref, expt_ref, a_ref, mb_ref, *, n_cmp):
    ii = pl.program_id(1)
    n_tiles = mb_ref.shape[2]
    n_sel = covt_ref.shape[0]
    wide = GROUP_R * Q_TILE
    heads_of = lambda g: range(g * GROUP_R, (g + 1) * GROUP_R)
    chains =[(sb, g) for sb in range(Q_BLOCKS) for g in range(N_KV_A)]
    blk_i = [Q_BLOCKS * ii + sb for sb in range(Q_BLOCKS)]
    cols = lambda sb: slice(sb * Q_TILE, (sb + 1) * Q_TILE)
    qg_t = {(sb, g): jnp.concatenate([q_ref[0, h * LANES:(h + 1) * LANES, cols(sb)] for h in heads_of(g)], axis=1)
            for sb, g in chains}
    init = (jnp.full((1, wide), NEG, F32), jnp.zeros((LANES, wide), F32))

    n_win = WINDOW // LANES + 1
    t0 = [jnp.maximum(i - (n_win - 1), 0) for i in blk_i]
    s_win = {}
    for sb, i in enumerate(blk_i):
        win_off = pl.multiple_of(jnp.maximum(n_win - 1 - i, 0) * LANES, LANES)
        k_win = _key_rows(kw_ref, t0[sb], n_win)
        for g in range(N_KV_A):
            s_win[sb, g] = _dot(k_win, qg_t[sb, g]) + win_add_ref[g, pl.ds(win_off, n_win * LANES), :]

    ncp = kc_ref.shape[1]
    j_sub = lax.broadcasted_iota(jnp.int32, (ncp, wide), 0)
    q_in_tile = lax.broadcasted_iota(jnp.int32, (ncp, wide), 1) & (Q_TILE - 1)
    s_cmp, any_usable = {}, []
    for sb, i in enumerate(blk_i):
        q_lane = i * Q_TILE + q_in_tile
        usable_add = jnp.where((j_sub * CMP_STRIDE + (CMP_LEN - 1) <= q_lane) & (j_sub < n_cmp), 0.0, NEG)
        any_usable.append(jnp.where(q_lane[0:1, :] >= CMP_LEN - 1, 1.0, 0.0))
        for g in range(N_KV_A):
            s_cmp[sb, g] = _dot(kc_ref[0], qg_t[sb, g]) + bcmp_ref[sb, g] + usable_add
    p_c = {}
    for sb, g in chains:
        e = jnp.exp(s_cmp[sb, g] - jnp.max(s_cmp[sb, g], axis=0, keepdims=True)) * any_usable[sb]
        p_c[sb, g] = e / jnp.maximum(jnp.sum(e, axis=0, keepdims=True), 1e-30)
    oc_t = {c: _dot(vct_ref[0], p_c[c].astype(BF16)) for c in chains}

    blk = lax.broadcasted_iota(jnp.int32, (n_sel, Q_TILE), 0)
    q_in_blk = lax.broadcasted_iota(jnp.int32, (n_sel, Q_TILE), 1)
    ones_row = jnp.where(lax.broadcasted_iota(jnp.int32, (MASK_PAD, Q_TILE), 0) == 0, 1.0, 0.0)
    for sb, g in chains:
        qpos = blk_i[sb] * Q_TILE + q_in_blk
        cur = qpos // SEL_BLOCK
        valid = blk * SEL_BLOCK <= qpos
        forced = valid & ((blk == 0) | (blk == cur) | (blk == cur - 1))
        p_sum = p_c[sb, g][:, 0:Q_TILE]
        for r in range(1, GROUP_R):
            p_sum = p_sum + p_c[sb, g][:, r * Q_TILE:(r + 1) * Q_TILE]
        imp_t = _dot_exact_lhs(covt_ref[...], p_sum)
        score = jnp.where(forced, -NEG, jnp.where(valid, imp_t, NEG))
        chosen = jnp.where(_rank_rows(score) < float(min(TOP_K_BLOCKS, n_sel)), 1.0, 0.0)
        mask_all = _dot(expt_ref[...], jnp.concatenate([chosen, ones_row], axis=0).astype(BF16))
        for t in range(n_tiles):
            mb_ref[sb, g, t] = mask_all[t * LANES:(t + 1) * LANES, :]

    soft_win = {c: _softmax_part(init[0], s_win[c]) for c in chains}
    ow_acc = {(sb, g): _dot(_value_cols(vw_ref, t0[sb], n_win, g), soft_win[sb, g][2]) for sb, g in chains}

    tile_in_iter = lax.broadcasted_iota(jnp.int32, (FAR_TILES * LANES, Q_TILE), 0) // LANES

    def far_body(it, carry):
        t = FAR_TILES * it
        k_far = _key_rows(ks_ref, t, FAR_TILES)
        s = {}
        for sb, i in enumerate(blk_i):
            past_far = jnp.where(t + tile_in_iter <= i - 2, 0.0, NEG)
            for g in range(N_KV_A):
                add = _tile_heads(mb_ref[sb, g, pl.ds(t, FAR_TILES)].reshape(FAR_TILES * LANES, Q_TILE) + past_far)
                s[sb, g] = _dot(k_far, qg_t[sb, g]) + add
        soft = [_softmax_part(carry[n][0], s[c]) for n, c in enumerate(chains)]
        return tuple(_accumulate(carry[n][1], soft[n], _value_cols(vs_ref, t, FAR_TILES, c[1]))
                     for n, c in enumerate(chains))

    n_far = blk_i[-1] - 1
    carry = lax.fori_loop(0, (n_far + FAR_TILES - 1) // FAR_TILES, far_body, (init,) * len(chains))

    t1 = [jnp.maximum(i - 1, 0) for i in blk_i]
    s_near = {}
    for sb, i in enumerate(blk_i):
        k_near = _key_rows(ks_ref, t1[sb], 2)
        near_off = pl.multiple_of(jnp.where(i == 0, LANES, 0), LANES)
        for g in range(N_KV_A):
            s_near[sb, g] = _dot(k_near, qg_t[sb, g]) + sel_add_ref[g, pl.ds(near_off, 2 * LANES), :] + _tile_heads(
                mb_ref[sb, g, pl.ds(t1[sb], 2)].reshape(2 * LANES, Q_TILE))
    soft_near = [_softmax_part(carry[n][0], s_near[c]) for n, c in enumerate(chains)]
    for n, (sb, g) in enumerate(chains):
        denom = (1 - g) * HEAD_DIM_A
        _, acc = _accumulate(carry[n][1], soft_near[n], _value_cols(vs_ref, t1[sb], 2, g))
        os_t = acc / acc[denom:denom + 1, :]
        ow_t = ow_acc[sb, g] / ow_acc[sb, g][denom:denom + 1, :]

        heads = heads_of(g)
        gate_row = lambda br: jnp.concatenate(
            [gate_ref[0, br * N_HEADS_A + h:br * N_HEADS_A + h + 1, cols(sb)] for h in heads], axis=1)
        out_t = gate_row(0) * oc_t[sb, g] + gate_row(1) * os_t + gate_row(2) * ow_t
        own = slice(g * HEAD_DIM_A, (g + 1) * HEAD_DIM_A)
        for pair in range(GROUP_R // 2):
            two = jnp.concatenate([out_t[own, (2 * pair + k) * Q_TILE:(2 * pair + k + 1) * Q_TILE] for k in range(2)],
                                  axis=0)
            col = (g * GROUP_R // 2 + pair) * LANES
            a_ref[0, cols(sb), col:col + LANES] = two.T.astype(BF16)


def _nsa_prompt(q_t, gate_t, kc, vc_t, ks, vs, kw, vw, sel_add, win_add, bcmp, cov_t, expand_t, n_cmp):
    nb, _, t = q_t.shape
    nqb = t // Q_TILE
    assert nqb > WINDOW // LANES and nqb % Q_BLOCKS == 0
    step = Q_BLOCKS * Q_TILE
    seq = lambda a: pl.BlockSpec((1,) + a.shape[1:], lambda b, i: (b,) + (0,) * (a.ndim - 1))
    return pl.pallas_call(
        functools.partial(_nsa_prompt_kernel, n_cmp=n_cmp),
        grid=(nb, nqb // Q_BLOCKS),
        in_specs=[pl.BlockSpec((1, N_HEADS_A * LANES, step), lambda b, i: (b, 0, i)),
                  pl.BlockSpec((1, LANES, step), lambda b, i: (b, 0, i)),
                  seq(kc), seq(vc_t), seq(ks), seq(vs), seq(kw), seq(vw),
                  _const_spec(sel_add.shape), _const_spec(win_add.shape),
                  pl.BlockSpec((Q_BLOCKS,) + bcmp.shape[1:], lambda b, i: (i, 0, 0, 0)),
                  _const_spec(cov_t.shape), _const_spec(expand_t.shape)],
        out_specs=pl.BlockSpec((1, step, D_A), lambda b, i: (b, i, 0)),
        out_shape=jax.ShapeDtypeStruct((nb, t, D_A), BF16),
        scratch_shapes=[pltpu.VMEM((Q_BLOCKS, N_KV_A, t // LANES, LANES, Q_TILE), F32)],
        compiler_params=_params("parallel", "arbitrary"),
        name="nsa_prompt",
    )(q_t, gate_t, kc, vc_t, ks, vs, kw, vw, sel_add, win_add, bcmp, cov_t, expand_t)


def _head_norm_gate(h_all, mn_ref, o):
    outs = []
    for h in range(N_HEADS_M):
        hs = slice(h * HEAD_DIM_M, (h + 1) * HEAD_DIM_M)
        outs.append(_rms_rows(h_all[h], mn_ref[:, hs]))
    return _sigmoid(o) * jnp.concatenate(outs, axis=-1)


def _mlstm_prompt_kernel(qk_ref, v_ref, o_ref, small_ref, cw_ref, cb_ref, bif_ref, mn_ref, tri_ref, triu_ref,
                         y_ref, c_ref, n_ref, m_ref, xbuf):
    chunk = qk_ref.shape[1]
    pad = 8

    @pl.when(pl.program_id(1) == 0)
    def _():
        xbuf[0:pad, :] = jnp.zeros((pad, 2 * D_M), F32)
        c_ref[...] = jnp.zeros_like(c_ref)
        n_ref[...] = jnp.zeros_like(n_ref)
        m_ref[...] = jnp.zeros_like(m_ref)

    x = qk_ref[0]
    xbuf[pad:pad + chunk, :] = x
    y = xbuf[pad - 3:pad - 3 + chunk, :] * cw_ref[0:1, :]
    y = y + xbuf[pad - 2:pad - 2 + chunk, :] * cw_ref[1:2, :]
    y = y + xbuf[pad - 1:pad - 1 + chunk, :] * cw_ref[2:3, :]
    y = y + x * cw_ref[3:4, :] + cb_ref[...]
    qkc = y * _sigmoid(y)
    xbuf[0:pad, :] = xbuf[chunk:chunk + pad, :]

    gi = small_ref[0] + bif_ref[...]
    n_gate = 3 * N_HEADS_A
    gate_rows = gi.T[n_gate:n_gate + 2 * N_HEADS_M, :]
    b_col = _dot_exact_lhs(tri_ref[...], _log_sigmoid(gi))
    b_row = _dot_exact_rhs(_log_sigmoid(gate_rows), triu_ref[...])
    t_col = lax.broadcasted_iota(jnp.int32, (chunk, chunk), 0)
    s_row = lax.broadcasted_iota(jnp.int32, (chunk, chunk), 1)
    causal = s_row <= t_col
    m_all = m_ref[0]
    lane_row = lax.broadcasted_iota(jnp.int32, m_all.shape, 1)

    h_all = []
    for h in range(N_HEADS_M):
        hs = slice(h * HEAD_DIM_M, (h + 1) * HEAD_DIM_M)
        ks = slice(D_M + h * HEAD_DIM_M, D_M + (h + 1) * HEAD_DIM_M)
        q = qkc[:, hs]
        k = qkc[:, ks] * F32(HEAD_DIM_M ** -0.5)
        v = v_ref[0, :, hs]
        qb, kb = q.astype(BF16), k.astype(BF16)
        ci, fi = 3 * N_HEADS_A + h, 3 * N_HEADS_A + N_HEADS_M + h
        bt = b_col[:, fi:fi + 1]
        bs = b_row[N_HEADS_M + h:N_HEADS_M + h + 1, :]
        ig_row = gate_rows[h:h + 1, :]
        ig_col = gi[:, ci:ci + 1]
        m_prev = m_all[:, h:h + 1]
        c_prev = c_ref[0, h]
        n_prev = n_ref[0, h:h + 1, :]

        dlog = jnp.where(causal, bt - bs + ig_row, NEG)
        inter = bt + m_prev
        m_t = jnp.maximum(inter, jnp.max(dlog, axis=-1, keepdims=True))
        s_qk = _dot_nt(qb, kb) * jnp.exp(dlog - m_t)
        dec = jnp.exp(inter - m_t)
        num = _dot(s_qk.astype(BF16), v.astype(BF16)) + dec * _dot_nt(qb, c_prev.astype(BF16))
        den = jnp.sum(s_qk, axis=-1, keepdims=True) + dec * jnp.sum(q * n_prev, axis=-1, keepdims=True)
        h_all.append(num / jnp.maximum(jnp.abs(den), jnp.exp(-m_t)))

        m_new = m_t[chunk - 1:chunk, :]
        b_end = bt[chunk - 1:chunk, :]
        w_end = jnp.exp(b_end - bt + ig_col - m_new)
        dec_end = jnp.exp(b_end + m_prev - m_new)
        wv_t = (w_end * v).T.astype(BF16)
        c_ref[0, h] = dec_end * c_prev + _dot(wv_t, kb)
        n_ref[0, h:h + 1, :] = dec_end * n_prev + jnp.sum(w_end * k, axis=0, keepdims=True)
        m_all = jnp.where(lane_row == h, m_new, m_all)

    m_ref[0] = m_all
    y_ref[0] = _head_norm_gate(h_all, mn_ref, o_ref[0]).astype(BF16)


def _mlstm_prompt(qk, v, o, small, conv_w, conv_b, bif, mn_g, chunk):
    nb, t, _ = qk.shape
    tri = jnp.asarray(np.tril(np.ones((chunk, chunk), np.float32)), BF16)
    triu = jnp.asarray(np.triu(np.ones((chunk, chunk), np.float32)), BF16)
    tok = lambda w: pl.BlockSpec((1, chunk, w), lambda b, c: (b, c, 0))
    return pl.pallas_call(
        _mlstm_prompt_kernel,
        grid=(nb, t // chunk),
        in_specs=[tok(2 * D_M), tok(D_M), tok(D_M), tok(LANES)]
        + [_const_spec(a.shape) for a in (conv_w, conv_b, bif, mn_g, tri, triu)],
        out_specs=[tok(D_M),
                   pl.BlockSpec((1, N_HEADS_M, HEAD_DIM_M, HEAD_DIM_M), lambda b, c: (b, 0, 0, 0)),
                   pl.BlockSpec((1, N_HEADS_M, HEAD_DIM_M), lambda b, c: (b, 0, 0)),
                   pl.BlockSpec((1, 1, LANES), lambda b, c: (b, 0, 0))],
        out_shape=[jax.ShapeDtypeStruct((nb, t, D_M), BF16),
                   jax.ShapeDtypeStruct((nb, N_HEADS_M, HEAD_DIM_M, HEAD_DIM_M), F32),
                   jax.ShapeDtypeStruct((nb, N_HEADS_M, HEAD_DIM_M), F32),
                   jax.ShapeDtypeStruct((nb, 1, LANES), F32)],
        scratch_shapes=[pltpu.VMEM((chunk + 8, 2 * D_M), F32)],
        compiler_params=_params("parallel", "arbitrary"),
        name="mlstm_prompt",
    )(qk, v, o, small, conv_w, conv_b, bif, mn_g, tri, triu)


def _softmax_rows(s):
    e = jnp.exp(s - jnp.max(s, axis=-1, keepdims=True))
    return e, jnp.sum(e, axis=-1, keepdims=True)


def _sample_cmp_kernel(pt_ref, *refs, n_pages, n_sel, qpos):
    pages = refs[:n_pages]
    w_ref, b_ref, g_ref, q_ref, bias_ref, cov_ref, oc_ref, idx_ref, xk_ref, xv_ref = refs[n_pages:]
    for half, rows_ref in ((0, xk_ref), (1, xv_ref)):
        for u in range(n_pages):
            rows_ref[u * PAGE_SIZE:(u + 1) * PAGE_SIZE, :] = pages[u][0, half * KV_W:(half + 1) * KV_W, :].T
    nc = n_pages * PAGE_SIZE // CMP_STRIDE
    kc, vc = _compress_rows(xk_ref, xv_ref, nc, w_ref, b_ref, g_ref)
    q = q_ref[0]
    j_row = lax.broadcasted_iota(jnp.int32, (N_HEADS_A, nc), 1)
    s = _dot_nt(q, kc.astype(BF16)) + bias_ref[...] + jnp.where(j_row < nc - 1, 0.0, NEG)
    e, l = _softmax_rows(s)
    p = e / jnp.maximum(l, 1e-30)
    oc_ref[0] = _dot(p.astype(BF16), vc.astype(BF16))

    nsp = cov_ref.shape[1]
    blk_row = lax.broadcasted_iota(jnp.int32, (1, nsp), 1)
    sub = lax.broadcasted_iota(jnp.int32, (nsp, nsp), 0)
    lan = lax.broadcasted_iota(jnp.int32, (nsp, nsp), 1)
    cur = qpos // SEL_BLOCK
    valid = (blk_row * SEL_BLOCK <= qpos) & (blk_row < n_sel)
    forced = valid & ((blk_row == 0) | (blk_row == cur) | (blk_row == cur - 1))
    k_col = lax.broadcasted_iota(jnp.int32, (TOP_K_BLOCKS, nsp), 0).astype(F32)
    blk_f = lax.broadcasted_iota(jnp.int32, (TOP_K_BLOCKS, nsp), 1).astype(F32)
    for g in range(N_KV_A):
        p_sum = jnp.sum(p[g * GROUP_R:(g + 1) * GROUP_R, :], axis=0, keepdims=True)
        imp = _dot_exact_rhs(jnp.broadcast_to(p_sum, (8, nc)), cov_ref[...])[0:1, :]
        score = jnp.where(forced, -NEG, jnp.where(valid, imp, NEG))
        score_col = jnp.sum(jnp.where(sub == lan, score, 0.0), axis=-1, keepdims=True)
        ahead = jnp.where(sub < lan, jnp.where(score_col >= score, 1.0, 0.0), jnp.where(score_col > score, 1.0, 0.0))
        rank = jnp.sum(ahead, axis=0, keepdims=True)
        chosen = jnp.sum(jnp.where(rank == k_col, blk_f, 0.0), axis=-1, keepdims=True)
        idx_ref[0, g * TOP_K_BLOCKS:(g + 1) * TOP_K_BLOCKS, :] = jnp.broadcast_to(
            chosen, (TOP_K_BLOCKS, LANES)).astype(jnp.int32)


def _sample_cmp(page_table, pool, w_big, b_big, g_k, q3, bias_c, cover, n_sel, qpos):
    ns, n_pages = page_table.shape
    page_spec = lambda u: pl.BlockSpec((1, KV_ROW, PAGE_SIZE), lambda s, pt: (pt[s, u], 0, 0))
    const = lambda a: pl.BlockSpec(a.shape, lambda s, pt: (0,) * a.ndim, pipeline_mode=pl.Buffered(1))
    grid_spec = pltpu.PrefetchScalarGridSpec(
        num_scalar_prefetch=1,
        grid=(ns,),
        in_specs=[page_spec(u) for u in range(n_pages)]
        + [const(w_big), const(b_big), const(g_k),
           pl.BlockSpec((1, N_HEADS_A, LANES), lambda s, pt: (s, 0, 0)), const(bias_c), const(cover)],
        out_specs=[pl.BlockSpec((1, N_HEADS_A, LANES), lambda s, pt: (s, 0, 0)),
                   pl.BlockSpec((1, N_KV_A * TOP_K_BLOCKS, LANES), lambda s, pt: (s, 0, 0))],
        scratch_shapes=[pltpu.VMEM((n_pages * PAGE_SIZE, KV_W), F32)] * 2,
    )
    return pl.pallas_call(
        functools.partial(_sample_cmp_kernel, n_pages=n_pages, n_sel=n_sel, qpos=qpos),
        grid_spec=grid_spec,
        out_shape=[jax.ShapeDtypeStruct((ns, N_HEADS_A, LANES), F32),
                   jax.ShapeDtypeStruct((ns, N_KV_A * TOP_K_BLOCKS, LANES), jnp.int32)],
        compiler_params=_params("arbitrary"),
        name="sample_cmp_topk",
    )(page_table, *([pool] * n_pages), w_big, b_big, g_k, q3, bias_c, cover)


def _sample_attn_kernel(idx_ref, pt_ref, *refs, n_past_blocks):
    n_slots = N_KV_A * TOP_K_BLOCKS
    blocks = refs[:n_slots]
    (win_ref, q_ref, selnew_ref, winnew_ref, wint_ref, oc_ref, gate_ref, bsel_ref, bwin_ref, b0_ref,
     a_ref, winout_ref) = refs[n_slots:]
    s_id = pl.program_id(0)
    q = q_ref[0]
    qf = q.astype(F32)
    lane = lax.broadcasted_iota(jnp.int32, (N_HEADS_A, LANES), 1)
    row = lax.broadcasted_iota(jnp.int32, (N_HEADS_A, LANES), 0)
    lo = lane < SEL_BLOCK

    def with_new_key(s, vt_mat, new_row):
        s_new = jnp.sum(qf * new_row[:, 0:KV_W], axis=-1, keepdims=True) + b0_ref[:, 0:1]
        m = jnp.maximum(jnp.max(s, axis=-1, keepdims=True), s_new)
        p = jnp.exp(s - m)
        p_new = jnp.exp(s_new - m)
        l = jnp.sum(p, axis=-1, keepdims=True) + p_new
        return (_dot_nt(p.astype(BF16), vt_mat) + p_new * new_row[:, KV_W:KV_ROW]) / l

    sel_new = selnew_ref[0]
    o_s = []
    for g in range(N_KV_A):
        slots = range(g * TOP_K_BLOCKS, (g + 1) * TOP_K_BLOCKS)
        kt_cat = jnp.concatenate([blocks[k][0, 0:KV_W, :] for k in slots], axis=1).astype(BF16)
        vt_cat = jnp.concatenate([blocks[k][0, KV_W:KV_ROW, :] for k in slots], axis=1).astype(BF16)
        adds = []
        for k in slots:
            b = idx_ref[s_id * n_slots + k]
            add = jnp.where(b == n_past_blocks - 2, bsel_ref[0], jnp.where(b == n_past_blocks - 1, bsel_ref[1], 0.0))
            other_half = jnp.where(b % 2 == 0, jnp.where(lo, 0.0, NEG), jnp.where(lo, NEG, 0.0))
            adds.append(add + other_half + jnp.where(b >= n_past_blocks, NEG, 0.0))
        o_s.append(with_new_key(_dot(q, kt_cat) + jnp.concatenate(adds, axis=-1), vt_cat, sel_new))
    o_sel = jnp.where(row < GROUP_R, o_s[0], o_s[1])

    win = win_ref[0]
    wb = win.shape[1]
    w_lane = lax.broadcasted_iota(jnp.int32, (N_HEADS_A, wb), 1)
    s_w = _dot(q, win[0:KV_W, :].astype(BF16)) + bwin_ref[...] + jnp.where(w_lane == 0, NEG, 0.0)
    o_win = with_new_key(s_w, win[KV_W:KV_ROW, :].astype(BF16), winnew_ref[0])

    na = N_HEADS_A
    out = gate_ref[0, 0:na] * oc_ref[0] + gate_ref[0, na:2 * na] * o_sel + gate_ref[0, 2 * na:3 * na] * o_win
    a_ref[0] = jnp.where(lo == (row < GROUP_R), out, 0.0).astype(BF16)

    wint = wint_ref[...]
    seq_lane = lax.broadcasted_iota(jnp.int32, wint.shape, 1)
    new_col = jnp.sum(jnp.where(seq_lane == s_id, wint, 0.0), axis=-1, keepdims=True)
    buf_lane = lax.broadcasted_iota(jnp.int32, win.shape, 1)
    winout_ref[0] = jnp.where(buf_lane == wb - 1, new_col, pltpu.roll(win, wb - 1, 1))


def _sample_attn(idx, page_table, pool, win_t, q3, sel_new, win_new, win_new_t, o_c, gate_b, bsel, bwin, b0):
    ns, n_pages = page_table.shape
    n_slots = N_KV_A * TOP_K_BLOCKS
    per_page = PAGE_SIZE // SEL_BLOCK
    n_past_blocks = n_pages * per_page
    wb = win_t.shape[2]

    def slot_spec(k):
        def index(s, idx_ref, pt_ref):
            b = jnp.minimum(idx_ref[s * n_slots + k], n_past_blocks - 1)
            return (pt_ref[s, b // per_page], 0, 0)
        return pl.BlockSpec((1, KV_ROW, PAGE_SIZE), index)

    per_seq = lambda a: pl.BlockSpec((1,) + a.shape[1:], lambda s, *_: (s,) + (0,) * (a.ndim - 1))
    const = lambda a: pl.BlockSpec(a.shape, lambda s, *_: (0,) * a.ndim, pipeline_mode=pl.Buffered(1))
    grid_spec = pltpu.PrefetchScalarGridSpec(
        num_scalar_prefetch=2,
        grid=(ns,),
        in_specs=[slot_spec(k) for k in range(n_slots)]
        + [per_seq(win_t), per_seq(q3), per_seq(sel_new), per_seq(win_new), const(win_new_t), per_seq(o_c),
           per_seq(gate_b), const(bsel), const(bwin), const(b0)],
        out_specs=[pl.BlockSpec((1, N_HEADS_A, LANES), lambda s, *_: (s, 0, 0)),
                   pl.BlockSpec((1, KV_ROW, wb), lambda s, *_: (s, 0, 0))],
    )
    return pl.pallas_call(
        functools.partial(_sample_attn_kernel, n_past_blocks=n_past_blocks),
        grid_spec=grid_spec,
        out_shape=[jax.ShapeDtypeStruct((ns, N_HEADS_A, LANES), BF16),
                   jax.ShapeDtypeStruct((ns, KV_ROW, wb), F32)],
        compiler_params=_params("arbitrary"),
        name="sample_sel_win",
    )(idx, page_table, *([pool] * n_slots), win_t, q3, sel_new, win_new, win_new_t, o_c, gate_b, bsel, bwin, b0)


def _mlstm_step_kernel(qk_ref, conv_ref, v_ref, o_ref, small_ref, c_ref, n_ref, m_ref, cw_ref, cb_ref, bif_ref,
                       mn_ref, y_ref, c_out, n_out, m_out):
    seqs = qk_ref.shape[0]
    sub = lax.broadcasted_iota(jnp.int32, (HEAD_DIM_M, HEAD_DIM_M), 0)
    lan = lax.broadcasted_iota(jnp.int32, (HEAD_DIM_M, HEAD_DIM_M), 1)
    eye = sub == lan
    lane_row = lax.broadcasted_iota(jnp.int32, (1, LANES), 1)
    for s in range(seqs):
        hist = conv_ref[s]
        y = hist[0:1, :] * cw_ref[0:1, :]
        y = y + hist[1:2, :] * cw_ref[1:2, :]
        y = y + hist[2:3, :] * cw_ref[2:3, :]
        y = y + qk_ref[s] * cw_ref[3:4, :] + cb_ref[...]
        qkc = y * _sigmoid(y)
        gi = small_ref[s] + bif_ref[...]
        lf_all = _log_sigmoid(gi)
        m_all = m_ref[s]
        m_new_all = m_all
        h_all = []
        for h in range(N_HEADS_M):
            hs = slice(h * HEAD_DIM_M, (h + 1) * HEAD_DIM_M)
            ks = slice(D_M + h * HEAD_DIM_M, D_M + (h + 1) * HEAD_DIM_M)
            q = qkc[:, hs]
            k = qkc[:, ks] * F32(HEAD_DIM_M ** -0.5)
            v = v_ref[s][:, hs]
            ci, fi = 3 * N_HEADS_A + h, 3 * N_HEADS_A + N_HEADS_M + h
            ig = gi[:, ci:ci + 1]
            lf = lf_all[:, fi:fi + 1]
            m_prev = m_all[:, h:h + 1]
            c_prev = c_ref[s, h]
            n_prev = n_ref[s, h:h + 1, :]
            inter = lf + m_prev
            m_t = jnp.maximum(inter, ig)
            w_in = jnp.exp(ig - m_t)
            dec = jnp.exp(inter - m_t)
            s_qk = jnp.sum(q * k, axis=-1, keepdims=True) * w_in
            cq = _dot_nt(jnp.broadcast_to(q, (8, HEAD_DIM_M)).astype(BF16), c_prev.astype(BF16))[0:1, :]
            num = s_qk * v + dec * cq
            den = s_qk + dec * jnp.sum(n_prev * q, axis=-1, keepdims=True)
            h_all.append(num / jnp.maximum(jnp.abs(den), jnp.exp(-m_t)))
            v_col = jnp.sum(jnp.where(eye, v, 0.0), axis=-1, keepdims=True)
            c_out[s, h] = dec * c_prev + (w_in * v_col) * k
            n_out[s, h:h + 1, :] = dec * n_prev + w_in * k
            m_new_all = jnp.where(lane_row == h, m_t, m_new_all)
        m_out[s] = m_new_all
        y_ref[s] = _head_norm_gate(h_all, mn_ref, o_ref[s]).astype(BF16)


def _mlstm_step(qk, conv0, v, o, small, c0, n0, m0, conv_w, conv_b, bif, mn_g):
    ns = qk.shape[0]
    sb = 8 if ns % 8 == 0 else 1
    per = lambda a: pl.BlockSpec((sb,) + a.shape[1:], lambda i: (i,) + (0,) * (a.ndim - 1))
    ins = (qk, conv0, v, o, small, c0, n0, m0)
    outs = [jax.ShapeDtypeStruct((ns, 1, D_M), BF16), jax.ShapeDtypeStruct(c0.shape, F32),
            jax.ShapeDtypeStruct(n0.shape, F32), jax.ShapeDtypeStruct(m0.shape, F32)]
    return pl.pallas_call(
        _mlstm_step_kernel,
        grid=(ns // sb,),
        in_specs=[per(a) for a in ins] + [_const_spec(a.shape) for a in (conv_w, conv_b, bif, mn_g)],
        out_specs=[per(a) for a in outs],
        out_shape=outs,
        compiler_params=_params("parallel"),
        name="mlstm_step",
    )(*ins, conv_w, conv_b, bif, mn_g)


def _cover_np(n_cmp, n_sel):
    cs = np.arange(n_cmp)[:, None] * CMP_STRIDE
    bs = np.arange(n_sel)[None, :] * SEL_BLOCK
    shared = np.clip(np.minimum(cs + CMP_LEN, bs + SEL_BLOCK) - np.maximum(cs, bs), 0, None)
    return (shared / CMP_LEN).astype(np.float32)


def _prep_in_proj(w_in):
    d = w_in.shape[0]
    o_kv = D_A
    o_gate = o_kv + 6 * KV_W
    o_m = o_gate + 3 * N_HEADS_A
    o_if = o_m + 4 * D_M
    w_kv_t = w_in[:, o_kv:o_gate].T
    w_m = w_in[:, o_m:o_if]
    w_small = jnp.concatenate([w_in[:, o_gate:o_m], w_in[:, o_if:],
                               jnp.zeros((d, LANES - 3 * N_HEADS_A - 2 * N_HEADS_M), w_in.dtype)], axis=1)
    return [w.astype(BF16) for w in (w_in[:, :D_A].T, w_kv_t, w_m, w_small, w_small.T)]


def _prep_compress(w_k, b_k, w_v, b_v):
    per_c = jnp.stack([w_k, w_k, w_v, w_v]).reshape(4, 2, CMP_STRIDE, HEAD_DIM_A, HEAD_DIM_A)
    w_big = jnp.einsum('chtde,cf->tcdhfe', per_c, jnp.eye(4, dtype=w_k.dtype))
    w_big = w_big.reshape(CMP_STRIDE, KV_ROW, 2 * KV_ROW).astype(BF16)
    pick = lambda h: jnp.concatenate([w_big[:, h * KV_W:(h + 1) * KV_W, h * KV_W:(h + 1) * KV_W],
                                      w_big[:, h * KV_W:(h + 1) * KV_W, KV_ROW + h * KV_W:KV_ROW + (h + 1) * KV_W]],
                                     axis=-1)
    w_big = jnp.stack([pick(0), pick(1)])
    b_big = jnp.concatenate([b_k, b_k, b_v, b_v]).reshape(1, KV_ROW)
    return w_big, b_big


def _bucket_tiles(t, past, wb):
    r = np.arange(LANES)
    tiles = [_t5_bucket_np(d * LANES + r[None, :] - r[:, None]) for d in range(2)]
    for i in range(t // Q_TILE):
        tiles.append(_t5_bucket_np(i * Q_TILE + r[None, :] - (CMP_STRIDE * r[:, None] + CMP_LEN - 1)))
    flat = np.arange(LANES * LANES).reshape(LANES, LANES)
    tiles.append(_t5_bucket_np(past - (CMP_STRIDE * flat + CMP_LEN - 1)))
    tiles.append(_t5_bucket_np(wb - flat))
    tiles.append(_t5_bucket_np(past - (r[:, None] * SEL_BLOCK + r[None, :] % SEL_BLOCK)))
    return np.stack(tiles).astype(np.int32)


def _near_tables(bnear):
    r = np.arange(LANES)
    tile4 = lambda a: jnp.asarray(np.tile(a, (1, GROUP_R)), F32)
    causal = tile4(np.where(r[:, None] <= r[None, :], 0.0, NEG))
    oldest = tile4(np.where(r[:, None] > r[None, :], 0.0, NEG))
    zero = jnp.zeros_like(causal)
    masked = jnp.full_like(causal, NEG)
    n_mid = WINDOW // LANES - 2
    sel_add, win_add = [], []
    for g in range(N_KV_A):
        near = [bnear[1, g], bnear[0, g] + causal]
        sel_add.append(jnp.concatenate(near + [masked], axis=0))
        win_add.append(jnp.concatenate([oldest] + [zero] * n_mid + near + [masked] * (WINDOW // LANES), axis=0))
    return jnp.stack(sel_add), jnp.stack(win_add)


def kernel(x_prompt, x_sample, cache_cmp_kv, cache_sel_kv, cache_win_kv, state_C, state_n, state_m, state_conv, page_table, rel_bias, g_ffn1, w1_gate, w1_up, w1_down, g_mix, w_in, qn_g, kn_cmp_g, kn_sel_g, kn_win_g, w_cmp_k, b_cmp_k, w_cmp_v, b_cmp_v, conv_w, conv_b, b_if, mn_g, w_out, g_ffn2, w2_gate, w2_up, w2_down):
    assert x_prompt.shape[2] == D_MODEL and g_ffn1.shape[0] == 1
    nb, t, _ = x_prompt.shape
    ns, ds, _ = x_sample.shape
    assert ds == 1 and t % Q_TILE == 0
    n_pages = page_table.shape[1]
    past = n_pages * PAGE_SIZE
    wb = cache_win_kv.shape[2]
    assert wb == WINDOW
    kv_shape = (2, N_KV_A, HEAD_DIM_A)

    row = lambda a: a.reshape(1, -1)
    two = lambda a: jnp.tile(a, 2).reshape(1, KV_W)
    w1 = (row(g_ffn1[0]), w1_gate[0].astype(BF16), w1_up[0].astype(BF16), w1_down[0].astype(BF16))
    w2 = (row(g_ffn2[0]), w2_gate[0].astype(BF16), w2_up[0].astype(BF16), w2_down[0].astype(BF16))
    wqt, wkvt, wm, ws, wst = _prep_in_proj(w_in[0])
    qg = (qn_g[0] * F32(HEAD_DIM_A ** -0.5))[:, None]
    kgt = jnp.stack([jnp.tile(kn_sel_g[0], 2), jnp.tile(kn_win_g[0], 2)])[:, :, None]
    w_big, b_big = _prep_compress(w_cmp_k[0], b_cmp_k[0], w_cmp_v[0], b_cmp_v[0])
    g_cmp = two(kn_cmp_g[0])
    woa, wom = w_out[0][:D_A].astype(BF16), w_out[0][D_A:].astype(BF16)
    n_gate = 3 * N_HEADS_A
    bif = jnp.zeros((1, LANES), F32).at[0, n_gate:n_gate + 2 * N_HEADS_M].set(b_if[0])
    conv_b2 = row(conv_b[0])
    mn = row(mn_g[0])

    nqb = t // Q_TILE
    tiles = _bias_tiles(rel_bias, jnp.asarray(_bucket_tiles(t, past, wb)))
    group_lanes = lambda a: a.reshape(a.shape[0], N_KV_A, GROUP_R, LANES, LANES).transpose(0, 1, 3, 2, 4).reshape(
        a.shape[0], N_KV_A, LANES, GROUP_R * LANES)
    bnear = group_lanes(tiles[0:2])
    sel_add, win_add = _near_tables(bnear)
    bcmp = group_lanes(tiles[2:2 + nqb])
    t_cmp, t_win, t_sel = tiles[2 + nqb], tiles[3 + nqb], tiles[4 + nqb]

    def dense_in(x, seq_len):
        x1 = _ffn(x, *w1)
        return (x1,) + tuple(_inproj(x1, seq_len, row(g_mix[0]), wqt, wkvt, wm, ws, wst, qg, kgt))

    from_chan = lambda a: a.reshape((1, a.shape[0]) + kv_shape + (a.shape[2],)).transpose(0, 1, 5, 2, 3, 4)
    to_chan = lambda a: a.transpose(0, 2, 3, 4, 1).reshape(a.shape[0], KV_ROW, a.shape[1])

    (x1, q_t, cmp_rows, cmp_t, sel_t, win_t, ks, vs, kw, vw, qk_m, v_m, o_m, small, gate_t) = dense_in(
        x_prompt.reshape(nb * t, D_MODEL), t)
    seq = lambda a: a.reshape(nb, t, a.shape[-1])
    n16 = t // CMP_STRIDE
    assert n16 == LANES
    kc, vc_t = _compress(cmp_rows, t, w_big, b_big, g_cmp)
    n_sel = -(-t // SEL_BLOCK)
    cov_t = jnp.asarray(np.pad(_cover_np(n16 - 1, n_sel), ((0, 1), (0, 0))).T, BF16)
    expand_np = np.zeros((t, n_sel + MASK_PAD), np.float32)
    expand_np[:, :n_sel] = MASK_BIG * np.repeat(np.eye(n_sel, dtype=np.float32), SEL_BLOCK, axis=0)[:t]
    expand_np[:, n_sel] = -MASK_BIG
    expand_t = jnp.asarray(expand_np, BF16)
    a_pad = _nsa_prompt(q_t, gate_t, kc, vc_t, ks, vs, kw, vw, sel_add, win_add, bcmp, cov_t, expand_t, n16 - 1)
    chunk = 256 if t % 256 == 0 else Q_TILE
    m_out, c_p, n_p, m_p = _mlstm_prompt(seq(qk_m), seq(v_m), seq(o_m), seq(small), conv_w[0], conv_b2, bif, mn,
                                         chunk)
    y_prompt = _post(x1, a_pad.reshape(nb * t, -1), m_out.reshape(nb * t, -1), woa, wom, *w2)
    kv6 = lambda a, n: a.reshape((1, n, -1) + kv_shape)
    prompt_states = (from_chan(cmp_t), from_chan(sel_t), from_chan(win_t[:, :, t - wb:]), c_p[None], n_p[None],
                     m_p[None, :, 0, :N_HEADS_M], seq(qk_m)[None, :, t - (CONV_W - 1):])

    (x1s, q_st, cmp_halves, _, sel_st, win_st, _, _, _, _, qk_s, v_s, o_s, small_s, gate_st) = dense_in(
        x_sample.reshape(ns, D_MODEL), ns)
    cmp_s = jnp.concatenate([cmp_halves[0], cmp_halves[1]], axis=1)
    sel_s, win_s, gates_s = sel_st[0].T, win_st[0].T, gate_st[0].T
    q3 = q_st[0].T.reshape(ns, N_HEADS_A, LANES)
    n16s = past // CMP_STRIDE
    n_sel_s = -(-(past + 1) // SEL_BLOCK)
    n_sel_pad = -(-n_sel_s // LANES) * LANES
    cover_s = jnp.asarray(np.pad(_cover_np(n16s - 1, n_sel_s), ((0, 1), (0, n_sel_pad - n_sel_s))), BF16)
    bias_c = t_cmp[:, :n16s // LANES, :].reshape(N_HEADS_A, n16s)
    o_c, idx = _sample_cmp(page_table, to_chan(cache_cmp_kv[0]), w_big, b_big, g_cmp, q3, bias_c, cover_s, n_sel_s,
                           past)
    n_past_blocks = past // SEL_BLOCK
    two_lanes = lambda a: jnp.concatenate([a[:, :SEL_BLOCK], a[:, :SEL_BLOCK]], axis=-1)
    bsel = jnp.stack([two_lanes(t_sel[:, n_past_blocks - 2, :]), two_lanes(t_sel[:, n_past_blocks - 1, :])])
    bwin = t_win[:, :wb // LANES, :].reshape(N_HEADS_A, wb)
    b0 = jnp.broadcast_to(t_win[:, wb // LANES, 0:1], (N_HEADS_A, LANES))
    gate_b = jnp.broadcast_to(gates_s[:, :n_gate, None], (ns, n_gate, LANES))
    a_s, win_out = _sample_attn(idx[:, :, 0].reshape(-1), page_table, to_chan(cache_sel_kv[0]),
                                to_chan(cache_win_kv[0]), q3, sel_s[:, None, :], win_s[:, None, :], win_st[0],
                                o_c, gate_b, bsel, bwin, b0)
    m0 = jnp.pad(state_m[0], ((0, 0), (0, LANES - N_HEADS_M)))[:, None, :]
    m_s, c_s, n_s, m_new = _mlstm_step(qk_s[:, None, :], state_conv[0], v_s[:, None, :], o_s[:, None, :],
                                       small_s[:, None, :], state_C[0], state_n[0], m0, conv_w[0], conv_b2, bif, mn)
    a_dense = jnp.concatenate([a_s[:, h, (h // GROUP_R) * HEAD_DIM_A:(h // GROUP_R + 1) * HEAD_DIM_A]
                               for h in range(N_HEADS_A)], axis=1)
    y_sample = _post(x1s, a_dense, m_s.reshape(ns, -1), woa, wom, *w2)
    conv_new = jnp.concatenate([state_conv[0][:, 1:], qk_s[:, None, :]], axis=1)
    sample_states = (kv6(cmp_s, ns), kv6(sel_s, ns), from_chan(win_out), c_s[None], n_s[None],
                     m_new[None, :, 0, :N_HEADS_M], conv_new[None])

    return (y_prompt.reshape(nb, t, D_MODEL), y_sample.reshape(ns, 1, D_MODEL)) + prompt_states + sample_states
```

```python
import functools
import math

import jax
import jax.numpy as jnp
import numpy as np
from jax import lax
from jax.experimental import pallas as pl
from jax.experimental.pallas import tpu as pltpu

F32 = jnp.float32
BF16 = jnp.bfloat16

D_MODEL = 1024
PAGE_SIZE = 128
N_HEADS_A = 8
HEAD_DIM_A = 64
N_KV_A = 2
GROUP_R = N_HEADS_A // N_KV_A
D_A = N_HEADS_A * HEAD_DIM_A
KV_W = N_KV_A * HEAD_DIM_A
CMP_STRIDE = 16
CMP_LEN = 2 * CMP_STRIDE
SEL_BLOCK = 64
TOP_K_BLOCKS = 16
WINDOW = 512
N_HEADS_M = 4
HEAD_DIM_M = 128
D_M = N_HEADS_M * HEAD_DIM_M
CONV_W = 4
N_BUCKETS = 32
MAX_DISTANCE = 128
EPS = 1e-6

LANES = 128
Q_TILE = 128
Q_BLOCKS = 2
FAR_TILES = 4
KV_ROW = 2 * KV_W
CHUNK_ROW = CMP_STRIDE * KV_ROW
LOG2_E = 1.4426950408889634
NEG = -1e30
MASK_BIG = 2.0 ** 100
VMEM_LIMIT = 56 * 1024 * 1024

NT_DIMS = (((1,), (1,)), ((), ()))


def _dot(a, b):
    return jnp.dot(a, b, preferred_element_type=F32)


def _dot_nt(a, b):
    return lax.dot_general(a, b, NT_DIMS, preferred_element_type=F32)


def _split3(x):
    x1 = x.astype(BF16)
    r1 = x - x1.astype(F32)
    x2 = r1.astype(BF16)
    x3 = (r1 - x2.astype(F32)).astype(BF16)
    return x1, x2, x3


def _dot_exact_rhs(x, m):
    x1, x2, x3 = _split3(x)
    return _dot(x1, m) + _dot(x2, m) + _dot(x3, m)


def _dot_exact_lhs(m, x):
    x1, x2, x3 = _split3(x)
    return _dot(m, x1) + _dot(m, x2) + _dot(m, x3)


def _rms_rows(x, g):
    ms = jnp.mean(x * x, axis=-1, keepdims=True)
    return x * lax.rsqrt(ms + EPS) * g


def _rms_two_groups(k, g):
    sq = k * k
    lane = lax.broadcasted_iota(jnp.int32, sq.shape, 1)
    lo = lane < HEAD_DIM_A
    s0 = jnp.sum(jnp.where(lo, sq, 0.0), axis=-1, keepdims=True)
    s1 = jnp.sum(jnp.where(lo, 0.0, sq), axis=-1, keepdims=True)
    ms = jnp.where(lo, s0, s1) * (1.0 / HEAD_DIM_A)
    return k * lax.rsqrt(ms + EPS) * g


def _sigmoid(x):
    return jax.nn.sigmoid(x)


def _log_sigmoid(x):
    return jnp.minimum(x, 0.0) - jnp.log1p(jnp.exp(-jnp.abs(x)))


def _const_spec(shape):
    nd = len(shape)
    return pl.BlockSpec(shape, lambda *_: (0,) * nd, pipeline_mode=pl.Buffered(1))


def _params(*sem):
    return pltpu.CompilerParams(dimension_semantics=sem, vmem_limit_bytes=VMEM_LIMIT)


def _swiglu_residual(x, g_ref, wg_ref, wu_ref, wd_ref):
    xn = _rms_rows(x, g_ref[...]).astype(BF16)
    d_ff = wg_ref.shape[1]
    n_split = 2 if d_ff % (2 * LANES) == 0 else 1
    step = d_ff // n_split
    acc = jnp.zeros_like(x)
    for c in range(n_split):
        hg = _dot(xn, wg_ref[:, c * step:(c + 1) * step])
        hu = _dot(xn, wu_ref[:, c * step:(c + 1) * step])
        h = (hg * _sigmoid(hg)) * hu
        acc = acc + _dot(h.astype(BF16), wd_ref[c * step:(c + 1) * step, :])
    return x + 0.5 * acc


def _ffn_kernel(x_ref, g_ref, wg_ref, wu_ref, wd_ref, y_ref):
    y_ref[...] = _swiglu_residual(x_ref[...], g_ref, wg_ref, wu_ref, wd_ref)


def _post_kernel(x_ref, a_ref, m_ref, woa_ref, wom_ref, g_ref, wg_ref, wu_ref, wd_ref, y_ref):
    x = x_ref[...] + (_dot(a_ref[...], woa_ref[...]) + _dot(m_ref[...], wom_ref[...]))
    y_ref[...] = _swiglu_residual(x, g_ref, wg_ref, wu_ref, wd_ref)


def _token_tile(n):
    return 512 if n % 512 == 0 else n


def _ffn(x, g, wg, wu, wd):
    n = x.shape[0]
    tm = _token_tile(n)
    row = lambda w: pl.BlockSpec((tm, w), lambda i: (i, 0))
    return pl.pallas_call(
        _ffn_kernel,
        grid=(n // tm,),
        in_specs=[row(D_MODEL), _const_spec(g.shape), _const_spec(wg.shape), _const_spec(wu.shape),
                  _const_spec(wd.shape)],
        out_specs=row(D_MODEL),
        out_shape=jax.ShapeDtypeStruct((n, D_MODEL), F32),
        compiler_params=_params("parallel"),
        name="ffn1",
    )(x, g, wg, wu, wd)


def _post(x, a_pad, m_out, woa, wom, g, wg, wu, wd):
    n = x.shape[0]
    tm = _token_tile(n)
    row = lambda w: pl.BlockSpec((tm, w), lambda i: (i, 0))
    return pl.pallas_call(
        _post_kernel,
        grid=(n // tm,),
        in_specs=[row(D_MODEL), row(a_pad.shape[1]), row(m_out.shape[1]), _const_spec(woa.shape),
                  _const_spec(wom.shape), _const_spec(g.shape), _const_spec(wg.shape), _const_spec(wu.shape),
                  _const_spec(wd.shape)],
        out_specs=row(D_MODEL),
        out_shape=jax.ShapeDtypeStruct((n, D_MODEL), F32),
        compiler_params=_params("parallel"),
        name="post_mix_ffn2",
    )(x, a_pad, m_out, woa, wom, g, wg, wu, wd)


def _rms_two_groups_t(k, g_col):
    sq = k * k
    half = k.shape[0] // 2
    s0 = jnp.sum(sq[0:half], axis=0, keepdims=True)
    s1 = jnp.sum(sq[half:], axis=0, keepdims=True)
    ms = jnp.concatenate([jnp.broadcast_to(s0, (half, k.shape[1])), jnp.broadcast_to(s1, (half, k.shape[1]))],
                         axis=0) * (1.0 / HEAD_DIM_A)
    return k * lax.rsqrt(ms + EPS) * g_col


def _inproj_kernel(x_ref, kmask_ref, g_ref, wqt_ref, wkvt_ref, wm_ref, ws_ref, wst_ref, qg_ref, kgt_ref,
                   q_out, cmp_rows, cmpt_out, selt_out, wint_out, kst, vst, kwt, vwt, qk_out, v_out, o_out,
                   small_out, gate_out):
    xn = _rms_rows(x_ref[...], g_ref[...]).astype(BF16)
    qt = _dot_nt(wqt_ref[...], xn)
    zeros = jnp.zeros((HEAD_DIM_A, qt.shape[1]), BF16)
    for h in range(N_HEADS_A):
        qh = qt[h * HEAD_DIM_A:(h + 1) * HEAD_DIM_A, :]
        ms = jnp.sum(qh * qh, axis=0, keepdims=True) * (1.0 / HEAD_DIM_A)
        own = h * LANES + (h // GROUP_R) * HEAD_DIM_A
        other = h * LANES + (1 - h // GROUP_R) * HEAD_DIM_A
        q_out[0, own:own + HEAD_DIM_A, :] = (qh * lax.rsqrt(ms + EPS) * qg_ref[...]).astype(BF16)
        q_out[0, other:other + HEAD_DIM_A, :] = zeros
    kvt = _dot_nt(wkvt_ref[...], xn)
    cmpt_out[0] = kvt[0:KV_ROW]
    cmp_rows[0] = kvt[0:KV_W].T
    cmp_rows[1] = kvt[KV_W:KV_ROW].T
    ks = _rms_two_groups_t(kvt[2 * KV_W:3 * KV_W], kgt_ref[0])
    vs = kvt[3 * KV_W:4 * KV_W]
    kw = _rms_two_groups_t(kvt[4 * KV_W:5 * KV_W], kgt_ref[1])
    vw = kvt[5 * KV_W:6 * KV_W]
    selt_out[0, 0:KV_W] = ks
    selt_out[0, KV_W:KV_ROW] = vs
    wint_out[0, 0:KV_W] = kw
    wint_out[0, KV_W:KV_ROW] = vw
    ones = jnp.ones((HEAD_DIM_A, LANES), F32)
    for j in range(kst.shape[1]):
        cols = slice(j * LANES, (j + 1) * LANES)
        kst[0, j] = jnp.concatenate([ks[:, cols].T.astype(BF16), kmask_ref[cols, :]], axis=1)
        kwt[0, j] = kw[:, cols].T.astype(BF16)
        for v, out in ((vs, vst), (vw, vwt)):
            out[0, j, 0] = jnp.concatenate([v[0:HEAD_DIM_A, cols], ones], axis=0).astype(BF16)
            out[0, j, 1] = jnp.concatenate([ones, v[HEAD_DIM_A:KV_W, cols]], axis=0).astype(BF16)
    m = _dot(xn, wm_ref[...])
    qk_out[...] = m[:, 0:2 * D_M]
    v_out[...] = m[:, 2 * D_M:3 * D_M]
    o_out[...] = m[:, 3 * D_M:4 * D_M]
    small_out[...] = _dot(xn, ws_ref[...])
    gate_out[0] = _sigmoid(_dot_nt(wst_ref[...], xn))


def _inproj(x, seq_len, kmask, g, wqt, wkvt, wm, ws, wst, qg, kgt):
    n = x.shape[0]
    nb = n // seq_len
    tm = _token_tile(seq_len)
    tpb = seq_len // tm
    row = lambda w: pl.BlockSpec((tm, w), lambda i: (i, 0))
    rows_out = lambda w, dt: (row(w), jax.ShapeDtypeStruct((n, w), dt))
    chan_out = lambda c, dt: (pl.BlockSpec((1, c, tm), lambda i: (i // tpb, 0, i % tpb)),
                              jax.ShapeDtypeStruct((nb, c, seq_len), dt))
    k_tiles = (pl.BlockSpec((1, tm // LANES, LANES, KV_W), lambda i: (i // tpb, i % tpb, 0, 0)),
               jax.ShapeDtypeStruct((nb, seq_len // LANES, LANES, KV_W), BF16))
    ksel_tiles = (pl.BlockSpec((1, tm // LANES, LANES, 2 * KV_W), lambda i: (i // tpb, i % tpb, 0, 0)),
                  jax.ShapeDtypeStruct((nb, seq_len // LANES, LANES, 2 * KV_W), BF16))
    v_tiles = (pl.BlockSpec((1, tm // LANES, N_KV_A, KV_W, LANES), lambda i: (i // tpb, i % tpb, 0, 0, 0)),
               jax.ShapeDtypeStruct((nb, seq_len // LANES, N_KV_A, KV_W, LANES), BF16))
    halves_out = (pl.BlockSpec((2, tm, KV_W), lambda i: (0, i, 0)), jax.ShapeDtypeStruct((2, n, KV_W), F32))
    outs = [chan_out(N_HEADS_A * LANES, BF16), halves_out, chan_out(KV_ROW, F32), chan_out(KV_ROW, F32),
            chan_out(KV_ROW, F32), ksel_tiles, v_tiles, k_tiles, v_tiles,
            rows_out(2 * D_M, F32), rows_out(D_M, F32), rows_out(D_M, F32), rows_out(LANES, F32),
            chan_out(LANES, F32)]
    consts = (g, wqt, wkvt, wm, ws, wst, qg, kgt)
    return pl.pallas_call(
        _inproj_kernel,
        grid=(n // tm,),
        in_specs=[row(D_MODEL), pl.BlockSpec((tm, LANES), lambda i: (i % tpb, 0))]
        + [_const_spec(a.shape) for a in consts],
        out_specs=[o[0] for o in outs],
        out_shape=[o[1] for o in outs],
        compiler_params=_params("parallel"),
        name="in_proj",
    )(x, kmask, *consts)


def _t5_bucket_np(dist):
    n = np.maximum(dist, 0)
    exact = N_BUCKETS // 2
    nf = np.maximum(n, 1).astype(np.float32)
    ratio = np.log(nf / np.float32(exact)) / np.float32(math.log(MAX_DISTANCE / exact))
    large = exact + (ratio * np.float32(N_BUCKETS - exact)).astype(np.int32)
    return np.where(n < exact, n, np.minimum(large, N_BUCKETS - 1)).astype(np.int32)


def _bias_kernel(tbl_ref, bkt_ref, out_ref):
    bkt = bkt_ref[0]
    for h in range(N_HEADS_A):
        far = tbl_ref[N_BUCKETS - 1, h]
        acc = jnp.zeros(bkt.shape, F32)
        for b in range(N_BUCKETS - 1):
            acc = jnp.where(bkt == b, tbl_ref[b, h] - far, acc)
        out_ref[0, h] = acc


def _bias_tiles(rel_bias, buckets):
    n = buckets.shape[0]
    return pl.pallas_call(
        _bias_kernel,
        grid=(n,),
        in_specs=[pl.BlockSpec(memory_space=pltpu.SMEM),
                  pl.BlockSpec((1, LANES, LANES), lambda i: (i, 0, 0))],
        out_specs=pl.BlockSpec((1, N_HEADS_A, LANES, LANES), lambda i: (i, 0, 0, 0)),
        out_shape=jax.ShapeDtypeStruct((n, N_HEADS_A, LANES, LANES), F32),
        compiler_params=_params("parallel"),
        name="bias_tiles",
    )(rel_bias, buckets)


def _compress_half(rows_ref, nc, w_ref, half, bias):
    fs = None
    for t in range(CMP_STRIDE):
        part = _dot(rows_ref[pl.ds(t, nc, stride=CMP_STRIDE), :].astype(BF16), w_ref[half, t])
        fs = part if fs is None else fs + part
    return fs[:, 0:KV_W] + pltpu.roll(fs[:, KV_W:KV_ROW], nc - 1, 0) + bias


def _compress_rows(k_rows_ref, v_rows_ref, nc, w_ref, b_ref, g_ref):
    kc = _rms_two_groups(_compress_half(k_rows_ref, nc, w_ref, 0, b_ref[:, 0:KV_W]), g_ref[...])
    vc = _compress_half(v_rows_ref, nc, w_ref, 1, b_ref[:, KV_W:KV_ROW])
    complete = lax.broadcasted_iota(jnp.int32, kc.shape, 0) < nc - 1
    return jnp.where(complete, kc, 0.0), jnp.where(complete, vc, 0.0)


def _compress_kernel(x_ref, w_ref, b_ref, g_ref, kc_ref, vc_ref):
    nc = kc_ref.shape[1]
    kc, vc = _compress_rows(x_ref.at[0], x_ref.at[1], nc, w_ref, b_ref, g_ref)
    kc_ref[0] = kc.astype(BF16)
    vc_ref[0] = vc.T.astype(BF16)


def _compress(cmp_rows, t, w_big, b_big, g_k):
    nb = cmp_rows.shape[1] // t
    nc = t // CMP_STRIDE
    return pl.pallas_call(
        _compress_kernel,
        grid=(nb,),
        in_specs=[pl.BlockSpec((2, t, KV_W), lambda b: (0, b, 0)), _const_spec(w_big.shape),
                  _const_spec(b_big.shape), _const_spec(g_k.shape)],
        out_specs=[pl.BlockSpec((1, nc, KV_W), lambda b: (b, 0, 0)),
                   pl.BlockSpec((1, KV_W, nc), lambda b: (b, 0, 0))],
        out_shape=[jax.ShapeDtypeStruct((nb, nc, KV_W), BF16), jax.ShapeDtypeStruct((nb, KV_W, nc), BF16)],
        compiler_params=_params("parallel"),
        name="compress_prompt",
    )(cmp_rows, w_big, b_big, g_k)


def _tile_heads(x):
    return jnp.concatenate([x] * GROUP_R, axis=1)


def _softmax_part(m, s):
    m_new = jnp.maximum(m, jnp.max(s, axis=0, keepdims=True))
    return m_new, jnp.exp2(m - m_new), jnp.exp2(s - m_new).astype(BF16)


def _accumulate(acc, soft, vt):
    m_new, alpha, p = soft
    return m_new, alpha * acc + _dot(vt, p)


def _key_rows(ref, t, n):
    return ref[0, pl.ds(t, n)].reshape(n * LANES, ref.shape[-1])


def _value_cols(ref, t, n, g):
    return jnp.concatenate([ref[0, t + k, g] for k in range(n)], axis=1)


def _rank_rows(score):
    n_blk = score.shape[0]
    blk = lax.broadcasted_iota(jnp.int32, score.shape, 0)
    rank = jnp.zeros(score.shape, F32)
    for b in range(n_blk):
        row = score[b:b + 1, :]
        rank = rank + jnp.where(blk > b, jnp.where(row >= score, 1.0, 0.0), jnp.where(row > score, 1.0, 0.0))
    return rank


def _nsa_prompt_kernel(q_ref, gate_ref, kc_ref, vct_ref, ks_ref, vs_ref, kw_ref, vw_ref, sel_add_ref, win_add_ref,
                       bcmp_ref, covt_ref, a_ref, *, n_cmp):
    ii = pl.program_id(1)
    n_tiles = ks_ref.shape[1]
    n_sel = covt_ref.shape[0]
    wide = GROUP_R * Q_TILE
    heads_of = lambda g: range(g * GROUP_R, (g + 1) * GROUP_R)
    chains =[(sb, g) for sb in range(Q_BLOCKS) for g in range(N_KV_A)]
    blk_i = [Q_BLOCKS * ii + sb for sb in range(Q_BLOCKS)]
    cols = lambda sb: slice(sb * Q_TILE, (sb + 1) * Q_TILE)
    qg_t = {(sb, g): jnp.concatenate([q_ref[0, h * LANES:(h + 1) * LANES, cols(sb)] for h in heads_of(g)], axis=1)
            for sb, g in chains}
    init = (jnp.full((1, wide), NEG, F32), jnp.zeros((LANES, wide), F32))

    n_win = WINDOW // LANES + 1
    t0 = [jnp.maximum(i - (n_win - 1), 0) for i in blk_i]
    s_win = {}
    for sb, i in enumerate(blk_i):
        win_off = pl.multiple_of(jnp.maximum(n_win - 1 - i, 0) * LANES, LANES)
        k_win = _key_rows(kw_ref, t0[sb], n_win)
        for g in range(N_KV_A):
            s_win[sb, g] = _dot(k_win, qg_t[sb, g]) + win_add_ref[g, pl.ds(win_off, n_win * LANES), :]

    ncp = kc_ref.shape[1]
    j_sub = lax.broadcasted_iota(jnp.int32, (ncp, wide), 0)
    q_in_tile = lax.broadcasted_iota(jnp.int32, (ncp, wide), 1) & (Q_TILE - 1)
    s_cmp, any_usable = {}, []
    for sb, i in enumerate(blk_i):
        q_lane = i * Q_TILE + q_in_tile
        usable_add = jnp.where((j_sub * CMP_STRIDE + (CMP_LEN - 1) <= q_lane) & (j_sub < n_cmp), 0.0, NEG)
        any_usable.append(jnp.where(q_lane[0:1, :] >= CMP_LEN - 1, 1.0, 0.0))
        for g in range(N_KV_A):
            s_cmp[sb, g] = _dot(kc_ref[0], qg_t[sb, g]) + bcmp_ref[sb, g] + usable_add
    p_c = {}
    for sb, g in chains:
        e = jnp.exp2(s_cmp[sb, g] - jnp.max(s_cmp[sb, g], axis=0, keepdims=True)) * any_usable[sb]
        p_c[sb, g] = e / jnp.maximum(jnp.sum(e, axis=0, keepdims=True), 1e-30)
    oc_t = {c: _dot(vct_ref[0], p_c[c].astype(BF16)) for c in chains}

    blk = lax.broadcasted_iota(jnp.int32, (n_sel, Q_TILE), 0)
    q_in_blk = lax.broadcasted_iota(jnp.int32, (n_sel, Q_TILE), 1)
    tile_row = lax.broadcasted_iota(jnp.int32, (n_tiles, wide), 0)
    pad_rows = jnp.zeros((LANES - n_sel - n_tiles, wide), BF16)
    q_far, q_near = {}, {}
    for sb, g in chains:
        qpos = blk_i[sb] * Q_TILE + q_in_blk
        cur = qpos // SEL_BLOCK
        valid = blk * SEL_BLOCK <= qpos
        forced = valid & ((blk == 0) | (blk == cur) | (blk == cur - 1))
        p_sum = p_c[sb, g][:, 0:Q_TILE]
        for r in range(1, GROUP_R):
            p_sum = p_sum + p_c[sb, g][:, r * Q_TILE:(r + 1) * Q_TILE]
        imp_t = _dot_exact_lhs(covt_ref[...], p_sum)
        score = jnp.where(forced, -NEG, jnp.where(valid, imp_t, NEG))
        dropped = _tile_heads(jnp.where(_rank_rows(score) < float(min(TOP_K_BLOCKS, n_sel)), 0.0, 1.0).astype(BF16))
        past_far = jnp.where(tile_row > blk_i[sb] - 2, 1.0, 0.0).astype(BF16)
        q_far[sb, g] = jnp.concatenate([qg_t[sb, g], dropped, past_far, pad_rows], axis=0)
        q_near[sb, g] = jnp.concatenate([qg_t[sb, g], dropped, jnp.zeros_like(past_far), pad_rows], axis=0)

    soft_win = {c: _softmax_part(init[0], s_win[c]) for c in chains}
    ow_acc = {(sb, g): _dot(_value_cols(vw_ref, t0[sb], n_win, g), soft_win[sb, g][2]) for sb, g in chains}

    def far_body(it, carry):
        t = FAR_TILES * it
        k_far = _key_rows(ks_ref, t, FAR_TILES)
        s = {c: _dot(k_far, q_far[c]) for c in chains}
        soft = [_softmax_part(carry[n][0], s[c]) for n, c in enumerate(chains)]
        return tuple(_accumulate(carry[n][1], soft[n], _value_cols(vs_ref, t, FAR_TILES, c[1]))
                     for n, c in enumerate(chains))

    n_far = blk_i[-1] - 1
    carry = lax.fori_loop(0, (n_far + FAR_TILES - 1) // FAR_TILES, far_body, (init,) * len(chains))

    t1 = [jnp.maximum(i - 1, 0) for i in blk_i]
    s_near = {}
    for sb, i in enumerate(blk_i):
        k_near = _key_rows(ks_ref, t1[sb], 2)
        near_off = pl.multiple_of(jnp.where(i == 0, LANES, 0), LANES)
        for g in range(N_KV_A):
            s_near[sb, g] = _dot(k_near, q_near[sb, g]) + sel_add_ref[g, pl.ds(near_off, 2 * LANES), :]
    soft_near = [_softmax_part(carry[n][0], s_near[c]) for n, c in enumerate(chains)]
    for n, (sb, g) in enumerate(chains):
        denom = (1 - g) * HEAD_DIM_A
        _, acc = _accumulate(carry[n][1], soft_near[n], _value_cols(vs_ref, t1[sb], 2, g))
        os_t = acc / acc[denom:denom + 1, :]
        ow_t = ow_acc[sb, g] / ow_acc[sb, g][denom:denom + 1, :]

        heads = heads_of(g)
        gate_row = lambda br: jnp.concatenate(
            [gate_ref[0, br * N_HEADS_A + h:br * N_HEADS_A + h + 1, cols(sb)] for h in heads], axis=1)
        out_t = gate_row(0) * oc_t[sb, g] + gate_row(1) * os_t + gate_row(2) * ow_t
        own = slice(g * HEAD_DIM_A, (g + 1) * HEAD_DIM_A)
        for pair in range(GROUP_R // 2):
            two = jnp.concatenate([out_t[own, (2 * pair + k) * Q_TILE:(2 * pair + k + 1) * Q_TILE] for k in range(2)],
                                  axis=0)
            col = (g * GROUP_R // 2 + pair) * LANES
            a_ref[0, cols(sb), col:col + LANES] = two.T.astype(BF16)


def _nsa_prompt(q_t, gate_t, kc, vc_t, ks, vs, kw, vw, sel_add, win_add, bcmp, cov_t, n_cmp):
    nb, _, t = q_t.shape
    nqb = t // Q_TILE
    assert nqb > WINDOW // LANES and nqb % Q_BLOCKS == 0
    step = Q_BLOCKS * Q_TILE
    seq = lambda a: pl.BlockSpec((1,) + a.shape[1:], lambda b, i: (b,) + (0,) * (a.ndim - 1))
    return pl.pallas_call(
        functools.partial(_nsa_prompt_kernel, n_cmp=n_cmp),
        grid=(nb, nqb // Q_BLOCKS),
        in_specs=[pl.BlockSpec((1, N_HEADS_A * LANES, step), lambda b, i: (b, 0, i)),
                  pl.BlockSpec((1, LANES, step), lambda b, i: (b, 0, i)),
                  seq(kc), seq(vc_t), seq(ks), seq(vs), seq(kw), seq(vw),
                  _const_spec(sel_add.shape), _const_spec(win_add.shape),
                  pl.BlockSpec((Q_BLOCKS,) + bcmp.shape[1:], lambda b, i: (i, 0, 0, 0)),
                  _const_spec(cov_t.shape)],
        out_specs=pl.BlockSpec((1, step, D_A), lambda b, i: (b, i, 0)),
        out_shape=jax.ShapeDtypeStruct((nb, t, D_A), BF16),
        compiler_params=_params("parallel", "arbitrary"),
        name="nsa_prompt",
    )(q_t, gate_t, kc, vc_t, ks, vs, kw, vw, sel_add, win_add, bcmp, cov_t)


def _head_norm_gate(h_all, mn_ref, o):
    outs = []
    for h in range(N_HEADS_M):
        hs = slice(h * HEAD_DIM_M, (h + 1) * HEAD_DIM_M)
        outs.append(_rms_rows(h_all[h], mn_ref[:, hs]))
    return _sigmoid(o) * jnp.concatenate(outs, axis=-1)


def _mlstm_prompt_kernel(qk_ref, v_ref, o_ref, small_ref, cw_ref, cb_ref, bif_ref, mn_ref, tri_ref, triu_ref,
                         y_ref, c_ref, n_ref, m_ref, xbuf):
    chunk = qk_ref.shape[1]
    pad = 8

    @pl.when(pl.program_id(1) == 0)
    def _():
        xbuf[0:pad, :] = jnp.zeros((pad, 2 * D_M), F32)
        c_ref[...] = jnp.zeros_like(c_ref)
        n_ref[...] = jnp.zeros_like(n_ref)
        m_ref[...] = jnp.zeros_like(m_ref)

    x = qk_ref[0]
    xbuf[pad:pad + chunk, :] = x
    y = xbuf[pad - 3:pad - 3 + chunk, :] * cw_ref[0:1, :]
    y = y + xbuf[pad - 2:pad - 2 + chunk, :] * cw_ref[1:2, :]
    y = y + xbuf[pad - 1:pad - 1 + chunk, :] * cw_ref[2:3, :]
    y = y + x * cw_ref[3:4, :] + cb_ref[...]
    qkc = y * _sigmoid(y)
    xbuf[0:pad, :] = xbuf[chunk:chunk + pad, :]

    gi = small_ref[0] + bif_ref[...]
    n_gate = 3 * N_HEADS_A
    gate_rows = gi.T[n_gate:n_gate + 2 * N_HEADS_M, :]
    b_col = _dot_exact_lhs(tri_ref[...], _log_sigmoid(gi))
    b_row = _dot_exact_rhs(_log_sigmoid(gate_rows), triu_ref[...])
    t_col = lax.broadcasted_iota(jnp.int32, (chunk, chunk), 0)
    s_row = lax.broadcasted_iota(jnp.int32, (chunk, chunk), 1)
    causal = s_row <= t_col
    m_all = m_ref[0]
    lane_row = lax.broadcasted_iota(jnp.int32, m_all.shape, 1)

    h_all = []
    for h in range(N_HEADS_M):
        hs = slice(h * HEAD_DIM_M, (h + 1) * HEAD_DIM_M)
        ks = slice(D_M + h * HEAD_DIM_M, D_M + (h + 1) * HEAD_DIM_M)
        q = qkc[:, hs]
        k = qkc[:, ks] * F32(HEAD_DIM_M ** -0.5)
        v = v_ref[0, :, hs]
        qb, kb = q.astype(BF16), k.astype(BF16)
        ci, fi = 3 * N_HEADS_A + h, 3 * N_HEADS_A + N_HEADS_M + h
        bt = b_col[:, fi:fi + 1]
        bs = b_row[N_HEADS_M + h:N_HEADS_M + h + 1, :]
        ig_row = gate_rows[h:h + 1, :]
        ig_col = gi[:, ci:ci + 1]
        m_prev = m_all[:, h:h + 1]
        c_prev = c_ref[0, h]
        n_prev = n_ref[0, h:h + 1, :]

        dlog = jnp.where(causal, bt - bs + ig_row, NEG)
        inter = bt + m_prev
        m_t = jnp.maximum(inter, jnp.max(dlog, axis=-1, keepdims=True))
        s_qk = _dot_nt(qb, kb) * jnp.exp(dlog - m_t)
        dec = jnp.exp(inter - m_t)
        num = _dot(s_qk.astype(BF16), v.astype(BF16)) + dec * _dot_nt(qb, c_prev.astype(BF16))
        den = jnp.sum(s_qk, axis=-1, keepdims=True) + dec * jnp.sum(q * n_prev, axis=-1, keepdims=True)
        h_all.append(num / jnp.maximum(jnp.abs(den), jnp.exp(-m_t)))

        m_new = m_t[chunk - 1:chunk, :]
        b_end = bt[chunk - 1:chunk, :]
        w_end = jnp.exp(b_end - bt + ig_col - m_new)
        dec_end = jnp.exp(b_end + m_prev - m_new)
        wv_t = (w_end * v).T.astype(BF16)
        c_ref[0, h] = dec_end * c_prev + _dot(wv_t, kb)
        n_ref[0, h:h + 1, :] = dec_end * n_prev + jnp.sum(w_end * k, axis=0, keepdims=True)
        m_all = jnp.where(lane_row == h, m_new, m_all)

    m_ref[0] = m_all
    y_ref[0] = _head_norm_gate(h_all, mn_ref, o_ref[0]).astype(BF16)


def _mlstm_prompt(qk, v, o, small, conv_w, conv_b, bif, mn_g, chunk):
    nb, t, _ = qk.shape
    tri = jnp.asarray(np.tril(np.ones((chunk, chunk), np.float32)), BF16)
    triu = jnp.asarray(np.triu(np.ones((chunk, chunk), np.float32)), BF16)
    tok = lambda w: pl.BlockSpec((1, chunk, w), lambda b, c: (b, c, 0))
    return pl.pallas_call(
        _mlstm_prompt_kernel,
        grid=(nb, t // chunk),
        in_specs=[tok(2 * D_M), tok(D_M), tok(D_M), tok(LANES)]
        + [_const_spec(a.shape) for a in (conv_w, conv_b, bif, mn_g, tri, triu)],
        out_specs=[tok(D_M),
                   pl.BlockSpec((1, N_HEADS_M, HEAD_DIM_M, HEAD_DIM_M), lambda b, c: (b, 0, 0, 0)),
                   pl.BlockSpec((1, N_HEADS_M, HEAD_DIM_M), lambda b, c: (b, 0, 0)),
                   pl.BlockSpec((1, 1, LANES), lambda b, c: (b, 0, 0))],
        out_shape=[jax.ShapeDtypeStruct((nb, t, D_M), BF16),
                   jax.ShapeDtypeStruct((nb, N_HEADS_M, HEAD_DIM_M, HEAD_DIM_M), F32),
                   jax.ShapeDtypeStruct((nb, N_HEADS_M, HEAD_DIM_M), F32),
                   jax.ShapeDtypeStruct((nb, 1, LANES), F32)],
        scratch_shapes=[pltpu.VMEM((chunk + 8, 2 * D_M), F32)],
        compiler_params=_params("parallel", "arbitrary"),
        name="mlstm_prompt",
    )(qk, v, o, small, conv_w, conv_b, bif, mn_g, tri, triu)


def _softmax_rows(s):
    e = jnp.exp(s - jnp.max(s, axis=-1, keepdims=True))
    return e, jnp.sum(e, axis=-1, keepdims=True)


def _sample_cmp_kernel(pt_ref, *refs, n_pages, n_sel, qpos):
    pages = refs[:n_pages]
    w_ref, b_ref, g_ref, q_ref, bias_ref, cov_ref, oc_ref, idx_ref, xk_ref, xv_ref = refs[n_pages:]
    for half, rows_ref in ((0, xk_ref), (1, xv_ref)):
        for u in range(n_pages):
            rows_ref[u * PAGE_SIZE:(u + 1) * PAGE_SIZE, :] = pages[u][0, half * KV_W:(half + 1) * KV_W, :].T
    nc = n_pages * PAGE_SIZE // CMP_STRIDE
    kc, vc = _compress_rows(xk_ref, xv_ref, nc, w_ref, b_ref, g_ref)
    q = q_ref[0]
    j_row = lax.broadcasted_iota(jnp.int32, (N_HEADS_A, nc), 1)
    s = _dot_nt(q, kc.astype(BF16)) + bias_ref[...] + jnp.where(j_row < nc - 1, 0.0, NEG)
    e, l = _softmax_rows(s)
    p = e / jnp.maximum(l, 1e-30)
    oc_ref[0] = _dot(p.astype(BF16), vc.astype(BF16))

    nsp = cov_ref.shape[1]
    blk_row = lax.broadcasted_iota(jnp.int32, (1, nsp), 1)
    sub = lax.broadcasted_iota(jnp.int32, (nsp, nsp), 0)
    lan = lax.broadcasted_iota(jnp.int32, (nsp, nsp), 1)
    cur = qpos // SEL_BLOCK
    valid = (blk_row * SEL_BLOCK <= qpos) & (blk_row < n_sel)
    forced = valid & ((blk_row == 0) | (blk_row == cur) | (blk_row == cur - 1))
    k_col = lax.broadcasted_iota(jnp.int32, (TOP_K_BLOCKS, nsp), 0).astype(F32)
    blk_f = lax.broadcasted_iota(jnp.int32, (TOP_K_BLOCKS, nsp), 1).astype(F32)
    for g in range(N_KV_A):
        p_sum = jnp.sum(p[g * GROUP_R:(g + 1) * GROUP_R, :], axis=0, keepdims=True)
        imp = _dot_exact_rhs(jnp.broadcast_to(p_sum, (8, nc)), cov_ref[...])[0:1, :]
        score = jnp.where(forced, -NEG, jnp.where(valid, imp, NEG))
        score_col = jnp.sum(jnp.where(sub == lan, score, 0.0), axis=-1, keepdims=True)
        ahead = jnp.where(sub < lan, jnp.where(score_col >= score, 1.0, 0.0), jnp.where(score_col > score, 1.0, 0.0))
        rank = jnp.sum(ahead, axis=0, keepdims=True)
        chosen = jnp.sum(jnp.where(rank == k_col, blk_f, 0.0), axis=-1, keepdims=True)
        idx_ref[0, g * TOP_K_BLOCKS:(g + 1) * TOP_K_BLOCKS, :] = jnp.broadcast_to(
            chosen, (TOP_K_BLOCKS, LANES)).astype(jnp.int32)


def _sample_cmp(page_table, pool, w_big, b_big, g_k, q3, bias_c, cover, n_sel, qpos):
    ns, n_pages = page_table.shape
    page_spec = lambda u: pl.BlockSpec((1, KV_ROW, PAGE_SIZE), lambda s, pt: (pt[s, u], 0, 0))
    const = lambda a: pl.BlockSpec(a.shape, lambda s, pt: (0,) * a.ndim, pipeline_mode=pl.Buffered(1))
    grid_spec = pltpu.PrefetchScalarGridSpec(
        num_scalar_prefetch=1,
        grid=(ns,),
        in_specs=[page_spec(u) for u in range(n_pages)]
        + [const(w_big), const(b_big), const(g_k),
           pl.BlockSpec((1, N_HEADS_A, LANES), lambda s, pt: (s, 0, 0)), const(bias_c), const(cover)],
        out_specs=[pl.BlockSpec((1, N_HEADS_A, LANES), lambda s, pt: (s, 0, 0)),
                   pl.BlockSpec((1, N_KV_A * TOP_K_BLOCKS, LANES), lambda s, pt: (s, 0, 0))],
        scratch_shapes=[pltpu.VMEM((n_pages * PAGE_SIZE, KV_W), F32)] * 2,
    )
    return pl.pallas_call(
        functools.partial(_sample_cmp_kernel, n_pages=n_pages, n_sel=n_sel, qpos=qpos),
        grid_spec=grid_spec,
        out_shape=[jax.ShapeDtypeStruct((ns, N_HEADS_A, LANES), F32),
                   jax.ShapeDtypeStruct((ns, N_KV_A * TOP_K_BLOCKS, LANES), jnp.int32)],
        compiler_params=_params("arbitrary"),
        name="sample_cmp_topk",
    )(page_table, *([pool] * n_pages), w_big, b_big, g_k, q3, bias_c, cover)


def _sample_attn_kernel(idx_ref, pt_ref, *refs, n_past_blocks):
    n_slots = N_KV_A * TOP_K_BLOCKS
    blocks = refs[:n_slots]
    (win_ref, q_ref, selnew_ref, winnew_ref, wint_ref, oc_ref, gate_ref, bsel_ref, bwin_ref, b0_ref,
     a_ref, winout_ref) = refs[n_slots:]
    s_id = pl.program_id(0)
    q = q_ref[0]
    qf = q.astype(F32)
    lane = lax.broadcasted_iota(jnp.int32, (N_HEADS_A, LANES), 1)
    row = lax.broadcasted_iota(jnp.int32, (N_HEADS_A, LANES), 0)
    lo = lane < SEL_BLOCK

    def with_new_key(s, vt_mat, new_row):
        s_new = jnp.sum(qf * new_row[:, 0:KV_W], axis=-1, keepdims=True) + b0_ref[:, 0:1]
        m = jnp.maximum(jnp.max(s, axis=-1, keepdims=True), s_new)
        p = jnp.exp(s - m)
        p_new = jnp.exp(s_new - m)
        l = jnp.sum(p, axis=-1, keepdims=True) + p_new
        return (_dot_nt(p.astype(BF16), vt_mat) + p_new * new_row[:, KV_W:KV_ROW]) / l

    sel_new = selnew_ref[0]
    o_s = []
    for g in range(N_KV_A):
        slots = range(g * TOP_K_BLOCKS, (g + 1) * TOP_K_BLOCKS)
        kt_cat = jnp.concatenate([blocks[k][0, 0:KV_W, :] for k in slots], axis=1).astype(BF16)
        vt_cat = jnp.concatenate([blocks[k][0, KV_W:KV_ROW, :] for k in slots], axis=1).astype(BF16)
        adds = []
        for k in slots:
            b = idx_ref[s_id * n_slots + k]
            add = jnp.where(b == n_past_blocks - 2, bsel_ref[0], jnp.where(b == n_past_blocks - 1, bsel_ref[1], 0.0))
            other_half = jnp.where(b % 2 == 0, jnp.where(lo, 0.0, NEG), jnp.where(lo, NEG, 0.0))
            adds.append(add + other_half + jnp.where(b >= n_past_blocks, NEG, 0.0))
        o_s.append(with_new_key(_dot(q, kt_cat) + jnp.concatenate(adds, axis=-1), vt_cat, sel_new))
    o_sel = jnp.where(row < GROUP_R, o_s[0], o_s[1])

    win = win_ref[0]
    wb = win.shape[1]
    w_lane = lax.broadcasted_iota(jnp.int32, (N_HEADS_A, wb), 1)
    s_w = _dot(q, win[0:KV_W, :].astype(BF16)) + bwin_ref[...] + jnp.where(w_lane == 0, NEG, 0.0)
    o_win = with_new_key(s_w, win[KV_W:KV_ROW, :].astype(BF16), winnew_ref[0])

    na = N_HEADS_A
    out = gate_ref[0, 0:na] * oc_ref[0] + gate_ref[0, na:2 * na] * o_sel + gate_ref[0, 2 * na:3 * na] * o_win
    a_ref[0] = jnp.where(lo == (row < GROUP_R), out, 0.0).astype(BF16)

    wint = wint_ref[...]
    seq_lane = lax.broadcasted_iota(jnp.int32, wint.shape, 1)
    new_col = jnp.sum(jnp.where(seq_lane == s_id, wint, 0.0), axis=-1, keepdims=True)
    buf_lane = lax.broadcasted_iota(jnp.int32, win.shape, 1)
    winout_ref[0] = jnp.where(buf_lane == wb - 1, new_col, pltpu.roll(win, wb - 1, 1))


def _sample_attn(idx, page_table, pool, win_t, q3, sel_new, win_new, win_new_t, o_c, gate_b, bsel, bwin, b0):
    ns, n_pages = page_table.shape
    n_slots = N_KV_A * TOP_K_BLOCKS
    per_page = PAGE_SIZE // SEL_BLOCK
    n_past_blocks = n_pages * per_page
    wb = win_t.shape[2]

    def slot_spec(k):
        def index(s, idx_ref, pt_ref):
            b = jnp.minimum(idx_ref[s * n_slots + k], n_past_blocks - 1)
            return (pt_ref[s, b // per_page], 0, 0)
        return pl.BlockSpec((1, KV_ROW, PAGE_SIZE), index)

    per_seq = lambda a: pl.BlockSpec((1,) + a.shape[1:], lambda s, *_: (s,) + (0,) * (a.ndim - 1))
    const = lambda a: pl.BlockSpec(a.shape, lambda s, *_: (0,) * a.ndim, pipeline_mode=pl.Buffered(1))
    grid_spec = pltpu.PrefetchScalarGridSpec(
        num_scalar_prefetch=2,
        grid=(ns,),
        in_specs=[slot_spec(k) for k in range(n_slots)]
        + [per_seq(win_t), per_seq(q3), per_seq(sel_new), per_seq(win_new), const(win_new_t), per_seq(o_c),
           per_seq(gate_b), const(bsel), const(bwin), const(b0)],
        out_specs=[pl.BlockSpec((1, N_HEADS_A, LANES), lambda s, *_: (s, 0, 0)),
                   pl.BlockSpec((1, KV_ROW, wb), lambda s, *_: (s, 0, 0))],
    )
    return pl.pallas_call(
        functools.partial(_sample_attn_kernel, n_past_blocks=n_past_blocks),
        grid_spec=grid_spec,
        out_shape=[jax.ShapeDtypeStruct((ns, N_HEADS_A, LANES), BF16),
                   jax.ShapeDtypeStruct((ns, KV_ROW, wb), F32)],
        compiler_params=_params("arbitrary"),
        name="sample_sel_win",
    )(idx, page_table, *([pool] * n_slots), win_t, q3, sel_new, win_new, win_new_t, o_c, gate_b, bsel, bwin, b0)


def _mlstm_step_kernel(qk_ref, conv_ref, v_ref, o_ref, small_ref, c_ref, n_ref, m_ref, cw_ref, cb_ref, bif_ref,
                       mn_ref, y_ref, c_out, n_out, m_out):
    seqs = qk_ref.shape[0]
    sub = lax.broadcasted_iota(jnp.int32, (HEAD_DIM_M, HEAD_DIM_M), 0)
    lan = lax.broadcasted_iota(jnp.int32, (HEAD_DIM_M, HEAD_DIM_M), 1)
    eye = sub == lan
    lane_row = lax.broadcasted_iota(jnp.int32, (1, LANES), 1)
    for s in range(seqs):
        hist = conv_ref[s]
        y = hist[0:1, :] * cw_ref[0:1, :]
        y = y + hist[1:2, :] * cw_ref[1:2, :]
        y = y + hist[2:3, :] * cw_ref[2:3, :]
        y = y + qk_ref[s] * cw_ref[3:4, :] + cb_ref[...]
        qkc = y * _sigmoid(y)
        gi = small_ref[s] + bif_ref[...]
        lf_all = _log_sigmoid(gi)
        m_all = m_ref[s]
        m_new_all = m_all
        h_all = []
        for h in range(N_HEADS_M):
            hs = slice(h * HEAD_DIM_M, (h + 1) * HEAD_DIM_M)
            ks = slice(D_M + h * HEAD_DIM_M, D_M + (h + 1) * HEAD_DIM_M)
            q = qkc[:, hs]
            k = qkc[:, ks] * F32(HEAD_DIM_M ** -0.5)
            v = v_ref[s][:, hs]
            ci, fi = 3 * N_HEADS_A + h, 3 * N_HEADS_A + N_HEADS_M + h
            ig = gi[:, ci:ci + 1]
            lf = lf_all[:, fi:fi + 1]
            m_prev = m_all[:, h:h + 1]
            c_prev = c_ref[s, h]
            n_prev = n_ref[s, h:h + 1, :]
            inter = lf + m_prev
            m_t = jnp.maximum(inter, ig)
            w_in = jnp.exp(ig - m_t)
            dec = jnp.exp(inter - m_t)
            s_qk = jnp.sum(q * k, axis=-1, keepdims=True) * w_in
            cq = _dot_nt(jnp.broadcast_to(q, (8, HEAD_DIM_M)).astype(BF16), c_prev.astype(BF16))[0:1, :]
            num = s_qk * v + dec * cq
            den = s_qk + dec * jnp.sum(n_prev * q, axis=-1, keepdims=True)
            h_all.append(num / jnp.maximum(jnp.abs(den), jnp.exp(-m_t)))
            v_col = jnp.sum(jnp.where(eye, v, 0.0), axis=-1, keepdims=True)
            c_out[s, h] = dec * c_prev + (w_in * v_col) * k
            n_out[s, h:h + 1, :] = dec * n_prev + w_in * k
            m_new_all = jnp.where(lane_row == h, m_t, m_new_all)
        m_out[s] = m_new_all
        y_ref[s] = _head_norm_gate(h_all, mn_ref, o_ref[s]).astype(BF16)


def _mlstm_step(qk, conv0, v, o, small, c0, n0, m0, conv_w, conv_b, bif, mn_g):
    ns = qk.shape[0]
    sb = 8 if ns % 8 == 0 else 1
    per = lambda a: pl.BlockSpec((sb,) + a.shape[1:], lambda i: (i,) + (0,) * (a.ndim - 1))
    ins = (qk, conv0, v, o, small, c0, n0, m0)
    outs = [jax.ShapeDtypeStruct((ns, 1, D_M), BF16), jax.ShapeDtypeStruct(c0.shape, F32),
            jax.ShapeDtypeStruct(n0.shape, F32), jax.ShapeDtypeStruct(m0.shape, F32)]
    return pl.pallas_call(
        _mlstm_step_kernel,
        grid=(ns // sb,),
        in_specs=[per(a) for a in ins] + [_const_spec(a.shape) for a in (conv_w, conv_b, bif, mn_g)],
        out_specs=[per(a) for a in outs],
        out_shape=outs,
        compiler_params=_params("parallel"),
        name="mlstm_step",
    )(*ins, conv_w, conv_b, bif, mn_g)


def _cover_np(n_cmp, n_sel):
    cs = np.arange(n_cmp)[:, None] * CMP_STRIDE
    bs = np.arange(n_sel)[None, :] * SEL_BLOCK
    shared = np.clip(np.minimum(cs + CMP_LEN, bs + SEL_BLOCK) - np.maximum(cs, bs), 0, None)
    return (shared / CMP_LEN).astype(np.float32)


def _prep_in_proj(w_in):
    d = w_in.shape[0]
    o_kv = D_A
    o_gate = o_kv + 6 * KV_W
    o_m = o_gate + 3 * N_HEADS_A
    o_if = o_m + 4 * D_M
    w_kv_t = w_in[:, o_kv:o_gate].T
    w_m = w_in[:, o_m:o_if]
    w_small = jnp.concatenate([w_in[:, o_gate:o_m], w_in[:, o_if:],
                               jnp.zeros((d, LANES - 3 * N_HEADS_A - 2 * N_HEADS_M), w_in.dtype)], axis=1)
    return [w.astype(BF16) for w in (w_in[:, :D_A].T, w_kv_t, w_m, w_small, w_small.T)]


def _prep_compress(w_k, b_k, w_v, b_v):
    per_c = jnp.stack([w_k, w_k, w_v, w_v]).reshape(4, 2, CMP_STRIDE, HEAD_DIM_A, HEAD_DIM_A)
    w_big = jnp.einsum('chtde,cf->tcdhfe', per_c, jnp.eye(4, dtype=w_k.dtype))
    w_big = w_big.reshape(CMP_STRIDE, KV_ROW, 2 * KV_ROW).astype(BF16)
    pick = lambda h: jnp.concatenate([w_big[:, h * KV_W:(h + 1) * KV_W, h * KV_W:(h + 1) * KV_W],
                                      w_big[:, h * KV_W:(h + 1) * KV_W, KV_ROW + h * KV_W:KV_ROW + (h + 1) * KV_W]],
                                     axis=-1)
    w_big = jnp.stack([pick(0), pick(1)])
    b_big = jnp.concatenate([b_k, b_k, b_v, b_v]).reshape(1, KV_ROW)
    return w_big, b_big


def _bucket_tiles(t, past, wb):
    r = np.arange(LANES)
    tiles = [_t5_bucket_np(d * LANES + r[None, :] - r[:, None]) for d in range(2)]
    for i in range(t // Q_TILE):
        tiles.append(_t5_bucket_np(i * Q_TILE + r[None, :] - (CMP_STRIDE * r[:, None] + CMP_LEN - 1)))
    flat = np.arange(LANES * LANES).reshape(LANES, LANES)
    tiles.append(_t5_bucket_np(past - (CMP_STRIDE * flat + CMP_LEN - 1)))
    tiles.append(_t5_bucket_np(wb - flat))
    tiles.append(_t5_bucket_np(past - (r[:, None] * SEL_BLOCK + r[None, :] % SEL_BLOCK)))
    return np.stack(tiles).astype(np.int32)


def _near_tables(bnear):
    r = np.arange(LANES)
    tile4 = lambda a: jnp.asarray(np.tile(a, (1, GROUP_R)), F32)
    causal = tile4(np.where(r[:, None] <= r[None, :], 0.0, NEG))
    oldest = tile4(np.where(r[:, None] > r[None, :], 0.0, NEG))
    zero = jnp.zeros_like(causal)
    masked = jnp.full_like(causal, NEG)
    n_mid = WINDOW // LANES - 2
    sel_add, win_add = [], []
    for g in range(N_KV_A):
        near = [bnear[1, g], bnear[0, g] + causal]
        sel_add.append(jnp.concatenate(near + [masked], axis=0))
        win_add.append(jnp.concatenate([oldest] + [zero] * n_mid + near + [masked] * (WINDOW // LANES), axis=0))
    return jnp.stack(sel_add), jnp.stack(win_add)


def kernel(x_prompt, x_sample, cache_cmp_kv, cache_sel_kv, cache_win_kv, state_C, state_n, state_m, state_conv, page_table, rel_bias, g_ffn1, w1_gate, w1_up, w1_down, g_mix, w_in, qn_g, kn_cmp_g, kn_sel_g, kn_win_g, w_cmp_k, b_cmp_k, w_cmp_v, b_cmp_v, conv_w, conv_b, b_if, mn_g, w_out, g_ffn2, w2_gate, w2_up, w2_down):
    assert x_prompt.shape[2] == D_MODEL and g_ffn1.shape[0] == 1
    nb, t, _ = x_prompt.shape
    ns, ds, _ = x_sample.shape
    assert ds == 1 and t % Q_TILE == 0
    n_pages = page_table.shape[1]
    past = n_pages * PAGE_SIZE
    wb = cache_win_kv.shape[2]
    assert wb == WINDOW
    kv_shape = (2, N_KV_A, HEAD_DIM_A)

    row = lambda a: a.reshape(1, -1)
    two = lambda a: jnp.tile(a, 2).reshape(1, KV_W)
    w1 = (row(g_ffn1[0]), w1_gate[0].astype(BF16), w1_up[0].astype(BF16), w1_down[0].astype(BF16))
    w2 = (row(g_ffn2[0]), w2_gate[0].astype(BF16), w2_up[0].astype(BF16), w2_down[0].astype(BF16))
    wqt, wkvt, wm, ws, wst = _prep_in_proj(w_in[0])
    qg = (qn_g[0] * F32(HEAD_DIM_A ** -0.5))[:, None]
    kgt = jnp.stack([jnp.tile(kn_sel_g[0], 2), jnp.tile(kn_win_g[0], 2)])[:, :, None]
    w_big, b_big = _prep_compress(w_cmp_k[0], b_cmp_k[0], w_cmp_v[0], b_cmp_v[0])
    g_cmp = two(kn_cmp_g[0])
    woa, wom = w_out[0][:D_A].astype(BF16), w_out[0][D_A:].astype(BF16)
    n_gate = 3 * N_HEADS_A
    bif = jnp.zeros((1, LANES), F32).at[0, n_gate:n_gate + 2 * N_HEADS_M].set(b_if[0])
    conv_b2 = row(conv_b[0])
    mn = row(mn_g[0])

    nqb = t // Q_TILE
    tiles = _bias_tiles(rel_bias, jnp.asarray(_bucket_tiles(t, past, wb)))
    group_lanes = lambda a: a.reshape(a.shape[0], N_KV_A, GROUP_R, LANES, LANES).transpose(0, 1, 3, 2, 4).reshape(
        a.shape[0], N_KV_A, LANES, GROUP_R * LANES)
    bnear = group_lanes(tiles[0:2]) * F32(LOG2_E)
    sel_add, win_add = _near_tables(bnear)
    bcmp = group_lanes(tiles[2:2 + nqb]) * F32(LOG2_E)
    t_cmp, t_win, t_sel = tiles[2 + nqb], tiles[3 + nqb], tiles[4 + nqb]

    def dense_in(x, seq_len, q_gain, kmask):
        x1 = _ffn(x, *w1)
        return (x1,) + tuple(_inproj(x1, seq_len, kmask, row(g_mix[0]), wqt, wkvt, wm, ws, wst, q_gain, kgt))

    n_sel_p = -(-t // SEL_BLOCK)
    assert n_sel_p + t // LANES <= LANES
    kmask_np = np.zeros((t, LANES), np.float32)
    kmask_np[np.arange(t), np.arange(t) // SEL_BLOCK] = -MASK_BIG
    kmask_np[np.arange(t), n_sel_p + np.arange(t) // LANES] = -MASK_BIG

    from_chan = lambda a: a.reshape((1, a.shape[0]) + kv_shape + (a.shape[2],)).transpose(0, 1, 5, 2, 3, 4)
    to_chan = lambda a: a.transpose(0, 2, 3, 4, 1).reshape(a.shape[0], KV_ROW, a.shape[1])

    (x1, q_t, cmp_rows, cmp_t, sel_t, win_t, ks, vs, kw, vw, qk_m, v_m, o_m, small, gate_t) = dense_in(
        x_prompt.reshape(nb * t, D_MODEL), t, qg * F32(LOG2_E),
        jnp.asarray(kmask_np, BF16))
    seq = lambda a: a.reshape(nb, t, a.shape[-1])
    n16 = t // CMP_STRIDE
    assert n16 == LANES
    kc, vc_t = _compress(cmp_rows, t, w_big, b_big, g_cmp)
    n_sel = -(-t // SEL_BLOCK)
    cov_t = jnp.asarray(np.pad(_cover_np(n16 - 1, n_sel), ((0, 1), (0, 0))).T, BF16)
    a_pad = _nsa_prompt(q_t, gate_t, kc, vc_t, ks, vs, kw, vw, sel_add, win_add, bcmp, cov_t, n16 - 1)
    chunk = 256 if t % 256 == 0 else Q_TILE
    m_out, c_p, n_p, m_p = _mlstm_prompt(seq(qk_m), seq(v_m), seq(o_m), seq(small), conv_w[0], conv_b2, bif, mn,
                                         chunk)
    y_prompt = _post(x1, a_pad.reshape(nb * t, -1), m_out.reshape(nb * t, -1), woa, wom, *w2)
    kv6 = lambda a, n: a.reshape((1, n, -1) + kv_shape)
    prompt_states = (from_chan(cmp_t), from_chan(sel_t), from_chan(win_t[:, :, t - wb:]), c_p[None], n_p[None],
                     m_p[None, :, 0, :N_HEADS_M], seq(qk_m)[None, :, t - (CONV_W - 1):])

    (x1s, q_st, cmp_halves, _, sel_st, win_st, _, _, _, _, qk_s, v_s, o_s, small_s, gate_st) = dense_in(
        x_sample.reshape(ns, D_MODEL), ns, qg, jnp.zeros((ns, LANES), BF16))
    cmp_s = jnp.concatenate([cmp_halves[0], cmp_halves[1]], axis=1)
    sel_s, win_s, gates_s = sel_st[0].T, win_st[0].T, gate_st[0].T
    q3 = q_st[0].T.reshape(ns, N_HEADS_A, LANES)
    n16s = past // CMP_STRIDE
    n_sel_s = -(-(past + 1) // SEL_BLOCK)
    n_sel_pad = -(-n_sel_s // LANES) * LANES
    cover_s = jnp.asarray(np.pad(_cover_np(n16s - 1, n_sel_s), ((0, 1), (0, n_sel_pad - n_sel_s))), BF16)
    bias_c = t_cmp[:, :n16s // LANES, :].reshape(N_HEADS_A, n16s)
    o_c, idx = _sample_cmp(page_table, to_chan(cache_cmp_kv[0]), w_big, b_big, g_cmp, q3, bias_c, cover_s, n_sel_s,
                           past)
    n_past_blocks = past // SEL_BLOCK
    two_lanes = lambda a: jnp.concatenate([a[:, :SEL_BLOCK], a[:, :SEL_BLOCK]], axis=-1)
    bsel = jnp.stack([two_lanes(t_sel[:, n_past_blocks - 2, :]), two_lanes(t_sel[:, n_past_blocks - 1, :])])
    bwin = t_win[:, :wb // LANES, :].reshape(N_HEADS_A, wb)
    b0 = jnp.broadcast_to(t_win[:, wb // LANES, 0:1], (N_HEADS_A, LANES))
    gate_b = jnp.broadcast_to(gates_s[:, :n_gate, None], (ns, n_gate, LANES))
    a_s, win_out = _sample_attn(idx[:, :, 0].reshape(-1), page_table, to_chan(cache_sel_kv[0]),
                                to_chan(cache_win_kv[0]), q3, sel_s[:, None, :], win_s[:, None, :], win_st[0],
                                o_c, gate_b, bsel, bwin, b0)
    m0 = jnp.pad(state_m[0], ((0, 0), (0, LANES - N_HEADS_M)))[:, None, :]
    m_s, c_s, n_s, m_new = _mlstm_step(qk_s[:, None, :], state_conv[0], v_s[:, None, :], o_s[:, None, :],
                                       small_s[:, None, :], state_C[0], state_n[0], m0, conv_w[0], conv_b2, bif, mn)
    a_dense = jnp.concatenate([a_s[:, h, (h // GROUP_R) * HEAD_DIM_A:(h // GROUP_R + 1) * HEAD_DIM_A]
                               for h in range(N_HEADS_A)], axis=1)
    y_sample = _post(x1s, a_dense, m_s.reshape(ns, -1), woa, wom, *w2)
    conv_new = jnp.concatenate([state_conv[0][:, 1:], qk_s[:, None, :]], axis=1)
    sample_states = (kv6(cmp_s, ns), kv6(sel_s, ns), from_chan(win_out), c_s[None], n_s[None],
                     m_new[None, :, 0, :N_HEADS_M], conv_new[None])

    return (y_prompt.reshape(nb, t, D_MODEL), y_sample.reshape(ns, 1, D_MODEL)) + prompt_states + sample_states
```

```python
import functools
import math

import jax
import jax.numpy as jnp
import numpy as np
from jax import lax
from jax.experimental import pallas as pl
from jax.experimental.pallas import tpu as pltpu

F32 = jnp.float32
BF16 = jnp.bfloat16

D_MODEL = 1024
PAGE_SIZE = 128
N_HEADS_A = 8
HEAD_DIM_A = 64
N_KV_A = 2
GROUP_R = N_HEADS_A // N_KV_A
D_A = N_HEADS_A * HEAD_DIM_A
KV_W = N_KV_A * HEAD_DIM_A
CMP_STRIDE = 16
CMP_LEN = 2 * CMP_STRIDE
SEL_BLOCK = 64
TOP_K_BLOCKS = 16
WINDOW = 512
N_HEADS_M = 4
HEAD_DIM_M = 128
D_M = N_HEADS_M * HEAD_DIM_M
CONV_W = 4
N_BUCKETS = 32
MAX_DISTANCE = 128
EPS = 1e-6

LANES = 128
SUBLANES = 8
TOKEN_TILE = 512
MLSTM_CHUNK = 256
STEP_SEQS = 8
Q_TILE = 128
Q_BLOCKS = 2
V_ROWS = HEAD_DIM_A + 16
FAR_TILES = 4
KV_ROW = 2 * KV_W
CHUNK_ROW = CMP_STRIDE * KV_ROW
LOG2_E = 1.4426950408889634
NEG = -1e30
MASK_BIG = 2.0 ** 100
VMEM_LIMIT = 56 * 1024 * 1024

NT_DIMS = (((1,), (1,)), ((), ()))


def _dot(a, b):
    return jnp.dot(a, b, preferred_element_type=F32)


def _dot_nt(a, b):
    return lax.dot_general(a, b, NT_DIMS, preferred_element_type=F32)


def _split3(x):
    x1 = x.astype(BF16)
    r1 = x - x1.astype(F32)
    x2 = r1.astype(BF16)
    x3 = (r1 - x2.astype(F32)).astype(BF16)
    return x1, x2, x3


def _dot_exact_rhs(x, m):
    x1, x2, x3 = _split3(x)
    return _dot(x1, m) + _dot(x2, m) + _dot(x3, m)


def _dot_exact_lhs(m, x):
    x1, x2, x3 = _split3(x)
    return _dot(m, x1) + _dot(m, x2) + _dot(m, x3)


def _rms_rows(x, g):
    ms = jnp.mean(x * x, axis=-1, keepdims=True)
    return x * lax.rsqrt(ms + EPS) * g


def _rms_two_groups(k, g):
    sq = k * k
    lane = lax.broadcasted_iota(jnp.int32, sq.shape, 1)
    lo = lane < HEAD_DIM_A
    s0 = jnp.sum(jnp.where(lo, sq, 0.0), axis=-1, keepdims=True)
    s1 = jnp.sum(jnp.where(lo, 0.0, sq), axis=-1, keepdims=True)
    ms = jnp.where(lo, s0, s1) * (1.0 / HEAD_DIM_A)
    return k * lax.rsqrt(ms + EPS) * g


def _sigmoid(x):
    return jax.nn.sigmoid(x)


def _log_sigmoid(x):
    return jnp.minimum(x, 0.0) - jnp.log1p(jnp.exp(-jnp.abs(x)))


def _const_spec(shape):
    nd = len(shape)
    return pl.BlockSpec(shape, lambda *_: (0,) * nd, pipeline_mode=pl.Buffered(1))


def _params(*sem):
    return pltpu.CompilerParams(dimension_semantics=sem, vmem_limit_bytes=VMEM_LIMIT)


def _swiglu_residual(x, g_ref, wg_ref, wu_ref, wd_ref):
    xn = _rms_rows(x, g_ref[...]).astype(BF16)
    d_ff = wg_ref.shape[1]
    n_split = 2 if d_ff % (2 * LANES) == 0 else 1
    step = d_ff // n_split
    acc = jnp.zeros_like(x)
    for c in range(n_split):
        hg = _dot(xn, wg_ref[:, c * step:(c + 1) * step])
        hu = _dot(xn, wu_ref[:, c * step:(c + 1) * step])
        h = (hg * _sigmoid(hg)) * hu
        acc = acc + _dot(h.astype(BF16), wd_ref[c * step:(c + 1) * step, :])
    return x + 0.5 * acc


def _ffn_kernel(x_ref, g_ref, wg_ref, wu_ref, wd_ref, y_ref):
    y_ref[...] = _swiglu_residual(x_ref[...], g_ref, wg_ref, wu_ref, wd_ref)


def _post_kernel(x_ref, a_ref, m_ref, woa_ref, wom_ref, g_ref, wg_ref, wu_ref, wd_ref, y_ref):
    x = x_ref[...] + (_dot(a_ref[...], woa_ref[...]) + _dot(m_ref[...], wom_ref[...]))
    y_ref[...] = _swiglu_residual(x, g_ref, wg_ref, wu_ref, wd_ref)


def _token_tile(n):
    return TOKEN_TILE if n % TOKEN_TILE == 0 else n


def _ffn(x, g, wg, wu, wd):
    n = x.shape[0]
    tm = _token_tile(n)
    row = lambda w: pl.BlockSpec((tm, w), lambda i: (i, 0))
    return pl.pallas_call(
        _ffn_kernel,
        grid=(n // tm,),
        in_specs=[row(D_MODEL), _const_spec(g.shape), _const_spec(wg.shape), _const_spec(wu.shape),
                  _const_spec(wd.shape)],
        out_specs=row(D_MODEL),
        out_shape=jax.ShapeDtypeStruct((n, D_MODEL), F32),
        compiler_params=_params("parallel"),
        name="ffn1",
    )(x, g, wg, wu, wd)


def _post(x, a_pad, m_out, woa, wom, g, wg, wu, wd):
    n = x.shape[0]
    tm = _token_tile(n)
    row = lambda w: pl.BlockSpec((tm, w), lambda i: (i, 0))
    return pl.pallas_call(
        _post_kernel,
        grid=(n // tm,),
        in_specs=[row(D_MODEL), row(a_pad.shape[1]), row(m_out.shape[1]), _const_spec(woa.shape),
                  _const_spec(wom.shape), _const_spec(g.shape), _const_spec(wg.shape), _const_spec(wu.shape),
                  _const_spec(wd.shape)],
        out_specs=row(D_MODEL),
        out_shape=jax.ShapeDtypeStruct((n, D_MODEL), F32),
        compiler_params=_params("parallel"),
        name="post_mix_ffn2",
    )(x, a_pad, m_out, woa, wom, g, wg, wu, wd)


def _rms_two_groups_t(k, g_col):
    sq = k * k
    half = k.shape[0] // 2
    s0 = jnp.sum(sq[0:half], axis=0, keepdims=True)
    s1 = jnp.sum(sq[half:], axis=0, keepdims=True)
    ms = jnp.concatenate([jnp.broadcast_to(s0, (half, k.shape[1])), jnp.broadcast_to(s1, (half, k.shape[1]))],
                         axis=0) * (1.0 / HEAD_DIM_A)
    return k * lax.rsqrt(ms + EPS) * g_col


def _inproj_kernel(x_ref, kmask_ref, g_ref, wqt_ref, wkvt_ref, wm_ref, ws_ref, wst_ref, qg_ref, kgt_ref,
                   q_out, cmp_rows, cmpt_out, selt_out, wint_out, kst, vst, kwt, vwt, qk_out, v_out, o_out,
                   small_out, gate_out):
    xn = _rms_rows(x_ref[...], g_ref[...]).astype(BF16)
    qt = _dot_nt(wqt_ref[...], xn)
    zeros = jnp.zeros((HEAD_DIM_A, qt.shape[1]), BF16)
    for h in range(N_HEADS_A):
        qh = qt[h * HEAD_DIM_A:(h + 1) * HEAD_DIM_A, :]
        ms = jnp.sum(qh * qh, axis=0, keepdims=True) * (1.0 / HEAD_DIM_A)
        own = h * LANES + (h // GROUP_R) * HEAD_DIM_A
        other = h * LANES + (1 - h // GROUP_R) * HEAD_DIM_A
        q_out[0, own:own + HEAD_DIM_A, :] = (qh * lax.rsqrt(ms + EPS) * qg_ref[...]).astype(BF16)
        q_out[0, other:other + HEAD_DIM_A, :] = zeros
    kvt = _dot_nt(wkvt_ref[...], xn)
    cmpt_out[0] = kvt[0:KV_ROW]
    cmp_rows[0] = kvt[0:KV_W].T
    cmp_rows[1] = kvt[KV_W:KV_ROW].T
    ks = _rms_two_groups_t(kvt[2 * KV_W:3 * KV_W], kgt_ref[0])
    vs = kvt[3 * KV_W:4 * KV_W]
    kw = _rms_two_groups_t(kvt[4 * KV_W:5 * KV_W], kgt_ref[1])
    vw = kvt[5 * KV_W:6 * KV_W]
    selt_out[0, 0:KV_W] = ks
    selt_out[0, KV_W:KV_ROW] = vs
    wint_out[0, 0:KV_W] = kw
    wint_out[0, KV_W:KV_ROW] = vw
    ones = jnp.ones((V_ROWS - HEAD_DIM_A, LANES), F32)
    for j in range(kst.shape[1]):
        cols = slice(j * LANES, (j + 1) * LANES)
        kst[0, j] = jnp.concatenate([ks[:, cols].T.astype(BF16), kmask_ref[cols, :]], axis=1)
        kwt[0, j] = kw[:, cols].T.astype(BF16)
        for v, out in ((vs, vst), (vw, vwt)):
            for g in range(N_KV_A):
                out[0, j, g] = jnp.concatenate([v[g * HEAD_DIM_A:(g + 1) * HEAD_DIM_A, cols], ones],
                                               axis=0).astype(BF16)
    m = _dot(xn, wm_ref[...])
    qk_out[...] = m[:, 0:2 * D_M]
    v_out[...] = m[:, 2 * D_M:3 * D_M]
    o_out[...] = m[:, 3 * D_M:4 * D_M]
    small_out[...] = _dot(xn, ws_ref[...])
    gate_out[0] = _sigmoid(_dot_nt(wst_ref[...], xn))


def _inproj(x, seq_len, kmask, g, wqt, wkvt, wm, ws, wst, qg, kgt):
    n = x.shape[0]
    nb = n // seq_len
    tm = _token_tile(seq_len)
    tpb = seq_len // tm
    row = lambda w: pl.BlockSpec((tm, w), lambda i: (i, 0))
    rows_out = lambda w, dt: (row(w), jax.ShapeDtypeStruct((n, w), dt))
    chan_out = lambda c, dt: (pl.BlockSpec((1, c, tm), lambda i: (i // tpb, 0, i % tpb)),
                              jax.ShapeDtypeStruct((nb, c, seq_len), dt))
    k_tiles = (pl.BlockSpec((1, tm // LANES, LANES, KV_W), lambda i: (i // tpb, i % tpb, 0, 0)),
               jax.ShapeDtypeStruct((nb, seq_len // LANES, LANES, KV_W), BF16))
    ksel_tiles = (pl.BlockSpec((1, tm // LANES, LANES, 2 * KV_W), lambda i: (i // tpb, i % tpb, 0, 0)),
                  jax.ShapeDtypeStruct((nb, seq_len // LANES, LANES, 2 * KV_W), BF16))
    v_tiles = (pl.BlockSpec((1, tm // LANES, N_KV_A, V_ROWS, LANES), lambda i: (i // tpb, i % tpb, 0, 0, 0)),
               jax.ShapeDtypeStruct((nb, seq_len // LANES, N_KV_A, V_ROWS, LANES), BF16))
    halves_out = (pl.BlockSpec((2, tm, KV_W), lambda i: (0, i, 0)), jax.ShapeDtypeStruct((2, n, KV_W), F32))
    outs = [chan_out(N_HEADS_A * LANES, BF16), halves_out, chan_out(KV_ROW, F32), chan_out(KV_ROW, F32),
            chan_out(KV_ROW, F32), ksel_tiles, v_tiles, k_tiles, v_tiles,
            rows_out(2 * D_M, F32), rows_out(D_M, F32), rows_out(D_M, F32), rows_out(LANES, F32),
            chan_out(LANES, F32)]
    consts = (g, wqt, wkvt, wm, ws, wst, qg, kgt)
    return pl.pallas_call(
        _inproj_kernel,
        grid=(n // tm,),
        in_specs=[row(D_MODEL), pl.BlockSpec((tm, LANES), lambda i: (i % tpb, 0))]
        + [_const_spec(a.shape) for a in consts],
        out_specs=[o[0] for o in outs],
        out_shape=[o[1] for o in outs],
        compiler_params=_params("parallel"),
        name="in_proj",
    )(x, kmask, *consts)


def _t5_bucket_np(dist):
    n = np.maximum(dist, 0)
    exact = N_BUCKETS // 2
    nf = np.maximum(n, 1).astype(np.float32)
    ratio = np.log(nf / np.float32(exact)) / np.float32(math.log(MAX_DISTANCE / exact))
    large = exact + (ratio * np.float32(N_BUCKETS - exact)).astype(np.int32)
    return np.where(n < exact, n, np.minimum(large, N_BUCKETS - 1)).astype(np.int32)


def _bias_kernel(tbl_ref, bkt_ref, out_ref):
    bkt = bkt_ref[0]
    for h in range(N_HEADS_A):
        far = tbl_ref[N_BUCKETS - 1, h]
        acc = jnp.zeros(bkt.shape, F32)
        for b in range(N_BUCKETS - 1):
            acc = jnp.where(bkt == b, tbl_ref[b, h] - far, acc)
        out_ref[0, h] = acc


def _bias_tiles(rel_bias, buckets):
    n = buckets.shape[0]
    return pl.pallas_call(
        _bias_kernel,
        grid=(n,),
        in_specs=[pl.BlockSpec(memory_space=pltpu.SMEM),
                  pl.BlockSpec((1, LANES, LANES), lambda i: (i, 0, 0))],
        out_specs=pl.BlockSpec((1, N_HEADS_A, LANES, LANES), lambda i: (i, 0, 0, 0)),
        out_shape=jax.ShapeDtypeStruct((n, N_HEADS_A, LANES, LANES), F32),
        compiler_params=_params("parallel"),
        name="bias_tiles",
    )(rel_bias, buckets)


def _compress_half(rows_ref, nc, w_ref, half, bias):
    fs = None
    for t in range(CMP_STRIDE):
        part = _dot(rows_ref[pl.ds(t, nc, stride=CMP_STRIDE), :].astype(BF16), w_ref[half, t])
        fs = part if fs is None else fs + part
    return fs[:, 0:KV_W] + pltpu.roll(fs[:, KV_W:KV_ROW], nc - 1, 0) + bias


def _compress_rows(k_rows_ref, v_rows_ref, nc, w_ref, b_ref, g_ref):
    kc = _rms_two_groups(_compress_half(k_rows_ref, nc, w_ref, 0, b_ref[:, 0:KV_W]), g_ref[...])
    vc = _compress_half(v_rows_ref, nc, w_ref, 1, b_ref[:, KV_W:KV_ROW])
    complete = lax.broadcasted_iota(jnp.int32, kc.shape, 0) < nc - 1
    return jnp.where(complete, kc, 0.0), jnp.where(complete, vc, 0.0)


def _compress_kernel(x_ref, w_ref, b_ref, g_ref, kc_ref, vc_ref):
    nc = kc_ref.shape[1]
    kc, vc = _compress_rows(x_ref.at[0], x_ref.at[1], nc, w_ref, b_ref, g_ref)
    kc_ref[0] = kc.astype(BF16)
    vc_ref[0] = vc.T.astype(BF16)


def _compress(cmp_rows, t, w_big, b_big, g_k):
    nb = cmp_rows.shape[1] // t
    nc = t // CMP_STRIDE
    return pl.pallas_call(
        _compress_kernel,
        grid=(nb,),
        in_specs=[pl.BlockSpec((2, t, KV_W), lambda b: (0, b, 0)), _const_spec(w_big.shape),
                  _const_spec(b_big.shape), _const_spec(g_k.shape)],
        out_specs=[pl.BlockSpec((1, nc, KV_W), lambda b: (b, 0, 0)),
                   pl.BlockSpec((1, KV_W, nc), lambda b: (b, 0, 0))],
        out_shape=[jax.ShapeDtypeStruct((nb, nc, KV_W), BF16), jax.ShapeDtypeStruct((nb, KV_W, nc), BF16)],
        compiler_params=_params("parallel"),
        name="compress_prompt",
    )(cmp_rows, w_big, b_big, g_k)


def _tile_heads(x):
    return jnp.concatenate([x] * GROUP_R, axis=1)


def _softmax_part(m, s):
    m_new = jnp.maximum(m, jnp.max(s, axis=0, keepdims=True))
    return m_new, jnp.exp2(m - m_new), jnp.exp2(s - m_new).astype(BF16)


def _accumulate(acc, soft, vt):
    m_new, alpha, p = soft
    return m_new, alpha * acc + _dot(vt, p)


def _key_rows(ref, t, n):
    return ref[0, pl.ds(t, n)].reshape(n * LANES, ref.shape[-1])


def _value_cols(ref, t, n, g):
    return jnp.concatenate([ref[0, t + k, g] for k in range(n)], axis=1)


def _rank_rows(score):
    n_blk = score.shape[0]
    blk = lax.broadcasted_iota(jnp.int32, score.shape, 0)
    rank = jnp.zeros(score.shape, F32)
    for b in range(n_blk):
        row = score[b:b + 1, :]
        rank = rank + jnp.where(blk > b, jnp.where(row >= score, 1.0, 0.0), jnp.where(row > score, 1.0, 0.0))
    return rank


def _nsa_prompt_kernel(q_ref, gate_ref, kc_ref, vct_ref, ks_ref, vs_ref, kw_ref, vw_ref, sel_add_ref, win_add_ref,
                       bcmp_ref, covt_ref, a_ref, *, n_cmp):
    ii = pl.program_id(1)
    n_tiles = ks_ref.shape[1]
    n_sel = covt_ref.shape[0]
    wide = GROUP_R * Q_TILE
    heads_of = lambda g: range(g * GROUP_R, (g + 1) * GROUP_R)
    chains =[(sb, g) for sb in range(Q_BLOCKS) for g in range(N_KV_A)]
    blk_i = [Q_BLOCKS * ii + sb for sb in range(Q_BLOCKS)]
    cols = lambda sb: slice(sb * Q_TILE, (sb + 1) * Q_TILE)
    qg_t = {(sb, g): jnp.concatenate([q_ref[0, h * LANES:(h + 1) * LANES, cols(sb)] for h in heads_of(g)], axis=1)
            for sb, g in chains}
    init = (jnp.full((1, wide), NEG, F32), jnp.zeros((V_ROWS, wide), F32))
    normalised = lambda acc: acc[0:HEAD_DIM_A, :] / acc[HEAD_DIM_A:HEAD_DIM_A + 1, :]

    n_win = WINDOW // LANES + 1
    t0 = [jnp.maximum(i - (n_win - 1), 0) for i in blk_i]
    s_win = {}
    for sb, i in enumerate(blk_i):
        win_off = pl.multiple_of(jnp.maximum(n_win - 1 - i, 0) * LANES, LANES)
        k_win = _key_rows(kw_ref, t0[sb], n_win)
        for g in range(N_KV_A):
            s_win[sb, g] = _dot(k_win, qg_t[sb, g]) + win_add_ref[g, pl.ds(win_off, n_win * LANES), :]

    ncp = kc_ref.shape[1]
    j_sub = lax.broadcasted_iota(jnp.int32, (ncp, wide), 0)
    q_in_tile = lax.broadcasted_iota(jnp.int32, (ncp, wide), 1) & (Q_TILE - 1)
    s_cmp, any_usable = {}, []
    for sb, i in enumerate(blk_i):
        q_lane = i * Q_TILE + q_in_tile
        usable_add = jnp.where((j_sub * CMP_STRIDE + (CMP_LEN - 1) <= q_lane) & (j_sub < n_cmp), 0.0, NEG)
        any_usable.append(jnp.where(q_lane[0:1, :] >= CMP_LEN - 1, 1.0, 0.0))
        for g in range(N_KV_A):
            s_cmp[sb, g] = _dot(kc_ref[0], qg_t[sb, g]) + bcmp_ref[sb, g] + usable_add
    p_c = {}
    for sb, g in chains:
        e = jnp.exp2(s_cmp[sb, g] - jnp.max(s_cmp[sb, g], axis=0, keepdims=True)) * any_usable[sb]
        p_c[sb, g] = e / jnp.maximum(jnp.sum(e, axis=0, keepdims=True), 1e-30)
    oc_t = {c: _dot(vct_ref[0], p_c[c].astype(BF16)) for c in chains}

    blk = lax.broadcasted_iota(jnp.int32, (n_sel, Q_TILE), 0)
    q_in_blk = lax.broadcasted_iota(jnp.int32, (n_sel, Q_TILE), 1)
    tile_row = lax.broadcasted_iota(jnp.int32, (n_tiles, wide), 0)
    pad_rows = jnp.zeros((LANES - n_sel - n_tiles, wide), BF16)
    q_far, q_near = {}, {}
    for sb, g in chains:
        qpos = blk_i[sb] * Q_TILE + q_in_blk
        cur = qpos // SEL_BLOCK
        valid = blk * SEL_BLOCK <= qpos
        forced = valid & ((blk == 0) | (blk == cur) | (blk == cur - 1))
        p_sum = p_c[sb, g][:, 0:Q_TILE]
        for r in range(1, GROUP_R):
            p_sum = p_sum + p_c[sb, g][:, r * Q_TILE:(r + 1) * Q_TILE]
        imp_t = _dot_exact_lhs(covt_ref[...], p_sum)
        score = jnp.where(forced, -NEG, jnp.where(valid, imp_t, NEG))
        dropped = _tile_heads(jnp.where(_rank_rows(score) < float(min(TOP_K_BLOCKS, n_sel)), 0.0, 1.0).astype(BF16))
        past_far = jnp.where(tile_row > blk_i[sb] - 2, 1.0, 0.0).astype(BF16)
        q_far[sb, g] = jnp.concatenate([qg_t[sb, g], dropped, past_far, pad_rows], axis=0)
        q_near[sb, g] = jnp.concatenate([qg_t[sb, g], dropped, jnp.zeros_like(past_far), pad_rows], axis=0)

    soft_win = {c: _softmax_part(init[0], s_win[c]) for c in chains}
    ow_acc = {(sb, g): _dot(_value_cols(vw_ref, t0[sb], n_win, g), soft_win[sb, g][2]) for sb, g in chains}

    def far_body(it, carry):
        t = FAR_TILES * it
        k_far = _key_rows(ks_ref, t, FAR_TILES)
        s = {c: _dot(k_far, q_far[c]) for c in chains}
        soft = [_softmax_part(carry[n][0], s[c]) for n, c in enumerate(chains)]
        return tuple(_accumulate(carry[n][1], soft[n], _value_cols(vs_ref, t, FAR_TILES, c[1]))
                     for n, c in enumerate(chains))

    n_far = blk_i[-1] - 1
    carry = lax.fori_loop(0, (n_far + FAR_TILES - 1) // FAR_TILES, far_body, (init,) * len(chains))

    t1 = [jnp.maximum(i - 1, 0) for i in blk_i]
    s_near = {}
    for sb, i in enumerate(blk_i):
        k_near = _key_rows(ks_ref, t1[sb], 2)
        near_off = pl.multiple_of(jnp.where(i == 0, LANES, 0), LANES)
        for g in range(N_KV_A):
            s_near[sb, g] = _dot(k_near, q_near[sb, g]) + sel_add_ref[g, pl.ds(near_off, 2 * LANES), :]
    soft_near = [_softmax_part(carry[n][0], s_near[c]) for n, c in enumerate(chains)]
    for n, (sb, g) in enumerate(chains):
        _, acc = _accumulate(carry[n][1], soft_near[n], _value_cols(vs_ref, t1[sb], 2, g))
        heads = heads_of(g)
        gate_row = lambda br: jnp.concatenate(
            [gate_ref[0, br * N_HEADS_A + h:br * N_HEADS_A + h + 1, cols(sb)] for h in heads], axis=1)
        out_t = (gate_row(0) * oc_t[sb, g][g * HEAD_DIM_A:(g + 1) * HEAD_DIM_A, :] + gate_row(1) * normalised(acc)
                 + gate_row(2) * normalised(ow_acc[sb, g]))
        for pair in range(GROUP_R // 2):
            two = jnp.concatenate([out_t[:, (2 * pair + k) * Q_TILE:(2 * pair + k + 1) * Q_TILE] for k in range(2)],
                                  axis=0)
            col = (g * GROUP_R // 2 + pair) * LANES
            a_ref[0, cols(sb), col:col + LANES] = two.T.astype(BF16)


def _nsa_prompt(q_t, gate_t, kc, vc_t, ks, vs, kw, vw, sel_add, win_add, bcmp, cov_t, n_cmp):
    nb, _, t = q_t.shape
    nqb = t // Q_TILE
    assert nqb > WINDOW // LANES and nqb % Q_BLOCKS == 0
    step = Q_BLOCKS * Q_TILE
    seq = lambda a: pl.BlockSpec((1,) + a.shape[1:], lambda b, i: (b,) + (0,) * (a.ndim - 1))
    return pl.pallas_call(
        functools.partial(_nsa_prompt_kernel, n_cmp=n_cmp),
        grid=(nb, nqb // Q_BLOCKS),
        in_specs=[pl.BlockSpec((1, N_HEADS_A * LANES, step), lambda b, i: (b, 0, i)),
                  pl.BlockSpec((1, LANES, step), lambda b, i: (b, 0, i)),
                  seq(kc), seq(vc_t), seq(ks), seq(vs), seq(kw), seq(vw),
                  _const_spec(sel_add.shape), _const_spec(win_add.shape),
                  pl.BlockSpec((Q_BLOCKS,) + bcmp.shape[1:], lambda b, i: (i, 0, 0, 0)),
                  _const_spec(cov_t.shape)],
        out_specs=pl.BlockSpec((1, step, D_A), lambda b, i: (b, i, 0)),
        out_shape=jax.ShapeDtypeStruct((nb, t, D_A), BF16),
        compiler_params=_params("parallel", "arbitrary"),
        name="nsa_prompt",
    )(q_t, gate_t, kc, vc_t, ks, vs, kw, vw, sel_add, win_add, bcmp, cov_t)


def _head_norm_gate(h_all, mn_ref, o):
    outs = []
    for h in range(N_HEADS_M):
        hs = slice(h * HEAD_DIM_M, (h + 1) * HEAD_DIM_M)
        outs.append(_rms_rows(h_all[h], mn_ref[:, hs]))
    return _sigmoid(o) * jnp.concatenate(outs, axis=-1)


def _mlstm_prompt_kernel(qk_ref, v_ref, o_ref, small_ref, cw_ref, cb_ref, bif_ref, mn_ref, tri_ref, triu_ref,
                         y_ref, c_ref, n_ref, m_ref, xbuf):
    chunk = qk_ref.shape[1]
    pad = SUBLANES

    @pl.when(pl.program_id(1) == 0)
    def _():
        xbuf[0:pad, :] = jnp.zeros((pad, 2 * D_M), F32)
        c_ref[...] = jnp.zeros_like(c_ref)
        n_ref[...] = jnp.zeros_like(n_ref)
        m_ref[...] = jnp.zeros_like(m_ref)

    x = qk_ref[0]
    xbuf[pad:pad + chunk, :] = x
    y = xbuf[pad - 3:pad - 3 + chunk, :] * cw_ref[0:1, :]
    y = y + xbuf[pad - 2:pad - 2 + chunk, :] * cw_ref[1:2, :]
    y = y + xbuf[pad - 1:pad - 1 + chunk, :] * cw_ref[2:3, :]
    y = y + x * cw_ref[3:4, :] + cb_ref[...]
    qkc = y * _sigmoid(y)
    xbuf[0:pad, :] = xbuf[chunk:chunk + pad, :]

    gi = small_ref[0] + bif_ref[...]
    n_gate = 3 * N_HEADS_A
    gate_rows = gi.T[n_gate:n_gate + 2 * N_HEADS_M, :]
    b_col = _dot_exact_lhs(tri_ref[...], _log_sigmoid(gi))
    b_row = _dot_exact_rhs(_log_sigmoid(gate_rows), triu_ref[...])
    t_col = lax.broadcasted_iota(jnp.int32, (chunk, chunk), 0)
    s_row = lax.broadcasted_iota(jnp.int32, (chunk, chunk), 1)
    causal = s_row <= t_col
    m_all = m_ref[0]
    lane_row = lax.broadcasted_iota(jnp.int32, m_all.shape, 1)

    h_all = []
    for h in range(N_HEADS_M):
        hs = slice(h * HEAD_DIM_M, (h + 1) * HEAD_DIM_M)
        ks = slice(D_M + h * HEAD_DIM_M, D_M + (h + 1) * HEAD_DIM_M)
        q = qkc[:, hs]
        k = qkc[:, ks] * F32(HEAD_DIM_M ** -0.5)
        v = v_ref[0, :, hs]
        qb, kb = q.astype(BF16), k.astype(BF16)
        ci, fi = 3 * N_HEADS_A + h, 3 * N_HEADS_A + N_HEADS_M + h
        bt = b_col[:, fi:fi + 1]
        bs = b_row[N_HEADS_M + h:N_HEADS_M + h + 1, :]
        ig_row = gate_rows[h:h + 1, :]
        ig_col = gi[:, ci:ci + 1]
        m_prev = m_all[:, h:h + 1]
        c_prev = c_ref[0, h]
        n_prev = n_ref[0, h:h + 1, :]

        dlog = jnp.where(causal, bt - bs + ig_row, NEG)
        inter = bt + m_prev
        m_t = jnp.maximum(inter, jnp.max(dlog, axis=-1, keepdims=True))
        s_qk = _dot_nt(qb, kb) * jnp.exp(dlog - m_t)
        dec = jnp.exp(inter - m_t)
        num = _dot(s_qk.astype(BF16), v.astype(BF16)) + dec * _dot_nt(qb, c_prev.astype(BF16))
        den = jnp.sum(s_qk, axis=-1, keepdims=True) + dec * jnp.sum(q * n_prev, axis=-1, keepdims=True)
        h_all.append(num / jnp.maximum(jnp.abs(den), jnp.exp(-m_t)))

        m_new = m_t[chunk - 1:chunk, :]
        b_end = bt[chunk - 1:chunk, :]
        w_end = jnp.exp(b_end - bt + ig_col - m_new)
        dec_end = jnp.exp(b_end + m_prev - m_new)
        wv_t = (w_end * v).T.astype(BF16)
        c_ref[0, h] = dec_end * c_prev + _dot(wv_t, kb)
        n_ref[0, h:h + 1, :] = dec_end * n_prev + jnp.sum(w_end * k, axis=0, keepdims=True)
        m_all = jnp.where(lane_row == h, m_new, m_all)

    m_ref[0] = m_all
    y_ref[0] = _head_norm_gate(h_all, mn_ref, o_ref[0]).astype(BF16)


def _mlstm_prompt(qk, v, o, small, conv_w, conv_b, bif, mn_g, chunk):
    nb, t, _ = qk.shape
    tri = jnp.asarray(np.tril(np.ones((chunk, chunk), np.float32)), BF16)
    triu = jnp.asarray(np.triu(np.ones((chunk, chunk), np.float32)), BF16)
    tok = lambda w: pl.BlockSpec((1, chunk, w), lambda b, c: (b, c, 0))
    return pl.pallas_call(
        _mlstm_prompt_kernel,
        grid=(nb, t // chunk),
        in_specs=[tok(2 * D_M), tok(D_M), tok(D_M), tok(LANES)]
        + [_const_spec(a.shape) for a in (conv_w, conv_b, bif, mn_g, tri, triu)],
        out_specs=[tok(D_M),
                   pl.BlockSpec((1, N_HEADS_M, HEAD_DIM_M, HEAD_DIM_M), lambda b, c: (b, 0, 0, 0)),
                   pl.BlockSpec((1, N_HEADS_M, HEAD_DIM_M), lambda b, c: (b, 0, 0)),
                   pl.BlockSpec((1, 1, LANES), lambda b, c: (b, 0, 0))],
        out_shape=[jax.ShapeDtypeStruct((nb, t, D_M), BF16),
                   jax.ShapeDtypeStruct((nb, N_HEADS_M, HEAD_DIM_M, HEAD_DIM_M), F32),
                   jax.ShapeDtypeStruct((nb, N_HEADS_M, HEAD_DIM_M), F32),
                   jax.ShapeDtypeStruct((nb, 1, LANES), F32)],
        scratch_shapes=[pltpu.VMEM((chunk + SUBLANES, 2 * D_M), F32)],
        compiler_params=_params("parallel", "arbitrary"),
        name="mlstm_prompt",
    )(qk, v, o, small, conv_w, conv_b, bif, mn_g, tri, triu)


def _softmax_rows(s):
    e = jnp.exp(s - jnp.max(s, axis=-1, keepdims=True))
    return e, jnp.sum(e, axis=-1, keepdims=True)


def _sample_cmp_kernel(pt_ref, *refs, n_pages, n_sel, qpos):
    pages = refs[:n_pages]
    w_ref, b_ref, g_ref, q_ref, bias_ref, cov_ref, oc_ref, idx_ref, xk_ref, xv_ref = refs[n_pages:]
    for half, rows_ref in ((0, xk_ref), (1, xv_ref)):
        for u in range(n_pages):
            rows_ref[u * PAGE_SIZE:(u + 1) * PAGE_SIZE, :] = pages[u][0, half * KV_W:(half + 1) * KV_W, :].T
    nc = n_pages * PAGE_SIZE // CMP_STRIDE
    kc, vc = _compress_rows(xk_ref, xv_ref, nc, w_ref, b_ref, g_ref)
    q = q_ref[0]
    j_row = lax.broadcasted_iota(jnp.int32, (N_HEADS_A, nc), 1)
    s = _dot_nt(q, kc.astype(BF16)) + bias_ref[...] + jnp.where(j_row < nc - 1, 0.0, NEG)
    e, l = _softmax_rows(s)
    p = e / jnp.maximum(l, 1e-30)
    oc_ref[0] = _dot(p.astype(BF16), vc.astype(BF16))

    nsp = cov_ref.shape[1]
    blk_row = lax.broadcasted_iota(jnp.int32, (1, nsp), 1)
    sub = lax.broadcasted_iota(jnp.int32, (nsp, nsp), 0)
    lan = lax.broadcasted_iota(jnp.int32, (nsp, nsp), 1)
    cur = qpos // SEL_BLOCK
    valid = (blk_row * SEL_BLOCK <= qpos) & (blk_row < n_sel)
    forced = valid & ((blk_row == 0) | (blk_row == cur) | (blk_row == cur - 1))
    k_col = lax.broadcasted_iota(jnp.int32, (TOP_K_BLOCKS, nsp), 0).astype(F32)
    blk_f = lax.broadcasted_iota(jnp.int32, (TOP_K_BLOCKS, nsp), 1).astype(F32)
    for g in range(N_KV_A):
        p_sum = jnp.sum(p[g * GROUP_R:(g + 1) * GROUP_R, :], axis=0, keepdims=True)
        imp = _dot_exact_rhs(jnp.broadcast_to(p_sum, (SUBLANES, nc)), cov_ref[...])[0:1, :]
        score = jnp.where(forced, -NEG, jnp.where(valid, imp, NEG))
        score_col = jnp.sum(jnp.where(sub == lan, score, 0.0), axis=-1, keepdims=True)
        ahead = jnp.where(sub < lan, jnp.where(score_col >= score, 1.0, 0.0), jnp.where(score_col > score, 1.0, 0.0))
        rank = jnp.sum(ahead, axis=0, keepdims=True)
        chosen = jnp.sum(jnp.where(rank == k_col, blk_f, 0.0), axis=-1, keepdims=True)
        idx_ref[0, g * TOP_K_BLOCKS:(g + 1) * TOP_K_BLOCKS, :] = jnp.broadcast_to(
            chosen, (TOP_K_BLOCKS, LANES)).astype(jnp.int32)


def _sample_cmp(page_table, pool, w_big, b_big, g_k, q3, bias_c, cover, n_sel, qpos):
    ns, n_pages = page_table.shape
    page_spec = lambda u: pl.BlockSpec((1, KV_ROW, PAGE_SIZE), lambda s, pt: (pt[s, u], 0, 0))
    const = lambda a: pl.BlockSpec(a.shape, lambda s, pt: (0,) * a.ndim, pipeline_mode=pl.Buffered(1))
    grid_spec = pltpu.PrefetchScalarGridSpec(
        num_scalar_prefetch=1,
        grid=(ns,),
        in_specs=[page_spec(u) for u in range(n_pages)]
        + [const(w_big), const(b_big), const(g_k),
           pl.BlockSpec((1, N_HEADS_A, LANES), lambda s, pt: (s, 0, 0)), const(bias_c), const(cover)],
        out_specs=[pl.BlockSpec((1, N_HEADS_A, LANES), lambda s, pt: (s, 0, 0)),
                   pl.BlockSpec((1, N_KV_A * TOP_K_BLOCKS, LANES), lambda s, pt: (s, 0, 0))],
        scratch_shapes=[pltpu.VMEM((n_pages * PAGE_SIZE, KV_W), F32)] * 2,
    )
    return pl.pallas_call(
        functools.partial(_sample_cmp_kernel, n_pages=n_pages, n_sel=n_sel, qpos=qpos),
        grid_spec=grid_spec,
        out_shape=[jax.ShapeDtypeStruct((ns, N_HEADS_A, LANES), F32),
                   jax.ShapeDtypeStruct((ns, N_KV_A * TOP_K_BLOCKS, LANES), jnp.int32)],
        compiler_params=_params("arbitrary"),
        name="sample_cmp_topk",
    )(page_table, *([pool] * n_pages), w_big, b_big, g_k, q3, bias_c, cover)


def _sample_attn_kernel(idx_ref, pt_ref, *refs, n_past_blocks):
    n_slots = N_KV_A * TOP_K_BLOCKS
    blocks = refs[:n_slots]
    (win_ref, q_ref, selnew_ref, winnew_ref, wint_ref, oc_ref, gate_ref, bsel_ref, bwin_ref, b0_ref,
     a_ref, winout_ref) = refs[n_slots:]
    s_id = pl.program_id(0)
    q = q_ref[0]
    qf = q.astype(F32)
    lane = lax.broadcasted_iota(jnp.int32, (N_HEADS_A, LANES), 1)
    row = lax.broadcasted_iota(jnp.int32, (N_HEADS_A, LANES), 0)
    lo = lane < SEL_BLOCK

    def with_new_key(s, vt_mat, new_row):
        s_new = jnp.sum(qf * new_row[:, 0:KV_W], axis=-1, keepdims=True) + b0_ref[:, 0:1]
        m = jnp.maximum(jnp.max(s, axis=-1, keepdims=True), s_new)
        p = jnp.exp(s - m)
        p_new = jnp.exp(s_new - m)
        l = jnp.sum(p, axis=-1, keepdims=True) + p_new
        return (_dot_nt(p.astype(BF16), vt_mat) + p_new * new_row[:, KV_W:KV_ROW]) / l

    sel_new = selnew_ref[0]
    o_s = []
    for g in range(N_KV_A):
        slots = range(g * TOP_K_BLOCKS, (g + 1) * TOP_K_BLOCKS)
        kt_cat = jnp.concatenate([blocks[k][0, 0:KV_W, :] for k in slots], axis=1).astype(BF16)
        vt_cat = jnp.concatenate([blocks[k][0, KV_W:KV_ROW, :] for k in slots], axis=1).astype(BF16)
        adds = []
        for k in slots:
            b = idx_ref[s_id * n_slots + k]
            add = jnp.where(b == n_past_blocks - 2, bsel_ref[0], jnp.where(b == n_past_blocks - 1, bsel_ref[1], 0.0))
            other_half = jnp.where(b % 2 == 0, jnp.where(lo, 0.0, NEG), jnp.where(lo, NEG, 0.0))
            adds.append(add + other_half + jnp.where(b >= n_past_blocks, NEG, 0.0))
        o_s.append(with_new_key(_dot(q, kt_cat) + jnp.concatenate(adds, axis=-1), vt_cat, sel_new))
    o_sel = jnp.where(row < GROUP_R, o_s[0], o_s[1])

    win = win_ref[0]
    wb = win.shape[1]
    w_lane = lax.broadcasted_iota(jnp.int32, (N_HEADS_A, wb), 1)
    s_w = _dot(q, win[0:KV_W, :].astype(BF16)) + bwin_ref[...] + jnp.where(w_lane == 0, NEG, 0.0)
    o_win = with_new_key(s_w, win[KV_W:KV_ROW, :].astype(BF16), winnew_ref[0])

    na = N_HEADS_A
    out = gate_ref[0, 0:na] * oc_ref[0] + gate_ref[0, na:2 * na] * o_sel + gate_ref[0, 2 * na:3 * na] * o_win
    a_ref[0] = jnp.where(lo == (row < GROUP_R), out, 0.0).astype(BF16)

    wint = wint_ref[...]
    seq_lane = lax.broadcasted_iota(jnp.int32, wint.shape, 1)
    new_col = jnp.sum(jnp.where(seq_lane == s_id, wint, 0.0), axis=-1, keepdims=True)
    buf_lane = lax.broadcasted_iota(jnp.int32, win.shape, 1)
    winout_ref[0] = jnp.where(buf_lane == wb - 1, new_col, pltpu.roll(win, wb - 1, 1))


def _sample_attn(idx, page_table, pool, win_t, q3, sel_new, win_new, win_new_t, o_c, gate_b, bsel, bwin, b0):
    ns, n_pages = page_table.shape
    n_slots = N_KV_A * TOP_K_BLOCKS
    per_page = PAGE_SIZE // SEL_BLOCK
    n_past_blocks = n_pages * per_page
    wb = win_t.shape[2]

    def slot_spec(k):
        def index(s, idx_ref, pt_ref):
            b = jnp.minimum(idx_ref[s * n_slots + k], n_past_blocks - 1)
            return (pt_ref[s, b // per_page], 0, 0)
        return pl.BlockSpec((1, KV_ROW, PAGE_SIZE), index)

    per_seq = lambda a: pl.BlockSpec((1,) + a.shape[1:], lambda s, *_: (s,) + (0,) * (a.ndim - 1))
    const = lambda a: pl.BlockSpec(a.shape, lambda s, *_: (0,) * a.ndim, pipeline_mode=pl.Buffered(1))
    grid_spec = pltpu.PrefetchScalarGridSpec(
        num_scalar_prefetch=2,
        grid=(ns,),
        in_specs=[slot_spec(k) for k in range(n_slots)]
        + [per_seq(win_t), per_seq(q3), per_seq(sel_new), per_seq(win_new), const(win_new_t), per_seq(o_c),
           per_seq(gate_b), const(bsel), const(bwin), const(b0)],
        out_specs=[pl.BlockSpec((1, N_HEADS_A, LANES), lambda s, *_: (s, 0, 0)),
                   pl.BlockSpec((1, KV_ROW, wb), lambda s, *_: (s, 0, 0))],
    )
    return pl.pallas_call(
        functools.partial(_sample_attn_kernel, n_past_blocks=n_past_blocks),
        grid_spec=grid_spec,
        out_shape=[jax.ShapeDtypeStruct((ns, N_HEADS_A, LANES), BF16),
                   jax.ShapeDtypeStruct((ns, KV_ROW, wb), F32)],
        compiler_params=_params("arbitrary"),
        name="sample_sel_win",
    )(idx, page_table, *([pool] * n_slots), win_t, q3, sel_new, win_new, win_new_t, o_c, gate_b, bsel, bwin, b0)


def _mlstm_step_kernel(qk_ref, conv_ref, v_ref, o_ref, small_ref, c_ref, n_ref, m_ref, cw_ref, cb_ref, bif_ref,
                       mn_ref, y_ref, c_out, n_out, m_out):
    seqs = qk_ref.shape[0]
    sub = lax.broadcasted_iota(jnp.int32, (HEAD_DIM_M, HEAD_DIM_M), 0)
    lan = lax.broadcasted_iota(jnp.int32, (HEAD_DIM_M, HEAD_DIM_M), 1)
    eye = sub == lan
    lane_row = lax.broadcasted_iota(jnp.int32, (1, LANES), 1)
    for s in range(seqs):
        hist = conv_ref[s]
        y = hist[0:1, :] * cw_ref[0:1, :]
        y = y + hist[1:2, :] * cw_ref[1:2, :]
        y = y + hist[2:3, :] * cw_ref[2:3, :]
        y = y + qk_ref[s] * cw_ref[3:4, :] + cb_ref[...]
        qkc = y * _sigmoid(y)
        gi = small_ref[s] + bif_ref[...]
        lf_all = _log_sigmoid(gi)
        m_all = m_ref[s]
        m_new_all = m_all
        h_all = []
        for h in range(N_HEADS_M):
            hs = slice(h * HEAD_DIM_M, (h + 1) * HEAD_DIM_M)
            ks = slice(D_M + h * HEAD_DIM_M, D_M + (h + 1) * HEAD_DIM_M)
            q = qkc[:, hs]
            k = qkc[:, ks] * F32(HEAD_DIM_M ** -0.5)
            v = v_ref[s][:, hs]
            ci, fi = 3 * N_HEADS_A + h, 3 * N_HEADS_A + N_HEADS_M + h
            ig = gi[:, ci:ci + 1]
            lf = lf_all[:, fi:fi + 1]
            m_prev = m_all[:, h:h + 1]
            c_prev = c_ref[s, h]
            n_prev = n_ref[s, h:h + 1, :]
            inter = lf + m_prev
            m_t = jnp.maximum(inter, ig)
            w_in = jnp.exp(ig - m_t)
            dec = jnp.exp(inter - m_t)
            s_qk = jnp.sum(q * k, axis=-1, keepdims=True) * w_in
            cq = _dot_nt(jnp.broadcast_to(q, (SUBLANES, HEAD_DIM_M)).astype(BF16), c_prev.astype(BF16))[0:1, :]
            num = s_qk * v + dec * cq
            den = s_qk + dec * jnp.sum(n_prev * q, axis=-1, keepdims=True)
            h_all.append(num / jnp.maximum(jnp.abs(den), jnp.exp(-m_t)))
            v_col = jnp.sum(jnp.where(eye, v, 0.0), axis=-1, keepdims=True)
            c_out[s, h] = dec * c_prev + (w_in * v_col) * k
            n_out[s, h:h + 1, :] = dec * n_prev + w_in * k
            m_new_all = jnp.where(lane_row == h, m_t, m_new_all)
        m_out[s] = m_new_all
        y_ref[s] = _head_norm_gate(h_all, mn_ref, o_ref[s]).astype(BF16)


def _mlstm_step(qk, conv0, v, o, small, c0, n0, m0, conv_w, conv_b, bif, mn_g):
    ns = qk.shape[0]
    sb = STEP_SEQS if ns % STEP_SEQS == 0 else 1
    per = lambda a: pl.BlockSpec((sb,) + a.shape[1:], lambda i: (i,) + (0,) * (a.ndim - 1))
    ins = (qk, conv0, v, o, small, c0, n0, m0)
    outs = [jax.ShapeDtypeStruct((ns, 1, D_M), BF16), jax.ShapeDtypeStruct(c0.shape, F32),
            jax.ShapeDtypeStruct(n0.shape, F32), jax.ShapeDtypeStruct(m0.shape, F32)]
    return pl.pallas_call(
        _mlstm_step_kernel,
        grid=(ns // sb,),
        in_specs=[per(a) for a in ins] + [_const_spec(a.shape) for a in (conv_w, conv_b, bif, mn_g)],
        out_specs=[per(a) for a in outs],
        out_shape=outs,
        compiler_params=_params("parallel"),
        name="mlstm_step",
    )(*ins, conv_w, conv_b, bif, mn_g)


def _cover_np(n_cmp, n_sel):
    cs = np.arange(n_cmp)[:, None] * CMP_STRIDE
    bs = np.arange(n_sel)[None, :] * SEL_BLOCK
    shared = np.clip(np.minimum(cs + CMP_LEN, bs + SEL_BLOCK) - np.maximum(cs, bs), 0, None)
    return (shared / CMP_LEN).astype(np.float32)


def _prep_in_proj(w_in):
    d = w_in.shape[0]
    o_kv = D_A
    o_gate = o_kv + 6 * KV_W
    o_m = o_gate + 3 * N_HEADS_A
    o_if = o_m + 4 * D_M
    w_kv_t = w_in[:, o_kv:o_gate].T
    w_m = w_in[:, o_m:o_if]
    w_small = jnp.concatenate([w_in[:, o_gate:o_m], w_in[:, o_if:],
                               jnp.zeros((d, LANES - 3 * N_HEADS_A - 2 * N_HEADS_M), w_in.dtype)], axis=1)
    return [w.astype(BF16) for w in (w_in[:, :D_A].T, w_kv_t, w_m, w_small, w_small.T)]


def _prep_compress(w_k, b_k, w_v, b_v):
    per_c = jnp.stack([w_k, w_k, w_v, w_v]).reshape(4, 2, CMP_STRIDE, HEAD_DIM_A, HEAD_DIM_A)
    w_big = jnp.einsum('chtde,cf->tcdhfe', per_c, jnp.eye(4, dtype=w_k.dtype))
    w_big = w_big.reshape(CMP_STRIDE, KV_ROW, 2 * KV_ROW).astype(BF16)
    pick = lambda h: jnp.concatenate([w_big[:, h * KV_W:(h + 1) * KV_W, h * KV_W:(h + 1) * KV_W],
                                      w_big[:, h * KV_W:(h + 1) * KV_W, KV_ROW + h * KV_W:KV_ROW + (h + 1) * KV_W]],
                                     axis=-1)
    w_big = jnp.stack([pick(0), pick(1)])
    b_big = jnp.concatenate([b_k, b_k, b_v, b_v]).reshape(1, KV_ROW)
    return w_big, b_big


def _bucket_tiles(t, past, wb):
    r = np.arange(LANES)
    tiles = [_t5_bucket_np(d * LANES + r[None, :] - r[:, None]) for d in range(2)]
    for i in range(t // Q_TILE):
        tiles.append(_t5_bucket_np(i * Q_TILE + r[None, :] - (CMP_STRIDE * r[:, None] + CMP_LEN - 1)))
    flat = np.arange(LANES * LANES).reshape(LANES, LANES)
    tiles.append(_t5_bucket_np(past - (CMP_STRIDE * flat + CMP_LEN - 1)))
    tiles.append(_t5_bucket_np(wb - flat))
    tiles.append(_t5_bucket_np(past - (r[:, None] * SEL_BLOCK + r[None, :] % SEL_BLOCK)))
    return np.stack(tiles).astype(np.int32)


def _near_tables(bnear):
    r = np.arange(LANES)
    tile4 = lambda a: jnp.asarray(np.tile(a, (1, GROUP_R)), F32)
    causal = tile4(np.where(r[:, None] <= r[None, :], 0.0, NEG))
    oldest = tile4(np.where(r[:, None] > r[None, :], 0.0, NEG))
    zero = jnp.zeros_like(causal)
    masked = jnp.full_like(causal, NEG)
    n_mid = WINDOW // LANES - 2
    sel_add, win_add = [], []
    for g in range(N_KV_A):
        near = [bnear[1, g], bnear[0, g] + causal]
        sel_add.append(jnp.concatenate(near + [masked], axis=0))
        win_add.append(jnp.concatenate([oldest] + [zero] * n_mid + near + [masked] * (WINDOW // LANES), axis=0))
    return jnp.stack(sel_add), jnp.stack(win_add)


def kernel(x_prompt, x_sample, cache_cmp_kv, cache_sel_kv, cache_win_kv, state_C, state_n, state_m, state_conv, page_table, rel_bias, g_ffn1, w1_gate, w1_up, w1_down, g_mix, w_in, qn_g, kn_cmp_g, kn_sel_g, kn_win_g, w_cmp_k, b_cmp_k, w_cmp_v, b_cmp_v, conv_w, conv_b, b_if, mn_g, w_out, g_ffn2, w2_gate, w2_up, w2_down):
    assert x_prompt.shape[2] == D_MODEL and g_ffn1.shape[0] == 1
    nb, t, _ = x_prompt.shape
    ns, ds, _ = x_sample.shape
    assert ds == 1 and t % Q_TILE == 0
    n_pages = page_table.shape[1]
    past = n_pages * PAGE_SIZE
    wb = cache_win_kv.shape[2]
    assert wb == WINDOW
    kv_shape = (2, N_KV_A, HEAD_DIM_A)

    row = lambda a: a.reshape(1, -1)
    two = lambda a: jnp.tile(a, 2).reshape(1, KV_W)
    w1 = (row(g_ffn1[0]), w1_gate[0].astype(BF16), w1_up[0].astype(BF16), w1_down[0].astype(BF16))
    w2 = (row(g_ffn2[0]), w2_gate[0].astype(BF16), w2_up[0].astype(BF16), w2_down[0].astype(BF16))
    wqt, wkvt, wm, ws, wst = _prep_in_proj(w_in[0])
    qg = (qn_g[0] * F32(HEAD_DIM_A ** -0.5))[:, None]
    kgt = jnp.stack([jnp.tile(kn_sel_g[0], 2), jnp.tile(kn_win_g[0], 2)])[:, :, None]
    w_big, b_big = _prep_compress(w_cmp_k[0], b_cmp_k[0], w_cmp_v[0], b_cmp_v[0])
    g_cmp = two(kn_cmp_g[0])
    woa, wom = w_out[0][:D_A].astype(BF16), w_out[0][D_A:].astype(BF16)
    n_gate = 3 * N_HEADS_A
    bif = jnp.zeros((1, LANES), F32).at[0, n_gate:n_gate + 2 * N_HEADS_M].set(b_if[0])
    conv_b2 = row(conv_b[0])
    mn = row(mn_g[0])

    nqb = t // Q_TILE
    tiles = _bias_tiles(rel_bias, jnp.asarray(_bucket_tiles(t, past, wb)))
    group_lanes = lambda a: a.reshape(a.shape[0], N_KV_A, GROUP_R, LANES, LANES).transpose(0, 1, 3, 2, 4).reshape(
        a.shape[0], N_KV_A, LANES, GROUP_R * LANES)
    bnear = group_lanes(tiles[0:2]) * F32(LOG2_E)
    sel_add, win_add = _near_tables(bnear)
    bcmp = group_lanes(tiles[2:2 + nqb]) * F32(LOG2_E)
    t_cmp, t_win, t_sel = tiles[2 + nqb], tiles[3 + nqb], tiles[4 + nqb]

    def dense_in(x, seq_len, q_gain, kmask):
        x1 = _ffn(x, *w1)
        return (x1,) + tuple(_inproj(x1, seq_len, kmask, row(g_mix[0]), wqt, wkvt, wm, ws, wst, q_gain, kgt))

    n_sel_p = -(-t // SEL_BLOCK)
    assert n_sel_p + t // LANES <= LANES
    kmask_np = np.zeros((t, LANES), np.float32)
    kmask_np[np.arange(t), np.arange(t) // SEL_BLOCK] = -MASK_BIG
    kmask_np[np.arange(t), n_sel_p + np.arange(t) // LANES] = -MASK_BIG

    from_chan = lambda a: a.reshape((1, a.shape[0]) + kv_shape + (a.shape[2],)).transpose(0, 1, 5, 2, 3, 4)
    to_chan = lambda a: a.transpose(0, 2, 3, 4, 1).reshape(a.shape[0], KV_ROW, a.shape[1])

    (x1, q_t, cmp_rows, cmp_t, sel_t, win_t, ks, vs, kw, vw, qk_m, v_m, o_m, small, gate_t) = dense_in(
        x_prompt.reshape(nb * t, D_MODEL), t, qg * F32(LOG2_E),
        jnp.asarray(kmask_np, BF16))
    seq = lambda a: a.reshape(nb, t, a.shape[-1])
    n16 = t // CMP_STRIDE
    assert n16 == LANES
    kc, vc_t = _compress(cmp_rows, t, w_big, b_big, g_cmp)
    n_sel = -(-t // SEL_BLOCK)
    cov_t = jnp.asarray(np.pad(_cover_np(n16 - 1, n_sel), ((0, 1), (0, 0))).T, BF16)
    a_pad = _nsa_prompt(q_t, gate_t, kc, vc_t, ks, vs, kw, vw, sel_add, win_add, bcmp, cov_t, n16 - 1)
    chunk = MLSTM_CHUNK if t % MLSTM_CHUNK == 0 else Q_TILE
    m_out, c_p, n_p, m_p = _mlstm_prompt(seq(qk_m), seq(v_m), seq(o_m), seq(small), conv_w[0], conv_b2, bif, mn,
                                         chunk)
    y_prompt = _post(x1, a_pad.reshape(nb * t, -1), m_out.reshape(nb * t, -1), woa, wom, *w2)
    kv6 = lambda a, n: a.reshape((1, n, -1) + kv_shape)
    prompt_states = (from_chan(cmp_t), from_chan(sel_t), from_chan(win_t[:, :, t - wb:]), c_p[None], n_p[None],
                     m_p[None, :, 0, :N_HEADS_M], seq(qk_m)[None, :, t - (CONV_W - 1):])

    (x1s, q_st, cmp_halves, _, sel_st, win_st, _, _, _, _, qk_s, v_s, o_s, small_s, gate_st) = dense_in(
        x_sample.reshape(ns, D_MODEL), ns, qg, jnp.zeros((ns, LANES), BF16))
    cmp_s = jnp.concatenate([cmp_halves[0], cmp_halves[1]], axis=1)
    sel_s, win_s, gates_s = sel_st[0].T, win_st[0].T, gate_st[0].T
    q3 = q_st[0].T.reshape(ns, N_HEADS_A, LANES)
    n16s = past // CMP_STRIDE
    n_sel_s = -(-(past + 1) // SEL_BLOCK)
    n_sel_pad = -(-n_sel_s // LANES) * LANES
    cover_s = jnp.asarray(np.pad(_cover_np(n16s - 1, n_sel_s), ((0, 1), (0, n_sel_pad - n_sel_s))), BF16)
    bias_c = t_cmp[:, :n16s // LANES, :].reshape(N_HEADS_A, n16s)
    o_c, idx = _sample_cmp(page_table, to_chan(cache_cmp_kv[0]), w_big, b_big, g_cmp, q3, bias_c, cover_s, n_sel_s,
                           past)
    n_past_blocks = past // SEL_BLOCK
    two_lanes = lambda a: jnp.concatenate([a[:, :SEL_BLOCK], a[:, :SEL_BLOCK]], axis=-1)
    bsel = jnp.stack([two_lanes(t_sel[:, n_past_blocks - 2, :]), two_lanes(t_sel[:, n_past_blocks - 1, :])])
    bwin = t_win[:, :wb // LANES, :].reshape(N_HEADS_A, wb)
    b0 = jnp.broadcast_to(t_win[:, wb // LANES, 0:1], (N_HEADS_A, LANES))
    gate_b = jnp.broadcast_to(gates_s[:, :n_gate, None], (ns, n_gate, LANES))
    a_s, win_out = _sample_attn(idx[:, :, 0].reshape(-1), page_table, to_chan(cache_sel_kv[0]),
                                to_chan(cache_win_kv[0]), q3, sel_s[:, None, :], win_s[:, None, :], win_st[0],
                                o_c, gate_b, bsel, bwin, b0)
    m0 = jnp.pad(state_m[0], ((0, 0), (0, LANES - N_HEADS_M)))[:, None, :]
    m_s, c_s, n_s, m_new = _mlstm_step(qk_s[:, None, :], state_conv[0], v_s[:, None, :], o_s[:, None, :],
                                       small_s[:, None, :], state_C[0], state_n[0], m0, conv_w[0], conv_b2, bif, mn)
    a_dense = jnp.concatenate([a_s[:, h, (h // GROUP_R) * HEAD_DIM_A:(h // GROUP_R + 1) * HEAD_DIM_A]
                               for h in range(N_HEADS_A)], axis=1)
    y_sample = _post(x1s, a_dense, m_s.reshape(ns, -1), woa, wom, *w2)
    conv_new = jnp.concatenate([state_conv[0][:, 1:], qk_s[:, None, :]], axis=1)
    sample_states = (kv6(cmp_s, ns), kv6(sel_s, ns), from_chan(win_out), c_s[None], n_s[None],
                     m_new[None, :, 0, :N_HEADS_M], conv_new[None])

    return (y_prompt.reshape(nb, t, D_MODEL), y_sample.reshape(ns, 1, D_MODEL)) + prompt_states + sample_states
```

```python
import functools
import math

import jax
import jax.numpy as jnp
import numpy as np
from jax import lax
from jax.experimental import pallas as pl
from jax.experimental.pallas import tpu as pltpu

F32 = jnp.float32
BF16 = jnp.bfloat16

D_MODEL = 1024
PAGE_SIZE = 128
N_HEADS_A = 8
HEAD_DIM_A = 64
N_KV_A = 2
GROUP_R = N_HEADS_A // N_KV_A
D_A = N_HEADS_A * HEAD_DIM_A
KV_W = N_KV_A * HEAD_DIM_A
CMP_STRIDE = 16
CMP_LEN = 2 * CMP_STRIDE
SEL_BLOCK = 64
TOP_K_BLOCKS = 16
WINDOW = 512
N_HEADS_M = 4
HEAD_DIM_M = 128
D_M = N_HEADS_M * HEAD_DIM_M
CONV_W = 4
N_BUCKETS = 32
MAX_DISTANCE = 128
EPS = 1e-6

LANES = 128
SUBLANES = 8
TOKEN_TILE = 512
MLSTM_CHUNK = 256
STEP_SEQS = 8
Q_TILE = 128
Q_BLOCKS = 2
V_ROWS = HEAD_DIM_A + 16
FAR_TILES = 4
KV_ROW = 2 * KV_W
CHUNK_ROW = CMP_STRIDE * KV_ROW
LOG2_E = 1.4426950408889634
NEG = -1e30
MASK_BIG = 2.0 ** 100
VMEM_LIMIT = 56 * 1024 * 1024

NT_DIMS = (((1,), (1,)), ((), ()))


def _dot(a, b):
    return jnp.dot(a, b, preferred_element_type=F32)


def _dot_nt(a, b):
    return lax.dot_general(a, b, NT_DIMS, preferred_element_type=F32)


def _split3(x):
    x1 = x.astype(BF16)
    r1 = x - x1.astype(F32)
    x2 = r1.astype(BF16)
    x3 = (r1 - x2.astype(F32)).astype(BF16)
    return x1, x2, x3


def _dot_exact_rhs(x, m):
    x1, x2, x3 = _split3(x)
    return _dot(x1, m) + _dot(x2, m) + _dot(x3, m)


def _dot_exact_lhs(m, x):
    x1, x2, x3 = _split3(x)
    return _dot(m, x1) + _dot(m, x2) + _dot(m, x3)


def _rms_rows(x, g):
    ms = jnp.mean(x * x, axis=-1, keepdims=True)
    return x * lax.rsqrt(ms + EPS) * g


def _rms_two_groups(k, g):
    sq = k * k
    lane = lax.broadcasted_iota(jnp.int32, sq.shape, 1)
    lo = lane < HEAD_DIM_A
    s0 = jnp.sum(jnp.where(lo, sq, 0.0), axis=-1, keepdims=True)
    s1 = jnp.sum(jnp.where(lo, 0.0, sq), axis=-1, keepdims=True)
    ms = jnp.where(lo, s0, s1) * (1.0 / HEAD_DIM_A)
    return k * lax.rsqrt(ms + EPS) * g


def _sigmoid(x):
    return jax.nn.sigmoid(x)


def _log_sigmoid(x):
    return jnp.minimum(x, 0.0) - jnp.log1p(jnp.exp(-jnp.abs(x)))


def _const_spec(shape):
    nd = len(shape)
    return pl.BlockSpec(shape, lambda *_: (0,) * nd, pipeline_mode=pl.Buffered(1))


def _params(*sem):
    return pltpu.CompilerParams(dimension_semantics=sem, vmem_limit_bytes=VMEM_LIMIT)


def _swiglu_residual(x, g_ref, wg_ref, wu_ref, wd_ref):
    xn = _rms_rows(x, g_ref[...]).astype(BF16)
    d_ff = wg_ref.shape[1]
    n_split = 2 if d_ff % (2 * LANES) == 0 else 1
    step = d_ff // n_split
    acc = jnp.zeros_like(x)
    for c in range(n_split):
        hg = _dot(xn, wg_ref[:, c * step:(c + 1) * step])
        hu = _dot(xn, wu_ref[:, c * step:(c + 1) * step])
        h = (hg * _sigmoid(hg)) * hu
        acc = acc + _dot(h.astype(BF16), wd_ref[c * step:(c + 1) * step, :])
    return x + 0.5 * acc


def _ffn_kernel(x_ref, g_ref, wg_ref, wu_ref, wd_ref, y_ref):
    y_ref[...] = _swiglu_residual(x_ref[...], g_ref, wg_ref, wu_ref, wd_ref)


def _post_kernel(x_ref, a_ref, m_ref, woa_ref, wom_ref, g_ref, wg_ref, wu_ref, wd_ref, y_ref):
    x = x_ref[...] + (_dot(a_ref[...], woa_ref[...]) + _dot(m_ref[...], wom_ref[...]))
    y_ref[...] = _swiglu_residual(x, g_ref, wg_ref, wu_ref, wd_ref)


def _token_tile(n):
    return TOKEN_TILE if n % TOKEN_TILE == 0 else n


def _ffn(x, g, wg, wu, wd):
    n = x.shape[0]
    tm = _token_tile(n)
    row = lambda w: pl.BlockSpec((tm, w), lambda i: (i, 0))
    return pl.pallas_call(
        _ffn_kernel,
        grid=(n // tm,),
        in_specs=[row(D_MODEL), _const_spec(g.shape), _const_spec(wg.shape), _const_spec(wu.shape),
                  _const_spec(wd.shape)],
        out_specs=row(D_MODEL),
        out_shape=jax.ShapeDtypeStruct((n, D_MODEL), F32),
        compiler_params=_params("parallel"),
        name="ffn1",
    )(x, g, wg, wu, wd)


def _post(x, a_pad, m_out, woa, wom, g, wg, wu, wd):
    n = x.shape[0]
    tm = _token_tile(n)
    row = lambda w: pl.BlockSpec((tm, w), lambda i: (i, 0))
    return pl.pallas_call(
        _post_kernel,
        grid=(n // tm,),
        in_specs=[row(D_MODEL), row(a_pad.shape[1]), row(m_out.shape[1]), _const_spec(woa.shape),
                  _const_spec(wom.shape), _const_spec(g.shape), _const_spec(wg.shape), _const_spec(wu.shape),
                  _const_spec(wd.shape)],
        out_specs=row(D_MODEL),
        out_shape=jax.ShapeDtypeStruct((n, D_MODEL), F32),
        compiler_params=_params("parallel"),
        name="post_mix_ffn2",
    )(x, a_pad, m_out, woa, wom, g, wg, wu, wd)


def _rms_two_groups_t(k, g_col):
    sq = k * k
    half = k.shape[0] // 2
    s0 = jnp.sum(sq[0:half], axis=0, keepdims=True)
    s1 = jnp.sum(sq[half:], axis=0, keepdims=True)
    ms = jnp.concatenate([jnp.broadcast_to(s0, (half, k.shape[1])), jnp.broadcast_to(s1, (half, k.shape[1]))],
                         axis=0) * (1.0 / HEAD_DIM_A)
    return k * lax.rsqrt(ms + EPS) * g_col


def _inproj_kernel(x_ref, kmask_ref, g_ref, wqt_ref, wkvt_ref, wm_ref, ws_ref, wst_ref, qg_ref, kgt_ref,
                   q_out, cmp_rows, cmpt_out, selt_out, wint_out, kst, vst, kwt, vwt, qk_out, v_out, o_out,
                   small_out, gate_out):
    xn = _rms_rows(x_ref[...], g_ref[...]).astype(BF16)
    qt = _dot_nt(wqt_ref[...], xn)
    zeros = jnp.zeros((HEAD_DIM_A, qt.shape[1]), BF16)
    for h in range(N_HEADS_A):
        qh = qt[h * HEAD_DIM_A:(h + 1) * HEAD_DIM_A, :]
        ms = jnp.sum(qh * qh, axis=0, keepdims=True) * (1.0 / HEAD_DIM_A)
        own = h * LANES + (h // GROUP_R) * HEAD_DIM_A
        other = h * LANES + (1 - h // GROUP_R) * HEAD_DIM_A
        q_out[0, own:own + HEAD_DIM_A, :] = (qh * lax.rsqrt(ms + EPS) * qg_ref[...]).astype(BF16)
        q_out[0, other:other + HEAD_DIM_A, :] = zeros
    kvt = _dot_nt(wkvt_ref[...], xn)
    cmpt_out[0] = kvt[0:KV_ROW]
    cmp_rows[0] = kvt[0:KV_W].T
    cmp_rows[1] = kvt[KV_W:KV_ROW].T
    ks = _rms_two_groups_t(kvt[2 * KV_W:3 * KV_W], kgt_ref[0])
    vs = kvt[3 * KV_W:4 * KV_W]
    kw = _rms_two_groups_t(kvt[4 * KV_W:5 * KV_W], kgt_ref[1])
    vw = kvt[5 * KV_W:6 * KV_W]
    selt_out[0, 0:KV_W] = ks
    selt_out[0, KV_W:KV_ROW] = vs
    wint_out[0, 0:KV_W] = kw
    wint_out[0, KV_W:KV_ROW] = vw
    ones = jnp.ones((V_ROWS - HEAD_DIM_A, LANES), F32)
    for j in range(kst.shape[1]):
        cols = slice(j * LANES, (j + 1) * LANES)
        kst[0, j] = jnp.concatenate([ks[:, cols].T.astype(BF16), kmask_ref[cols, :]], axis=1)
        kwt[0, j] = kw[:, cols].T.astype(BF16)
        for v, out in ((vs, vst), (vw, vwt)):
            for g in range(N_KV_A):
                out[0, j, g] = jnp.concatenate([v[g * HEAD_DIM_A:(g + 1) * HEAD_DIM_A, cols], ones],
                                               axis=0).astype(BF16)
    m = _dot(xn, wm_ref[...])
    qk_out[...] = m[:, 0:2 * D_M]
    v_out[...] = m[:, 2 * D_M:3 * D_M]
    o_out[...] = m[:, 3 * D_M:4 * D_M]
    small_out[...] = _dot(xn, ws_ref[...])
    gate_out[0] = _sigmoid(_dot_nt(wst_ref[...], xn))


def _inproj(x, seq_len, kmask, g, wqt, wkvt, wm, ws, wst, qg, kgt):
    n = x.shape[0]
    nb = n // seq_len
    tm = _token_tile(seq_len)
    tpb = seq_len // tm
    row = lambda w: pl.BlockSpec((tm, w), lambda i: (i, 0))
    rows_out = lambda w, dt: (row(w), jax.ShapeDtypeStruct((n, w), dt))
    chan_out = lambda c, dt: (pl.BlockSpec((1, c, tm), lambda i: (i // tpb, 0, i % tpb)),
                              jax.ShapeDtypeStruct((nb, c, seq_len), dt))
    k_tiles = (pl.BlockSpec((1, tm // LANES, LANES, KV_W), lambda i: (i // tpb, i % tpb, 0, 0)),
               jax.ShapeDtypeStruct((nb, seq_len // LANES, LANES, KV_W), BF16))
    ksel_tiles = (pl.BlockSpec((1, tm // LANES, LANES, 2 * KV_W), lambda i: (i // tpb, i % tpb, 0, 0)),
                  jax.ShapeDtypeStruct((nb, seq_len // LANES, LANES, 2 * KV_W), BF16))
    v_tiles = (pl.BlockSpec((1, tm // LANES, N_KV_A, V_ROWS, LANES), lambda i: (i // tpb, i % tpb, 0, 0, 0)),
               jax.ShapeDtypeStruct((nb, seq_len // LANES, N_KV_A, V_ROWS, LANES), BF16))
    halves_out = (pl.BlockSpec((2, tm, KV_W), lambda i: (0, i, 0)), jax.ShapeDtypeStruct((2, n, KV_W), F32))
    outs = [chan_out(N_HEADS_A * LANES, BF16), halves_out, chan_out(KV_ROW, F32), chan_out(KV_ROW, F32),
            chan_out(KV_ROW, F32), ksel_tiles, v_tiles, k_tiles, v_tiles,
            rows_out(2 * D_M, F32), rows_out(D_M, F32), rows_out(D_M, F32), rows_out(LANES, F32),
            chan_out(LANES, F32)]
    consts = (g, wqt, wkvt, wm, ws, wst, qg, kgt)
    return pl.pallas_call(
        _inproj_kernel,
        grid=(n // tm,),
        in_specs=[row(D_MODEL), pl.BlockSpec((tm, LANES), lambda i: (i % tpb, 0))]
        + [_const_spec(a.shape) for a in consts],
        out_specs=[o[0] for o in outs],
        out_shape=[o[1] for o in outs],
        compiler_params=_params("parallel"),
        name="in_proj",
    )(x, kmask, *consts)


def _t5_bucket_np(dist):
    n = np.maximum(dist, 0)
    exact = N_BUCKETS // 2
    nf = np.maximum(n, 1).astype(np.float32)
    ratio = np.log(nf / np.float32(exact)) / np.float32(math.log(MAX_DISTANCE / exact))
    large = exact + (ratio * np.float32(N_BUCKETS - exact)).astype(np.int32)
    return np.where(n < exact, n, np.minimum(large, N_BUCKETS - 1)).astype(np.int32)


def _bias_kernel(tbl_ref, bkt_ref, out_ref):
    bkt = bkt_ref[0]
    for h in range(N_HEADS_A):
        far = tbl_ref[N_BUCKETS - 1, h]
        acc = jnp.zeros(bkt.shape, F32)
        for b in range(N_BUCKETS - 1):
            acc = jnp.where(bkt == b, tbl_ref[b, h] - far, acc)
        out_ref[0, h] = acc


def _bias_tiles(rel_bias, buckets):
    n = buckets.shape[0]
    return pl.pallas_call(
        _bias_kernel,
        grid=(n,),
        in_specs=[pl.BlockSpec(memory_space=pltpu.SMEM),
                  pl.BlockSpec((1, LANES, LANES), lambda i: (i, 0, 0))],
        out_specs=pl.BlockSpec((1, N_HEADS_A, LANES, LANES), lambda i: (i, 0, 0, 0)),
        out_shape=jax.ShapeDtypeStruct((n, N_HEADS_A, LANES, LANES), F32),
        compiler_params=_params("parallel"),
        name="bias_tiles",
    )(rel_bias, buckets)


def _compress_half(rows_ref, nc, w_ref, half, bias):
    fs = None
    for t in range(CMP_STRIDE):
        part = _dot(rows_ref[pl.ds(t, nc, stride=CMP_STRIDE), :].astype(BF16), w_ref[half, t])
        fs = part if fs is None else fs + part
    return fs[:, 0:KV_W] + pltpu.roll(fs[:, KV_W:KV_ROW], nc - 1, 0) + bias


def _compress_rows(k_rows_ref, v_rows_ref, nc, w_ref, b_ref, g_ref):
    kc = _rms_two_groups(_compress_half(k_rows_ref, nc, w_ref, 0, b_ref[:, 0:KV_W]), g_ref[...])
    vc = _compress_half(v_rows_ref, nc, w_ref, 1, b_ref[:, KV_W:KV_ROW])
    complete = lax.broadcasted_iota(jnp.int32, kc.shape, 0) < nc - 1
    return jnp.where(complete, kc, 0.0), jnp.where(complete, vc, 0.0)


def _compress_kernel(x_ref, w_ref, b_ref, g_ref, kc_ref, vc_ref):
    nc = kc_ref.shape[1]
    kc, vc = _compress_rows(x_ref.at[0], x_ref.at[1], nc, w_ref, b_ref, g_ref)
    kc_ref[0] = kc.astype(BF16)
    vc_ref[0] = vc.T.astype(BF16)


def _compress(cmp_rows, t, w_big, b_big, g_k):
    nb = cmp_rows.shape[1] // t
    nc = t // CMP_STRIDE
    return pl.pallas_call(
        _compress_kernel,
        grid=(nb,),
        in_specs=[pl.BlockSpec((2, t, KV_W), lambda b: (0, b, 0)), _const_spec(w_big.shape),
                  _const_spec(b_big.shape), _const_spec(g_k.shape)],
        out_specs=[pl.BlockSpec((1, nc, KV_W), lambda b: (b, 0, 0)),
                   pl.BlockSpec((1, KV_W, nc), lambda b: (b, 0, 0))],
        out_shape=[jax.ShapeDtypeStruct((nb, nc, KV_W), BF16), jax.ShapeDtypeStruct((nb, KV_W, nc), BF16)],
        compiler_params=_params("parallel"),
        name="compress_prompt",
    )(cmp_rows, w_big, b_big, g_k)


def _tile_heads(x):
    return jnp.concatenate([x] * GROUP_R, axis=1)


def _softmax_part(m, s):
    m_new = jnp.maximum(m, jnp.max(s, axis=0, keepdims=True))
    return m_new, jnp.exp2(m - m_new), jnp.exp2(s - m_new).astype(BF16)


def _accumulate(acc, soft, vt):
    m_new, alpha, p = soft
    return m_new, alpha * acc + _dot(vt, p)


def _key_rows(ref, t, n):
    return ref[0, pl.ds(t, n)].reshape(n * LANES, ref.shape[-1])


def _value_cols(ref, t, n, g):
    return jnp.concatenate([ref[0, t + k, g] for k in range(n)], axis=1)


def _rank_rows(score):
    n_blk = score.shape[0]
    blk = lax.broadcasted_iota(jnp.int32, score.shape, 0)
    rank = jnp.zeros(score.shape, F32)
    for b in range(n_blk):
        row = score[b:b + 1, :]
        rank = rank + jnp.where(blk > b, jnp.where(row >= score, 1.0, 0.0), jnp.where(row > score, 1.0, 0.0))
    return rank


def _nsa_prompt_kernel(q_ref, gate_ref, kc_ref, vct_ref, ks_ref, vs_ref, kw_ref, vw_ref, sel_add_ref, win_add_ref,
                       bcmp_ref, covt_ref, a_ref, *, n_cmp):
    ii = pl.program_id(1)
    n_tiles = ks_ref.shape[1]
    n_sel = covt_ref.shape[0]
    wide = GROUP_R * Q_TILE
    heads_of = lambda g: range(g * GROUP_R, (g + 1) * GROUP_R)
    chains =[(sb, g) for sb in range(Q_BLOCKS) for g in range(N_KV_A)]
    blk_i = [Q_BLOCKS * ii + sb for sb in range(Q_BLOCKS)]
    cols = lambda sb: slice(sb * Q_TILE, (sb + 1) * Q_TILE)
    qg_t = {(sb, g): jnp.concatenate([q_ref[0, h * LANES:(h + 1) * LANES, cols(sb)] for h in heads_of(g)], axis=1)
            for sb, g in chains}
    init = (jnp.full((1, wide), NEG, F32), jnp.zeros((V_ROWS, wide), F32))
    normalised = lambda acc: acc[0:HEAD_DIM_A, :] / acc[HEAD_DIM_A:HEAD_DIM_A + 1, :]

    n_win = WINDOW // LANES + 1
    t0 = [jnp.maximum(i - (n_win - 1), 0) for i in blk_i]
    s_win = {}
    for sb, i in enumerate(blk_i):
        win_off = pl.multiple_of(jnp.maximum(n_win - 1 - i, 0) * LANES, LANES)
        k_win = _key_rows(kw_ref, t0[sb], n_win)
        for g in range(N_KV_A):
            s_win[sb, g] = _dot(k_win, qg_t[sb, g]) + win_add_ref[g, pl.ds(win_off, n_win * LANES), :]

    ncp = kc_ref.shape[1]
    j_sub = lax.broadcasted_iota(jnp.int32, (ncp, wide), 0)
    q_in_tile = lax.broadcasted_iota(jnp.int32, (ncp, wide), 1) & (Q_TILE - 1)
    s_cmp, any_usable = {}, []
    for sb, i in enumerate(blk_i):
        q_lane = i * Q_TILE + q_in_tile
        usable_add = jnp.where((j_sub * CMP_STRIDE + (CMP_LEN - 1) <= q_lane) & (j_sub < n_cmp), 0.0, NEG)
        any_usable.append(jnp.where(q_lane[0:1, :] >= CMP_LEN - 1, 1.0, 0.0))
        for g in range(N_KV_A):
            s_cmp[sb, g] = _dot(kc_ref[0], qg_t[sb, g]) + bcmp_ref[sb, g] + usable_add
    p_c = {}
    for sb, g in chains:
        e = jnp.exp2(s_cmp[sb, g] - jnp.max(s_cmp[sb, g], axis=0, keepdims=True)) * any_usable[sb]
        p_c[sb, g] = e / jnp.maximum(jnp.sum(e, axis=0, keepdims=True), 1e-30)
    oc_t = {c: _dot(vct_ref[0], p_c[c].astype(BF16)) for c in chains}

    blk = lax.broadcasted_iota(jnp.int32, (n_sel, Q_TILE), 0)
    q_in_blk = lax.broadcasted_iota(jnp.int32, (n_sel, Q_TILE), 1)
    tile_row = lax.broadcasted_iota(jnp.int32, (n_tiles, wide), 0)
    pad_rows = jnp.zeros((LANES - n_sel - n_tiles, wide), BF16)
    q_far, q_near = {}, {}
    for sb, g in chains:
        qpos = blk_i[sb] * Q_TILE + q_in_blk
        cur = qpos // SEL_BLOCK
        valid = blk * SEL_BLOCK <= qpos
        forced = valid & ((blk == 0) | (blk == cur) | (blk == cur - 1))
        p_sum = p_c[sb, g][:, 0:Q_TILE]
        for r in range(1, GROUP_R):
            p_sum = p_sum + p_c[sb, g][:, r * Q_TILE:(r + 1) * Q_TILE]
        imp_t = _dot_exact_lhs(covt_ref[...], p_sum)
        score = jnp.where(forced, -NEG, jnp.where(valid, imp_t, NEG))
        dropped = _tile_heads(jnp.where(_rank_rows(score) < float(min(TOP_K_BLOCKS, n_sel)), 0.0, 1.0).astype(BF16))
        past_far = jnp.where(tile_row > blk_i[sb] - 2, 1.0, 0.0).astype(BF16)
        q_far[sb, g] = jnp.concatenate([qg_t[sb, g], dropped, past_far, pad_rows], axis=0)
        q_near[sb, g] = jnp.concatenate([qg_t[sb, g], dropped, jnp.zeros_like(past_far), pad_rows], axis=0)

    soft_win = {c: _softmax_part(init[0], s_win[c]) for c in chains}
    ow_acc = {(sb, g): _dot(_value_cols(vw_ref, t0[sb], n_win, g), soft_win[sb, g][2]) for sb, g in chains}

    def far_body(it, carry):
        t = FAR_TILES * it
        k_far = _key_rows(ks_ref, t, FAR_TILES)
        s = {c: _dot(k_far, q_far[c]) for c in chains}
        soft = [_softmax_part(carry[n][0], s[c]) for n, c in enumerate(chains)]
        return tuple(_accumulate(carry[n][1], soft[n], _value_cols(vs_ref, t, FAR_TILES, c[1]))
                     for n, c in enumerate(chains))

    n_far = blk_i[-1] - 1
    carry = lax.fori_loop(0, (n_far + FAR_TILES - 1) // FAR_TILES, far_body, (init,) * len(chains))

    t1 = [jnp.maximum(i - 1, 0) for i in blk_i]
    s_near = {}
    for sb, i in enumerate(blk_i):
        k_near = _key_rows(ks_ref, t1[sb], 2)
        near_off = pl.multiple_of(jnp.where(i == 0, LANES, 0), LANES)
        for g in range(N_KV_A):
            s_near[sb, g] = _dot(k_near, q_near[sb, g]) + sel_add_ref[g, pl.ds(near_off, 2 * LANES), :]
    soft_near = [_softmax_part(carry[n][0], s_near[c]) for n, c in enumerate(chains)]
    for n, (sb, g) in enumerate(chains):
        _, acc = _accumulate(carry[n][1], soft_near[n], _value_cols(vs_ref, t1[sb], 2, g))
        heads = heads_of(g)
        gate_row = lambda br: jnp.concatenate(
            [gate_ref[0, br * N_HEADS_A + h:br * N_HEADS_A + h + 1, cols(sb)] for h in heads], axis=1)
        out_t = (gate_row(0) * oc_t[sb, g][g * HEAD_DIM_A:(g + 1) * HEAD_DIM_A, :] + gate_row(1) * normalised(acc)
                 + gate_row(2) * normalised(ow_acc[sb, g]))
        for pair in range(GROUP_R // 2):
            two = jnp.concatenate([out_t[:, (2 * pair + k) * Q_TILE:(2 * pair + k + 1) * Q_TILE] for k in range(2)],
                                  axis=0)
            col = (g * GROUP_R // 2 + pair) * LANES
            a_ref[0, cols(sb), col:col + LANES] = two.T.astype(BF16)


def _nsa_prompt(q_t, gate_t, kc, vc_t, ks, vs, kw, vw, sel_add, win_add, bcmp, cov_t, n_cmp):
    nb, _, t = q_t.shape
    nqb = t // Q_TILE
    assert nqb > WINDOW // LANES and nqb % Q_BLOCKS == 0
    step = Q_BLOCKS * Q_TILE
    seq = lambda a: pl.BlockSpec((1,) + a.shape[1:], lambda b, i: (b,) + (0,) * (a.ndim - 1))
    return pl.pallas_call(
        functools.partial(_nsa_prompt_kernel, n_cmp=n_cmp),
        grid=(nb, nqb // Q_BLOCKS),
        in_specs=[pl.BlockSpec((1, N_HEADS_A * LANES, step), lambda b, i: (b, 0, i)),
                  pl.BlockSpec((1, LANES, step), lambda b, i: (b, 0, i)),
                  seq(kc), seq(vc_t), seq(ks), seq(vs), seq(kw), seq(vw),
                  _const_spec(sel_add.shape), _const_spec(win_add.shape),
                  pl.BlockSpec((Q_BLOCKS,) + bcmp.shape[1:], lambda b, i: (i, 0, 0, 0)),
                  _const_spec(cov_t.shape)],
        out_specs=pl.BlockSpec((1, step, D_A), lambda b, i: (b, i, 0)),
        out_shape=jax.ShapeDtypeStruct((nb, t, D_A), BF16),
        compiler_params=_params("parallel", "arbitrary"),
        name="nsa_prompt",
    )(q_t, gate_t, kc, vc_t, ks, vs, kw, vw, sel_add, win_add, bcmp, cov_t)


def _head_norm_gate(h_all, mn_ref, o):
    outs = []
    for h in range(N_HEADS_M):
        hs = slice(h * HEAD_DIM_M, (h + 1) * HEAD_DIM_M)
        outs.append(_rms_rows(h_all[h], mn_ref[:, hs]))
    return _sigmoid(o) * jnp.concatenate(outs, axis=-1)


def _mlstm_prompt_kernel(qk_ref, v_ref, o_ref, small_ref, cw_ref, cb_ref, bif_ref, mn_ref, tri_ref, triu_ref,
                         y_ref, c_ref, n_ref, m_ref, xbuf):
    chunk = qk_ref.shape[1]
    pad = SUBLANES

    @pl.when(pl.program_id(1) == 0)
    def _():
        xbuf[...] = jnp.zeros_like(xbuf)
        c_ref[...] = jnp.zeros_like(c_ref)
        n_ref[...] = jnp.zeros_like(n_ref)
        m_ref[...] = jnp.zeros_like(m_ref)

    x = qk_ref[0]
    tail = xbuf[...]
    head_row = lax.broadcasted_iota(jnp.int32, tail.shape, 0)

    def delayed(s):
        rolled = pltpu.roll(x, s, 0)
        head = jnp.where(head_row < s, pltpu.roll(tail, s, 0), rolled[0:pad, :])
        return jnp.concatenate([head, rolled[pad:, :]], axis=0)

    y = delayed(3) * cw_ref[0:1, :]
    y = y + delayed(2) * cw_ref[1:2, :]
    y = y + delayed(1) * cw_ref[2:3, :]
    y = y + x * cw_ref[3:4, :] + cb_ref[...]
    qkc = y * _sigmoid(y)
    xbuf[...] = x[chunk - pad:chunk, :]

    gi = small_ref[0] + bif_ref[...]
    n_gate = 3 * N_HEADS_A
    gate_rows = gi.T[n_gate:n_gate + 2 * N_HEADS_M, :]
    b_col = _dot_exact_lhs(tri_ref[...], _log_sigmoid(gi))
    b_row = _dot_exact_rhs(_log_sigmoid(gate_rows), triu_ref[...])
    t_col = lax.broadcasted_iota(jnp.int32, (chunk, chunk), 0)
    s_row = lax.broadcasted_iota(jnp.int32, (chunk, chunk), 1)
    causal = s_row <= t_col
    m_all = m_ref[0]
    lane_row = lax.broadcasted_iota(jnp.int32, m_all.shape, 1)

    h_all = []
    for h in range(N_HEADS_M):
        hs = slice(h * HEAD_DIM_M, (h + 1) * HEAD_DIM_M)
        ks = slice(D_M + h * HEAD_DIM_M, D_M + (h + 1) * HEAD_DIM_M)
        q = qkc[:, hs]
        k = qkc[:, ks] * F32(HEAD_DIM_M ** -0.5)
        v = v_ref[0, :, hs]
        qb, kb = q.astype(BF16), k.astype(BF16)
        ci, fi = 3 * N_HEADS_A + h, 3 * N_HEADS_A + N_HEADS_M + h
        bt = b_col[:, fi:fi + 1]
        bs = b_row[N_HEADS_M + h:N_HEADS_M + h + 1, :]
        ig_row = gate_rows[h:h + 1, :]
        ig_col = gi[:, ci:ci + 1]
        m_prev = m_all[:, h:h + 1]
        c_prev = c_ref[0, h]
        n_prev = n_ref[0, h:h + 1, :]

        dlog = jnp.where(causal, bt - bs + ig_row, NEG)
        inter = bt + m_prev
        m_t = jnp.maximum(inter, jnp.max(dlog, axis=-1, keepdims=True))
        s_qk = _dot_nt(qb, kb) * jnp.exp(dlog - m_t)
        dec = jnp.exp(inter - m_t)
        num = _dot(s_qk.astype(BF16), v.astype(BF16)) + dec * _dot_nt(qb, c_prev.astype(BF16))
        den = jnp.sum(s_qk, axis=-1, keepdims=True) + dec * jnp.sum(q * n_prev, axis=-1, keepdims=True)
        h_all.append(num / jnp.maximum(jnp.abs(den), jnp.exp(-m_t)))

        m_new = m_t[chunk - 1:chunk, :]
        b_end = bt[chunk - 1:chunk, :]
        w_end = jnp.exp(b_end - bt + ig_col - m_new)
        dec_end = jnp.exp(b_end + m_prev - m_new)
        wv_t = (w_end * v).T.astype(BF16)
        c_ref[0, h] = dec_end * c_prev + _dot(wv_t, kb)
        n_ref[0, h:h + 1, :] = dec_end * n_prev + jnp.sum(w_end * k, axis=0, keepdims=True)
        m_all = jnp.where(lane_row == h, m_new, m_all)

    m_ref[0] = m_all
    y_ref[0] = _head_norm_gate(h_all, mn_ref, o_ref[0]).astype(BF16)


def _mlstm_prompt(qk, v, o, small, conv_w, conv_b, bif, mn_g, chunk):
    nb, t, _ = qk.shape
    tri = jnp.asarray(np.tril(np.ones((chunk, chunk), np.float32)), BF16)
    triu = jnp.asarray(np.triu(np.ones((chunk, chunk), np.float32)), BF16)
    tok = lambda w: pl.BlockSpec((1, chunk, w), lambda b, c: (b, c, 0))
    return pl.pallas_call(
        _mlstm_prompt_kernel,
        grid=(nb, t // chunk),
        in_specs=[tok(2 * D_M), tok(D_M), tok(D_M), tok(LANES)]
        + [_const_spec(a.shape) for a in (conv_w, conv_b, bif, mn_g, tri, triu)],
        out_specs=[tok(D_M),
                   pl.BlockSpec((1, N_HEADS_M, HEAD_DIM_M, HEAD_DIM_M), lambda b, c: (b, 0, 0, 0)),
                   pl.BlockSpec((1, N_HEADS_M, HEAD_DIM_M), lambda b, c: (b, 0, 0)),
                   pl.BlockSpec((1, 1, LANES), lambda b, c: (b, 0, 0))],
        out_shape=[jax.ShapeDtypeStruct((nb, t, D_M), BF16),
                   jax.ShapeDtypeStruct((nb, N_HEADS_M, HEAD_DIM_M, HEAD_DIM_M), F32),
                   jax.ShapeDtypeStruct((nb, N_HEADS_M, HEAD_DIM_M), F32),
                   jax.ShapeDtypeStruct((nb, 1, LANES), F32)],
        scratch_shapes=[pltpu.VMEM((SUBLANES, 2 * D_M), F32)],
        compiler_params=_params("parallel", "arbitrary"),
        name="mlstm_prompt",
    )(qk, v, o, small, conv_w, conv_b, bif, mn_g, tri, triu)


def _softmax_rows(s):
    e = jnp.exp(s - jnp.max(s, axis=-1, keepdims=True))
    return e, jnp.sum(e, axis=-1, keepdims=True)


def _sample_cmp_kernel(pt_ref, *refs, n_pages, n_sel, qpos):
    pages = refs[:n_pages]
    w_ref, b_ref, g_ref, q_ref, bias_ref, cov_ref, oc_ref, idx_ref, xk_ref, xv_ref = refs[n_pages:]
    for half, rows_ref in ((0, xk_ref), (1, xv_ref)):
        for u in range(n_pages):
            rows_ref[u * PAGE_SIZE:(u + 1) * PAGE_SIZE, :] = pages[u][0, half * KV_W:(half + 1) * KV_W, :].T
    nc = n_pages * PAGE_SIZE // CMP_STRIDE
    kc, vc = _compress_rows(xk_ref, xv_ref, nc, w_ref, b_ref, g_ref)
    q = q_ref[0]
    j_row = lax.broadcasted_iota(jnp.int32, (N_HEADS_A, nc), 1)
    s = _dot_nt(q, kc.astype(BF16)) + bias_ref[...] + jnp.where(j_row < nc - 1, 0.0, NEG)
    e, l = _softmax_rows(s)
    p = e / jnp.maximum(l, 1e-30)
    oc_ref[0] = _dot(p.astype(BF16), vc.astype(BF16))

    nsp = cov_ref.shape[1]
    blk_row = lax.broadcasted_iota(jnp.int32, (1, nsp), 1)
    sub = lax.broadcasted_iota(jnp.int32, (nsp, nsp), 0)
    lan = lax.broadcasted_iota(jnp.int32, (nsp, nsp), 1)
    cur = qpos // SEL_BLOCK
    valid = (blk_row * SEL_BLOCK <= qpos) & (blk_row < n_sel)
    forced = valid & ((blk_row == 0) | (blk_row == cur) | (blk_row == cur - 1))
    k_col = lax.broadcasted_iota(jnp.int32, (TOP_K_BLOCKS, nsp), 0).astype(F32)
    blk_f = lax.broadcasted_iota(jnp.int32, (TOP_K_BLOCKS, nsp), 1).astype(F32)
    for g in range(N_KV_A):
        p_sum = jnp.sum(p[g * GROUP_R:(g + 1) * GROUP_R, :], axis=0, keepdims=True)
        imp = _dot_exact_rhs(jnp.broadcast_to(p_sum, (SUBLANES, nc)), cov_ref[...])[0:1, :]
        score = jnp.where(forced, -NEG, jnp.where(valid, imp, NEG))
        score_col = jnp.sum(jnp.where(sub == lan, score, 0.0), axis=-1, keepdims=True)
        ahead = jnp.where(sub < lan, jnp.where(score_col >= score, 1.0, 0.0), jnp.where(score_col > score, 1.0, 0.0))
        rank = jnp.sum(ahead, axis=0, keepdims=True)
        chosen = jnp.sum(jnp.where(rank == k_col, blk_f, 0.0), axis=-1, keepdims=True)
        idx_ref[0, g * TOP_K_BLOCKS:(g + 1) * TOP_K_BLOCKS, :] = jnp.broadcast_to(
            chosen, (TOP_K_BLOCKS, LANES)).astype(jnp.int32)


def _sample_cmp(page_table, pool, w_big, b_big, g_k, q3, bias_c, cover, n_sel, qpos):
    ns, n_pages = page_table.shape
    page_spec = lambda u: pl.BlockSpec((1, KV_ROW, PAGE_SIZE), lambda s, pt: (pt[s, u], 0, 0))
    const = lambda a: pl.BlockSpec(a.shape, lambda s, pt: (0,) * a.ndim, pipeline_mode=pl.Buffered(1))
    grid_spec = pltpu.PrefetchScalarGridSpec(
        num_scalar_prefetch=1,
        grid=(ns,),
        in_specs=[page_spec(u) for u in range(n_pages)]
        + [const(w_big), const(b_big), const(g_k),
           pl.BlockSpec((1, N_HEADS_A, LANES), lambda s, pt: (s, 0, 0)), const(bias_c), const(cover)],
        out_specs=[pl.BlockSpec((1, N_HEADS_A, LANES), lambda s, pt: (s, 0, 0)),
                   pl.BlockSpec((1, N_KV_A * TOP_K_BLOCKS, LANES), lambda s, pt: (s, 0, 0))],
        scratch_shapes=[pltpu.VMEM((n_pages * PAGE_SIZE, KV_W), F32)] * 2,
    )
    return pl.pallas_call(
        functools.partial(_sample_cmp_kernel, n_pages=n_pages, n_sel=n_sel, qpos=qpos),
        grid_spec=grid_spec,
        out_shape=[jax.ShapeDtypeStruct((ns, N_HEADS_A, LANES), F32),
                   jax.ShapeDtypeStruct((ns, N_KV_A * TOP_K_BLOCKS, LANES), jnp.int32)],
        compiler_params=_params("arbitrary"),
        name="sample_cmp_topk",
    )(page_table, *([pool] * n_pages), w_big, b_big, g_k, q3, bias_c, cover)


def _sample_attn_kernel(idx_ref, pt_ref, *refs, n_past_blocks):
    n_slots = N_KV_A * TOP_K_BLOCKS
    blocks = refs[:n_slots]
    (win_ref, q_ref, selnew_ref, winnew_ref, wint_ref, oc_ref, gate_ref, bsel_ref, bwin_ref, b0_ref,
     a_ref, winout_ref) = refs[n_slots:]
    s_id = pl.program_id(0)
    q = q_ref[0]
    qf = q.astype(F32)
    lane = lax.broadcasted_iota(jnp.int32, (N_HEADS_A, LANES), 1)
    row = lax.broadcasted_iota(jnp.int32, (N_HEADS_A, LANES), 0)
    lo = lane < SEL_BLOCK

    def with_new_key(s, vt_mat, new_row):
        s_new = jnp.sum(qf * new_row[:, 0:KV_W], axis=-1, keepdims=True) + b0_ref[:, 0:1]
        m = jnp.maximum(jnp.max(s, axis=-1, keepdims=True), s_new)
        p = jnp.exp(s - m)
        p_new = jnp.exp(s_new - m)
        l = jnp.sum(p, axis=-1, keepdims=True) + p_new
        return (_dot_nt(p.astype(BF16), vt_mat) + p_new * new_row[:, KV_W:KV_ROW]) / l

    sel_new = selnew_ref[0]
    o_s = []
    for g in range(N_KV_A):
        slots = range(g * TOP_K_BLOCKS, (g + 1) * TOP_K_BLOCKS)
        kt_cat = jnp.concatenate([blocks[k][0, 0:KV_W, :] for k in slots], axis=1).astype(BF16)
        vt_cat = jnp.concatenate([blocks[k][0, KV_W:KV_ROW, :] for k in slots], axis=1).astype(BF16)
        adds = []
        for k in slots:
            b = idx_ref[s_id * n_slots + k]
            add = jnp.where(b == n_past_blocks - 2, bsel_ref[0], jnp.where(b == n_past_blocks - 1, bsel_ref[1], 0.0))
            other_half = jnp.where(b % 2 == 0, jnp.where(lo, 0.0, NEG), jnp.where(lo, NEG, 0.0))
            adds.append(add + other_half + jnp.where(b >= n_past_blocks, NEG, 0.0))
        o_s.append(with_new_key(_dot(q, kt_cat) + jnp.concatenate(adds, axis=-1), vt_cat, sel_new))
    o_sel = jnp.where(row < GROUP_R, o_s[0], o_s[1])

    win = win_ref[0]
    wb = win.shape[1]
    w_lane = lax.broadcasted_iota(jnp.int32, (N_HEADS_A, wb), 1)
    s_w = _dot(q, win[0:KV_W, :].astype(BF16)) + bwin_ref[...] + jnp.where(w_lane == 0, NEG, 0.0)
    o_win = with_new_key(s_w, win[KV_W:KV_ROW, :].astype(BF16), winnew_ref[0])

    na = N_HEADS_A
    out = gate_ref[0, 0:na] * oc_ref[0] + gate_ref[0, na:2 * na] * o_sel + gate_ref[0, 2 * na:3 * na] * o_win
    a_ref[0] = jnp.where(lo == (row < GROUP_R), out, 0.0).astype(BF16)

    wint = wint_ref[...]
    seq_lane = lax.broadcasted_iota(jnp.int32, wint.shape, 1)
    new_col = jnp.sum(jnp.where(seq_lane == s_id, wint, 0.0), axis=-1, keepdims=True)
    buf_lane = lax.broadcasted_iota(jnp.int32, win.shape, 1)
    winout_ref[0] = jnp.where(buf_lane == wb - 1, new_col, pltpu.roll(win, wb - 1, 1))


def _sample_attn(idx, page_table, pool, win_t, q3, sel_new, win_new, win_new_t, o_c, gate_b, bsel, bwin, b0):
    ns, n_pages = page_table.shape
    n_slots = N_KV_A * TOP_K_BLOCKS
    per_page = PAGE_SIZE // SEL_BLOCK
    n_past_blocks = n_pages * per_page
    wb = win_t.shape[2]

    def slot_spec(k):
        def index(s, idx_ref, pt_ref):
            b = jnp.minimum(idx_ref[s * n_slots + k], n_past_blocks - 1)
            return (pt_ref[s, b // per_page], 0, 0)
        return pl.BlockSpec((1, KV_ROW, PAGE_SIZE), index)

    per_seq = lambda a: pl.BlockSpec((1,) + a.shape[1:], lambda s, *_: (s,) + (0,) * (a.ndim - 1))
    const = lambda a: pl.BlockSpec(a.shape, lambda s, *_: (0,) * a.ndim, pipeline_mode=pl.Buffered(1))
    grid_spec = pltpu.PrefetchScalarGridSpec(
        num_scalar_prefetch=2,
        grid=(ns,),
        in_specs=[slot_spec(k) for k in range(n_slots)]
        + [per_seq(win_t), per_seq(q3), per_seq(sel_new), per_seq(win_new), const(win_new_t), per_seq(o_c),
           per_seq(gate_b), const(bsel), const(bwin), const(b0)],
        out_specs=[pl.BlockSpec((1, N_HEADS_A, LANES), lambda s, *_: (s, 0, 0)),
                   pl.BlockSpec((1, KV_ROW, wb), lambda s, *_: (s, 0, 0))],
    )
    return pl.pallas_call(
        functools.partial(_sample_attn_kernel, n_past_blocks=n_past_blocks),
        grid_spec=grid_spec,
        out_shape=[jax.ShapeDtypeStruct((ns, N_HEADS_A, LANES), BF16),
                   jax.ShapeDtypeStruct((ns, KV_ROW, wb), F32)],
        compiler_params=_params("arbitrary"),
        name="sample_sel_win",
    )(idx, page_table, *([pool] * n_slots), win_t, q3, sel_new, win_new, win_new_t, o_c, gate_b, bsel, bwin, b0)


def _mlstm_step_kernel(qk_ref, conv_ref, v_ref, o_ref, small_ref, c_ref, n_ref, m_ref, cw_ref, cb_ref, bif_ref,
                       mn_ref, y_ref, c_out, n_out, m_out):
    seqs = qk_ref.shape[0]
    sub = lax.broadcasted_iota(jnp.int32, (HEAD_DIM_M, HEAD_DIM_M), 0)
    lan = lax.broadcasted_iota(jnp.int32, (HEAD_DIM_M, HEAD_DIM_M), 1)
    eye = sub == lan
    lane_row = lax.broadcasted_iota(jnp.int32, (1, LANES), 1)
    for s in range(seqs):
        hist = conv_ref[s]
        y = hist[0:1, :] * cw_ref[0:1, :]
        y = y + hist[1:2, :] * cw_ref[1:2, :]
        y = y + hist[2:3, :] * cw_ref[2:3, :]
        y = y + qk_ref[s] * cw_ref[3:4, :] + cb_ref[...]
        qkc = y * _sigmoid(y)
        gi = small_ref[s] + bif_ref[...]
        lf_all = _log_sigmoid(gi)
        m_all = m_ref[s]
        m_new_all = m_all
        h_all = []
        for h in range(N_HEADS_M):
            hs = slice(h * HEAD_DIM_M, (h + 1) * HEAD_DIM_M)
            ks = slice(D_M + h * HEAD_DIM_M, D_M + (h + 1) * HEAD_DIM_M)
            q = qkc[:, hs]
            k = qkc[:, ks] * F32(HEAD_DIM_M ** -0.5)
            v = v_ref[s][:, hs]
            ci, fi = 3 * N_HEADS_A + h, 3 * N_HEADS_A + N_HEADS_M + h
            ig = gi[:, ci:ci + 1]
            lf = lf_all[:, fi:fi + 1]
            m_prev = m_all[:, h:h + 1]
            c_prev = c_ref[s, h]
            n_prev = n_ref[s, h:h + 1, :]
            inter = lf + m_prev
            m_t = jnp.maximum(inter, ig)
            w_in = jnp.exp(ig - m_t)
            dec = jnp.exp(inter - m_t)
            s_qk = jnp.sum(q * k, axis=-1, keepdims=True) * w_in
            cq = _dot_nt(jnp.broadcast_to(q, (SUBLANES, HEAD_DIM_M)).astype(BF16), c_prev.astype(BF16))[0:1, :]
            num = s_qk * v + dec * cq
            den = s_qk + dec * jnp.sum(n_prev * q, axis=-1, keepdims=True)
            h_all.append(num / jnp.maximum(jnp.abs(den), jnp.exp(-m_t)))
            v_col = jnp.sum(jnp.where(eye, v, 0.0), axis=-1, keepdims=True)
            c_out[s, h] = dec * c_prev + (w_in * v_col) * k
            n_out[s, h:h + 1, :] = dec * n_prev + w_in * k
            m_new_all = jnp.where(lane_row == h, m_t, m_new_all)
        m_out[s] = m_new_all
        y_ref[s] = _head_norm_gate(h_all, mn_ref, o_ref[s]).astype(BF16)


def _mlstm_step(qk, conv0, v, o, small, c0, n0, m0, conv_w, conv_b, bif, mn_g):
    ns = qk.shape[0]
    sb = STEP_SEQS if ns % STEP_SEQS == 0 else 1
    per = lambda a: pl.BlockSpec((sb,) + a.shape[1:], lambda i: (i,) + (0,) * (a.ndim - 1))
    ins = (qk, conv0, v, o, small, c0, n0, m0)
    outs = [jax.ShapeDtypeStruct((ns, 1, D_M), BF16), jax.ShapeDtypeStruct(c0.shape, F32),
            jax.ShapeDtypeStruct(n0.shape, F32), jax.ShapeDtypeStruct(m0.shape, F32)]
    return pl.pallas_call(
        _mlstm_step_kernel,
        grid=(ns // sb,),
        in_specs=[per(a) for a in ins] + [_const_spec(a.shape) for a in (conv_w, conv_b, bif, mn_g)],
        out_specs=[per(a) for a in outs],
        out_shape=outs,
        compiler_params=_params("parallel"),
        name="mlstm_step",
    )(*ins, conv_w, conv_b, bif, mn_g)


def _cover_np(n_cmp, n_sel):
    cs = np.arange(n_cmp)[:, None] * CMP_STRIDE
    bs = np.arange(n_sel)[None, :] * SEL_BLOCK
    shared = np.clip(np.minimum(cs + CMP_LEN, bs + SEL_BLOCK) - np.maximum(cs, bs), 0, None)
    return (shared / CMP_LEN).astype(np.float32)


def _prep_in_proj(w_in):
    d = w_in.shape[0]
    o_kv = D_A
    o_gate = o_kv + 6 * KV_W
    o_m = o_gate + 3 * N_HEADS_A
    o_if = o_m + 4 * D_M
    w_kv_t = w_in[:, o_kv:o_gate].T
    w_m = w_in[:, o_m:o_if]
    w_small = jnp.concatenate([w_in[:, o_gate:o_m], w_in[:, o_if:],
                               jnp.zeros((d, LANES - 3 * N_HEADS_A - 2 * N_HEADS_M), w_in.dtype)], axis=1)
    return [w.astype(BF16) for w in (w_in[:, :D_A].T, w_kv_t, w_m, w_small, w_small.T)]


def _prep_compress(w_k, b_k, w_v, b_v):
    per_c = jnp.stack([w_k, w_k, w_v, w_v]).reshape(4, 2, CMP_STRIDE, HEAD_DIM_A, HEAD_DIM_A)
    w_big = jnp.einsum('chtde,cf->tcdhfe', per_c, jnp.eye(4, dtype=w_k.dtype))
    w_big = w_big.reshape(CMP_STRIDE, KV_ROW, 2 * KV_ROW).astype(BF16)
    pick = lambda h: jnp.concatenate([w_big[:, h * KV_W:(h + 1) * KV_W, h * KV_W:(h + 1) * KV_W],
                                      w_big[:, h * KV_W:(h + 1) * KV_W, KV_ROW + h * KV_W:KV_ROW + (h + 1) * KV_W]],
                                     axis=-1)
    w_big = jnp.stack([pick(0), pick(1)])
    b_big = jnp.concatenate([b_k, b_k, b_v, b_v]).reshape(1, KV_ROW)
    return w_big, b_big


def _bucket_tiles(t, past, wb):
    r = np.arange(LANES)
    tiles = [_t5_bucket_np(d * LANES + r[None, :] - r[:, None]) for d in range(2)]
    for i in range(t // Q_TILE):
        tiles.append(_t5_bucket_np(i * Q_TILE + r[None, :] - (CMP_STRIDE * r[:, None] + CMP_LEN - 1)))
    flat = np.arange(LANES * LANES).reshape(LANES, LANES)
    tiles.append(_t5_bucket_np(past - (CMP_STRIDE * flat + CMP_LEN - 1)))
    tiles.append(_t5_bucket_np(wb - flat))
    tiles.append(_t5_bucket_np(past - (r[:, None] * SEL_BLOCK + r[None, :] % SEL_BLOCK)))
    return np.stack(tiles).astype(np.int32)


def _near_tables(bnear):
    r = np.arange(LANES)
    tile4 = lambda a: jnp.asarray(np.tile(a, (1, GROUP_R)), F32)
    causal = tile4(np.where(r[:, None] <= r[None, :], 0.0, NEG))
    oldest = tile4(np.where(r[:, None] > r[None, :], 0.0, NEG))
    zero = jnp.zeros_like(causal)
    masked = jnp.full_like(causal, NEG)
    n_mid = WINDOW // LANES - 2
    sel_add, win_add = [], []
    for g in range(N_KV_A):
        near = [bnear[1, g], bnear[0, g] + causal]
        sel_add.append(jnp.concatenate(near + [masked], axis=0))
        win_add.append(jnp.concatenate([oldest] + [zero] * n_mid + near + [masked] * (WINDOW // LANES), axis=0))
    return jnp.stack(sel_add), jnp.stack(win_add)


def kernel(x_prompt, x_sample, cache_cmp_kv, cache_sel_kv, cache_win_kv, state_C, state_n, state_m, state_conv, page_table, rel_bias, g_ffn1, w1_gate, w1_up, w1_down, g_mix, w_in, qn_g, kn_cmp_g, kn_sel_g, kn_win_g, w_cmp_k, b_cmp_k, w_cmp_v, b_cmp_v, conv_w, conv_b, b_if, mn_g, w_out, g_ffn2, w2_gate, w2_up, w2_down):
    assert x_prompt.shape[2] == D_MODEL and g_ffn1.shape[0] == 1
    nb, t, _ = x_prompt.shape
    ns, ds, _ = x_sample.shape
    assert ds == 1 and t % Q_TILE == 0
    n_pages = page_table.shape[1]
    past = n_pages * PAGE_SIZE
    wb = cache_win_kv.shape[2]
    assert wb == WINDOW
    kv_shape = (2, N_KV_A, HEAD_DIM_A)

    row = lambda a: a.reshape(1, -1)
    two = lambda a: jnp.tile(a, 2).reshape(1, KV_W)
    w1 = (row(g_ffn1[0]), w1_gate[0].astype(BF16), w1_up[0].astype(BF16), w1_down[0].astype(BF16))
    w2 = (row(g_ffn2[0]), w2_gate[0].astype(BF16), w2_up[0].astype(BF16), w2_down[0].astype(BF16))
    wqt, wkvt, wm, ws, wst = _prep_in_proj(w_in[0])
    qg = (qn_g[0] * F32(HEAD_DIM_A ** -0.5))[:, None]
    kgt = jnp.stack([jnp.tile(kn_sel_g[0], 2), jnp.tile(kn_win_g[0], 2)])[:, :, None]
    w_big, b_big = _prep_compress(w_cmp_k[0], b_cmp_k[0], w_cmp_v[0], b_cmp_v[0])
    g_cmp = two(kn_cmp_g[0])
    woa, wom = w_out[0][:D_A].astype(BF16), w_out[0][D_A:].astype(BF16)
    n_gate = 3 * N_HEADS_A
    bif = jnp.zeros((1, LANES), F32).at[0, n_gate:n_gate + 2 * N_HEADS_M].set(b_if[0])
    conv_b2 = row(conv_b[0])
    mn = row(mn_g[0])

    nqb = t // Q_TILE
    tiles = _bias_tiles(rel_bias, jnp.asarray(_bucket_tiles(t, past, wb)))
    group_lanes = lambda a: a.reshape(a.shape[0], N_KV_A, GROUP_R, LANES, LANES).transpose(0, 1, 3, 2, 4).reshape(
        a.shape[0], N_KV_A, LANES, GROUP_R * LANES)
    bnear = group_lanes(tiles[0:2]) * F32(LOG2_E)
    sel_add, win_add = _near_tables(bnear)
    bcmp = group_lanes(tiles[2:2 + nqb]) * F32(LOG2_E)
    t_cmp, t_win, t_sel = tiles[2 + nqb], tiles[3 + nqb], tiles[4 + nqb]

    def dense_in(x, seq_len, q_gain, kmask):
        x1 = _ffn(x, *w1)
        return (x1,) + tuple(_inproj(x1, seq_len, kmask, row(g_mix[0]), wqt, wkvt, wm, ws, wst, q_gain, kgt))

    n_sel_p = -(-t // SEL_BLOCK)
    assert n_sel_p + t // LANES <= LANES
    kmask_np = np.zeros((t, LANES), np.float32)
    kmask_np[np.arange(t), np.arange(t) // SEL_BLOCK] = -MASK_BIG
    kmask_np[np.arange(t), n_sel_p + np.arange(t) // LANES] = -MASK_BIG

    from_chan = lambda a: a.reshape((1, a.shape[0]) + kv_shape + (a.shape[2],)).transpose(0, 1, 5, 2, 3, 4)
    to_chan = lambda a: a.transpose(0, 2, 3, 4, 1).reshape(a.shape[0], KV_ROW, a.shape[1])

    (x1, q_t, cmp_rows, cmp_t, sel_t, win_t, ks, vs, kw, vw, qk_m, v_m, o_m, small, gate_t) = dense_in(
        x_prompt.reshape(nb * t, D_MODEL), t, qg * F32(LOG2_E),
        jnp.asarray(kmask_np, BF16))
    seq = lambda a: a.reshape(nb, t, a.shape[-1])
    n16 = t // CMP_STRIDE
    assert n16 == LANES
    kc, vc_t = _compress(cmp_rows, t, w_big, b_big, g_cmp)
    n_sel = -(-t // SEL_BLOCK)
    cov_t = jnp.asarray(np.pad(_cover_np(n16 - 1, n_sel), ((0, 1), (0, 0))).T, BF16)
    a_pad = _nsa_prompt(q_t, gate_t, kc, vc_t, ks, vs, kw, vw, sel_add, win_add, bcmp, cov_t, n16 - 1)
    chunk = MLSTM_CHUNK if t % MLSTM_CHUNK == 0 else Q_TILE
    m_out, c_p, n_p, m_p = _mlstm_prompt(seq(qk_m), seq(v_m), seq(o_m), seq(small), conv_w[0], conv_b2, bif, mn,
                                         chunk)
    y_prompt = _post(x1, a_pad.reshape(nb * t, -1), m_out.reshape(nb * t, -1), woa, wom, *w2)
    kv6 = lambda a, n: a.reshape((1, n, -1) + kv_shape)
    prompt_states = (from_chan(cmp_t), from_chan(sel_t), from_chan(win_t[:, :, t - wb:]), c_p[None], n_p[None],
                     m_p[None, :, 0, :N_HEADS_M], seq(qk_m)[None, :, t - (CONV_W - 1):])

    (x1s, q_st, cmp_halves, _, sel_st, win_st, _, _, _, _, qk_s, v_s, o_s, small_s, gate_st) = dense_in(
        x_sample.reshape(ns, D_MODEL), ns, qg, jnp.zeros((ns, LANES), BF16))
    cmp_s = jnp.concatenate([cmp_halves[0], cmp_halves[1]], axis=1)
    sel_s, win_s, gates_s = sel_st[0].T, win_st[0].T, gate_st[0].T
    q3 = q_st[0].T.reshape(ns, N_HEADS_A, LANES)
    n16s = past // CMP_STRIDE
    n_sel_s = -(-(past + 1) // SEL_BLOCK)
    n_sel_pad = -(-n_sel_s // LANES) * LANES
    cover_s = jnp.asarray(np.pad(_cover_np(n16s - 1, n_sel_s), ((0, 1), (0, n_sel_pad - n_sel_s))), BF16)
    bias_c = t_cmp[:, :n16s // LANES, :].reshape(N_HEADS_A, n16s)
    o_c, idx = _sample_cmp(page_table, to_chan(cache_cmp_kv[0]), w_big, b_big, g_cmp, q3, bias_c, cover_s, n_sel_s,
                           past)
    n_past_blocks = past // SEL_BLOCK
    two_lanes = lambda a: jnp.concatenate([a[:, :SEL_BLOCK], a[:, :SEL_BLOCK]], axis=-1)
    bsel = jnp.stack([two_lanes(t_sel[:, n_past_blocks - 2, :]), two_lanes(t_sel[:, n_past_blocks - 1, :])])
    bwin = t_win[:, :wb // LANES, :].reshape(N_HEADS_A, wb)
    b0 = jnp.broadcast_to(t_win[:, wb // LANES, 0:1], (N_HEADS_A, LANES))
    gate_b = jnp.broadcast_to(gates_s[:, :n_gate, None], (ns, n_gate, LANES))
    a_s, win_out = _sample_attn(idx[:, :, 0].reshape(-1), page_table, to_chan(cache_sel_kv[0]),
                                to_chan(cache_win_kv[0]), q3, sel_s[:, None, :], win_s[:, None, :], win_st[0],
                                o_c, gate_b, bsel, bwin, b0)
    m0 = jnp.pad(state_m[0], ((0, 0), (0, LANES - N_HEADS_M)))[:, None, :]
    m_s, c_s, n_s, m_new = _mlstm_step(qk_s[:, None, :], state_conv[0], v_s[:, None, :], o_s[:, None, :],
                                       small_s[:, None, :], state_C[0], state_n[0], m0, conv_w[0], conv_b2, bif, mn)
    a_dense = jnp.concatenate([a_s[:, h, (h // GROUP_R) * HEAD_DIM_A:(h // GROUP_R + 1) * HEAD_DIM_A]
                               for h in range(N_HEADS_A)], axis=1)
    y_sample = _post(x1s, a_dense, m_s.reshape(ns, -1), woa, wom, *w2)
    conv_new = jnp.concatenate([state_conv[0][:, 1:], qk_s[:, None, :]], axis=1)
    sample_states = (kv6(cmp_s, ns), kv6(sel_s, ns), from_chan(win_out), c_s[None], n_s[None],
                     m_new[None, :, 0, :N_HEADS_M], conv_new[None])

    return (y_prompt.reshape(nb, t, D_MODEL), y_sample.reshape(ns, 1, D_MODEL)) + prompt_states + sample_states
```

```python
import functools
import math

import jax
import jax.numpy as jnp
import numpy as np
from jax import lax
from jax.experimental import pallas as pl
from jax.experimental.pallas import tpu as pltpu

F32 = jnp.float32
BF16 = jnp.bfloat16

D_MODEL = 1024
PAGE_SIZE = 128
N_HEADS_A = 8
HEAD_DIM_A = 64
N_KV_A = 2
GROUP_R = N_HEADS_A // N_KV_A
D_A = N_HEADS_A * HEAD_DIM_A
KV_W = N_KV_A * HEAD_DIM_A
CMP_STRIDE = 16
CMP_LEN = 2 * CMP_STRIDE
SEL_BLOCK = 64
TOP_K_BLOCKS = 16
WINDOW = 512
N_HEADS_M = 4
HEAD_DIM_M = 128
D_M = N_HEADS_M * HEAD_DIM_M
CONV_W = 4
N_BUCKETS = 32
MAX_DISTANCE = 128
EPS = 1e-6

LANES = 128
SUBLANES = 8
TOKEN_TILE = 512
MLSTM_CHUNK = 256
STEP_SEQS = 8
Q_TILE = 128
Q_BLOCKS = 2
V_ROWS = HEAD_DIM_A + 16
FAR_TILES = 4
KV_ROW = 2 * KV_W
LOG2_E = 1.4426950408889634
NEG = -1e30
MASK_BIG = 2.0 ** 100
VMEM_LIMIT = 56 * 1024 * 1024

NT_DIMS = (((1,), (1,)), ((), ()))


def _dot(a, b):
    return jnp.dot(a, b, preferred_element_type=F32)


def _dot_nt(a, b):
    return lax.dot_general(a, b, NT_DIMS, preferred_element_type=F32)


def _split3(x):
    x1 = x.astype(BF16)
    r1 = x - x1.astype(F32)
    x2 = r1.astype(BF16)
    x3 = (r1 - x2.astype(F32)).astype(BF16)
    return x1, x2, x3


def _dot_exact_rhs(x, m):
    x1, x2, x3 = _split3(x)
    return _dot(x1, m) + _dot(x2, m) + _dot(x3, m)


def _dot_exact_lhs(m, x):
    x1, x2, x3 = _split3(x)
    return _dot(m, x1) + _dot(m, x2) + _dot(m, x3)


def _rms_rows(x, g):
    ms = jnp.mean(x * x, axis=-1, keepdims=True)
    return x * lax.rsqrt(ms + EPS) * g


def _rms_two_groups(k, g):
    sq = k * k
    lane = lax.broadcasted_iota(jnp.int32, sq.shape, 1)
    lo = lane < HEAD_DIM_A
    s0 = jnp.sum(jnp.where(lo, sq, 0.0), axis=-1, keepdims=True)
    s1 = jnp.sum(jnp.where(lo, 0.0, sq), axis=-1, keepdims=True)
    ms = jnp.where(lo, s0, s1) * (1.0 / HEAD_DIM_A)
    return k * lax.rsqrt(ms + EPS) * g


def _sigmoid(x):
    return jax.nn.sigmoid(x)


def _log_sigmoid(x):
    return jnp.minimum(x, 0.0) - jnp.log1p(jnp.exp(-jnp.abs(x)))


def _const_spec(shape):
    nd = len(shape)
    return pl.BlockSpec(shape, lambda *_: (0,) * nd, pipeline_mode=pl.Buffered(1))


def _params(*sem):
    return pltpu.CompilerParams(dimension_semantics=sem, vmem_limit_bytes=VMEM_LIMIT)


def _swiglu_residual(x, g_ref, wg_ref, wu_ref, wd_ref):
    xn = _rms_rows(x, g_ref[...]).astype(BF16)
    d_ff = wg_ref.shape[1]
    n_split = 2 if d_ff % (2 * LANES) == 0 else 1
    step = d_ff // n_split
    acc = jnp.zeros_like(x)
    for c in range(n_split):
        hg = _dot(xn, wg_ref[:, c * step:(c + 1) * step])
        hu = _dot(xn, wu_ref[:, c * step:(c + 1) * step])
        h = (hg * _sigmoid(hg)) * hu
        acc = acc + _dot(h.astype(BF16), wd_ref[c * step:(c + 1) * step, :])
    return x + 0.5 * acc


def _ffn_kernel(x_ref, g_ref, wg_ref, wu_ref, wd_ref, y_ref):
    y_ref[...] = _swiglu_residual(x_ref[...], g_ref, wg_ref, wu_ref, wd_ref)


def _post_kernel(x_ref, a_ref, m_ref, woa_ref, wom_ref, g_ref, wg_ref, wu_ref, wd_ref, y_ref):
    x = x_ref[...] + (_dot(a_ref[...], woa_ref[...]) + _dot(m_ref[...], wom_ref[...]))
    y_ref[...] = _swiglu_residual(x, g_ref, wg_ref, wu_ref, wd_ref)


def _token_tile(n):
    return TOKEN_TILE if n % TOKEN_TILE == 0 else n


def _ffn(x, g, wg, wu, wd):
    n = x.shape[0]
    tm = _token_tile(n)
    row = lambda w: pl.BlockSpec((tm, w), lambda i: (i, 0))
    return pl.pallas_call(
        _ffn_kernel,
        grid=(n // tm,),
        in_specs=[row(D_MODEL), _const_spec(g.shape), _const_spec(wg.shape), _const_spec(wu.shape),
                  _const_spec(wd.shape)],
        out_specs=row(D_MODEL),
        out_shape=jax.ShapeDtypeStruct((n, D_MODEL), F32),
        compiler_params=_params("parallel"),
        name="ffn1",
    )(x, g, wg, wu, wd)


def _post(x, a_pad, m_out, woa, wom, g, wg, wu, wd):
    n = x.shape[0]
    tm = _token_tile(n)
    row = lambda w: pl.BlockSpec((tm, w), lambda i: (i, 0))
    return pl.pallas_call(
        _post_kernel,
        grid=(n // tm,),
        in_specs=[row(D_MODEL), row(a_pad.shape[1]), row(m_out.shape[1]), _const_spec(woa.shape),
                  _const_spec(wom.shape), _const_spec(g.shape), _const_spec(wg.shape), _const_spec(wu.shape),
                  _const_spec(wd.shape)],
        out_specs=row(D_MODEL),
        out_shape=jax.ShapeDtypeStruct((n, D_MODEL), F32),
        compiler_params=_params("parallel"),
        name="post_mix_ffn2",
    )(x, a_pad, m_out, woa, wom, g, wg, wu, wd)


def _rms_two_groups_t(k, g_col):
    sq = k * k
    half = k.shape[0] // 2
    s0 = jnp.sum(sq[0:half], axis=0, keepdims=True)
    s1 = jnp.sum(sq[half:], axis=0, keepdims=True)
    ms = jnp.concatenate([jnp.broadcast_to(s0, (half, k.shape[1])), jnp.broadcast_to(s1, (half, k.shape[1]))],
                         axis=0) * (1.0 / HEAD_DIM_A)
    return k * lax.rsqrt(ms + EPS) * g_col


def _inproj_kernel(x_ref, kmask_ref, g_ref, wqt_ref, wkvt_ref, wm_ref, ws_ref, wst_ref, qg_ref, kgt_ref,
                   q_out, cmp_rows, cmpt_out, selt_out, wint_out, kst, vst, kwt, vwt, qk_out, v_out, o_out,
                   small_out, gate_out):
    xn = _rms_rows(x_ref[...], g_ref[...]).astype(BF16)
    qt = _dot_nt(wqt_ref[...], xn)
    zeros = jnp.zeros((HEAD_DIM_A, qt.shape[1]), BF16)
    for h in range(N_HEADS_A):
        qh = qt[h * HEAD_DIM_A:(h + 1) * HEAD_DIM_A, :]
        ms = jnp.sum(qh * qh, axis=0, keepdims=True) * (1.0 / HEAD_DIM_A)
        own = h * LANES + (h // GROUP_R) * HEAD_DIM_A
        other = h * LANES + (1 - h // GROUP_R) * HEAD_DIM_A
        q_out[0, own:own + HEAD_DIM_A, :] = (qh * lax.rsqrt(ms + EPS) * qg_ref[...]).astype(BF16)
        q_out[0, other:other + HEAD_DIM_A, :] = zeros
    kvt = _dot_nt(wkvt_ref[...], xn)
    cmpt_out[0] = kvt[0:KV_ROW]
    cmp_rows[0] = kvt[0:KV_W].T
    cmp_rows[1] = kvt[KV_W:KV_ROW].T
    ks = _rms_two_groups_t(kvt[2 * KV_W:3 * KV_W], kgt_ref[0])
    vs = kvt[3 * KV_W:4 * KV_W]
    kw = _rms_two_groups_t(kvt[4 * KV_W:5 * KV_W], kgt_ref[1])
    vw = kvt[5 * KV_W:6 * KV_W]
    selt_out[0, 0:KV_W] = ks
    selt_out[0, KV_W:KV_ROW] = vs
    wint_out[0, 0:KV_W] = kw
    wint_out[0, KV_W:KV_ROW] = vw
    ones = jnp.ones((V_ROWS - HEAD_DIM_A, LANES), F32)
    for j in range(kst.shape[1]):
        cols = slice(j * LANES, (j + 1) * LANES)
        kst[0, j] = jnp.concatenate([ks[:, cols].T.astype(BF16), kmask_ref[cols, :]], axis=1)
        kwt[0, j] = kw[:, cols].T.astype(BF16)
        for v, out in ((vs, vst), (vw, vwt)):
            for g in range(N_KV_A):
                out[0, j, g] = jnp.concatenate([v[g * HEAD_DIM_A:(g + 1) * HEAD_DIM_A, cols], ones],
                                               axis=0).astype(BF16)
    m = _dot(xn, wm_ref[...])
    qk_out[...] = m[:, 0:2 * D_M]
    v_out[...] = m[:, 2 * D_M:3 * D_M]
    o_out[...] = m[:, 3 * D_M:4 * D_M]
    small_out[...] = _dot(xn, ws_ref[...])
    gate_out[0] = _sigmoid(_dot_nt(wst_ref[...], xn))


def _inproj(x, seq_len, kmask, g, wqt, wkvt, wm, ws, wst, qg, kgt):
    n = x.shape[0]
    nb = n // seq_len
    tm = _token_tile(seq_len)
    tpb = seq_len // tm
    row = lambda w: pl.BlockSpec((tm, w), lambda i: (i, 0))
    rows_out = lambda w, dt: (row(w), jax.ShapeDtypeStruct((n, w), dt))
    chan_out = lambda c, dt: (pl.BlockSpec((1, c, tm), lambda i: (i // tpb, 0, i % tpb)),
                              jax.ShapeDtypeStruct((nb, c, seq_len), dt))
    k_tiles = (pl.BlockSpec((1, tm // LANES, LANES, KV_W), lambda i: (i // tpb, i % tpb, 0, 0)),
               jax.ShapeDtypeStruct((nb, seq_len // LANES, LANES, KV_W), BF16))
    ksel_tiles = (pl.BlockSpec((1, tm // LANES, LANES, 2 * KV_W), lambda i: (i // tpb, i % tpb, 0, 0)),
                  jax.ShapeDtypeStruct((nb, seq_len // LANES, LANES, 2 * KV_W), BF16))
    v_tiles = (pl.BlockSpec((1, tm // LANES, N_KV_A, V_ROWS, LANES), lambda i: (i // tpb, i % tpb, 0, 0, 0)),
               jax.ShapeDtypeStruct((nb, seq_len // LANES, N_KV_A, V_ROWS, LANES), BF16))
    halves_out = (pl.BlockSpec((2, tm, KV_W), lambda i: (0, i, 0)), jax.ShapeDtypeStruct((2, n, KV_W), F32))
    outs = [chan_out(N_HEADS_A * LANES, BF16), halves_out, chan_out(KV_ROW, F32), chan_out(KV_ROW, F32),
            chan_out(KV_ROW, F32), ksel_tiles, v_tiles, k_tiles, v_tiles,
            rows_out(2 * D_M, F32), rows_out(D_M, F32), rows_out(D_M, F32), rows_out(LANES, F32),
            chan_out(LANES, F32)]
    consts = (g, wqt, wkvt, wm, ws, wst, qg, kgt)
    return pl.pallas_call(
        _inproj_kernel,
        grid=(n // tm,),
        in_specs=[row(D_MODEL), pl.BlockSpec((tm, LANES), lambda i: (i % tpb, 0))]
        + [_const_spec(a.shape) for a in consts],
        out_specs=[o[0] for o in outs],
        out_shape=[o[1] for o in outs],
        compiler_params=_params("parallel"),
        name="in_proj",
    )(x, kmask, *consts)


def _t5_bucket_np(dist):
    n = np.maximum(dist, 0)
    exact = N_BUCKETS // 2
    nf = np.maximum(n, 1).astype(np.float32)
    ratio = np.log(nf / np.float32(exact)) / np.float32(math.log(MAX_DISTANCE / exact))
    large = exact + (ratio * np.float32(N_BUCKETS - exact)).astype(np.int32)
    return np.where(n < exact, n, np.minimum(large, N_BUCKETS - 1)).astype(np.int32)


def _bias_kernel(tbl_ref, bkt_ref, out_ref):
    bkt = bkt_ref[0]
    for h in range(N_HEADS_A):
        far = tbl_ref[N_BUCKETS - 1, h]
        acc = jnp.zeros(bkt.shape, F32)
        for b in range(N_BUCKETS - 1):
            acc = jnp.where(bkt == b, tbl_ref[b, h] - far, acc)
        out_ref[0, h] = acc


def _bias_tiles(rel_bias, buckets):
    n = buckets.shape[0]
    return pl.pallas_call(
        _bias_kernel,
        grid=(n,),
        in_specs=[pl.BlockSpec(memory_space=pltpu.SMEM),
                  pl.BlockSpec((1, LANES, LANES), lambda i: (i, 0, 0))],
        out_specs=pl.BlockSpec((1, N_HEADS_A, LANES, LANES), lambda i: (i, 0, 0, 0)),
        out_shape=jax.ShapeDtypeStruct((n, N_HEADS_A, LANES, LANES), F32),
        compiler_params=_params("parallel"),
        name="bias_tiles",
    )(rel_bias, buckets)


def _compress_half(rows_ref, nc, w_ref, half, bias):
    fs = None
    for t in range(CMP_STRIDE):
        part = _dot(rows_ref[pl.ds(t, nc, stride=CMP_STRIDE), :].astype(BF16), w_ref[half, t])
        fs = part if fs is None else fs + part
    return fs[:, 0:KV_W] + pltpu.roll(fs[:, KV_W:KV_ROW], nc - 1, 0) + bias


def _compress_rows(k_rows_ref, v_rows_ref, nc, w_ref, b_ref, g_ref):
    kc = _rms_two_groups(_compress_half(k_rows_ref, nc, w_ref, 0, b_ref[:, 0:KV_W]), g_ref[...])
    vc = _compress_half(v_rows_ref, nc, w_ref, 1, b_ref[:, KV_W:KV_ROW])
    complete = lax.broadcasted_iota(jnp.int32, kc.shape, 0) < nc - 1
    return jnp.where(complete, kc, 0.0), jnp.where(complete, vc, 0.0)


def _compress_kernel(x_ref, w_ref, b_ref, g_ref, kc_ref, vc_ref):
    nc = kc_ref.shape[1]
    kc, vc = _compress_rows(x_ref.at[0], x_ref.at[1], nc, w_ref, b_ref, g_ref)
    kc_ref[0] = kc.astype(BF16)
    vc_ref[0] = vc.T.astype(BF16)


def _compress(cmp_rows, t, w_big, b_big, g_k):
    nb = cmp_rows.shape[1] // t
    nc = t // CMP_STRIDE
    return pl.pallas_call(
        _compress_kernel,
        grid=(nb,),
        in_specs=[pl.BlockSpec((2, t, KV_W), lambda b: (0, b, 0)), _const_spec(w_big.shape),
                  _const_spec(b_big.shape), _const_spec(g_k.shape)],
        out_specs=[pl.BlockSpec((1, nc, KV_W), lambda b: (b, 0, 0)),
                   pl.BlockSpec((1, KV_W, nc), lambda b: (b, 0, 0))],
        out_shape=[jax.ShapeDtypeStruct((nb, nc, KV_W), BF16), jax.ShapeDtypeStruct((nb, KV_W, nc), BF16)],
        compiler_params=_params("parallel"),
        name="compress_prompt",
    )(cmp_rows, w_big, b_big, g_k)


def _tile_heads(x):
    return jnp.concatenate([x] * GROUP_R, axis=1)


def _softmax_part(m, s):
    m_new = jnp.maximum(m, jnp.max(s, axis=0, keepdims=True))
    return m_new, jnp.exp2(m - m_new), jnp.exp2(s - m_new).astype(BF16)


def _accumulate(acc, soft, vt):
    m_new, alpha, p = soft
    return m_new, alpha * acc + _dot(vt, p)


def _key_rows(ref, t, n):
    return ref[0, pl.ds(t, n)].reshape(n * LANES, ref.shape[-1])


def _value_cols(ref, t, n, g):
    return jnp.concatenate([ref[0, t + k, g] for k in range(n)], axis=1)


def _rank_rows(score):
    n_blk = score.shape[0]
    blk = lax.broadcasted_iota(jnp.int32, score.shape, 0)
    rank = jnp.zeros(score.shape, F32)
    for b in range(n_blk):
        row = score[b:b + 1, :]
        rank = rank + jnp.where(blk > b, jnp.where(row >= score, 1.0, 0.0), jnp.where(row > score, 1.0, 0.0))
    return rank


def _nsa_prompt_kernel(q_ref, gate_ref, kc_ref, vct_ref, ks_ref, vs_ref, kw_ref, vw_ref, sel_add_ref, win_add_ref,
                       bcmp_ref, covt_ref, a_ref, *, n_cmp):
    ii = pl.program_id(1)
    n_tiles = ks_ref.shape[1]
    n_sel = covt_ref.shape[0]
    wide = GROUP_R * Q_TILE
    heads_of = lambda g: range(g * GROUP_R, (g + 1) * GROUP_R)
    chains =[(sb, g) for sb in range(Q_BLOCKS) for g in range(N_KV_A)]
    blk_i = [Q_BLOCKS * ii + sb for sb in range(Q_BLOCKS)]
    cols = lambda sb: slice(sb * Q_TILE, (sb + 1) * Q_TILE)
    qg_t = {(sb, g): jnp.concatenate([q_ref[0, h * LANES:(h + 1) * LANES, cols(sb)] for h in heads_of(g)], axis=1)
            for sb, g in chains}
    init = (jnp.full((1, wide), NEG, F32), jnp.zeros((V_ROWS, wide), F32))
    normalised = lambda acc: acc[0:HEAD_DIM_A, :] / acc[HEAD_DIM_A:HEAD_DIM_A + 1, :]

    n_win = WINDOW // LANES + 1
    t0 = [jnp.maximum(i - (n_win - 1), 0) for i in blk_i]
    s_win = {}
    for sb, i in enumerate(blk_i):
        win_off = pl.multiple_of(jnp.maximum(n_win - 1 - i, 0) * LANES, LANES)
        k_win = _key_rows(kw_ref, t0[sb], n_win)
        for g in range(N_KV_A):
            s_win[sb, g] = _dot(k_win, qg_t[sb, g]) + win_add_ref[g, pl.ds(win_off, n_win * LANES), :]

    ncp = kc_ref.shape[1]
    j_sub = lax.broadcasted_iota(jnp.int32, (ncp, wide), 0)
    q_in_tile = lax.broadcasted_iota(jnp.int32, (ncp, wide), 1) & (Q_TILE - 1)
    s_cmp, any_usable = {}, []
    for sb, i in enumerate(blk_i):
        q_lane = i * Q_TILE + q_in_tile
        usable_add = jnp.where((j_sub * CMP_STRIDE + (CMP_LEN - 1) <= q_lane) & (j_sub < n_cmp), 0.0, NEG)
        any_usable.append(jnp.where(q_lane[0:1, :] >= CMP_LEN - 1, 1.0, 0.0))
        for g in range(N_KV_A):
            s_cmp[sb, g] = _dot(kc_ref[0], qg_t[sb, g]) + bcmp_ref[sb, g] + usable_add
    p_c = {}
    for sb, g in chains:
        e = jnp.exp2(s_cmp[sb, g] - jnp.max(s_cmp[sb, g], axis=0, keepdims=True)) * any_usable[sb]
        p_c[sb, g] = e / jnp.maximum(jnp.sum(e, axis=0, keepdims=True), 1e-30)
    oc_t = {c: _dot(vct_ref[0], p_c[c].astype(BF16)) for c in chains}

    blk = lax.broadcasted_iota(jnp.int32, (n_sel, Q_TILE), 0)
    q_in_blk = lax.broadcasted_iota(jnp.int32, (n_sel, Q_TILE), 1)
    tile_row = lax.broadcasted_iota(jnp.int32, (n_tiles, wide), 0)
    pad_rows = jnp.zeros((LANES - n_sel - n_tiles, wide), BF16)
    q_far, q_near = {}, {}
    for sb, g in chains:
        qpos = blk_i[sb] * Q_TILE + q_in_blk
        cur = qpos // SEL_BLOCK
        valid = blk * SEL_BLOCK <= qpos
        forced = valid & ((blk == 0) | (blk == cur) | (blk == cur - 1))
        p_sum = p_c[sb, g][:, 0:Q_TILE]
        for r in range(1, GROUP_R):
            p_sum = p_sum + p_c[sb, g][:, r * Q_TILE:(r + 1) * Q_TILE]
        imp_t = _dot_exact_lhs(covt_ref[...], p_sum)
        score = jnp.where(forced, -NEG, jnp.where(valid, imp_t, NEG))
        dropped = _tile_heads(jnp.where(_rank_rows(score) < float(min(TOP_K_BLOCKS, n_sel)), 0.0, 1.0).astype(BF16))
        past_far = jnp.where(tile_row > blk_i[sb] - 2, 1.0, 0.0).astype(BF16)
        q_far[sb, g] = jnp.concatenate([qg_t[sb, g], dropped, past_far, pad_rows], axis=0)
        q_near[sb, g] = jnp.concatenate([qg_t[sb, g], dropped, jnp.zeros_like(past_far), pad_rows], axis=0)

    soft_win = {c: _softmax_part(init[0], s_win[c]) for c in chains}
    ow_acc = {(sb, g): _dot(_value_cols(vw_ref, t0[sb], n_win, g), soft_win[sb, g][2]) for sb, g in chains}

    def far_body(it, carry):
        t = FAR_TILES * it
        k_far = _key_rows(ks_ref, t, FAR_TILES)
        s = {c: _dot(k_far, q_far[c]) for c in chains}
        soft = [_softmax_part(carry[n][0], s[c]) for n, c in enumerate(chains)]
        return tuple(_accumulate(carry[n][1], soft[n], _value_cols(vs_ref, t, FAR_TILES, c[1]))
                     for n, c in enumerate(chains))

    n_far = blk_i[-1] - 1
    carry = lax.fori_loop(0, (n_far + FAR_TILES - 1) // FAR_TILES, far_body, (init,) * len(chains))

    t1 = [jnp.maximum(i - 1, 0) for i in blk_i]
    s_near = {}
    for sb, i in enumerate(blk_i):
        k_near = _key_rows(ks_ref, t1[sb], 2)
        near_off = pl.multiple_of(jnp.where(i == 0, LANES, 0), LANES)
        for g in range(N_KV_A):
            s_near[sb, g] = _dot(k_near, q_near[sb, g]) + sel_add_ref[g, pl.ds(near_off, 2 * LANES), :]
    soft_near = [_softmax_part(carry[n][0], s_near[c]) for n, c in enumerate(chains)]
    for n, (sb, g) in enumerate(chains):
        _, acc = _accumulate(carry[n][1], soft_near[n], _value_cols(vs_ref, t1[sb], 2, g))
        heads = heads_of(g)
        gate_row = lambda br: jnp.concatenate(
            [gate_ref[0, br * N_HEADS_A + h:br * N_HEADS_A + h + 1, cols(sb)] for h in heads], axis=1)
        out_t = (gate_row(0) * oc_t[sb, g][g * HEAD_DIM_A:(g + 1) * HEAD_DIM_A, :] + gate_row(1) * normalised(acc)
                 + gate_row(2) * normalised(ow_acc[sb, g]))
        for pair in range(GROUP_R // 2):
            two = jnp.concatenate([out_t[:, (2 * pair + k) * Q_TILE:(2 * pair + k + 1) * Q_TILE] for k in range(2)],
                                  axis=0)
            col = (g * GROUP_R // 2 + pair) * LANES
            a_ref[0, cols(sb), col:col + LANES] = two.T.astype(BF16)


def _nsa_prompt(q_t, gate_t, kc, vc_t, ks, vs, kw, vw, sel_add, win_add, bcmp, cov_t, n_cmp):
    nb, _, t = q_t.shape
    nqb = t // Q_TILE
    assert nqb > WINDOW // LANES and nqb % Q_BLOCKS == 0
    step = Q_BLOCKS * Q_TILE
    seq = lambda a: pl.BlockSpec((1,) + a.shape[1:], lambda b, i: (b,) + (0,) * (a.ndim - 1))
    return pl.pallas_call(
        functools.partial(_nsa_prompt_kernel, n_cmp=n_cmp),
        grid=(nb, nqb // Q_BLOCKS),
        in_specs=[pl.BlockSpec((1, N_HEADS_A * LANES, step), lambda b, i: (b, 0, i)),
                  pl.BlockSpec((1, LANES, step), lambda b, i: (b, 0, i)),
                  seq(kc), seq(vc_t), seq(ks), seq(vs), seq(kw), seq(vw),
                  _const_spec(sel_add.shape), _const_spec(win_add.shape),
                  pl.BlockSpec((Q_BLOCKS,) + bcmp.shape[1:], lambda b, i: (i, 0, 0, 0)),
                  _const_spec(cov_t.shape)],
        out_specs=pl.BlockSpec((1, step, D_A), lambda b, i: (b, i, 0)),
        out_shape=jax.ShapeDtypeStruct((nb, t, D_A), BF16),
        compiler_params=_params("parallel", "arbitrary"),
        name="nsa_prompt",
    )(q_t, gate_t, kc, vc_t, ks, vs, kw, vw, sel_add, win_add, bcmp, cov_t)


def _head_norm_gate(h_all, mn_ref, o):
    outs = []
    for h in range(N_HEADS_M):
        hs = slice(h * HEAD_DIM_M, (h + 1) * HEAD_DIM_M)
        outs.append(_rms_rows(h_all[h], mn_ref[:, hs]))
    return _sigmoid(o) * jnp.concatenate(outs, axis=-1)


def _mlstm_prompt_kernel(qk_ref, v_ref, o_ref, small_ref, cw_ref, cb_ref, bif_ref, mn_ref, tri_ref, triu_ref,
                         y_ref, c_ref, n_ref, m_ref, xbuf):
    chunk = qk_ref.shape[1]
    pad = SUBLANES

    @pl.when(pl.program_id(1) == 0)
    def _():
        xbuf[...] = jnp.zeros_like(xbuf)
        c_ref[...] = jnp.zeros_like(c_ref)
        n_ref[...] = jnp.zeros_like(n_ref)
        m_ref[...] = jnp.zeros_like(m_ref)

    x = qk_ref[0]
    tail = xbuf[...]
    head_row = lax.broadcasted_iota(jnp.int32, tail.shape, 0)

    def delayed(s):
        rolled = pltpu.roll(x, s, 0)
        head = jnp.where(head_row < s, pltpu.roll(tail, s, 0), rolled[0:pad, :])
        return jnp.concatenate([head, rolled[pad:, :]], axis=0)

    y = delayed(3) * cw_ref[0:1, :]
    y = y + delayed(2) * cw_ref[1:2, :]
    y = y + delayed(1) * cw_ref[2:3, :]
    y = y + x * cw_ref[3:4, :] + cb_ref[...]
    qkc = y * _sigmoid(y)
    xbuf[...] = x[chunk - pad:chunk, :]

    gi = small_ref[0] + bif_ref[...]
    n_gate = 3 * N_HEADS_A
    gate_rows = gi.T[n_gate:n_gate + 2 * N_HEADS_M, :]
    b_col = _dot_exact_lhs(tri_ref[...], _log_sigmoid(gi))
    b_row = _dot_exact_rhs(_log_sigmoid(gate_rows), triu_ref[...])
    t_col = lax.broadcasted_iota(jnp.int32, (chunk, chunk), 0)
    s_row = lax.broadcasted_iota(jnp.int32, (chunk, chunk), 1)
    causal = s_row <= t_col
    m_all = m_ref[0]
    lane_row = lax.broadcasted_iota(jnp.int32, m_all.shape, 1)

    h_all = []
    for h in range(N_HEADS_M):
        hs = slice(h * HEAD_DIM_M, (h + 1) * HEAD_DIM_M)
        ks = slice(D_M + h * HEAD_DIM_M, D_M + (h + 1) * HEAD_DIM_M)
        q = qkc[:, hs]
        k = qkc[:, ks] * F32(HEAD_DIM_M ** -0.5)
        v = v_ref[0, :, hs]
        qb, kb = q.astype(BF16), k.astype(BF16)
        ci, fi = 3 * N_HEADS_A + h, 3 * N_HEADS_A + N_HEADS_M + h
        bt = b_col[:, fi:fi + 1]
        bs = b_row[N_HEADS_M + h:N_HEADS_M + h + 1, :]
        ig_row = gate_rows[h:h + 1, :]
        ig_col = gi[:, ci:ci + 1]
        m_prev = m_all[:, h:h + 1]
        c_prev = c_ref[0, h]
        n_prev = n_ref[0, h:h + 1, :]

        dlog = jnp.where(causal, bt - bs + ig_row, NEG)
        inter = bt + m_prev
        m_t = jnp.maximum(inter, jnp.max(dlog, axis=-1, keepdims=True))
        s_qk = _dot_nt(qb, kb) * jnp.exp(dlog - m_t)
        dec = jnp.exp(inter - m_t)
        num = _dot(s_qk.astype(BF16), v.astype(BF16)) + dec * _dot_nt(qb, c_prev.astype(BF16))
        den = jnp.sum(s_qk, axis=-1, keepdims=True) + dec * jnp.sum(q * n_prev, axis=-1, keepdims=True)
        h_all.append(num / jnp.maximum(jnp.abs(den), jnp.exp(-m_t)))

        m_new = m_t[chunk - 1:chunk, :]
        b_end = bt[chunk - 1:chunk, :]
        w_end = jnp.exp(b_end - bt + ig_col - m_new)
        dec_end = jnp.exp(b_end + m_prev - m_new)
        wv_t = (w_end * v).T.astype(BF16)
        c_ref[0, h] = dec_end * c_prev + _dot(wv_t, kb)
        n_ref[0, h:h + 1, :] = dec_end * n_prev + jnp.sum(w_end * k, axis=0, keepdims=True)
        m_all = jnp.where(lane_row == h, m_new, m_all)

    m_ref[0] = m_all
    y_ref[0] = _head_norm_gate(h_all, mn_ref, o_ref[0]).astype(BF16)


def _mlstm_prompt(qk, v, o, small, conv_w, conv_b, bif, mn_g, chunk):
    nb, t, _ = qk.shape
    tri = jnp.asarray(np.tril(np.ones((chunk, chunk), np.float32)), BF16)
    triu = jnp.asarray(np.triu(np.ones((chunk, chunk), np.float32)), BF16)
    tok = lambda w: pl.BlockSpec((1, chunk, w), lambda b, c: (b, c, 0))
    return pl.pallas_call(
        _mlstm_prompt_kernel,
        grid=(nb, t // chunk),
        in_specs=[tok(2 * D_M), tok(D_M), tok(D_M), tok(LANES)]
        + [_const_spec(a.shape) for a in (conv_w, conv_b, bif, mn_g, tri, triu)],
        out_specs=[tok(D_M),
                   pl.BlockSpec((1, N_HEADS_M, HEAD_DIM_M, HEAD_DIM_M), lambda b, c: (b, 0, 0, 0)),
                   pl.BlockSpec((1, N_HEADS_M, HEAD_DIM_M), lambda b, c: (b, 0, 0)),
                   pl.BlockSpec((1, 1, LANES), lambda b, c: (b, 0, 0))],
        out_shape=[jax.ShapeDtypeStruct((nb, t, D_M), BF16),
                   jax.ShapeDtypeStruct((nb, N_HEADS_M, HEAD_DIM_M, HEAD_DIM_M), F32),
                   jax.ShapeDtypeStruct((nb, N_HEADS_M, HEAD_DIM_M), F32),
                   jax.ShapeDtypeStruct((nb, 1, LANES), F32)],
        scratch_shapes=[pltpu.VMEM((SUBLANES, 2 * D_M), F32)],
        compiler_params=_params("parallel", "arbitrary"),
        name="mlstm_prompt",
    )(qk, v, o, small, conv_w, conv_b, bif, mn_g, tri, triu)


def _softmax_rows(s):
    e = jnp.exp(s - jnp.max(s, axis=-1, keepdims=True))
    return e, jnp.sum(e, axis=-1, keepdims=True)


def _sample_cmp_kernel(pt_ref, *refs, n_pages, n_sel, qpos):
    pages = refs[:n_pages]
    w_ref, b_ref, g_ref, q_ref, bias_ref, cov_ref, oc_ref, idx_ref, xk_ref, xv_ref = refs[n_pages:]
    for half, rows_ref in ((0, xk_ref), (1, xv_ref)):
        for u in range(n_pages):
            rows_ref[u * PAGE_SIZE:(u + 1) * PAGE_SIZE, :] = pages[u][0, half * KV_W:(half + 1) * KV_W, :].T
    nc = n_pages * PAGE_SIZE // CMP_STRIDE
    kc, vc = _compress_rows(xk_ref, xv_ref, nc, w_ref, b_ref, g_ref)
    q = q_ref[0]
    j_row = lax.broadcasted_iota(jnp.int32, (N_HEADS_A, nc), 1)
    s = _dot_nt(q, kc.astype(BF16)) + bias_ref[...] + jnp.where(j_row < nc - 1, 0.0, NEG)
    e, l = _softmax_rows(s)
    p = e / jnp.maximum(l, 1e-30)
    oc_ref[0] = _dot(p.astype(BF16), vc.astype(BF16))

    nsp = cov_ref.shape[1]
    blk_row = lax.broadcasted_iota(jnp.int32, (1, nsp), 1)
    sub = lax.broadcasted_iota(jnp.int32, (nsp, nsp), 0)
    lan = lax.broadcasted_iota(jnp.int32, (nsp, nsp), 1)
    cur = qpos // SEL_BLOCK
    valid = (blk_row * SEL_BLOCK <= qpos) & (blk_row < n_sel)
    forced = valid & ((blk_row == 0) | (blk_row == cur) | (blk_row == cur - 1))
    k_col = lax.broadcasted_iota(jnp.int32, (TOP_K_BLOCKS, nsp), 0).astype(F32)
    blk_f = lax.broadcasted_iota(jnp.int32, (TOP_K_BLOCKS, nsp), 1).astype(F32)
    for g in range(N_KV_A):
        p_sum = jnp.sum(p[g * GROUP_R:(g + 1) * GROUP_R, :], axis=0, keepdims=True)
        imp = _dot_exact_rhs(jnp.broadcast_to(p_sum, (SUBLANES, nc)), cov_ref[...])[0:1, :]
        score = jnp.where(forced, -NEG, jnp.where(valid, imp, NEG))
        score_col = jnp.sum(jnp.where(sub == lan, score, 0.0), axis=-1, keepdims=True)
        ahead = jnp.where(sub < lan, jnp.where(score_col >= score, 1.0, 0.0), jnp.where(score_col > score, 1.0, 0.0))
        rank = jnp.sum(ahead, axis=0, keepdims=True)
        chosen = jnp.sum(jnp.where(rank == k_col, blk_f, 0.0), axis=-1, keepdims=True)
        idx_ref[0, g * TOP_K_BLOCKS:(g + 1) * TOP_K_BLOCKS, :] = jnp.broadcast_to(
            chosen, (TOP_K_BLOCKS, LANES)).astype(jnp.int32)


def _sample_cmp(page_table, pool, w_big, b_big, g_k, q3, bias_c, cover, n_sel, qpos):
    ns, n_pages = page_table.shape
    page_spec = lambda u: pl.BlockSpec((1, KV_ROW, PAGE_SIZE), lambda s, pt: (pt[s, u], 0, 0))
    const = lambda a: pl.BlockSpec(a.shape, lambda s, pt: (0,) * a.ndim, pipeline_mode=pl.Buffered(1))
    grid_spec = pltpu.PrefetchScalarGridSpec(
        num_scalar_prefetch=1,
        grid=(ns,),
        in_specs=[page_spec(u) for u in range(n_pages)]
        + [const(w_big), const(b_big), const(g_k),
           pl.BlockSpec((1, N_HEADS_A, LANES), lambda s, pt: (s, 0, 0)), const(bias_c), const(cover)],
        out_specs=[pl.BlockSpec((1, N_HEADS_A, LANES), lambda s, pt: (s, 0, 0)),
                   pl.BlockSpec((1, N_KV_A * TOP_K_BLOCKS, LANES), lambda s, pt: (s, 0, 0))],
        scratch_shapes=[pltpu.VMEM((n_pages * PAGE_SIZE, KV_W), F32)] * 2,
    )
    return pl.pallas_call(
        functools.partial(_sample_cmp_kernel, n_pages=n_pages, n_sel=n_sel, qpos=qpos),
        grid_spec=grid_spec,
        out_shape=[jax.ShapeDtypeStruct((ns, N_HEADS_A, LANES), F32),
                   jax.ShapeDtypeStruct((ns, N_KV_A * TOP_K_BLOCKS, LANES), jnp.int32)],
        compiler_params=_params("arbitrary"),
        name="sample_cmp_topk",
    )(page_table, *([pool] * n_pages), w_big, b_big, g_k, q3, bias_c, cover)


def _sample_attn_kernel(idx_ref, pt_ref, *refs, n_past_blocks):
    n_slots = N_KV_A * TOP_K_BLOCKS
    blocks = refs[:n_slots]
    (win_ref, q_ref, selnew_ref, winnew_ref, wint_ref, oc_ref, gate_ref, bsel_ref, bwin_ref, b0_ref,
     a_ref, winout_ref) = refs[n_slots:]
    s_id = pl.program_id(0)
    q = q_ref[0]
    qf = q.astype(F32)
    lane = lax.broadcasted_iota(jnp.int32, (N_HEADS_A, LANES), 1)
    row = lax.broadcasted_iota(jnp.int32, (N_HEADS_A, LANES), 0)
    lo = lane < SEL_BLOCK

    def with_new_key(s, vt_mat, new_row):
        s_new = jnp.sum(qf * new_row[:, 0:KV_W], axis=-1, keepdims=True) + b0_ref[:, 0:1]
        m = jnp.maximum(jnp.max(s, axis=-1, keepdims=True), s_new)
        p = jnp.exp(s - m)
        p_new = jnp.exp(s_new - m)
        l = jnp.sum(p, axis=-1, keepdims=True) + p_new
        return (_dot_nt(p.astype(BF16), vt_mat) + p_new * new_row[:, KV_W:KV_ROW]) / l

    sel_new = selnew_ref[0]
    o_s = []
    for g in range(N_KV_A):
        slots = range(g * TOP_K_BLOCKS, (g + 1) * TOP_K_BLOCKS)
        kt_cat = jnp.concatenate([blocks[k][0, 0:KV_W, :] for k in slots], axis=1).astype(BF16)
        vt_cat = jnp.concatenate([blocks[k][0, KV_W:KV_ROW, :] for k in slots], axis=1).astype(BF16)
        adds = []
        for k in slots:
            b = idx_ref[s_id * n_slots + k]
            add = jnp.where(b == n_past_blocks - 2, bsel_ref[0], jnp.where(b == n_past_blocks - 1, bsel_ref[1], 0.0))
            other_half = jnp.where(b % 2 == 0, jnp.where(lo, 0.0, NEG), jnp.where(lo, NEG, 0.0))
            adds.append(add + other_half + jnp.where(b >= n_past_blocks, NEG, 0.0))
        o_s.append(with_new_key(_dot(q, kt_cat) + jnp.concatenate(adds, axis=-1), vt_cat, sel_new))
    o_sel = jnp.where(row < GROUP_R, o_s[0], o_s[1])

    win = win_ref[0]
    wb = win.shape[1]
    w_lane = lax.broadcasted_iota(jnp.int32, (N_HEADS_A, wb), 1)
    s_w = _dot(q, win[0:KV_W, :].astype(BF16)) + bwin_ref[...] + jnp.where(w_lane == 0, NEG, 0.0)
    o_win = with_new_key(s_w, win[KV_W:KV_ROW, :].astype(BF16), winnew_ref[0])

    na = N_HEADS_A
    out = gate_ref[0, 0:na] * oc_ref[0] + gate_ref[0, na:2 * na] * o_sel + gate_ref[0, 2 * na:3 * na] * o_win
    a_ref[0] = jnp.where(lo == (row < GROUP_R), out, 0.0).astype(BF16)

    wint = wint_ref[...]
    seq_lane = lax.broadcasted_iota(jnp.int32, wint.shape, 1)
    new_col = jnp.sum(jnp.where(seq_lane == s_id, wint, 0.0), axis=-1, keepdims=True)
    buf_lane = lax.broadcasted_iota(jnp.int32, win.shape, 1)
    winout_ref[0] = jnp.where(buf_lane == wb - 1, new_col, pltpu.roll(win, wb - 1, 1))


def _sample_attn(idx, page_table, pool, win_t, q3, sel_new, win_new, win_new_t, o_c, gate_b, bsel, bwin, b0):
    ns, n_pages = page_table.shape
    n_slots = N_KV_A * TOP_K_BLOCKS
    per_page = PAGE_SIZE // SEL_BLOCK
    n_past_blocks = n_pages * per_page
    wb = win_t.shape[2]

    def slot_spec(k):
        def index(s, idx_ref, pt_ref):
            b = jnp.minimum(idx_ref[s * n_slots + k], n_past_blocks - 1)
            return (pt_ref[s, b // per_page], 0, 0)
        return pl.BlockSpec((1, KV_ROW, PAGE_SIZE), index)

    per_seq = lambda a: pl.BlockSpec((1,) + a.shape[1:], lambda s, *_: (s,) + (0,) * (a.ndim - 1))
    const = lambda a: pl.BlockSpec(a.shape, lambda s, *_: (0,) * a.ndim, pipeline_mode=pl.Buffered(1))
    grid_spec = pltpu.PrefetchScalarGridSpec(
        num_scalar_prefetch=2,
        grid=(ns,),
        in_specs=[slot_spec(k) for k in range(n_slots)]
        + [per_seq(win_t), per_seq(q3), per_seq(sel_new), per_seq(win_new), const(win_new_t), per_seq(o_c),
           per_seq(gate_b), const(bsel), const(bwin), const(b0)],
        out_specs=[pl.BlockSpec((1, N_HEADS_A, LANES), lambda s, *_: (s, 0, 0)),
                   pl.BlockSpec((1, KV_ROW, wb), lambda s, *_: (s, 0, 0))],
    )
    return pl.pallas_call(
        functools.partial(_sample_attn_kernel, n_past_blocks=n_past_blocks),
        grid_spec=grid_spec,
        out_shape=[jax.ShapeDtypeStruct((ns, N_HEADS_A, LANES), BF16),
                   jax.ShapeDtypeStruct((ns, KV_ROW, wb), F32)],
        compiler_params=_params("arbitrary"),
        name="sample_sel_win",
    )(idx, page_table, *([pool] * n_slots), win_t, q3, sel_new, win_new, win_new_t, o_c, gate_b, bsel, bwin, b0)


def _mlstm_step_kernel(qk_ref, conv_ref, v_ref, o_ref, small_ref, c_ref, n_ref, m_ref, cw_ref, cb_ref, bif_ref,
                       mn_ref, y_ref, c_out, n_out, m_out):
    seqs = qk_ref.shape[0]
    sub = lax.broadcasted_iota(jnp.int32, (HEAD_DIM_M, HEAD_DIM_M), 0)
    lan = lax.broadcasted_iota(jnp.int32, (HEAD_DIM_M, HEAD_DIM_M), 1)
    eye = sub == lan
    lane_row = lax.broadcasted_iota(jnp.int32, (1, LANES), 1)
    for s in range(seqs):
        hist = conv_ref[s]
        y = hist[0:1, :] * cw_ref[0:1, :]
        y = y + hist[1:2, :] * cw_ref[1:2, :]
        y = y + hist[2:3, :] * cw_ref[2:3, :]
        y = y + qk_ref[s] * cw_ref[3:4, :] + cb_ref[...]
        qkc = y * _sigmoid(y)
        gi = small_ref[s] + bif_ref[...]
        lf_all = _log_sigmoid(gi)
        m_all = m_ref[s]
        m_new_all = m_all
        h_all = []
        for h in range(N_HEADS_M):
            hs = slice(h * HEAD_DIM_M, (h + 1) * HEAD_DIM_M)
            ks = slice(D_M + h * HEAD_DIM_M, D_M + (h + 1) * HEAD_DIM_M)
            q = qkc[:, hs]
            k = qkc[:, ks] * F32(HEAD_DIM_M ** -0.5)
            v = v_ref[s][:, hs]
            ci, fi = 3 * N_HEADS_A + h, 3 * N_HEADS_A + N_HEADS_M + h
            ig = gi[:, ci:ci + 1]
            lf = lf_all[:, fi:fi + 1]
            m_prev = m_all[:, h:h + 1]
            c_prev = c_ref[s, h]
            n_prev = n_ref[s, h:h + 1, :]
            inter = lf + m_prev
            m_t = jnp.maximum(inter, ig)
            w_in = jnp.exp(ig - m_t)
            dec = jnp.exp(inter - m_t)
            s_qk = jnp.sum(q * k, axis=-1, keepdims=True) * w_in
            cq = _dot_nt(jnp.broadcast_to(q, (SUBLANES, HEAD_DIM_M)).astype(BF16), c_prev.astype(BF16))[0:1, :]
            num = s_qk * v + dec * cq
            den = s_qk + dec * jnp.sum(n_prev * q, axis=-1, keepdims=True)
            h_all.append(num / jnp.maximum(jnp.abs(den), jnp.exp(-m_t)))
            v_col = jnp.sum(jnp.where(eye, v, 0.0), axis=-1, keepdims=True)
            c_out[s, h] = dec * c_prev + (w_in * v_col) * k
            n_out[s, h:h + 1, :] = dec * n_prev + w_in * k
            m_new_all = jnp.where(lane_row == h, m_t, m_new_all)
        m_out[s] = m_new_all
        y_ref[s] = _head_norm_gate(h_all, mn_ref, o_ref[s]).astype(BF16)


def _mlstm_step(qk, conv0, v, o, small, c0, n0, m0, conv_w, conv_b, bif, mn_g):
    ns = qk.shape[0]
    sb = STEP_SEQS if ns % STEP_SEQS == 0 else 1
    per = lambda a: pl.BlockSpec((sb,) + a.shape[1:], lambda i: (i,) + (0,) * (a.ndim - 1))
    ins = (qk, conv0, v, o, small, c0, n0, m0)
    outs = [jax.ShapeDtypeStruct((ns, 1, D_M), BF16), jax.ShapeDtypeStruct(c0.shape, F32),
            jax.ShapeDtypeStruct(n0.shape, F32), jax.ShapeDtypeStruct(m0.shape, F32)]
    return pl.pallas_call(
        _mlstm_step_kernel,
        grid=(ns // sb,),
        in_specs=[per(a) for a in ins] + [_const_spec(a.shape) for a in (conv_w, conv_b, bif, mn_g)],
        out_specs=[per(a) for a in outs],
        out_shape=outs,
        compiler_params=_params("parallel"),
        name="mlstm_step",
    )(*ins, conv_w, conv_b, bif, mn_g)


def _cover_np(n_cmp, n_sel):
    cs = np.arange(n_cmp)[:, None] * CMP_STRIDE
    bs = np.arange(n_sel)[None, :] * SEL_BLOCK
    shared = np.clip(np.minimum(cs + CMP_LEN, bs + SEL_BLOCK) - np.maximum(cs, bs), 0, None)
    return (shared / CMP_LEN).astype(np.float32)


def _prep_in_proj(w_in):
    d = w_in.shape[0]
    o_kv = D_A
    o_gate = o_kv + 6 * KV_W
    o_m = o_gate + 3 * N_HEADS_A
    o_if = o_m + 4 * D_M
    w_kv_t = w_in[:, o_kv:o_gate].T
    w_m = w_in[:, o_m:o_if]
    w_small = jnp.concatenate([w_in[:, o_gate:o_m], w_in[:, o_if:],
                               jnp.zeros((d, LANES - 3 * N_HEADS_A - 2 * N_HEADS_M), w_in.dtype)], axis=1)
    return [w.astype(BF16) for w in (w_in[:, :D_A].T, w_kv_t, w_m, w_small, w_small.T)]


def _prep_compress(w_k, b_k, w_v, b_v):
    per_c = jnp.stack([w_k, w_k, w_v, w_v]).reshape(4, 2, CMP_STRIDE, HEAD_DIM_A, HEAD_DIM_A)
    w_big = jnp.einsum('chtde,cf->tcdhfe', per_c, jnp.eye(4, dtype=w_k.dtype))
    w_big = w_big.reshape(CMP_STRIDE, KV_ROW, 2 * KV_ROW).astype(BF16)
    pick = lambda h: jnp.concatenate([w_big[:, h * KV_W:(h + 1) * KV_W, h * KV_W:(h + 1) * KV_W],
                                      w_big[:, h * KV_W:(h + 1) * KV_W, KV_ROW + h * KV_W:KV_ROW + (h + 1) * KV_W]],
                                     axis=-1)
    w_big = jnp.stack([pick(0), pick(1)])
    b_big = jnp.concatenate([b_k, b_k, b_v, b_v]).reshape(1, KV_ROW)
    return w_big, b_big


def _bucket_tiles(t, past, wb):
    r = np.arange(LANES)
    tiles = [_t5_bucket_np(d * LANES + r[None, :] - r[:, None]) for d in range(2)]
    for i in range(t // Q_TILE):
        tiles.append(_t5_bucket_np(i * Q_TILE + r[None, :] - (CMP_STRIDE * r[:, None] + CMP_LEN - 1)))
    flat = np.arange(LANES * LANES).reshape(LANES, LANES)
    tiles.append(_t5_bucket_np(past - (CMP_STRIDE * flat + CMP_LEN - 1)))
    tiles.append(_t5_bucket_np(wb - flat))
    tiles.append(_t5_bucket_np(past - (r[:, None] * SEL_BLOCK + r[None, :] % SEL_BLOCK)))
    return np.stack(tiles).astype(np.int32)


def _near_tables(bnear):
    r = np.arange(LANES)
    tile4 = lambda a: jnp.asarray(np.tile(a, (1, GROUP_R)), F32)
    causal = tile4(np.where(r[:, None] <= r[None, :], 0.0, NEG))
    oldest = tile4(np.where(r[:, None] > r[None, :], 0.0, NEG))
    zero = jnp.zeros_like(causal)
    masked = jnp.full_like(causal, NEG)
    n_mid = WINDOW // LANES - 2
    sel_add, win_add = [], []
    for g in range(N_KV_A):
        near = [bnear[1, g], bnear[0, g] + causal]
        sel_add.append(jnp.concatenate(near + [masked], axis=0))
        win_add.append(jnp.concatenate([oldest] + [zero] * n_mid + near + [masked] * (WINDOW // LANES), axis=0))
    return jnp.stack(sel_add), jnp.stack(win_add)


def kernel(x_prompt, x_sample, cache_cmp_kv, cache_sel_kv, cache_win_kv, state_C, state_n, state_m, state_conv, page_table, rel_bias, g_ffn1, w1_gate, w1_up, w1_down, g_mix, w_in, qn_g, kn_cmp_g, kn_sel_g, kn_win_g, w_cmp_k, b_cmp_k, w_cmp_v, b_cmp_v, conv_w, conv_b, b_if, mn_g, w_out, g_ffn2, w2_gate, w2_up, w2_down):
    assert x_prompt.shape[2] == D_MODEL and g_ffn1.shape[0] == 1
    nb, t, _ = x_prompt.shape
    ns, ds, _ = x_sample.shape
    assert ds == 1 and t % Q_TILE == 0
    n_pages = page_table.shape[1]
    past = n_pages * PAGE_SIZE
    wb = cache_win_kv.shape[2]
    assert wb == WINDOW
    kv_shape = (2, N_KV_A, HEAD_DIM_A)

    row = lambda a: a.reshape(1, -1)
    two = lambda a: jnp.tile(a, 2).reshape(1, KV_W)
    w1 = (row(g_ffn1[0]), w1_gate[0].astype(BF16), w1_up[0].astype(BF16), w1_down[0].astype(BF16))
    w2 = (row(g_ffn2[0]), w2_gate[0].astype(BF16), w2_up[0].astype(BF16), w2_down[0].astype(BF16))
    wqt, wkvt, wm, ws, wst = _prep_in_proj(w_in[0])
    qg = (qn_g[0] * F32(HEAD_DIM_A ** -0.5))[:, None]
    kgt = jnp.stack([jnp.tile(kn_sel_g[0], 2), jnp.tile(kn_win_g[0], 2)])[:, :, None]
    w_big, b_big = _prep_compress(w_cmp_k[0], b_cmp_k[0], w_cmp_v[0], b_cmp_v[0])
    g_cmp = two(kn_cmp_g[0])
    woa, wom = w_out[0][:D_A].astype(BF16), w_out[0][D_A:].astype(BF16)
    n_gate = 3 * N_HEADS_A
    bif = jnp.zeros((1, LANES), F32).at[0, n_gate:n_gate + 2 * N_HEADS_M].set(b_if[0])
    conv_b2 = row(conv_b[0])
    mn = row(mn_g[0])

    nqb = t // Q_TILE
    tiles = _bias_tiles(rel_bias, jnp.asarray(_bucket_tiles(t, past, wb)))
    group_lanes = lambda a: a.reshape(a.shape[0], N_KV_A, GROUP_R, LANES, LANES).transpose(0, 1, 3, 2, 4).reshape(
        a.shape[0], N_KV_A, LANES, GROUP_R * LANES)
    bnear = group_lanes(tiles[0:2]) * F32(LOG2_E)
    sel_add, win_add = _near_tables(bnear)
    bcmp = group_lanes(tiles[2:2 + nqb]) * F32(LOG2_E)
    t_cmp, t_win, t_sel = tiles[2 + nqb], tiles[3 + nqb], tiles[4 + nqb]

    def dense_in(x, seq_len, q_gain, kmask):
        x1 = _ffn(x, *w1)
        return (x1,) + tuple(_inproj(x1, seq_len, kmask, row(g_mix[0]), wqt, wkvt, wm, ws, wst, q_gain, kgt))

    n_sel_p = -(-t // SEL_BLOCK)
    assert n_sel_p + t // LANES <= LANES
    kmask_np = np.zeros((t, LANES), np.float32)
    kmask_np[np.arange(t), np.arange(t) // SEL_BLOCK] = -MASK_BIG
    kmask_np[np.arange(t), n_sel_p + np.arange(t) // LANES] = -MASK_BIG

    from_chan = lambda a: a.reshape((1, a.shape[0]) + kv_shape + (a.shape[2],)).transpose(0, 1, 5, 2, 3, 4)
    to_chan = lambda a: a.transpose(0, 2, 3, 4, 1).reshape(a.shape[0], KV_ROW, a.shape[1])

    (x1, q_t, cmp_rows, cmp_t, sel_t, win_t, ks, vs, kw, vw, qk_m, v_m, o_m, small, gate_t) = dense_in(
        x_prompt.reshape(nb * t, D_MODEL), t, qg * F32(LOG2_E),
        jnp.asarray(kmask_np, BF16))
    seq = lambda a: a.reshape(nb, t, a.shape[-1])
    n16 = t // CMP_STRIDE
    assert n16 == LANES
    kc, vc_t = _compress(cmp_rows, t, w_big, b_big, g_cmp)
    n_sel = -(-t // SEL_BLOCK)
    cov_t = jnp.asarray(np.pad(_cover_np(n16 - 1, n_sel), ((0, 1), (0, 0))).T, BF16)
    a_pad = _nsa_prompt(q_t, gate_t, kc, vc_t, ks, vs, kw, vw, sel_add, win_add, bcmp, cov_t, n16 - 1)
    chunk = MLSTM_CHUNK if t % MLSTM_CHUNK == 0 else Q_TILE
    m_out, c_p, n_p, m_p = _mlstm_prompt(seq(qk_m), seq(v_m), seq(o_m), seq(small), conv_w[0], conv_b2, bif, mn,
                                         chunk)
    y_prompt = _post(x1, a_pad.reshape(nb * t, -1), m_out.reshape(nb * t, -1), woa, wom, *w2)
    kv6 = lambda a, n: a.reshape((1, n, -1) + kv_shape)
    prompt_states = (from_chan(cmp_t), from_chan(sel_t), from_chan(win_t[:, :, t - wb:]), c_p[None], n_p[None],
                     m_p[None, :, 0, :N_HEADS_M], seq(qk_m)[None, :, t - (CONV_W - 1):])

    (x1s, q_st, cmp_halves, _, sel_st, win_st, _, _, _, _, qk_s, v_s, o_s, small_s, gate_st) = dense_in(
        x_sample.reshape(ns, D_MODEL), ns, qg, jnp.zeros((ns, LANES), BF16))
    cmp_s = jnp.concatenate([cmp_halves[0], cmp_halves[1]], axis=1)
    sel_s, win_s, gates_s = sel_st[0].T, win_st[0].T, gate_st[0].T
    q3 = q_st[0].T.reshape(ns, N_HEADS_A, LANES)
    n16s = past // CMP_STRIDE
    n_sel_s = -(-(past + 1) // SEL_BLOCK)
    n_sel_pad = -(-n_sel_s // LANES) * LANES
    cover_s = jnp.asarray(np.pad(_cover_np(n16s - 1, n_sel_s), ((0, 1), (0, n_sel_pad - n_sel_s))), BF16)
    bias_c = t_cmp[:, :n16s // LANES, :].reshape(N_HEADS_A, n16s)
    o_c, idx = _sample_cmp(page_table, to_chan(cache_cmp_kv[0]), w_big, b_big, g_cmp, q3, bias_c, cover_s, n_sel_s,
                           past)
    n_past_blocks = past // SEL_BLOCK
    two_lanes = lambda a: jnp.concatenate([a[:, :SEL_BLOCK], a[:, :SEL_BLOCK]], axis=-1)
    bsel = jnp.stack([two_lanes(t_sel[:, n_past_blocks - 2, :]), two_lanes(t_sel[:, n_past_blocks - 1, :])])
    bwin = t_win[:, :wb // LANES, :].reshape(N_HEADS_A, wb)
    b0 = jnp.broadcast_to(t_win[:, wb // LANES, 0:1], (N_HEADS_A, LANES))
    gate_b = jnp.broadcast_to(gates_s[:, :n_gate, None], (ns, n_gate, LANES))
    a_s, win_out = _sample_attn(idx[:, :, 0].reshape(-1), page_table, to_chan(cache_sel_kv[0]),
                                to_chan(cache_win_kv[0]), q3, sel_s[:, None, :], win_s[:, None, :], win_st[0],
                                o_c, gate_b, bsel, bwin, b0)
    m0 = jnp.pad(state_m[0], ((0, 0), (0, LANES - N_HEADS_M)))[:, None, :]
    m_s, c_s, n_s, m_new = _mlstm_step(qk_s[:, None, :], state_conv[0], v_s[:, None, :], o_s[:, None, :],
                                       small_s[:, None, :], state_C[0], state_n[0], m0, conv_w[0], conv_b2, bif, mn)
    a_dense = jnp.concatenate([a_s[:, h, (h // GROUP_R) * HEAD_DIM_A:(h // GROUP_R + 1) * HEAD_DIM_A]
                               for h in range(N_HEADS_A)], axis=1)
    y_sample = _post(x1s, a_dense, m_s.reshape(ns, -1), woa, wom, *w2)
    conv_new = jnp.concatenate([state_conv[0][:, 1:], qk_s[:, None, :]], axis=1)
    sample_states = (kv6(cmp_s, ns), kv6(sel_s, ns), from_chan(win_out), c_s[None], n_s[None],
                     m_new[None, :, 0, :N_HEADS_M], conv_new[None])

    return (y_prompt.reshape(nb, t, D_MODEL), y_sample.reshape(ns, 1, D_MODEL)) + prompt_states + sample_states
```

```python
import functools
import math

import jax
import jax.numpy as jnp
import numpy as np
from jax import lax
from jax.experimental import pallas as pl
from jax.experimental.pallas import tpu as pltpu

F32 = jnp.float32
BF16 = jnp.bfloat16

D_MODEL = 1024
PAGE_SIZE = 128
N_HEADS_A = 8
HEAD_DIM_A = 64
N_KV_A = 2
GROUP_R = N_HEADS_A // N_KV_A
D_A = N_HEADS_A * HEAD_DIM_A
KV_W = N_KV_A * HEAD_DIM_A
CMP_STRIDE = 16
CMP_LEN = 2 * CMP_STRIDE
SEL_BLOCK = 64
TOP_K_BLOCKS = 16
WINDOW = 512
N_HEADS_M = 4
HEAD_DIM_M = 128
D_M = N_HEADS_M * HEAD_DIM_M
CONV_W = 4
N_BUCKETS = 32
MAX_DISTANCE = 128
EPS = 1e-6

LANES = 128
SUBLANES = 8
TOKEN_TILE = 512
MLSTM_CHUNK = 256
STEP_SEQS = 8
Q_TILE = 128
Q_BLOCKS = 2
V_ROWS = HEAD_DIM_A + 16
FAR_TILES = 4
KV_ROW = 2 * KV_W
LOG2_E = 1.4426950408889634
NEG = -1e30
MASK_BIG = 2.0 ** 100
VMEM_LIMIT = 56 * 1024 * 1024

NT_DIMS = (((1,), (1,)), ((), ()))


def _dot(a, b):
    return jnp.dot(a, b, preferred_element_type=F32)


def _dot_nt(a, b):
    return lax.dot_general(a, b, NT_DIMS, preferred_element_type=F32)


def _split3(x):
    x1 = x.astype(BF16)
    r1 = x - x1.astype(F32)
    x2 = r1.astype(BF16)
    x3 = (r1 - x2.astype(F32)).astype(BF16)
    return x1, x2, x3


def _dot_exact_rhs(x, m):
    x1, x2, x3 = _split3(x)
    return _dot(x1, m) + _dot(x2, m) + _dot(x3, m)


def _dot_exact_lhs(m, x):
    x1, x2, x3 = _split3(x)
    return _dot(m, x1) + _dot(m, x2) + _dot(m, x3)


def _rms_rows(x, g):
    ms = jnp.mean(x * x, axis=-1, keepdims=True)
    return x * lax.rsqrt(ms + EPS) * g


def _rms_two_groups(k, g):
    sq = k * k
    lane = lax.broadcasted_iota(jnp.int32, sq.shape, 1)
    lo = lane < HEAD_DIM_A
    s0 = jnp.sum(jnp.where(lo, sq, 0.0), axis=-1, keepdims=True)
    s1 = jnp.sum(jnp.where(lo, 0.0, sq), axis=-1, keepdims=True)
    ms = jnp.where(lo, s0, s1) * (1.0 / HEAD_DIM_A)
    return k * lax.rsqrt(ms + EPS) * g


def _sigmoid(x):
    return jax.nn.sigmoid(x)


def _log_sigmoid(x):
    return jnp.minimum(x, 0.0) - jnp.log1p(jnp.exp(-jnp.abs(x)))


def _const_spec(shape):
    nd = len(shape)
    return pl.BlockSpec(shape, lambda *_: (0,) * nd, pipeline_mode=pl.Buffered(1))


def _params(*sem):
    return pltpu.CompilerParams(dimension_semantics=sem, vmem_limit_bytes=VMEM_LIMIT)


def _swiglu_residual(x, g_ref, wg_ref, wu_ref, wd_ref):
    xn = _rms_rows(x, g_ref[...]).astype(BF16)
    d_ff = wg_ref.shape[1]
    n_split = 2 if d_ff % (2 * LANES) == 0 else 1
    step = d_ff // n_split
    acc = jnp.zeros_like(x)
    for c in range(n_split):
        hg = _dot(xn, wg_ref[:, c * step:(c + 1) * step])
        hu = _dot(xn, wu_ref[:, c * step:(c + 1) * step])
        h = (hg * _sigmoid(hg)) * hu
        acc = acc + _dot(h.astype(BF16), wd_ref[c * step:(c + 1) * step, :])
    return x + 0.5 * acc


def _ffn_kernel(x_ref, g_ref, wg_ref, wu_ref, wd_ref, y_ref):
    y_ref[...] = _swiglu_residual(x_ref[...], g_ref, wg_ref, wu_ref, wd_ref)


def _post_kernel(x_ref, a_ref, m_ref, woa_ref, wom_ref, g_ref, wg_ref, wu_ref, wd_ref, y_ref):
    x = x_ref[...] + (_dot(a_ref[...], woa_ref[...]) + _dot(m_ref[...], wom_ref[...]))
    y_ref[...] = _swiglu_residual(x, g_ref, wg_ref, wu_ref, wd_ref)


def _token_tile(n):
    return TOKEN_TILE if n % TOKEN_TILE == 0 else n


def _ffn(x, g, wg, wu, wd):
    n = x.shape[0]
    tm = _token_tile(n)
    row = lambda w: pl.BlockSpec((tm, w), lambda i: (i, 0))
    return pl.pallas_call(
        _ffn_kernel,
        grid=(n // tm,),
        in_specs=[row(D_MODEL), _const_spec(g.shape), _const_spec(wg.shape), _const_spec(wu.shape),
                  _const_spec(wd.shape)],
        out_specs=row(D_MODEL),
        out_shape=jax.ShapeDtypeStruct((n, D_MODEL), F32),
        compiler_params=_params("parallel"),
        name="ffn1",
    )(x, g, wg, wu, wd)


def _post(x, a_pad, m_out, woa, wom, g, wg, wu, wd):
    n = x.shape[0]
    tm = _token_tile(n)
    row = lambda w: pl.BlockSpec((tm, w), lambda i: (i, 0))
    return pl.pallas_call(
        _post_kernel,
        grid=(n // tm,),
        in_specs=[row(D_MODEL), row(a_pad.shape[1]), row(m_out.shape[1]), _const_spec(woa.shape),
                  _const_spec(wom.shape), _const_spec(g.shape), _const_spec(wg.shape), _const_spec(wu.shape),
                  _const_spec(wd.shape)],
        out_specs=row(D_MODEL),
        out_shape=jax.ShapeDtypeStruct((n, D_MODEL), F32),
        compiler_params=_params("parallel"),
        name="post_mix_ffn2",
    )(x, a_pad, m_out, woa, wom, g, wg, wu, wd)


def _rms_two_groups_t(k, g_col):
    sq = k * k
    half = k.shape[0] // 2
    s0 = jnp.sum(sq[0:half], axis=0, keepdims=True)
    s1 = jnp.sum(sq[half:], axis=0, keepdims=True)
    ms = jnp.concatenate([jnp.broadcast_to(s0, (half, k.shape[1])), jnp.broadcast_to(s1, (half, k.shape[1]))],
                         axis=0) * (1.0 / HEAD_DIM_A)
    return k * lax.rsqrt(ms + EPS) * g_col


def _inproj_kernel(x_ref, kmask_ref, g_ref, wt_ref, wm_ref, ws_ref, qg_ref, kgt_ref,
                   q_out, cmp_rows, cmpt_out, selt_out, wint_out, kst, vst, kwt, vwt, qk_out, v_out, o_out,
                   small_out, gate_out):
    xn = _rms_rows(x_ref[...], g_ref[...]).astype(BF16)
    all_t = _dot_nt(wt_ref[...], xn)
    qt = all_t[0:D_A, :]
    zeros = jnp.zeros((HEAD_DIM_A, qt.shape[1]), BF16)
    for h in range(N_HEADS_A):
        qh = qt[h * HEAD_DIM_A:(h + 1) * HEAD_DIM_A, :]
        ms = jnp.sum(qh * qh, axis=0, keepdims=True) * (1.0 / HEAD_DIM_A)
        own = h * LANES + (h // GROUP_R) * HEAD_DIM_A
        other = h * LANES + (1 - h // GROUP_R) * HEAD_DIM_A
        q_out[0, own:own + HEAD_DIM_A, :] = (qh * lax.rsqrt(ms + EPS) * qg_ref[...]).astype(BF16)
        q_out[0, other:other + HEAD_DIM_A, :] = zeros
    kvt = all_t[D_A:D_A + 6 * KV_W, :]
    cmpt_out[0] = kvt[0:KV_ROW]
    cmp_rows[0] = kvt[0:KV_W].T
    cmp_rows[1] = kvt[KV_W:KV_ROW].T
    ks = _rms_two_groups_t(kvt[2 * KV_W:3 * KV_W], kgt_ref[0])
    vs = kvt[3 * KV_W:4 * KV_W]
    kw = _rms_two_groups_t(kvt[4 * KV_W:5 * KV_W], kgt_ref[1])
    vw = kvt[5 * KV_W:6 * KV_W]
    selt_out[0, 0:KV_W] = ks
    selt_out[0, KV_W:KV_ROW] = vs
    wint_out[0, 0:KV_W] = kw
    wint_out[0, KV_W:KV_ROW] = vw
    ones = jnp.ones((V_ROWS - HEAD_DIM_A, LANES), F32)
    for j in range(kst.shape[1]):
        cols = slice(j * LANES, (j + 1) * LANES)
        kst[0, j] = jnp.concatenate([ks[:, cols].T.astype(BF16), kmask_ref[cols, :]], axis=1)
        kwt[0, j] = kw[:, cols].T.astype(BF16)
        for v, out in ((vs, vst), (vw, vwt)):
            for g in range(N_KV_A):
                out[0, j, g] = jnp.concatenate([v[g * HEAD_DIM_A:(g + 1) * HEAD_DIM_A, cols], ones],
                                               axis=0).astype(BF16)
    m = _dot(xn, wm_ref[...])
    qk_out[...] = m[:, 0:2 * D_M]
    v_out[...] = m[:, 2 * D_M:3 * D_M]
    o_out[...] = m[:, 3 * D_M:4 * D_M]
    small_out[...] = _dot(xn, ws_ref[...])
    gate_out[0] = _sigmoid(all_t[D_A + 6 * KV_W:, :])


def _inproj(x, seq_len, kmask, g, wt, wm, ws, qg, kgt):
    n = x.shape[0]
    nb = n // seq_len
    tm = _token_tile(seq_len)
    tpb = seq_len // tm
    row = lambda w: pl.BlockSpec((tm, w), lambda i: (i, 0))
    rows_out = lambda w, dt: (row(w), jax.ShapeDtypeStruct((n, w), dt))
    chan_out = lambda c, dt: (pl.BlockSpec((1, c, tm), lambda i: (i // tpb, 0, i % tpb)),
                              jax.ShapeDtypeStruct((nb, c, seq_len), dt))
    k_tiles = (pl.BlockSpec((1, tm // LANES, LANES, KV_W), lambda i: (i // tpb, i % tpb, 0, 0)),
               jax.ShapeDtypeStruct((nb, seq_len // LANES, LANES, KV_W), BF16))
    ksel_tiles = (pl.BlockSpec((1, tm // LANES, LANES, 2 * KV_W), lambda i: (i // tpb, i % tpb, 0, 0)),
                  jax.ShapeDtypeStruct((nb, seq_len // LANES, LANES, 2 * KV_W), BF16))
    v_tiles = (pl.BlockSpec((1, tm // LANES, N_KV_A, V_ROWS, LANES), lambda i: (i // tpb, i % tpb, 0, 0, 0)),
               jax.ShapeDtypeStruct((nb, seq_len // LANES, N_KV_A, V_ROWS, LANES), BF16))
    halves_out = (pl.BlockSpec((2, tm, KV_W), lambda i: (0, i, 0)), jax.ShapeDtypeStruct((2, n, KV_W), F32))
    outs = [chan_out(N_HEADS_A * LANES, BF16), halves_out, chan_out(KV_ROW, F32), chan_out(KV_ROW, F32),
            chan_out(KV_ROW, F32), ksel_tiles, v_tiles, k_tiles, v_tiles,
            rows_out(2 * D_M, F32), rows_out(D_M, F32), rows_out(D_M, F32), rows_out(LANES, F32),
            chan_out(LANES, F32)]
    consts = (g, wt, wm, ws, qg, kgt)
    return pl.pallas_call(
        _inproj_kernel,
        grid=(n // tm,),
        in_specs=[row(D_MODEL), pl.BlockSpec((tm, LANES), lambda i: (i % tpb, 0))]
        + [_const_spec(a.shape) for a in consts],
        out_specs=[o[0] for o in outs],
        out_shape=[o[1] for o in outs],
        compiler_params=_params("parallel"),
        name="in_proj",
    )(x, kmask, *consts)


def _t5_bucket_np(dist):
    n = np.maximum(dist, 0)
    exact = N_BUCKETS // 2
    nf = np.maximum(n, 1).astype(np.float32)
    ratio = np.log(nf / np.float32(exact)) / np.float32(math.log(MAX_DISTANCE / exact))
    large = exact + (ratio * np.float32(N_BUCKETS - exact)).astype(np.int32)
    return np.where(n < exact, n, np.minimum(large, N_BUCKETS - 1)).astype(np.int32)


def _bias_kernel(tbl_ref, bkt_ref, out_ref):
    bkt = bkt_ref[0]
    for h in range(N_HEADS_A):
        far = tbl_ref[N_BUCKETS - 1, h]
        acc = jnp.zeros(bkt.shape, F32)
        for b in range(N_BUCKETS - 1):
            acc = jnp.where(bkt == b, tbl_ref[b, h] - far, acc)
        out_ref[0, h] = acc


def _bias_tiles(rel_bias, buckets):
    n = buckets.shape[0]
    return pl.pallas_call(
        _bias_kernel,
        grid=(n,),
        in_specs=[pl.BlockSpec(memory_space=pltpu.SMEM),
                  pl.BlockSpec((1, LANES, LANES), lambda i: (i, 0, 0))],
        out_specs=pl.BlockSpec((1, N_HEADS_A, LANES, LANES), lambda i: (i, 0, 0, 0)),
        out_shape=jax.ShapeDtypeStruct((n, N_HEADS_A, LANES, LANES), F32),
        compiler_params=_params("parallel"),
        name="bias_tiles",
    )(rel_bias, buckets)


def _compress_half(rows_ref, nc, w_ref, half, bias):
    fs = None
    for t in range(CMP_STRIDE):
        part = _dot(rows_ref[pl.ds(t, nc, stride=CMP_STRIDE), :].astype(BF16), w_ref[half, t])
        fs = part if fs is None else fs + part
    return fs[:, 0:KV_W] + pltpu.roll(fs[:, KV_W:KV_ROW], nc - 1, 0) + bias


def _compress_rows(k_rows_ref, v_rows_ref, nc, w_ref, b_ref, g_ref):
    kc = _rms_two_groups(_compress_half(k_rows_ref, nc, w_ref, 0, b_ref[:, 0:KV_W]), g_ref[...])
    vc = _compress_half(v_rows_ref, nc, w_ref, 1, b_ref[:, KV_W:KV_ROW])
    complete = lax.broadcasted_iota(jnp.int32, kc.shape, 0) < nc - 1
    return jnp.where(complete, kc, 0.0), jnp.where(complete, vc, 0.0)


def _compress_kernel(x_ref, w_ref, b_ref, g_ref, kc_ref, vc_ref):
    nc = kc_ref.shape[1]
    kc, vc = _compress_rows(x_ref.at[0], x_ref.at[1], nc, w_ref, b_ref, g_ref)
    kc_ref[0] = kc.astype(BF16)
    vc_ref[0] = vc.T.astype(BF16)


def _compress(cmp_rows, t, w_big, b_big, g_k):
    nb = cmp_rows.shape[1] // t
    nc = t // CMP_STRIDE
    return pl.pallas_call(
        _compress_kernel,
        grid=(nb,),
        in_specs=[pl.BlockSpec((2, t, KV_W), lambda b: (0, b, 0)), _const_spec(w_big.shape),
                  _const_spec(b_big.shape), _const_spec(g_k.shape)],
        out_specs=[pl.BlockSpec((1, nc, KV_W), lambda b: (b, 0, 0)),
                   pl.BlockSpec((1, KV_W, nc), lambda b: (b, 0, 0))],
        out_shape=[jax.ShapeDtypeStruct((nb, nc, KV_W), BF16), jax.ShapeDtypeStruct((nb, KV_W, nc), BF16)],
        compiler_params=_params("parallel"),
        name="compress_prompt",
    )(cmp_rows, w_big, b_big, g_k)


def _tile_heads(x):
    return jnp.concatenate([x] * GROUP_R, axis=1)


def _softmax_part(m, s):
    m_new = jnp.maximum(m, jnp.max(s, axis=0, keepdims=True))
    return m_new, jnp.exp2(m - m_new), jnp.exp2(s - m_new).astype(BF16)


def _accumulate(acc, soft, vt):
    m_new, alpha, p = soft
    return m_new, alpha * acc + _dot(vt, p)


def _key_rows(ref, t, n):
    return ref[0, pl.ds(t, n)].reshape(n * LANES, ref.shape[-1])


def _value_cols(ref, t, n, g):
    return jnp.concatenate([ref[0, t + k, g] for k in range(n)], axis=1)


def _rank_rows(score):
    n_blk = score.shape[0]
    blk = lax.broadcasted_iota(jnp.int32, score.shape, 0)
    rank = jnp.zeros(score.shape, F32)
    for b in range(n_blk):
        row = score[b:b + 1, :]
        rank = rank + jnp.where(blk > b, jnp.where(row >= score, 1.0, 0.0), jnp.where(row > score, 1.0, 0.0))
    return rank


def _nsa_prompt_kernel(q_ref, gate_ref, kc_ref, vct_ref, ks_ref, vs_ref, kw_ref, vw_ref, sel_add_ref, win_add_ref,
                       bcmp_ref, covt_ref, a_ref, *, n_cmp):
    ii = pl.program_id(1)
    n_tiles = ks_ref.shape[1]
    n_sel = covt_ref.shape[0]
    wide = GROUP_R * Q_TILE
    heads_of = lambda g: range(g * GROUP_R, (g + 1) * GROUP_R)
    chains =[(sb, g) for sb in range(Q_BLOCKS) for g in range(N_KV_A)]
    blk_i = [Q_BLOCKS * ii + sb for sb in range(Q_BLOCKS)]
    cols = lambda sb: slice(sb * Q_TILE, (sb + 1) * Q_TILE)
    qg_t = {(sb, g): jnp.concatenate([q_ref[0, h * LANES:(h + 1) * LANES, cols(sb)] for h in heads_of(g)], axis=1)
            for sb, g in chains}
    init = (jnp.full((1, wide), NEG, F32), jnp.zeros((V_ROWS, wide), F32))
    normalised = lambda acc: acc[0:HEAD_DIM_A, :] / acc[HEAD_DIM_A:HEAD_DIM_A + 1, :]

    n_win = WINDOW // LANES + 1
    t0 = [jnp.maximum(i - (n_win - 1), 0) for i in blk_i]
    s_win = {}
    for sb, i in enumerate(blk_i):
        win_off = pl.multiple_of(jnp.maximum(n_win - 1 - i, 0) * LANES, LANES)
        k_win = _key_rows(kw_ref, t0[sb], n_win)
        for g in range(N_KV_A):
            s_win[sb, g] = _dot(k_win, qg_t[sb, g]) + win_add_ref[g, pl.ds(win_off, n_win * LANES), :]

    ncp = kc_ref.shape[1]
    j_sub = lax.broadcasted_iota(jnp.int32, (ncp, wide), 0)
    q_in_tile = lax.broadcasted_iota(jnp.int32, (ncp, wide), 1) & (Q_TILE - 1)
    s_cmp, any_usable = {}, []
    for sb, i in enumerate(blk_i):
        q_lane = i * Q_TILE + q_in_tile
        usable_add = jnp.where((j_sub * CMP_STRIDE + (CMP_LEN - 1) <= q_lane) & (j_sub < n_cmp), 0.0, NEG)
        any_usable.append(jnp.where(q_lane[0:1, :] >= CMP_LEN - 1, 1.0, 0.0))
        for g in range(N_KV_A):
            s_cmp[sb, g] = _dot(kc_ref[0], qg_t[sb, g]) + bcmp_ref[sb, g] + usable_add
    p_c = {}
    for sb, g in chains:
        e = jnp.exp2(s_cmp[sb, g] - jnp.max(s_cmp[sb, g], axis=0, keepdims=True)) * any_usable[sb]
        p_c[sb, g] = e / jnp.maximum(jnp.sum(e, axis=0, keepdims=True), 1e-30)
    oc_t = {c: _dot(vct_ref[0], p_c[c].astype(BF16)) for c in chains}

    blk = lax.broadcasted_iota(jnp.int32, (n_sel, Q_TILE), 0)
    q_in_blk = lax.broadcasted_iota(jnp.int32, (n_sel, Q_TILE), 1)
    tile_row = lax.broadcasted_iota(jnp.int32, (n_tiles, wide), 0)
    pad_rows = jnp.zeros((LANES - n_sel - n_tiles, wide), BF16)
    q_far, q_near = {}, {}
    for sb, g in chains:
        qpos = blk_i[sb] * Q_TILE + q_in_blk
        cur = qpos // SEL_BLOCK
        valid = blk * SEL_BLOCK <= qpos
        forced = valid & ((blk == 0) | (blk == cur) | (blk == cur - 1))
        p_sum = p_c[sb, g][:, 0:Q_TILE]
        for r in range(1, GROUP_R):
            p_sum = p_sum + p_c[sb, g][:, r * Q_TILE:(r + 1) * Q_TILE]
        imp_t = _dot_exact_lhs(covt_ref[...], p_sum)
        score = jnp.where(forced, -NEG, jnp.where(valid, imp_t, NEG))
        dropped = _tile_heads(jnp.where(_rank_rows(score) < float(min(TOP_K_BLOCKS, n_sel)), 0.0, 1.0).astype(BF16))
        past_far = jnp.where(tile_row > blk_i[sb] - 2, 1.0, 0.0).astype(BF16)
        q_far[sb, g] = jnp.concatenate([qg_t[sb, g], dropped, past_far, pad_rows], axis=0)
        q_near[sb, g] = jnp.concatenate([qg_t[sb, g], dropped, jnp.zeros_like(past_far), pad_rows], axis=0)

    soft_win = {c: _softmax_part(init[0], s_win[c]) for c in chains}
    ow_acc = {(sb, g): _dot(_value_cols(vw_ref, t0[sb], n_win, g), soft_win[sb, g][2]) for sb, g in chains}

    def far_body(it, carry):
        t = FAR_TILES * it
        k_far = _key_rows(ks_ref, t, FAR_TILES)
        s = {c: _dot(k_far, q_far[c]) for c in chains}
        soft = [_softmax_part(carry[n][0], s[c]) for n, c in enumerate(chains)]
        return tuple(_accumulate(carry[n][1], soft[n], _value_cols(vs_ref, t, FAR_TILES, c[1]))
                     for n, c in enumerate(chains))

    n_far = blk_i[-1] - 1
    carry = lax.fori_loop(0, (n_far + FAR_TILES - 1) // FAR_TILES, far_body, (init,) * len(chains))

    t1 = [jnp.maximum(i - 1, 0) for i in blk_i]
    s_near = {}
    for sb, i in enumerate(blk_i):
        k_near = _key_rows(ks_ref, t1[sb], 2)
        near_off = pl.multiple_of(jnp.where(i == 0, LANES, 0), LANES)
        for g in range(N_KV_A):
            s_near[sb, g] = _dot(k_near, q_near[sb, g]) + sel_add_ref[g, pl.ds(near_off, 2 * LANES), :]
    soft_near = [_softmax_part(carry[n][0], s_near[c]) for n, c in enumerate(chains)]
    for n, (sb, g) in enumerate(chains):
        _, acc = _accumulate(carry[n][1], soft_near[n], _value_cols(vs_ref, t1[sb], 2, g))
        heads = heads_of(g)
        gate_row = lambda br: jnp.concatenate(
            [gate_ref[0, br * N_HEADS_A + h:br * N_HEADS_A + h + 1, cols(sb)] for h in heads], axis=1)
        out_t = (gate_row(0) * oc_t[sb, g][g * HEAD_DIM_A:(g + 1) * HEAD_DIM_A, :] + gate_row(1) * normalised(acc)
                 + gate_row(2) * normalised(ow_acc[sb, g]))
        for pair in range(GROUP_R // 2):
            two = jnp.concatenate([out_t[:, (2 * pair + k) * Q_TILE:(2 * pair + k + 1) * Q_TILE] for k in range(2)],
                                  axis=0)
            col = (g * GROUP_R // 2 + pair) * LANES
            a_ref[0, cols(sb), col:col + LANES] = two.T.astype(BF16)


def _nsa_prompt(q_t, gate_t, kc, vc_t, ks, vs, kw, vw, sel_add, win_add, bcmp, cov_t, n_cmp):
    nb, _, t = q_t.shape
    nqb = t // Q_TILE
    assert nqb > WINDOW // LANES and nqb % Q_BLOCKS == 0
    step = Q_BLOCKS * Q_TILE
    seq = lambda a: pl.BlockSpec((1,) + a.shape[1:], lambda b, i: (b,) + (0,) * (a.ndim - 1))
    return pl.pallas_call(
        functools.partial(_nsa_prompt_kernel, n_cmp=n_cmp),
        grid=(nb, nqb // Q_BLOCKS),
        in_specs=[pl.BlockSpec((1, N_HEADS_A * LANES, step), lambda b, i: (b, 0, i)),
                  pl.BlockSpec((1, LANES, step), lambda b, i: (b, 0, i)),
                  seq(kc), seq(vc_t), seq(ks), seq(vs), seq(kw), seq(vw),
                  _const_spec(sel_add.shape), _const_spec(win_add.shape),
                  pl.BlockSpec((Q_BLOCKS,) + bcmp.shape[1:], lambda b, i: (i, 0, 0, 0)),
                  _const_spec(cov_t.shape)],
        out_specs=pl.BlockSpec((1, step, D_A), lambda b, i: (b, i, 0)),
        out_shape=jax.ShapeDtypeStruct((nb, t, D_A), BF16),
        compiler_params=_params("parallel", "arbitrary"),
        name="nsa_prompt",
    )(q_t, gate_t, kc, vc_t, ks, vs, kw, vw, sel_add, win_add, bcmp, cov_t)


def _head_norm_gate(h_all, mn_ref, o):
    outs = []
    for h in range(N_HEADS_M):
        hs = slice(h * HEAD_DIM_M, (h + 1) * HEAD_DIM_M)
        outs.append(_rms_rows(h_all[h], mn_ref[:, hs]))
    return _sigmoid(o) * jnp.concatenate(outs, axis=-1)


def _mlstm_prompt_kernel(qk_ref, v_ref, o_ref, small_ref, cw_ref, cb_ref, bif_ref, mn_ref, tri_ref, triu_ref,
                         y_ref, c_ref, n_ref, m_ref, xbuf):
    chunk = qk_ref.shape[1]
    pad = SUBLANES

    @pl.when(pl.program_id(1) == 0)
    def _():
        xbuf[...] = jnp.zeros_like(xbuf)
        c_ref[...] = jnp.zeros_like(c_ref)
        n_ref[...] = jnp.zeros_like(n_ref)
        m_ref[...] = jnp.zeros_like(m_ref)

    x = qk_ref[0]
    tail = xbuf[...]
    head_row = lax.broadcasted_iota(jnp.int32, tail.shape, 0)

    def delayed(s):
        rolled = pltpu.roll(x, s, 0)
        head = jnp.where(head_row < s, pltpu.roll(tail, s, 0), rolled[0:pad, :])
        return jnp.concatenate([head, rolled[pad:, :]], axis=0)

    y = delayed(3) * cw_ref[0:1, :]
    y = y + delayed(2) * cw_ref[1:2, :]
    y = y + delayed(1) * cw_ref[2:3, :]
    y = y + x * cw_ref[3:4, :] + cb_ref[...]
    qkc = y * _sigmoid(y)
    xbuf[...] = x[chunk - pad:chunk, :]

    gi = small_ref[0] + bif_ref[...]
    n_gate = 3 * N_HEADS_A
    gate_rows = gi.T[n_gate:n_gate + 2 * N_HEADS_M, :]
    b_col = _dot_exact_lhs(tri_ref[...], _log_sigmoid(gi))
    b_row = _dot_exact_rhs(_log_sigmoid(gate_rows), triu_ref[...])
    t_col = lax.broadcasted_iota(jnp.int32, (chunk, chunk), 0)
    s_row = lax.broadcasted_iota(jnp.int32, (chunk, chunk), 1)
    causal = s_row <= t_col
    m_all = m_ref[0]
    lane_row = lax.broadcasted_iota(jnp.int32, m_all.shape, 1)

    h_all = []
    for h in range(N_HEADS_M):
        hs = slice(h * HEAD_DIM_M, (h + 1) * HEAD_DIM_M)
        ks = slice(D_M + h * HEAD_DIM_M, D_M + (h + 1) * HEAD_DIM_M)
        q = qkc[:, hs]
        k = qkc[:, ks] * F32(HEAD_DIM_M ** -0.5)
        v = v_ref[0, :, hs]
        qb, kb = q.astype(BF16), k.astype(BF16)
        ci, fi = 3 * N_HEADS_A + h, 3 * N_HEADS_A + N_HEADS_M + h
        bt = b_col[:, fi:fi + 1]
        bs = b_row[N_HEADS_M + h:N_HEADS_M + h + 1, :]
        ig_row = gate_rows[h:h + 1, :]
        ig_col = gi[:, ci:ci + 1]
        m_prev = m_all[:, h:h + 1]
        c_prev = c_ref[0, h]
        n_prev = n_ref[0, h:h + 1, :]

        dlog = jnp.where(causal, bt - bs + ig_row, NEG)
        inter = bt + m_prev
        m_t = jnp.maximum(inter, jnp.max(dlog, axis=-1, keepdims=True))
        s_qk = _dot_nt(qb, kb) * jnp.exp(dlog - m_t)
        dec = jnp.exp(inter - m_t)
        num = _dot(s_qk.astype(BF16), v.astype(BF16)) + dec * _dot_nt(qb, c_prev.astype(BF16))
        den = jnp.sum(s_qk, axis=-1, keepdims=True) + dec * jnp.sum(q * n_prev, axis=-1, keepdims=True)
        h_all.append(num / jnp.maximum(jnp.abs(den), jnp.exp(-m_t)))

        m_new = m_t[chunk - 1:chunk, :]
        b_end = bt[chunk - 1:chunk, :]
        w_end = jnp.exp(b_end - bt + ig_col - m_new)
        dec_end = jnp.exp(b_end + m_prev - m_new)
        wv_t = (w_end * v).T.astype(BF16)
        c_ref[0, h] = dec_end * c_prev + _dot(wv_t, kb)
        n_ref[0, h:h + 1, :] = dec_end * n_prev + jnp.sum(w_end * k, axis=0, keepdims=True)
        m_all = jnp.where(lane_row == h, m_new, m_all)

    m_ref[0] = m_all
    y_ref[0] = _head_norm_gate(h_all, mn_ref, o_ref[0]).astype(BF16)


def _mlstm_prompt(qk, v, o, small, conv_w, conv_b, bif, mn_g, chunk):
    nb, t, _ = qk.shape
    tri = jnp.asarray(np.tril(np.ones((chunk, chunk), np.float32)), BF16)
    triu = jnp.asarray(np.triu(np.ones((chunk, chunk), np.float32)), BF16)
    tok = lambda w: pl.BlockSpec((1, chunk, w), lambda b, c: (b, c, 0))
    return pl.pallas_call(
        _mlstm_prompt_kernel,
        grid=(nb, t // chunk),
        in_specs=[tok(2 * D_M), tok(D_M), tok(D_M), tok(LANES)]
        + [_const_spec(a.shape) for a in (conv_w, conv_b, bif, mn_g, tri, triu)],
        out_specs=[tok(D_M),
                   pl.BlockSpec((1, N_HEADS_M, HEAD_DIM_M, HEAD_DIM_M), lambda b, c: (b, 0, 0, 0)),
                   pl.BlockSpec((1, N_HEADS_M, HEAD_DIM_M), lambda b, c: (b, 0, 0)),
                   pl.BlockSpec((1, 1, LANES), lambda b, c: (b, 0, 0))],
        out_shape=[jax.ShapeDtypeStruct((nb, t, D_M), BF16),
                   jax.ShapeDtypeStruct((nb, N_HEADS_M, HEAD_DIM_M, HEAD_DIM_M), F32),
                   jax.ShapeDtypeStruct((nb, N_HEADS_M, HEAD_DIM_M), F32),
                   jax.ShapeDtypeStruct((nb, 1, LANES), F32)],
        scratch_shapes=[pltpu.VMEM((SUBLANES, 2 * D_M), F32)],
        compiler_params=_params("parallel", "arbitrary"),
        name="mlstm_prompt",
    )(qk, v, o, small, conv_w, conv_b, bif, mn_g, tri, triu)


def _softmax_rows(s):
    e = jnp.exp(s - jnp.max(s, axis=-1, keepdims=True))
    return e, jnp.sum(e, axis=-1, keepdims=True)


def _sample_cmp_kernel(pt_ref, *refs, n_pages, n_sel, qpos):
    pages = refs[:n_pages]
    w_ref, b_ref, g_ref, q_ref, bias_ref, cov_ref, oc_ref, idx_ref, xk_ref, xv_ref = refs[n_pages:]
    for half, rows_ref in ((0, xk_ref), (1, xv_ref)):
        for u in range(n_pages):
            rows_ref[u * PAGE_SIZE:(u + 1) * PAGE_SIZE, :] = pages[u][0, half * KV_W:(half + 1) * KV_W, :].T
    nc = n_pages * PAGE_SIZE // CMP_STRIDE
    kc, vc = _compress_rows(xk_ref, xv_ref, nc, w_ref, b_ref, g_ref)
    q = q_ref[0]
    j_row = lax.broadcasted_iota(jnp.int32, (N_HEADS_A, nc), 1)
    s = _dot_nt(q, kc.astype(BF16)) + bias_ref[...] + jnp.where(j_row < nc - 1, 0.0, NEG)
    e, l = _softmax_rows(s)
    p = e / jnp.maximum(l, 1e-30)
    oc_ref[0] = _dot(p.astype(BF16), vc.astype(BF16))

    nsp = cov_ref.shape[1]
    blk_row = lax.broadcasted_iota(jnp.int32, (1, nsp), 1)
    sub = lax.broadcasted_iota(jnp.int32, (nsp, nsp), 0)
    lan = lax.broadcasted_iota(jnp.int32, (nsp, nsp), 1)
    cur = qpos // SEL_BLOCK
    valid = (blk_row * SEL_BLOCK <= qpos) & (blk_row < n_sel)
    forced = valid & ((blk_row == 0) | (blk_row == cur) | (blk_row == cur - 1))
    k_col = lax.broadcasted_iota(jnp.int32, (TOP_K_BLOCKS, nsp), 0).astype(F32)
    blk_f = lax.broadcasted_iota(jnp.int32, (TOP_K_BLOCKS, nsp), 1).astype(F32)
    for g in range(N_KV_A):
        p_sum = jnp.sum(p[g * GROUP_R:(g + 1) * GROUP_R, :], axis=0, keepdims=True)
        imp = _dot_exact_rhs(jnp.broadcast_to(p_sum, (SUBLANES, nc)), cov_ref[...])[0:1, :]
        score = jnp.where(forced, -NEG, jnp.where(valid, imp, NEG))
        score_col = jnp.sum(jnp.where(sub == lan, score, 0.0), axis=-1, keepdims=True)
        ahead = jnp.where(sub < lan, jnp.where(score_col >= score, 1.0, 0.0), jnp.where(score_col > score, 1.0, 0.0))
        rank = jnp.sum(ahead, axis=0, keepdims=True)
        chosen = jnp.sum(jnp.where(rank == k_col, blk_f, 0.0), axis=-1, keepdims=True)
        idx_ref[0, g * TOP_K_BLOCKS:(g + 1) * TOP_K_BLOCKS, :] = jnp.broadcast_to(
            chosen, (TOP_K_BLOCKS, LANES)).astype(jnp.int32)


def _sample_cmp(page_table, pool, w_big, b_big, g_k, q3, bias_c, cover, n_sel, qpos):
    ns, n_pages = page_table.shape
    page_spec = lambda u: pl.BlockSpec((1, KV_ROW, PAGE_SIZE), lambda s, pt: (pt[s, u], 0, 0))
    const = lambda a: pl.BlockSpec(a.shape, lambda s, pt: (0,) * a.ndim, pipeline_mode=pl.Buffered(1))
    grid_spec = pltpu.PrefetchScalarGridSpec(
        num_scalar_prefetch=1,
        grid=(ns,),
        in_specs=[page_spec(u) for u in range(n_pages)]
        + [const(w_big), const(b_big), const(g_k),
           pl.BlockSpec((1, N_HEADS_A, LANES), lambda s, pt: (s, 0, 0)), const(bias_c), const(cover)],
        out_specs=[pl.BlockSpec((1, N_HEADS_A, LANES), lambda s, pt: (s, 0, 0)),
                   pl.BlockSpec((1, N_KV_A * TOP_K_BLOCKS, LANES), lambda s, pt: (s, 0, 0))],
        scratch_shapes=[pltpu.VMEM((n_pages * PAGE_SIZE, KV_W), F32)] * 2,
    )
    return pl.pallas_call(
        functools.partial(_sample_cmp_kernel, n_pages=n_pages, n_sel=n_sel, qpos=qpos),
        grid_spec=grid_spec,
        out_shape=[jax.ShapeDtypeStruct((ns, N_HEADS_A, LANES), F32),
                   jax.ShapeDtypeStruct((ns, N_KV_A * TOP_K_BLOCKS, LANES), jnp.int32)],
        compiler_params=_params("arbitrary"),
        name="sample_cmp_topk",
    )(page_table, *([pool] * n_pages), w_big, b_big, g_k, q3, bias_c, cover)


def _sample_attn_kernel(idx_ref, pt_ref, *refs, n_past_blocks):
    n_slots = N_KV_A * TOP_K_BLOCKS
    blocks = refs[:n_slots]
    (win_ref, q_ref, selnew_ref, winnew_ref, wint_ref, oc_ref, gate_ref, bsel_ref, bwin_ref, b0_ref,
     a_ref, winout_ref) = refs[n_slots:]
    s_id = pl.program_id(0)
    q = q_ref[0]
    qf = q.astype(F32)
    lane = lax.broadcasted_iota(jnp.int32, (N_HEADS_A, LANES), 1)
    row = lax.broadcasted_iota(jnp.int32, (N_HEADS_A, LANES), 0)
    lo = lane < SEL_BLOCK

    def with_new_key(s, vt_mat, new_row):
        s_new = jnp.sum(qf * new_row[:, 0:KV_W], axis=-1, keepdims=True) + b0_ref[:, 0:1]
        m = jnp.maximum(jnp.max(s, axis=-1, keepdims=True), s_new)
        p = jnp.exp(s - m)
        p_new = jnp.exp(s_new - m)
        l = jnp.sum(p, axis=-1, keepdims=True) + p_new
        return (_dot_nt(p.astype(BF16), vt_mat) + p_new * new_row[:, KV_W:KV_ROW]) / l

    sel_new = selnew_ref[0]
    o_s = []
    for g in range(N_KV_A):
        slots = range(g * TOP_K_BLOCKS, (g + 1) * TOP_K_BLOCKS)
        kt_cat = jnp.concatenate([blocks[k][0, 0:KV_W, :] for k in slots], axis=1).astype(BF16)
        vt_cat = jnp.concatenate([blocks[k][0, KV_W:KV_ROW, :] for k in slots], axis=1).astype(BF16)
        adds = []
        for k in slots:
            b = idx_ref[s_id * n_slots + k]
            add = jnp.where(b == n_past_blocks - 2, bsel_ref[0], jnp.where(b == n_past_blocks - 1, bsel_ref[1], 0.0))
            other_half = jnp.where(b % 2 == 0, jnp.where(lo, 0.0, NEG), jnp.where(lo, NEG, 0.0))
            adds.append(add + other_half + jnp.where(b >= n_past_blocks, NEG, 0.0))
        o_s.append(with_new_key(_dot(q, kt_cat) + jnp.concatenate(adds, axis=-1), vt_cat, sel_new))
    o_sel = jnp.where(row < GROUP_R, o_s[0], o_s[1])

    win = win_ref[0]
    wb = win.shape[1]
    w_lane = lax.broadcasted_iota(jnp.int32, (N_HEADS_A, wb), 1)
    s_w = _dot(q, win[0:KV_W, :].astype(BF16)) + bwin_ref[...] + jnp.where(w_lane == 0, NEG, 0.0)
    o_win = with_new_key(s_w, win[KV_W:KV_ROW, :].astype(BF16), winnew_ref[0])

    na = N_HEADS_A
    out = gate_ref[0, 0:na] * oc_ref[0] + gate_ref[0, na:2 * na] * o_sel + gate_ref[0, 2 * na:3 * na] * o_win
    a_ref[0] = jnp.where(lo == (row < GROUP_R), out, 0.0).astype(BF16)

    wint = wint_ref[...]
    seq_lane = lax.broadcasted_iota(jnp.int32, wint.shape, 1)
    new_col = jnp.sum(jnp.where(seq_lane == s_id, wint, 0.0), axis=-1, keepdims=True)
    buf_lane = lax.broadcasted_iota(jnp.int32, win.shape, 1)
    winout_ref[0] = jnp.where(buf_lane == wb - 1, new_col, pltpu.roll(win, wb - 1, 1))


def _sample_attn(idx, page_table, pool, win_t, q3, sel_new, win_new, win_new_t, o_c, gate_b, bsel, bwin, b0):
    ns, n_pages = page_table.shape
    n_slots = N_KV_A * TOP_K_BLOCKS
    per_page = PAGE_SIZE // SEL_BLOCK
    n_past_blocks = n_pages * per_page
    wb = win_t.shape[2]

    def slot_spec(k):
        def index(s, idx_ref, pt_ref):
            b = jnp.minimum(idx_ref[s * n_slots + k], n_past_blocks - 1)
            return (pt_ref[s, b // per_page], 0, 0)
        return pl.BlockSpec((1, KV_ROW, PAGE_SIZE), index)

    per_seq = lambda a: pl.BlockSpec((1,) + a.shape[1:], lambda s, *_: (s,) + (0,) * (a.ndim - 1))
    const = lambda a: pl.BlockSpec(a.shape, lambda s, *_: (0,) * a.ndim, pipeline_mode=pl.Buffered(1))
    grid_spec = pltpu.PrefetchScalarGridSpec(
        num_scalar_prefetch=2,
        grid=(ns,),
        in_specs=[slot_spec(k) for k in range(n_slots)]
        + [per_seq(win_t), per_seq(q3), per_seq(sel_new), per_seq(win_new), const(win_new_t), per_seq(o_c),
           per_seq(gate_b), const(bsel), const(bwin), const(b0)],
        out_specs=[pl.BlockSpec((1, N_HEADS_A, LANES), lambda s, *_: (s, 0, 0)),
                   pl.BlockSpec((1, KV_ROW, wb), lambda s, *_: (s, 0, 0))],
    )
    return pl.pallas_call(
        functools.partial(_sample_attn_kernel, n_past_blocks=n_past_blocks),
        grid_spec=grid_spec,
        out_shape=[jax.ShapeDtypeStruct((ns, N_HEADS_A, LANES), BF16),
                   jax.ShapeDtypeStruct((ns, KV_ROW, wb), F32)],
        compiler_params=_params("arbitrary"),
        name="sample_sel_win",
    )(idx, page_table, *([pool] * n_slots), win_t, q3, sel_new, win_new, win_new_t, o_c, gate_b, bsel, bwin, b0)


def _mlstm_step_kernel(qk_ref, conv_ref, v_ref, o_ref, small_ref, c_ref, n_ref, m_ref, cw_ref, cb_ref, bif_ref,
                       mn_ref, y_ref, c_out, n_out, m_out):
    seqs = qk_ref.shape[0]
    sub = lax.broadcasted_iota(jnp.int32, (HEAD_DIM_M, HEAD_DIM_M), 0)
    lan = lax.broadcasted_iota(jnp.int32, (HEAD_DIM_M, HEAD_DIM_M), 1)
    eye = sub == lan
    lane_row = lax.broadcasted_iota(jnp.int32, (1, LANES), 1)
    for s in range(seqs):
        hist = conv_ref[s]
        y = hist[0:1, :] * cw_ref[0:1, :]
        y = y + hist[1:2, :] * cw_ref[1:2, :]
        y = y + hist[2:3, :] * cw_ref[2:3, :]
        y = y + qk_ref[s] * cw_ref[3:4, :] + cb_ref[...]
        qkc = y * _sigmoid(y)
        gi = small_ref[s] + bif_ref[...]
        lf_all = _log_sigmoid(gi)
        m_all = m_ref[s]
        m_new_all = m_all
        h_all = []
        for h in range(N_HEADS_M):
            hs = slice(h * HEAD_DIM_M, (h + 1) * HEAD_DIM_M)
            ks = slice(D_M + h * HEAD_DIM_M, D_M + (h + 1) * HEAD_DIM_M)
            q = qkc[:, hs]
            k = qkc[:, ks] * F32(HEAD_DIM_M ** -0.5)
            v = v_ref[s][:, hs]
            ci, fi = 3 * N_HEADS_A + h, 3 * N_HEADS_A + N_HEADS_M + h
            ig = gi[:, ci:ci + 1]
            lf = lf_all[:, fi:fi + 1]
            m_prev = m_all[:, h:h + 1]
            c_prev = c_ref[s, h]
            n_prev = n_ref[s, h:h + 1, :]
            inter = lf + m_prev
            m_t = jnp.maximum(inter, ig)
            w_in = jnp.exp(ig - m_t)
            dec = jnp.exp(inter - m_t)
            s_qk = jnp.sum(q * k, axis=-1, keepdims=True) * w_in
            cq = _dot_nt(jnp.broadcast_to(q, (SUBLANES, HEAD_DIM_M)).astype(BF16), c_prev.astype(BF16))[0:1, :]
            num = s_qk * v + dec * cq
            den = s_qk + dec * jnp.sum(n_prev * q, axis=-1, keepdims=True)
            h_all.append(num / jnp.maximum(jnp.abs(den), jnp.exp(-m_t)))
            v_col = jnp.sum(jnp.where(eye, v, 0.0), axis=-1, keepdims=True)
            c_out[s, h] = dec * c_prev + (w_in * v_col) * k
            n_out[s, h:h + 1, :] = dec * n_prev + w_in * k
            m_new_all = jnp.where(lane_row == h, m_t, m_new_all)
        m_out[s] = m_new_all
        y_ref[s] = _head_norm_gate(h_all, mn_ref, o_ref[s]).astype(BF16)


def _mlstm_step(qk, conv0, v, o, small, c0, n0, m0, conv_w, conv_b, bif, mn_g):
    ns = qk.shape[0]
    sb = STEP_SEQS if ns % STEP_SEQS == 0 else 1
    per = lambda a: pl.BlockSpec((sb,) + a.shape[1:], lambda i: (i,) + (0,) * (a.ndim - 1))
    ins = (qk, conv0, v, o, small, c0, n0, m0)
    outs = [jax.ShapeDtypeStruct((ns, 1, D_M), BF16), jax.ShapeDtypeStruct(c0.shape, F32),
            jax.ShapeDtypeStruct(n0.shape, F32), jax.ShapeDtypeStruct(m0.shape, F32)]
    return pl.pallas_call(
        _mlstm_step_kernel,
        grid=(ns // sb,),
        in_specs=[per(a) for a in ins] + [_const_spec(a.shape) for a in (conv_w, conv_b, bif, mn_g)],
        out_specs=[per(a) for a in outs],
        out_shape=outs,
        compiler_params=_params("parallel"),
        name="mlstm_step",
    )(*ins, conv_w, conv_b, bif, mn_g)


def _cover_np(n_cmp, n_sel):
    cs = np.arange(n_cmp)[:, None] * CMP_STRIDE
    bs = np.arange(n_sel)[None, :] * SEL_BLOCK
    shared = np.clip(np.minimum(cs + CMP_LEN, bs + SEL_BLOCK) - np.maximum(cs, bs), 0, None)
    return (shared / CMP_LEN).astype(np.float32)


def _prep_in_proj(w_in):
    d = w_in.shape[0]
    o_kv = D_A
    o_gate = o_kv + 6 * KV_W
    o_m = o_gate + 3 * N_HEADS_A
    o_if = o_m + 4 * D_M
    w_kv_t = w_in[:, o_kv:o_gate].T
    w_m = w_in[:, o_m:o_if]
    w_small = jnp.concatenate([w_in[:, o_gate:o_m], w_in[:, o_if:],
                               jnp.zeros((d, LANES - 3 * N_HEADS_A - 2 * N_HEADS_M), w_in.dtype)], axis=1)
    w_t = jnp.concatenate([w_in[:, :D_A].T, w_kv_t, w_small.T], axis=0)
    return [w.astype(BF16) for w in (w_t, w_m, w_small)]


def _prep_compress(w_k, b_k, w_v, b_v):
    per_c = jnp.stack([w_k, w_k, w_v, w_v]).reshape(4, 2, CMP_STRIDE, HEAD_DIM_A, HEAD_DIM_A)
    w_big = jnp.einsum('chtde,cf->tcdhfe', per_c, jnp.eye(4, dtype=w_k.dtype))
    w_big = w_big.reshape(CMP_STRIDE, KV_ROW, 2 * KV_ROW).astype(BF16)
    pick = lambda h: jnp.concatenate([w_big[:, h * KV_W:(h + 1) * KV_W, h * KV_W:(h + 1) * KV_W],
                                      w_big[:, h * KV_W:(h + 1) * KV_W, KV_ROW + h * KV_W:KV_ROW + (h + 1) * KV_W]],
                                     axis=-1)
    w_big = jnp.stack([pick(0), pick(1)])
    b_big = jnp.concatenate([b_k, b_k, b_v, b_v]).reshape(1, KV_ROW)
    return w_big, b_big


def _bucket_tiles(t, past, wb):
    r = np.arange(LANES)
    tiles = [_t5_bucket_np(d * LANES + r[None, :] - r[:, None]) for d in range(2)]
    for i in range(t // Q_TILE):
        tiles.append(_t5_bucket_np(i * Q_TILE + r[None, :] - (CMP_STRIDE * r[:, None] + CMP_LEN - 1)))
    flat = np.arange(LANES * LANES).reshape(LANES, LANES)
    tiles.append(_t5_bucket_np(past - (CMP_STRIDE * flat + CMP_LEN - 1)))
    tiles.append(_t5_bucket_np(wb - flat))
    tiles.append(_t5_bucket_np(past - (r[:, None] * SEL_BLOCK + r[None, :] % SEL_BLOCK)))
    return np.stack(tiles).astype(np.int32)


def _near_tables(bnear):
    r = np.arange(LANES)
    tile4 = lambda a: jnp.asarray(np.tile(a, (1, GROUP_R)), F32)
    causal = tile4(np.where(r[:, None] <= r[None, :], 0.0, NEG))
    oldest = tile4(np.where(r[:, None] > r[None, :], 0.0, NEG))
    zero = jnp.zeros_like(causal)
    masked = jnp.full_like(causal, NEG)
    n_mid = WINDOW // LANES - 2
    sel_add, win_add = [], []
    for g in range(N_KV_A):
        near = [bnear[1, g], bnear[0, g] + causal]
        sel_add.append(jnp.concatenate(near + [masked], axis=0))
        win_add.append(jnp.concatenate([oldest] + [zero] * n_mid + near + [masked] * (WINDOW // LANES), axis=0))
    return jnp.stack(sel_add), jnp.stack(win_add)


def kernel(x_prompt, x_sample, cache_cmp_kv, cache_sel_kv, cache_win_kv, state_C, state_n, state_m, state_conv, page_table, rel_bias, g_ffn1, w1_gate, w1_up, w1_down, g_mix, w_in, qn_g, kn_cmp_g, kn_sel_g, kn_win_g, w_cmp_k, b_cmp_k, w_cmp_v, b_cmp_v, conv_w, conv_b, b_if, mn_g, w_out, g_ffn2, w2_gate, w2_up, w2_down):
    assert x_prompt.shape[2] == D_MODEL and g_ffn1.shape[0] == 1
    nb, t, _ = x_prompt.shape
    ns, ds, _ = x_sample.shape
    assert ds == 1 and t % Q_TILE == 0
    n_pages = page_table.shape[1]
    past = n_pages * PAGE_SIZE
    wb = cache_win_kv.shape[2]
    assert wb == WINDOW
    kv_shape = (2, N_KV_A, HEAD_DIM_A)

    row = lambda a: a.reshape(1, -1)
    two = lambda a: jnp.tile(a, 2).reshape(1, KV_W)
    w1 = (row(g_ffn1[0]), w1_gate[0].astype(BF16), w1_up[0].astype(BF16), w1_down[0].astype(BF16))
    w2 = (row(g_ffn2[0]), w2_gate[0].astype(BF16), w2_up[0].astype(BF16), w2_down[0].astype(BF16))
    wt, wm, ws = _prep_in_proj(w_in[0])
    qg = (qn_g[0] * F32(HEAD_DIM_A ** -0.5))[:, None]
    kgt = jnp.stack([jnp.tile(kn_sel_g[0], 2), jnp.tile(kn_win_g[0], 2)])[:, :, None]
    w_big, b_big = _prep_compress(w_cmp_k[0], b_cmp_k[0], w_cmp_v[0], b_cmp_v[0])
    g_cmp = two(kn_cmp_g[0])
    woa, wom = w_out[0][:D_A].astype(BF16), w_out[0][D_A:].astype(BF16)
    n_gate = 3 * N_HEADS_A
    bif = jnp.zeros((1, LANES), F32).at[0, n_gate:n_gate + 2 * N_HEADS_M].set(b_if[0])
    conv_b2 = row(conv_b[0])
    mn = row(mn_g[0])

    nqb = t // Q_TILE
    tiles = _bias_tiles(rel_bias, jnp.asarray(_bucket_tiles(t, past, wb)))
    group_lanes = lambda a: a.reshape(a.shape[0], N_KV_A, GROUP_R, LANES, LANES).transpose(0, 1, 3, 2, 4).reshape(
        a.shape[0], N_KV_A, LANES, GROUP_R * LANES)
    bnear = group_lanes(tiles[0:2]) * F32(LOG2_E)
    sel_add, win_add = _near_tables(bnear)
    bcmp = group_lanes(tiles[2:2 + nqb]) * F32(LOG2_E)
    t_cmp, t_win, t_sel = tiles[2 + nqb], tiles[3 + nqb], tiles[4 + nqb]

    def dense_in(x, seq_len, q_gain, kmask):
        x1 = _ffn(x, *w1)
        return (x1,) + tuple(_inproj(x1, seq_len, kmask, row(g_mix[0]), wt, wm, ws, q_gain, kgt))

    n_sel_p = -(-t // SEL_BLOCK)
    assert n_sel_p + t // LANES <= LANES
    kmask_np = np.zeros((t, LANES), np.float32)
    kmask_np[np.arange(t), np.arange(t) // SEL_BLOCK] = -MASK_BIG
    kmask_np[np.arange(t), n_sel_p + np.arange(t) // LANES] = -MASK_BIG

    from_chan = lambda a: a.reshape((1, a.shape[0]) + kv_shape + (a.shape[2],)).transpose(0, 1, 5, 2, 3, 4)
    to_chan = lambda a: a.transpose(0, 2, 3, 4, 1).reshape(a.shape[0], KV_ROW, a.shape[1])

    (x1, q_t, cmp_rows, cmp_t, sel_t, win_t, ks, vs, kw, vw, qk_m, v_m, o_m, small, gate_t) = dense_in(
        x_prompt.reshape(nb * t, D_MODEL), t, qg * F32(LOG2_E),
        jnp.asarray(kmask_np, BF16))
    seq = lambda a: a.reshape(nb, t, a.shape[-1])
    n16 = t // CMP_STRIDE
    assert n16 == LANES
    kc, vc_t = _compress(cmp_rows, t, w_big, b_big, g_cmp)
    n_sel = -(-t // SEL_BLOCK)
    cov_t = jnp.asarray(np.pad(_cover_np(n16 - 1, n_sel), ((0, 1), (0, 0))).T, BF16)
    a_pad = _nsa_prompt(q_t, gate_t, kc, vc_t, ks, vs, kw, vw, sel_add, win_add, bcmp, cov_t, n16 - 1)
    chunk = MLSTM_CHUNK if t % MLSTM_CHUNK == 0 else Q_TILE
    m_out, c_p, n_p, m_p = _mlstm_prompt(seq(qk_m), seq(v_m), seq(o_m), seq(small), conv_w[0], conv_b2, bif, mn,
                                         chunk)
    y_prompt = _post(x1, a_pad.reshape(nb * t, -1), m_out.reshape(nb * t, -1), woa, wom, *w2)
    kv6 = lambda a, n: a.reshape((1, n, -1) + kv_shape)
    prompt_states = (from_chan(cmp_t), from_chan(sel_t), from_chan(win_t[:, :, t - wb:]), c_p[None], n_p[None],
                     m_p[None, :, 0, :N_HEADS_M], seq(qk_m)[None, :, t - (CONV_W - 1):])

    (x1s, q_st, cmp_halves, _, sel_st, win_st, _, _, _, _, qk_s, v_s, o_s, small_s, gate_st) = dense_in(
        x_sample.reshape(ns, D_MODEL), ns, qg, jnp.zeros((ns, LANES), BF16))
    cmp_s = jnp.concatenate([cmp_halves[0], cmp_halves[1]], axis=1)
    sel_s, win_s, gates_s = sel_st[0].T, win_st[0].T, gate_st[0].T
    q3 = q_st[0].T.reshape(ns, N_HEADS_A, LANES)
    n16s = past // CMP_STRIDE
    n_sel_s = -(-(past + 1) // SEL_BLOCK)
    n_sel_pad = -(-n_sel_s // LANES) * LANES
    cover_s = jnp.asarray(np.pad(_cover_np(n16s - 1, n_sel_s), ((0, 1), (0, n_sel_pad - n_sel_s))), BF16)
    bias_c = t_cmp[:, :n16s // LANES, :].reshape(N_HEADS_A, n16s)
    o_c, idx = _sample_cmp(page_table, to_chan(cache_cmp_kv[0]), w_big, b_big, g_cmp, q3, bias_c, cover_s, n_sel_s,
                           past)
    n_past_blocks = past // SEL_BLOCK
    two_lanes = lambda a: jnp.concatenate([a[:, :SEL_BLOCK], a[:, :SEL_BLOCK]], axis=-1)
    bsel = jnp.stack([two_lanes(t_sel[:, n_past_blocks - 2, :]), two_lanes(t_sel[:, n_past_blocks - 1, :])])
    bwin = t_win[:, :wb // LANES, :].reshape(N_HEADS_A, wb)
    b0 = jnp.broadcast_to(t_win[:, wb // LANES, 0:1], (N_HEADS_A, LANES))
    gate_b = jnp.broadcast_to(gates_s[:, :n_gate, None], (ns, n_gate, LANES))
    a_s, win_out = _sample_attn(idx[:, :, 0].reshape(-1), page_table, to_chan(cache_sel_kv[0]),
                                to_chan(cache_win_kv[0]), q3, sel_s[:, None, :], win_s[:, None, :], win_st[0],
                                o_c, gate_b, bsel, bwin, b0)
    m0 = jnp.pad(state_m[0], ((0, 0), (0, LANES - N_HEADS_M)))[:, None, :]
    m_s, c_s, n_s, m_new = _mlstm_step(qk_s[:, None, :], state_conv[0], v_s[:, None, :], o_s[:, None, :],
                                       small_s[:, None, :], state_C[0], state_n[0], m0, conv_w[0], conv_b2, bif, mn)
    a_dense = jnp.concatenate([a_s[:, h, (h // GROUP_R) * HEAD_DIM_A:(h // GROUP_R + 1) * HEAD_DIM_A]
                               for h in range(N_HEADS_A)], axis=1)
    y_sample = _post(x1s, a_dense, m_s.reshape(ns, -1), woa, wom, *w2)
    conv_new = jnp.concatenate([state_conv[0][:, 1:], qk_s[:, None, :]], axis=1)
    sample_states = (kv6(cmp_s, ns), kv6(sel_s, ns), from_chan(win_out), c_s[None], n_s[None],
                     m_new[None, :, 0, :N_HEADS_M], conv_new[None])

    return (y_prompt.reshape(nb, t, D_MODEL), y_sample.reshape(ns, 1, D_MODEL)) + prompt_states + sample_states
```

```python
import functools
import math

import jax
import jax.numpy as jnp
import numpy as np
from jax import lax
from jax.experimental import pallas as pl
from jax.experimental.pallas import tpu as pltpu

F32 = jnp.float32
BF16 = jnp.bfloat16

D_MODEL = 1024
PAGE_SIZE = 128
N_HEADS_A = 8
HEAD_DIM_A = 64
N_KV_A = 2
GROUP_R = N_HEADS_A // N_KV_A
D_A = N_HEADS_A * HEAD_DIM_A
KV_W = N_KV_A * HEAD_DIM_A
CMP_STRIDE = 16
CMP_LEN = 2 * CMP_STRIDE
SEL_BLOCK = 64
TOP_K_BLOCKS = 16
WINDOW = 512
N_HEADS_M = 4
HEAD_DIM_M = 128
D_M = N_HEADS_M * HEAD_DIM_M
CONV_W = 4
N_BUCKETS = 32
MAX_DISTANCE = 128
EPS = 1e-6

LANES = 128
SUBLANES = 8
TOKEN_TILE = 512
MLSTM_CHUNK = 256
STEP_SEQS = 8
Q_TILE = 128
Q_BLOCKS = 2
V_ROWS = HEAD_DIM_A + 16
FAR_TILES = 4
KV_ROW = 2 * KV_W
LOG2_E = 1.4426950408889634
NEG = -1e30
MASK_BIG = 2.0 ** 100
VMEM_LIMIT = 56 * 1024 * 1024

NT_DIMS = (((1,), (1,)), ((), ()))


def _dot(a, b):
    return jnp.dot(a, b, preferred_element_type=F32)


def _dot_nt(a, b):
    return lax.dot_general(a, b, NT_DIMS, preferred_element_type=F32)


def _split3(x):
    x1 = x.astype(BF16)
    r1 = x - x1.astype(F32)
    x2 = r1.astype(BF16)
    x3 = (r1 - x2.astype(F32)).astype(BF16)
    return x1, x2, x3


def _dot_exact_rhs(x, m):
    x1, x2, x3 = _split3(x)
    return _dot(x1, m) + _dot(x2, m) + _dot(x3, m)


def _dot_exact_lhs(m, x):
    x1, x2, x3 = _split3(x)
    return _dot(m, x1) + _dot(m, x2) + _dot(m, x3)


def _rms_rows(x, g):
    ms = jnp.mean(x * x, axis=-1, keepdims=True)
    return x * lax.rsqrt(ms + EPS) * g


def _rms_two_groups(k, g):
    sq = k * k
    lane = lax.broadcasted_iota(jnp.int32, sq.shape, 1)
    lo = lane < HEAD_DIM_A
    s0 = jnp.sum(jnp.where(lo, sq, 0.0), axis=-1, keepdims=True)
    s1 = jnp.sum(jnp.where(lo, 0.0, sq), axis=-1, keepdims=True)
    ms = jnp.where(lo, s0, s1) * (1.0 / HEAD_DIM_A)
    return k * lax.rsqrt(ms + EPS) * g


def _sigmoid(x):
    return jax.nn.sigmoid(x)


def _log_sigmoid(x):
    return jnp.minimum(x, 0.0) - jnp.log1p(jnp.exp(-jnp.abs(x)))


def _const_spec(shape):
    nd = len(shape)
    return pl.BlockSpec(shape, lambda *_: (0,) * nd, pipeline_mode=pl.Buffered(1))


def _params(*sem):
    return pltpu.CompilerParams(dimension_semantics=sem, vmem_limit_bytes=VMEM_LIMIT)


def _swiglu_residual(x, g_ref, wg_ref, wu_ref, wd_ref):
    xn = _rms_rows(x, g_ref[...]).astype(BF16)
    d_ff = wg_ref.shape[1]
    n_split = 2 if d_ff % (2 * LANES) == 0 else 1
    step = d_ff // n_split
    acc = jnp.zeros_like(x)
    for c in range(n_split):
        hg = _dot(xn, wg_ref[:, c * step:(c + 1) * step])
        hu = _dot(xn, wu_ref[:, c * step:(c + 1) * step])
        h = (hg * _sigmoid(hg)) * hu
        acc = acc + _dot(h.astype(BF16), wd_ref[c * step:(c + 1) * step, :])
    return x + 0.5 * acc


def _ffn_kernel(x_ref, g_ref, wg_ref, wu_ref, wd_ref, y_ref):
    y_ref[...] = _swiglu_residual(x_ref[...], g_ref, wg_ref, wu_ref, wd_ref)


def _post_kernel(x_ref, a_ref, m_ref, woa_ref, wom_ref, g_ref, wg_ref, wu_ref, wd_ref, y_ref):
    x = x_ref[...] + (_dot(a_ref[...], woa_ref[...]) + _dot(m_ref[...], wom_ref[...]))
    y_ref[...] = _swiglu_residual(x, g_ref, wg_ref, wu_ref, wd_ref)


def _token_tile(n):
    return TOKEN_TILE if n % TOKEN_TILE == 0 else n


def _ffn(x, g, wg, wu, wd):
    n = x.shape[0]
    tm = _token_tile(n)
    row = lambda w: pl.BlockSpec((tm, w), lambda i: (i, 0))
    return pl.pallas_call(
        _ffn_kernel,
        grid=(n // tm,),
        in_specs=[row(D_MODEL), _const_spec(g.shape), _const_spec(wg.shape), _const_spec(wu.shape),
                  _const_spec(wd.shape)],
        out_specs=row(D_MODEL),
        out_shape=jax.ShapeDtypeStruct((n, D_MODEL), F32),
        compiler_params=_params("parallel"),
        name="ffn1",
    )(x, g, wg, wu, wd)


def _post(x, a_pad, m_out, woa, wom, g, wg, wu, wd):
    n = x.shape[0]
    tm = _token_tile(n)
    row = lambda w: pl.BlockSpec((tm, w), lambda i: (i, 0))
    return pl.pallas_call(
        _post_kernel,
        grid=(n // tm,),
        in_specs=[row(D_MODEL), row(a_pad.shape[1]), row(m_out.shape[1]), _const_spec(woa.shape),
                  _const_spec(wom.shape), _const_spec(g.shape), _const_spec(wg.shape), _const_spec(wu.shape),
                  _const_spec(wd.shape)],
        out_specs=row(D_MODEL),
        out_shape=jax.ShapeDtypeStruct((n, D_MODEL), F32),
        compiler_params=_params("parallel"),
        name="post_mix_ffn2",
    )(x, a_pad, m_out, woa, wom, g, wg, wu, wd)


def _rms_two_groups_t(k, g_col):
    sq = k * k
    half = k.shape[0] // 2
    s0 = jnp.sum(sq[0:half], axis=0, keepdims=True)
    s1 = jnp.sum(sq[half:], axis=0, keepdims=True)
    ms = jnp.concatenate([jnp.broadcast_to(s0, (half, k.shape[1])), jnp.broadcast_to(s1, (half, k.shape[1]))],
                         axis=0) * (1.0 / HEAD_DIM_A)
    return k * lax.rsqrt(ms + EPS) * g_col


def _inproj_kernel(x_ref, kmask_ref, g_ref, wt_ref, wm_ref, ws_ref, qg_ref, kgt_ref,
                   q_out, cmp_rows, cmpt_out, selt_out, wint_out, kst, vst, kwt, vwt, qk_out, v_out, o_out,
                   small_out, gate_out):
    xn = _rms_rows(x_ref[...], g_ref[...]).astype(BF16)
    all_t = _dot_nt(wt_ref[...], xn)
    qt = all_t[0:D_A, :]
    zeros = jnp.zeros((HEAD_DIM_A, qt.shape[1]), BF16)
    for h in range(N_HEADS_A):
        qh = qt[h * HEAD_DIM_A:(h + 1) * HEAD_DIM_A, :]
        ms = jnp.sum(qh * qh, axis=0, keepdims=True) * (1.0 / HEAD_DIM_A)
        own = h * LANES + (h // GROUP_R) * HEAD_DIM_A
        other = h * LANES + (1 - h // GROUP_R) * HEAD_DIM_A
        q_out[0, own:own + HEAD_DIM_A, :] = (qh * lax.rsqrt(ms + EPS) * qg_ref[...]).astype(BF16)
        q_out[0, other:other + HEAD_DIM_A, :] = zeros
    kvt = all_t[D_A:D_A + 6 * KV_W, :]
    cmpt_out[0] = kvt[0:KV_ROW]
    cmp_rows[0] = kvt[0:KV_W].T
    cmp_rows[1] = kvt[KV_W:KV_ROW].T
    ks = _rms_two_groups_t(kvt[2 * KV_W:3 * KV_W], kgt_ref[0])
    vs = kvt[3 * KV_W:4 * KV_W]
    kw = _rms_two_groups_t(kvt[4 * KV_W:5 * KV_W], kgt_ref[1])
    vw = kvt[5 * KV_W:6 * KV_W]
    selt_out[0, 0:KV_W] = ks
    selt_out[0, KV_W:KV_ROW] = vs
    wint_out[0, 0:KV_W] = kw
    wint_out[0, KV_W:KV_ROW] = vw
    ones = jnp.ones((V_ROWS - HEAD_DIM_A, LANES), F32)
    for j in range(kst.shape[1]):
        cols = slice(j * LANES, (j + 1) * LANES)
        kst[0, j] = jnp.concatenate([ks[:, cols].T.astype(BF16), kmask_ref[cols, :]], axis=1)
        kwt[0, j] = kw[:, cols].T.astype(BF16)
        for v, out in ((vs, vst), (vw, vwt)):
            for g in range(N_KV_A):
                out[0, j, g] = jnp.concatenate([v[g * HEAD_DIM_A:(g + 1) * HEAD_DIM_A, cols], ones],
                                               axis=0).astype(BF16)
    m = _dot(xn, wm_ref[...])
    qk_out[...] = m[:, 0:2 * D_M]
    v_out[...] = m[:, 2 * D_M:3 * D_M]
    o_out[...] = m[:, 3 * D_M:4 * D_M]
    small_out[...] = _dot(xn, ws_ref[...])
    gate_out[0] = _sigmoid(all_t[D_A + 6 * KV_W:, :])


def _inproj(x, seq_len, kmask, g, wt, wm, ws, qg, kgt):
    n = x.shape[0]
    nb = n // seq_len
    tm = _token_tile(seq_len)
    tpb = seq_len // tm
    row = lambda w: pl.BlockSpec((tm, w), lambda i: (i, 0))
    rows_out = lambda w, dt: (row(w), jax.ShapeDtypeStruct((n, w), dt))
    chan_out = lambda c, dt: (pl.BlockSpec((1, c, tm), lambda i: (i // tpb, 0, i % tpb)),
                              jax.ShapeDtypeStruct((nb, c, seq_len), dt))
    k_tiles = (pl.BlockSpec((1, tm // LANES, LANES, KV_W), lambda i: (i // tpb, i % tpb, 0, 0)),
               jax.ShapeDtypeStruct((nb, seq_len // LANES, LANES, KV_W), BF16))
    ksel_tiles = (pl.BlockSpec((1, tm // LANES, LANES, 2 * KV_W), lambda i: (i // tpb, i % tpb, 0, 0)),
                  jax.ShapeDtypeStruct((nb, seq_len // LANES, LANES, 2 * KV_W), BF16))
    v_tiles = (pl.BlockSpec((1, tm // LANES, N_KV_A, V_ROWS, LANES), lambda i: (i // tpb, i % tpb, 0, 0, 0)),
               jax.ShapeDtypeStruct((nb, seq_len // LANES, N_KV_A, V_ROWS, LANES), BF16))
    halves_out = (pl.BlockSpec((2, tm, KV_W), lambda i: (0, i, 0)), jax.ShapeDtypeStruct((2, n, KV_W), F32))
    outs = [chan_out(N_HEADS_A * LANES, BF16), halves_out, chan_out(KV_ROW, F32), chan_out(KV_ROW, F32),
            chan_out(KV_ROW, F32), ksel_tiles, v_tiles, k_tiles, v_tiles,
            rows_out(2 * D_M, F32), rows_out(D_M, F32), rows_out(D_M, F32), rows_out(LANES, F32),
            chan_out(LANES, F32)]
    consts = (g, wt, wm, ws, qg, kgt)
    return pl.pallas_call(
        _inproj_kernel,
        grid=(n // tm,),
        in_specs=[row(D_MODEL), pl.BlockSpec((tm, LANES), lambda i: (i % tpb, 0))]
        + [_const_spec(a.shape) for a in consts],
        out_specs=[o[0] for o in outs],
        out_shape=[o[1] for o in outs],
        compiler_params=_params("parallel"),
        name="in_proj",
    )(x, kmask, *consts)


def _t5_bucket_np(dist):
    n = np.maximum(dist, 0)
    exact = N_BUCKETS // 2
    nf = np.maximum(n, 1).astype(np.float32)
    ratio = np.log(nf / np.float32(exact)) / np.float32(math.log(MAX_DISTANCE / exact))
    large = exact + (ratio * np.float32(N_BUCKETS - exact)).astype(np.int32)
    return np.where(n < exact, n, np.minimum(large, N_BUCKETS - 1)).astype(np.int32)


def _bias_kernel(tbl_ref, bkt_ref, out_ref):
    bkt = bkt_ref[0]
    for h in range(N_HEADS_A):
        far = tbl_ref[N_BUCKETS - 1, h]
        acc = jnp.zeros(bkt.shape, F32)
        for b in range(N_BUCKETS - 1):
            acc = jnp.where(bkt == b, tbl_ref[b, h] - far, acc)
        out_ref[0, h] = acc


def _bias_tiles(rel_bias, buckets):
    n = buckets.shape[0]
    return pl.pallas_call(
        _bias_kernel,
        grid=(n,),
        in_specs=[pl.BlockSpec(memory_space=pltpu.SMEM),
                  pl.BlockSpec((1, LANES, LANES), lambda i: (i, 0, 0))],
        out_specs=pl.BlockSpec((1, N_HEADS_A, LANES, LANES), lambda i: (i, 0, 0, 0)),
        out_shape=jax.ShapeDtypeStruct((n, N_HEADS_A, LANES, LANES), F32),
        compiler_params=_params("parallel"),
        name="bias_tiles",
    )(rel_bias, buckets)


def _compress_half(rows_ref, nc, w_ref, half, bias):
    fs = None
    for t in range(CMP_STRIDE):
        part = _dot(rows_ref[pl.ds(t, nc, stride=CMP_STRIDE), :].astype(BF16), w_ref[half, t])
        fs = part if fs is None else fs + part
    return fs[:, 0:KV_W] + pltpu.roll(fs[:, KV_W:KV_ROW], nc - 1, 0) + bias


def _compress_rows(k_rows_ref, v_rows_ref, nc, w_ref, b_ref, g_ref):
    kc = _rms_two_groups(_compress_half(k_rows_ref, nc, w_ref, 0, b_ref[:, 0:KV_W]), g_ref[...])
    vc = _compress_half(v_rows_ref, nc, w_ref, 1, b_ref[:, KV_W:KV_ROW])
    complete = lax.broadcasted_iota(jnp.int32, kc.shape, 0) < nc - 1
    return jnp.where(complete, kc, 0.0), jnp.where(complete, vc, 0.0)


def _compress_kernel(x_ref, w_ref, b_ref, g_ref, kc_ref, vc_ref):
    nc = kc_ref.shape[1]
    kc, vc = _compress_rows(x_ref.at[0], x_ref.at[1], nc, w_ref, b_ref, g_ref)
    kc_ref[0] = kc.astype(BF16)
    vc_ref[0] = vc.T.astype(BF16)


def _compress(cmp_rows, t, w_big, b_big, g_k):
    nb = cmp_rows.shape[1] // t
    nc = t // CMP_STRIDE
    return pl.pallas_call(
        _compress_kernel,
        grid=(nb,),
        in_specs=[pl.BlockSpec((2, t, KV_W), lambda b: (0, b, 0)), _const_spec(w_big.shape),
                  _const_spec(b_big.shape), _const_spec(g_k.shape)],
        out_specs=[pl.BlockSpec((1, nc, KV_W), lambda b: (b, 0, 0)),
                   pl.BlockSpec((1, KV_W, nc), lambda b: (b, 0, 0))],
        out_shape=[jax.ShapeDtypeStruct((nb, nc, KV_W), BF16), jax.ShapeDtypeStruct((nb, KV_W, nc), BF16)],
        compiler_params=_params("parallel"),
        name="compress_prompt",
    )(cmp_rows, w_big, b_big, g_k)


def _tile_heads(x):
    return jnp.concatenate([x] * GROUP_R, axis=1)


def _softmax_part(m, s):
    m_new = jnp.maximum(m, jnp.max(s, axis=0, keepdims=True))
    return m_new, jnp.exp2(m - m_new), jnp.exp2(s - m_new).astype(BF16)


def _accumulate(acc, soft, vt):
    m_new, alpha, p = soft
    return m_new, alpha * acc + _dot(vt, p)


def _key_rows(ref, t, n):
    return ref[0, pl.ds(t, n)].reshape(n * LANES, ref.shape[-1])


def _value_cols(ref, t, n, g):
    return jnp.concatenate([ref[0, t + k, g] for k in range(n)], axis=1)


def _rank_rows(score):
    n_blk = score.shape[0]
    blk = lax.broadcasted_iota(jnp.int32, score.shape, 0)
    rank = jnp.zeros(score.shape, F32)
    for b in range(n_blk):
        row = score[b:b + 1, :]
        rank = rank + jnp.where(blk > b, jnp.where(row >= score, 1.0, 0.0), jnp.where(row > score, 1.0, 0.0))
    return rank


def _nsa_prompt_kernel(q_ref, gate_ref, kc_ref, vct_ref, ks_ref, vs_ref, kw_ref, vw_ref, sel_add_ref, win_add_ref,
                       bcmp_ref, covt_ref, a_ref, *, n_cmp):
    ii = pl.program_id(1)
    n_tiles = ks_ref.shape[1]
    n_sel = covt_ref.shape[0]
    wide = GROUP_R * Q_TILE
    heads_of = lambda g: range(g * GROUP_R, (g + 1) * GROUP_R)
    chains =[(sb, g) for sb in range(Q_BLOCKS) for g in range(N_KV_A)]
    blk_i = [Q_BLOCKS * ii + sb for sb in range(Q_BLOCKS)]
    cols = lambda sb: slice(sb * Q_TILE, (sb + 1) * Q_TILE)
    qg_t = {(sb, g): jnp.concatenate([q_ref[0, h * LANES:(h + 1) * LANES, cols(sb)] for h in heads_of(g)], axis=1)
            for sb, g in chains}
    init = (jnp.full((1, wide), NEG, F32), jnp.zeros((V_ROWS, wide), F32))
    normalised = lambda acc: acc[0:HEAD_DIM_A, :] / acc[HEAD_DIM_A:HEAD_DIM_A + 1, :]

    n_win = WINDOW // LANES + 1
    t0 = [jnp.maximum(i - (n_win - 1), 0) for i in blk_i]
    s_win = {}
    for sb, i in enumerate(blk_i):
        win_off = pl.multiple_of(jnp.maximum(n_win - 1 - i, 0) * LANES, LANES)
        k_win = _key_rows(kw_ref, t0[sb], n_win)
        for g in range(N_KV_A):
            s_win[sb, g] = _dot(k_win, qg_t[sb, g]) + win_add_ref[g, pl.ds(win_off, n_win * LANES), :]

    ncp = kc_ref.shape[1]
    j_sub = lax.broadcasted_iota(jnp.int32, (ncp, wide), 0)
    q_in_tile = lax.broadcasted_iota(jnp.int32, (ncp, wide), 1) & (Q_TILE - 1)
    s_cmp, any_usable = {}, []
    for sb, i in enumerate(blk_i):
        q_lane = i * Q_TILE + q_in_tile
        usable_add = jnp.where((j_sub * CMP_STRIDE + (CMP_LEN - 1) <= q_lane) & (j_sub < n_cmp), 0.0, NEG)
        any_usable.append(jnp.where(q_lane[0:1, :] >= CMP_LEN - 1, 1.0, 0.0))
        for g in range(N_KV_A):
            s_cmp[sb, g] = _dot(kc_ref[0], qg_t[sb, g]) + bcmp_ref[sb, g] + usable_add
    p_c = {}
    for sb, g in chains:
        e = jnp.exp2(s_cmp[sb, g] - jnp.max(s_cmp[sb, g], axis=0, keepdims=True)) * any_usable[sb]
        p_c[sb, g] = e / jnp.maximum(jnp.sum(e, axis=0, keepdims=True), 1e-30)
    oc_t = {c: _dot(vct_ref[0], p_c[c].astype(BF16)) for c in chains}

    blk = lax.broadcasted_iota(jnp.int32, (n_sel, Q_TILE), 0)
    q_in_blk = lax.broadcasted_iota(jnp.int32, (n_sel, Q_TILE), 1)
    tile_row = lax.broadcasted_iota(jnp.int32, (n_tiles, wide), 0)
    pad_rows = jnp.zeros((LANES - n_sel - n_tiles, wide), BF16)
    q_far, q_near = {}, {}
    for sb, g in chains:
        qpos = blk_i[sb] * Q_TILE + q_in_blk
        cur = qpos // SEL_BLOCK
        valid = blk * SEL_BLOCK <= qpos
        forced = valid & ((blk == 0) | (blk == cur) | (blk == cur - 1))
        p_sum = p_c[sb, g][:, 0:Q_TILE]
        for r in range(1, GROUP_R):
            p_sum = p_sum + p_c[sb, g][:, r * Q_TILE:(r + 1) * Q_TILE]
        imp_t = _dot_exact_lhs(covt_ref[...], p_sum)
        score = jnp.where(forced, -NEG, jnp.where(valid, imp_t, NEG))
        dropped = _tile_heads(jnp.where(_rank_rows(score) < float(min(TOP_K_BLOCKS, n_sel)), 0.0, 1.0).astype(BF16))
        past_far = jnp.where(tile_row > blk_i[sb] - 2, 1.0, 0.0).astype(BF16)
        q_far[sb, g] = jnp.concatenate([qg_t[sb, g], dropped, past_far, pad_rows], axis=0)
        q_near[sb, g] = jnp.concatenate([qg_t[sb, g], dropped, jnp.zeros_like(past_far), pad_rows], axis=0)

    soft_win = {c: _softmax_part(init[0], s_win[c]) for c in chains}
    ow_acc = {(sb, g): _dot(_value_cols(vw_ref, t0[sb], n_win, g), soft_win[sb, g][2]) for sb, g in chains}

    def far_body(it, carry):
        t = FAR_TILES * it
        k_far = _key_rows(ks_ref, t, FAR_TILES)
        s = {c: _dot(k_far, q_far[c]) for c in chains}
        soft = [_softmax_part(carry[n][0], s[c]) for n, c in enumerate(chains)]
        return tuple(_accumulate(carry[n][1], soft[n], _value_cols(vs_ref, t, FAR_TILES, c[1]))
                     for n, c in enumerate(chains))

    n_far = blk_i[-1] - 1
    carry = lax.fori_loop(0, (n_far + FAR_TILES - 1) // FAR_TILES, far_body, (init,) * len(chains))

    t1 = [jnp.maximum(i - 1, 0) for i in blk_i]
    s_near = {}
    for sb, i in enumerate(blk_i):
        k_near = _key_rows(ks_ref, t1[sb], 2)
        near_off = pl.multiple_of(jnp.where(i == 0, LANES, 0), LANES)
        for g in range(N_KV_A):
            s_near[sb, g] = _dot(k_near, q_near[sb, g]) + sel_add_ref[g, pl.ds(near_off, 2 * LANES), :]
    soft_near = [_softmax_part(carry[n][0], s_near[c]) for n, c in enumerate(chains)]
    for n, (sb, g) in enumerate(chains):
        _, acc = _accumulate(carry[n][1], soft_near[n], _value_cols(vs_ref, t1[sb], 2, g))
        heads = heads_of(g)
        gate_row = lambda br: jnp.concatenate(
            [gate_ref[0, br * N_HEADS_A + h:br * N_HEADS_A + h + 1, cols(sb)] for h in heads], axis=1)
        out_t = (gate_row(0) * oc_t[sb, g][g * HEAD_DIM_A:(g + 1) * HEAD_DIM_A, :] + gate_row(1) * normalised(acc)
                 + gate_row(2) * normalised(ow_acc[sb, g]))
        for pair in range(GROUP_R // 2):
            two = jnp.concatenate([out_t[:, (2 * pair + k) * Q_TILE:(2 * pair + k + 1) * Q_TILE] for k in range(2)],
                                  axis=0)
            col = (g * GROUP_R // 2 + pair) * LANES
            a_ref[0, cols(sb), col:col + LANES] = two.T.astype(BF16)


def _nsa_prompt(q_t, gate_t, kc, vc_t, ks, vs, kw, vw, sel_add, win_add, bcmp, cov_t, n_cmp):
    nb, _, t = q_t.shape
    nqb = t // Q_TILE
    assert nqb > WINDOW // LANES and nqb % Q_BLOCKS == 0
    step = Q_BLOCKS * Q_TILE
    seq = lambda a: pl.BlockSpec((1,) + a.shape[1:], lambda b, i: (b,) + (0,) * (a.ndim - 1))
    return pl.pallas_call(
        functools.partial(_nsa_prompt_kernel, n_cmp=n_cmp),
        grid=(nb, nqb // Q_BLOCKS),
        in_specs=[pl.BlockSpec((1, N_HEADS_A * LANES, step), lambda b, i: (b, 0, i)),
                  pl.BlockSpec((1, LANES, step), lambda b, i: (b, 0, i)),
                  seq(kc), seq(vc_t), seq(ks), seq(vs), seq(kw), seq(vw),
                  _const_spec(sel_add.shape), _const_spec(win_add.shape),
                  pl.BlockSpec((Q_BLOCKS,) + bcmp.shape[1:], lambda b, i: (i, 0, 0, 0)),
                  _const_spec(cov_t.shape)],
        out_specs=pl.BlockSpec((1, step, D_A), lambda b, i: (b, i, 0)),
        out_shape=jax.ShapeDtypeStruct((nb, t, D_A), BF16),
        compiler_params=_params("parallel", "arbitrary"),
        name="nsa_prompt",
    )(q_t, gate_t, kc, vc_t, ks, vs, kw, vw, sel_add, win_add, bcmp, cov_t)


def _head_norm_gate(h_all, mn_ref, o):
    outs = []
    for h in range(N_HEADS_M):
        hs = slice(h * HEAD_DIM_M, (h + 1) * HEAD_DIM_M)
        outs.append(_rms_rows(h_all[h], mn_ref[:, hs]))
    return _sigmoid(o) * jnp.concatenate(outs, axis=-1)


def _mlstm_prompt_kernel(qk_ref, v_ref, o_ref, small_ref, cw_ref, cb_ref, bif_ref, mn_ref, tri_ref, triu_ref,
                         y_ref, c_ref, n_ref, m_ref, xbuf):
    chunk = qk_ref.shape[1]
    pad = SUBLANES

    @pl.when(pl.program_id(1) == 0)
    def _():
        xbuf[...] = jnp.zeros_like(xbuf)
        c_ref[...] = jnp.zeros_like(c_ref)
        n_ref[...] = jnp.zeros_like(n_ref)
        m_ref[...] = jnp.zeros_like(m_ref)

    x = qk_ref[0]
    tail = xbuf[...]
    head_row = lax.broadcasted_iota(jnp.int32, tail.shape, 0)

    def delayed(s):
        rolled = pltpu.roll(x, s, 0)
        head = jnp.where(head_row < s, pltpu.roll(tail, s, 0), rolled[0:pad, :])
        return jnp.concatenate([head, rolled[pad:, :]], axis=0)

    y = delayed(3) * cw_ref[0:1, :]
    y = y + delayed(2) * cw_ref[1:2, :]
    y = y + delayed(1) * cw_ref[2:3, :]
    y = y + x * cw_ref[3:4, :] + cb_ref[...]
    qkc = y * _sigmoid(y)
    xbuf[...] = x[chunk - pad:chunk, :]

    gi = small_ref[0] + bif_ref[...]
    n_gate = 3 * N_HEADS_A
    gate_rows = gi.T[n_gate:n_gate + 2 * N_HEADS_M, :]
    b_col = _dot_exact_lhs(tri_ref[...], _log_sigmoid(gi))
    b_row = _dot_exact_rhs(_log_sigmoid(gate_rows), triu_ref[...])
    t_col = lax.broadcasted_iota(jnp.int32, (chunk, chunk), 0)
    s_row = lax.broadcasted_iota(jnp.int32, (chunk, chunk), 1)
    causal = s_row <= t_col
    m_all = m_ref[0]
    lane_row = lax.broadcasted_iota(jnp.int32, m_all.shape, 1)

    h_all = []
    for h in range(N_HEADS_M):
        hs = slice(h * HEAD_DIM_M, (h + 1) * HEAD_DIM_M)
        ks = slice(D_M + h * HEAD_DIM_M, D_M + (h + 1) * HEAD_DIM_M)
        q = qkc[:, hs]
        k = qkc[:, ks] * F32(HEAD_DIM_M ** -0.5)
        v = v_ref[0, :, hs]
        qb, kb = q.astype(BF16), k.astype(BF16)
        ci, fi = 3 * N_HEADS_A + h, 3 * N_HEADS_A + N_HEADS_M + h
        bt = b_col[:, fi:fi + 1]
        bs = b_row[N_HEADS_M + h:N_HEADS_M + h + 1, :]
        ig_row = gate_rows[h:h + 1, :]
        ig_col = gi[:, ci:ci + 1]
        m_prev = m_all[:, h:h + 1]
        c_prev = c_ref[0, h]
        n_prev = n_ref[0, h:h + 1, :]

        dlog = jnp.where(causal, bt - bs + ig_row, NEG)
        inter = bt + m_prev
        m_t = jnp.maximum(inter, jnp.max(dlog, axis=-1, keepdims=True))
        s_qk = _dot_nt(qb, kb) * jnp.exp(dlog - m_t)
        dec = jnp.exp(inter - m_t)
        num = _dot(s_qk.astype(BF16), v.astype(BF16)) + dec * _dot_nt(qb, c_prev.astype(BF16))
        den = jnp.sum(s_qk, axis=-1, keepdims=True) + dec * jnp.sum(q * n_prev, axis=-1, keepdims=True)
        h_all.append(num / jnp.maximum(jnp.abs(den), jnp.exp(-m_t)))

        m_new = m_t[chunk - 1:chunk, :]
        b_end = bt[chunk - 1:chunk, :]
        w_end = jnp.exp(b_end - bt + ig_col - m_new)
        dec_end = jnp.exp(b_end + m_prev - m_new)
        wv_t = (w_end * v).T.astype(BF16)
        c_ref[0, h] = dec_end * c_prev + _dot(wv_t, kb)
        n_ref[0, h:h + 1, :] = dec_end * n_prev + jnp.sum(w_end * k, axis=0, keepdims=True)
        m_all = jnp.where(lane_row == h, m_new, m_all)

    m_ref[0] = m_all
    y_ref[0] = _head_norm_gate(h_all, mn_ref, o_ref[0]).astype(BF16)


def _mlstm_prompt(qk, v, o, small, conv_w, conv_b, bif, mn_g, chunk):
    nb, t, _ = qk.shape
    tri = jnp.asarray(np.tril(np.ones((chunk, chunk), np.float32)), BF16)
    triu = jnp.asarray(np.triu(np.ones((chunk, chunk), np.float32)), BF16)
    tok = lambda w: pl.BlockSpec((1, chunk, w), lambda b, c: (b, c, 0))
    return pl.pallas_call(
        _mlstm_prompt_kernel,
        grid=(nb, t // chunk),
        in_specs=[tok(2 * D_M), tok(D_M), tok(D_M), tok(LANES)]
        + [_const_spec(a.shape) for a in (conv_w, conv_b, bif, mn_g, tri, triu)],
        out_specs=[tok(D_M),
                   pl.BlockSpec((1, N_HEADS_M, HEAD_DIM_M, HEAD_DIM_M), lambda b, c: (b, 0, 0, 0)),
                   pl.BlockSpec((1, N_HEADS_M, HEAD_DIM_M), lambda b, c: (b, 0, 0)),
                   pl.BlockSpec((1, 1, LANES), lambda b, c: (b, 0, 0))],
        out_shape=[jax.ShapeDtypeStruct((nb, t, D_M), BF16),
                   jax.ShapeDtypeStruct((nb, N_HEADS_M, HEAD_DIM_M, HEAD_DIM_M), F32),
                   jax.ShapeDtypeStruct((nb, N_HEADS_M, HEAD_DIM_M), F32),
                   jax.ShapeDtypeStruct((nb, 1, LANES), F32)],
        scratch_shapes=[pltpu.VMEM((SUBLANES, 2 * D_M), F32)],
        compiler_params=_params("parallel", "arbitrary"),
        name="mlstm_prompt",
    )(qk, v, o, small, conv_w, conv_b, bif, mn_g, tri, triu)


def _softmax_rows(s):
    e = jnp.exp(s - jnp.max(s, axis=-1, keepdims=True))
    return e, jnp.sum(e, axis=-1, keepdims=True)


def _sample_cmp_kernel(pt_ref, *refs, n_pages, n_sel, qpos):
    pages = refs[:n_pages]
    w_ref, b_ref, g_ref, q_ref, bias_ref, cov_ref, oc_ref, idx_ref, xk_ref, xv_ref = refs[n_pages:]
    for half, rows_ref in ((0, xk_ref), (1, xv_ref)):
        for u in range(n_pages):
            rows_ref[u * PAGE_SIZE:(u + 1) * PAGE_SIZE, :] = pages[u][0, half * KV_W:(half + 1) * KV_W, :].T
    nc = n_pages * PAGE_SIZE // CMP_STRIDE
    kc, vc = _compress_rows(xk_ref, xv_ref, nc, w_ref, b_ref, g_ref)
    q = q_ref[0]
    j_row = lax.broadcasted_iota(jnp.int32, (N_HEADS_A, nc), 1)
    s = _dot_nt(q, kc.astype(BF16)) + bias_ref[...] + jnp.where(j_row < nc - 1, 0.0, NEG)
    e, l = _softmax_rows(s)
    p = e / jnp.maximum(l, 1e-30)
    oc_ref[0] = _dot(p.astype(BF16), vc.astype(BF16))

    nsp = cov_ref.shape[1]
    blk_row = lax.broadcasted_iota(jnp.int32, (1, nsp), 1)
    sub = lax.broadcasted_iota(jnp.int32, (nsp, nsp), 0)
    lan = lax.broadcasted_iota(jnp.int32, (nsp, nsp), 1)
    cur = qpos // SEL_BLOCK
    valid = (blk_row * SEL_BLOCK <= qpos) & (blk_row < n_sel)
    forced = valid & ((blk_row == 0) | (blk_row == cur) | (blk_row == cur - 1))
    k_col = lax.broadcasted_iota(jnp.int32, (TOP_K_BLOCKS, nsp), 0).astype(F32)
    blk_f = lax.broadcasted_iota(jnp.int32, (TOP_K_BLOCKS, nsp), 1).astype(F32)
    for g in range(N_KV_A):
        p_sum = jnp.sum(p[g * GROUP_R:(g + 1) * GROUP_R, :], axis=0, keepdims=True)
        imp = _dot_exact_rhs(jnp.broadcast_to(p_sum, (SUBLANES, nc)), cov_ref[...])[0:1, :]
        score = jnp.where(forced, -NEG, jnp.where(valid, imp, NEG))
        score_col = jnp.sum(jnp.where(sub == lan, score, 0.0), axis=-1, keepdims=True)
        ahead = jnp.where(sub < lan, jnp.where(score_col >= score, 1.0, 0.0), jnp.where(score_col > score, 1.0, 0.0))
        rank = jnp.sum(ahead, axis=0, keepdims=True)
        chosen = jnp.sum(jnp.where(rank == k_col, blk_f, 0.0), axis=-1, keepdims=True)
        idx_ref[0, g * TOP_K_BLOCKS:(g + 1) * TOP_K_BLOCKS, :] = jnp.broadcast_to(
            chosen, (TOP_K_BLOCKS, LANES)).astype(jnp.int32)


def _sample_cmp(page_table, pool, w_big, b_big, g_k, q3, bias_c, cover, n_sel, qpos):
    ns, n_pages = page_table.shape
    page_spec = lambda u: pl.BlockSpec((1, KV_ROW, PAGE_SIZE), lambda s, pt: (pt[s, u], 0, 0))
    const = lambda a: pl.BlockSpec(a.shape, lambda s, pt: (0,) * a.ndim, pipeline_mode=pl.Buffered(1))
    grid_spec = pltpu.PrefetchScalarGridSpec(
        num_scalar_prefetch=1,
        grid=(ns,),
        in_specs=[page_spec(u) for u in range(n_pages)]
        + [const(w_big), const(b_big), const(g_k),
           pl.BlockSpec((1, N_HEADS_A, LANES), lambda s, pt: (s, 0, 0)), const(bias_c), const(cover)],
        out_specs=[pl.BlockSpec((1, N_HEADS_A, LANES), lambda s, pt: (s, 0, 0)),
                   pl.BlockSpec((1, N_KV_A * TOP_K_BLOCKS, LANES), lambda s, pt: (s, 0, 0))],
        scratch_shapes=[pltpu.VMEM((n_pages * PAGE_SIZE, KV_W), F32)] * 2,
    )
    return pl.pallas_call(
        functools.partial(_sample_cmp_kernel, n_pages=n_pages, n_sel=n_sel, qpos=qpos),
        grid_spec=grid_spec,
        out_shape=[jax.ShapeDtypeStruct((ns, N_HEADS_A, LANES), F32),
                   jax.ShapeDtypeStruct((ns, N_KV_A * TOP_K_BLOCKS, LANES), jnp.int32)],
        compiler_params=_params("arbitrary"),
        name="sample_cmp_topk",
    )(page_table, *([pool] * n_pages), w_big, b_big, g_k, q3, bias_c, cover)


def _sample_attn_kernel(idx_ref, pt_ref, *refs, n_past_blocks):
    n_slots = N_KV_A * TOP_K_BLOCKS
    blocks = refs[:n_slots]
    (win_ref, q_ref, selnew_ref, winnew_ref, wint_ref, oc_ref, gate_ref, bsel_ref, bwin_ref, b0_ref,
     a_ref, winout_ref) = refs[n_slots:]
    s_id = pl.program_id(0)
    q = q_ref[0]
    qf = q.astype(F32)
    lane = lax.broadcasted_iota(jnp.int32, (N_HEADS_A, LANES), 1)
    row = lax.broadcasted_iota(jnp.int32, (N_HEADS_A, LANES), 0)
    lo = lane < SEL_BLOCK

    def with_new_key(s, vt_mat, new_row):
        s_new = jnp.sum(qf * new_row[:, 0:KV_W], axis=-1, keepdims=True) + b0_ref[:, 0:1]
        m = jnp.maximum(jnp.max(s, axis=-1, keepdims=True), s_new)
        p = jnp.exp(s - m)
        p_new = jnp.exp(s_new - m)
        l = jnp.sum(p, axis=-1, keepdims=True) + p_new
        return (_dot_nt(p.astype(BF16), vt_mat) + p_new * new_row[:, KV_W:KV_ROW]) / l

    sel_new = selnew_ref[0]
    o_s = []
    for g in range(N_KV_A):
        slots = range(g * TOP_K_BLOCKS, (g + 1) * TOP_K_BLOCKS)
        kt_cat = jnp.concatenate([blocks[k][0, 0:KV_W, :] for k in slots], axis=1).astype(BF16)
        vt_cat = jnp.concatenate([blocks[k][0, KV_W:KV_ROW, :] for k in slots], axis=1).astype(BF16)
        adds = []
        for k in slots:
            b = idx_ref[s_id * n_slots + k]
            add = jnp.where(b == n_past_blocks - 2, bsel_ref[0], jnp.where(b == n_past_blocks - 1, bsel_ref[1], 0.0))
            other_half = jnp.where(b & 1 == 0, jnp.where(lo, 0.0, NEG), jnp.where(lo, NEG, 0.0))
            adds.append(add + other_half + jnp.where(b >= n_past_blocks, NEG, 0.0))
        o_s.append(with_new_key(_dot(q, kt_cat) + jnp.concatenate(adds, axis=-1), vt_cat, sel_new))
    o_sel = jnp.where(row < GROUP_R, o_s[0], o_s[1])

    win = win_ref[0]
    wb = win.shape[1]
    w_lane = lax.broadcasted_iota(jnp.int32, (N_HEADS_A, wb), 1)
    s_w = _dot(q, win[0:KV_W, :].astype(BF16)) + bwin_ref[...] + jnp.where(w_lane == 0, NEG, 0.0)
    o_win = with_new_key(s_w, win[KV_W:KV_ROW, :].astype(BF16), winnew_ref[0])

    na = N_HEADS_A
    out = gate_ref[0, 0:na] * oc_ref[0] + gate_ref[0, na:2 * na] * o_sel + gate_ref[0, 2 * na:3 * na] * o_win
    a_ref[0] = jnp.where(lo == (row < GROUP_R), out, 0.0).astype(BF16)

    wint = wint_ref[...]
    seq_lane = lax.broadcasted_iota(jnp.int32, wint.shape, 1)
    new_col = jnp.sum(jnp.where(seq_lane == s_id, wint, 0.0), axis=-1, keepdims=True)
    buf_lane = lax.broadcasted_iota(jnp.int32, win.shape, 1)
    winout_ref[0] = jnp.where(buf_lane == wb - 1, new_col, pltpu.roll(win, wb - 1, 1))


def _sample_attn(idx, page_table, pool, win_t, q3, sel_new, win_new, win_new_t, o_c, gate_b, bsel, bwin, b0):
    ns, n_pages = page_table.shape
    n_slots = N_KV_A * TOP_K_BLOCKS
    per_page = PAGE_SIZE // SEL_BLOCK
    n_past_blocks = n_pages * per_page
    wb = win_t.shape[2]

    def slot_spec(k):
        def index(s, idx_ref, pt_ref):
            b = jnp.minimum(idx_ref[s * n_slots + k], n_past_blocks - 1)
            return (pt_ref[s, lax.shift_right_logical(b, per_page.bit_length() - 1)], 0, 0)
        return pl.BlockSpec((1, KV_ROW, PAGE_SIZE), index)

    per_seq = lambda a: pl.BlockSpec((1,) + a.shape[1:], lambda s, *_: (s,) + (0,) * (a.ndim - 1))
    const = lambda a: pl.BlockSpec(a.shape, lambda s, *_: (0,) * a.ndim, pipeline_mode=pl.Buffered(1))
    grid_spec = pltpu.PrefetchScalarGridSpec(
        num_scalar_prefetch=2,
        grid=(ns,),
        in_specs=[slot_spec(k) for k in range(n_slots)]
        + [per_seq(win_t), per_seq(q3), per_seq(sel_new), per_seq(win_new), const(win_new_t), per_seq(o_c),
           per_seq(gate_b), const(bsel), const(bwin), const(b0)],
        out_specs=[pl.BlockSpec((1, N_HEADS_A, LANES), lambda s, *_: (s, 0, 0)),
                   pl.BlockSpec((1, KV_ROW, wb), lambda s, *_: (s, 0, 0))],
    )
    return pl.pallas_call(
        functools.partial(_sample_attn_kernel, n_past_blocks=n_past_blocks),
        grid_spec=grid_spec,
        out_shape=[jax.ShapeDtypeStruct((ns, N_HEADS_A, LANES), BF16),
                   jax.ShapeDtypeStruct((ns, KV_ROW, wb), F32)],
        compiler_params=_params("arbitrary"),
        name="sample_sel_win",
    )(idx, page_table, *([pool] * n_slots), win_t, q3, sel_new, win_new, win_new_t, o_c, gate_b, bsel, bwin, b0)


def _mlstm_step_kernel(qk_ref, conv_ref, v_ref, o_ref, small_ref, c_ref, n_ref, m_ref, cw_ref, cb_ref, bif_ref,
                       mn_ref, y_ref, c_out, n_out, m_out):
    seqs = qk_ref.shape[0]
    sub = lax.broadcasted_iota(jnp.int32, (HEAD_DIM_M, HEAD_DIM_M), 0)
    lan = lax.broadcasted_iota(jnp.int32, (HEAD_DIM_M, HEAD_DIM_M), 1)
    eye = sub == lan
    lane_row = lax.broadcasted_iota(jnp.int32, (1, LANES), 1)
    for s in range(seqs):
        hist = conv_ref[s]
        y = hist[0:1, :] * cw_ref[0:1, :]
        y = y + hist[1:2, :] * cw_ref[1:2, :]
        y = y + hist[2:3, :] * cw_ref[2:3, :]
        y = y + qk_ref[s] * cw_ref[3:4, :] + cb_ref[...]
        qkc = y * _sigmoid(y)
        gi = small_ref[s] + bif_ref[...]
        lf_all = _log_sigmoid(gi)
        m_all = m_ref[s]
        m_new_all = m_all
        h_all = []
        for h in range(N_HEADS_M):
            hs = slice(h * HEAD_DIM_M, (h + 1) * HEAD_DIM_M)
            ks = slice(D_M + h * HEAD_DIM_M, D_M + (h + 1) * HEAD_DIM_M)
            q = qkc[:, hs]
            k = qkc[:, ks] * F32(HEAD_DIM_M ** -0.5)
            v = v_ref[s][:, hs]
            ci, fi = 3 * N_HEADS_A + h, 3 * N_HEADS_A + N_HEADS_M + h
            ig = gi[:, ci:ci + 1]
            lf = lf_all[:, fi:fi + 1]
            m_prev = m_all[:, h:h + 1]
            c_prev = c_ref[s, h]
            n_prev = n_ref[s, h:h + 1, :]
            inter = lf + m_prev
            m_t = jnp.maximum(inter, ig)
            w_in = jnp.exp(ig - m_t)
            dec = jnp.exp(inter - m_t)
            s_qk = jnp.sum(q * k, axis=-1, keepdims=True) * w_in
            cq = _dot_nt(jnp.broadcast_to(q, (SUBLANES, HEAD_DIM_M)).astype(BF16), c_prev.astype(BF16))[0:1, :]
            num = s_qk * v + dec * cq
            den = s_qk + dec * jnp.sum(n_prev * q, axis=-1, keepdims=True)
            h_all.append(num / jnp.maximum(jnp.abs(den), jnp.exp(-m_t)))
            v_col = jnp.sum(jnp.where(eye, v, 0.0), axis=-1, keepdims=True)
            c_out[s, h] = dec * c_prev + (w_in * v_col) * k
            n_out[s, h:h + 1, :] = dec * n_prev + w_in * k
            m_new_all = jnp.where(lane_row == h, m_t, m_new_all)
        m_out[s] = m_new_all
        y_ref[s] = _head_norm_gate(h_all, mn_ref, o_ref[s]).astype(BF16)


def _mlstm_step(qk, conv0, v, o, small, c0, n0, m0, conv_w, conv_b, bif, mn_g):
    ns = qk.shape[0]
    sb = STEP_SEQS if ns % STEP_SEQS == 0 else 1
    per = lambda a: pl.BlockSpec((sb,) + a.shape[1:], lambda i: (i,) + (0,) * (a.ndim - 1))
    ins = (qk, conv0, v, o, small, c0, n0, m0)
    outs = [jax.ShapeDtypeStruct((ns, 1, D_M), BF16), jax.ShapeDtypeStruct(c0.shape, F32),
            jax.ShapeDtypeStruct(n0.shape, F32), jax.ShapeDtypeStruct(m0.shape, F32)]
    return pl.pallas_call(
        _mlstm_step_kernel,
        grid=(ns // sb,),
        in_specs=[per(a) for a in ins] + [_const_spec(a.shape) for a in (conv_w, conv_b, bif, mn_g)],
        out_specs=[per(a) for a in outs],
        out_shape=outs,
        compiler_params=_params("parallel"),
        name="mlstm_step",
    )(*ins, conv_w, conv_b, bif, mn_g)


def _cover_np(n_cmp, n_sel):
    cs = np.arange(n_cmp)[:, None] * CMP_STRIDE
    bs = np.arange(n_sel)[None, :] * SEL_BLOCK
    shared = np.clip(np.minimum(cs + CMP_LEN, bs + SEL_BLOCK) - np.maximum(cs, bs), 0, None)
    return (shared / CMP_LEN).astype(np.float32)


def _prep_in_proj(w_in):
    d = w_in.shape[0]
    o_kv = D_A
    o_gate = o_kv + 6 * KV_W
    o_m = o_gate + 3 * N_HEADS_A
    o_if = o_m + 4 * D_M
    w_kv_t = w_in[:, o_kv:o_gate].T
    w_m = w_in[:, o_m:o_if]
    w_small = jnp.concatenate([w_in[:, o_gate:o_m], w_in[:, o_if:],
                               jnp.zeros((d, LANES - 3 * N_HEADS_A - 2 * N_HEADS_M), w_in.dtype)], axis=1)
    w_t = jnp.concatenate([w_in[:, :D_A].T, w_kv_t, w_small.T], axis=0)
    return [w.astype(BF16) for w in (w_t, w_m, w_small)]


def _prep_compress(w_k, b_k, w_v, b_v):
    per_c = jnp.stack([w_k, w_k, w_v, w_v]).reshape(4, 2, CMP_STRIDE, HEAD_DIM_A, HEAD_DIM_A)
    w_big = jnp.einsum('chtde,cf->tcdhfe', per_c, jnp.eye(4, dtype=w_k.dtype))
    w_big = w_big.reshape(CMP_STRIDE, KV_ROW, 2 * KV_ROW).astype(BF16)
    pick = lambda h: jnp.concatenate([w_big[:, h * KV_W:(h + 1) * KV_W, h * KV_W:(h + 1) * KV_W],
                                      w_big[:, h * KV_W:(h + 1) * KV_W, KV_ROW + h * KV_W:KV_ROW + (h + 1) * KV_W]],
                                     axis=-1)
    w_big = jnp.stack([pick(0), pick(1)])
    b_big = jnp.concatenate([b_k, b_k, b_v, b_v]).reshape(1, KV_ROW)
    return w_big, b_big


def _bucket_tiles(t, past, wb):
    r = np.arange(LANES)
    tiles = [_t5_bucket_np(d * LANES + r[None, :] - r[:, None]) for d in range(2)]
    for i in range(t // Q_TILE):
        tiles.append(_t5_bucket_np(i * Q_TILE + r[None, :] - (CMP_STRIDE * r[:, None] + CMP_LEN - 1)))
    flat = np.arange(LANES * LANES).reshape(LANES, LANES)
    tiles.append(_t5_bucket_np(past - (CMP_STRIDE * flat + CMP_LEN - 1)))
    tiles.append(_t5_bucket_np(wb - flat))
    tiles.append(_t5_bucket_np(past - (r[:, None] * SEL_BLOCK + r[None, :] % SEL_BLOCK)))
    return np.stack(tiles).astype(np.int32)


def _near_tables(bnear):
    r = np.arange(LANES)
    tile4 = lambda a: jnp.asarray(np.tile(a, (1, GROUP_R)), F32)
    causal = tile4(np.where(r[:, None] <= r[None, :], 0.0, NEG))
    oldest = tile4(np.where(r[:, None] > r[None, :], 0.0, NEG))
    zero = jnp.zeros_like(causal)
    masked = jnp.full_like(causal, NEG)
    n_mid = WINDOW // LANES - 2
    sel_add, win_add = [], []
    for g in range(N_KV_A):
        near = [bnear[1, g], bnear[0, g] + causal]
        sel_add.append(jnp.concatenate(near + [masked], axis=0))
        win_add.append(jnp.concatenate([oldest] + [zero] * n_mid + near + [masked] * (WINDOW // LANES), axis=0))
    return jnp.stack(sel_add), jnp.stack(win_add)


def kernel(x_prompt, x_sample, cache_cmp_kv, cache_sel_kv, cache_win_kv, state_C, state_n, state_m, state_conv, page_table, rel_bias, g_ffn1, w1_gate, w1_up, w1_down, g_mix, w_in, qn_g, kn_cmp_g, kn_sel_g, kn_win_g, w_cmp_k, b_cmp_k, w_cmp_v, b_cmp_v, conv_w, conv_b, b_if, mn_g, w_out, g_ffn2, w2_gate, w2_up, w2_down):
    assert x_prompt.shape[2] == D_MODEL and g_ffn1.shape[0] == 1
    nb, t, _ = x_prompt.shape
    ns, ds, _ = x_sample.shape
    assert ds == 1 and t % Q_TILE == 0
    n_pages = page_table.shape[1]
    past = n_pages * PAGE_SIZE
    wb = cache_win_kv.shape[2]
    assert wb == WINDOW
    kv_shape = (2, N_KV_A, HEAD_DIM_A)

    row = lambda a: a.reshape(1, -1)
    two = lambda a: jnp.tile(a, 2).reshape(1, KV_W)
    w1 = (row(g_ffn1[0]), w1_gate[0].astype(BF16), w1_up[0].astype(BF16), w1_down[0].astype(BF16))
    w2 = (row(g_ffn2[0]), w2_gate[0].astype(BF16), w2_up[0].astype(BF16), w2_down[0].astype(BF16))
    wt, wm, ws = _prep_in_proj(w_in[0])
    qg = (qn_g[0] * F32(HEAD_DIM_A ** -0.5))[:, None]
    kgt = jnp.stack([jnp.tile(kn_sel_g[0], 2), jnp.tile(kn_win_g[0], 2)])[:, :, None]
    w_big, b_big = _prep_compress(w_cmp_k[0], b_cmp_k[0], w_cmp_v[0], b_cmp_v[0])
    g_cmp = two(kn_cmp_g[0])
    woa, wom = w_out[0][:D_A].astype(BF16), w_out[0][D_A:].astype(BF16)
    n_gate = 3 * N_HEADS_A
    bif = jnp.zeros((1, LANES), F32).at[0, n_gate:n_gate + 2 * N_HEADS_M].set(b_if[0])
    conv_b2 = row(conv_b[0])
    mn = row(mn_g[0])

    nqb = t // Q_TILE
    tiles = _bias_tiles(rel_bias, jnp.asarray(_bucket_tiles(t, past, wb)))
    group_lanes = lambda a: a.reshape(a.shape[0], N_KV_A, GROUP_R, LANES, LANES).transpose(0, 1, 3, 2, 4).reshape(
        a.shape[0], N_KV_A, LANES, GROUP_R * LANES)
    bnear = group_lanes(tiles[0:2]) * F32(LOG2_E)
    sel_add, win_add = _near_tables(bnear)
    bcmp = group_lanes(tiles[2:2 + nqb]) * F32(LOG2_E)
    t_cmp, t_win, t_sel = tiles[2 + nqb], tiles[3 + nqb], tiles[4 + nqb]

    def dense_in(x, seq_len, q_gain, kmask):
        x1 = _ffn(x, *w1)
        return (x1,) + tuple(_inproj(x1, seq_len, kmask, row(g_mix[0]), wt, wm, ws, q_gain, kgt))

    n_sel_p = -(-t // SEL_BLOCK)
    assert n_sel_p + t // LANES <= LANES
    kmask_np = np.zeros((t, LANES), np.float32)
    kmask_np[np.arange(t), np.arange(t) // SEL_BLOCK] = -MASK_BIG
    kmask_np[np.arange(t), n_sel_p + np.arange(t) // LANES] = -MASK_BIG

    from_chan = lambda a: a.reshape((1, a.shape[0]) + kv_shape + (a.shape[2],)).transpose(0, 1, 5, 2, 3, 4)
    to_chan = lambda a: a.transpose(0, 2, 3, 4, 1).reshape(a.shape[0], KV_ROW, a.shape[1])

    (x1, q_t, cmp_rows, cmp_t, sel_t, win_t, ks, vs, kw, vw, qk_m, v_m, o_m, small, gate_t) = dense_in(
        x_prompt.reshape(nb * t, D_MODEL), t, qg * F32(LOG2_E),
        jnp.asarray(kmask_np, BF16))
    seq = lambda a: a.reshape(nb, t, a.shape[-1])
    n16 = t // CMP_STRIDE
    assert n16 == LANES
    kc, vc_t = _compress(cmp_rows, t, w_big, b_big, g_cmp)
    n_sel = -(-t // SEL_BLOCK)
    cov_t = jnp.asarray(np.pad(_cover_np(n16 - 1, n_sel), ((0, 1), (0, 0))).T, BF16)
    a_pad = _nsa_prompt(q_t, gate_t, kc, vc_t, ks, vs, kw, vw, sel_add, win_add, bcmp, cov_t, n16 - 1)
    chunk = MLSTM_CHUNK if t % MLSTM_CHUNK == 0 else Q_TILE
    m_out, c_p, n_p, m_p = _mlstm_prompt(seq(qk_m), seq(v_m), seq(o_m), seq(small), conv_w[0], conv_b2, bif, mn,
                                         chunk)
    y_prompt = _post(x1, a_pad.reshape(nb * t, -1), m_out.reshape(nb * t, -1), woa, wom, *w2)
    kv6 = lambda a, n: a.reshape((1, n, -1) + kv_shape)
    prompt_states = (from_chan(cmp_t), from_chan(sel_t), from_chan(win_t[:, :, t - wb:]), c_p[None], n_p[None],
                     m_p[None, :, 0, :N_HEADS_M], seq(qk_m)[None, :, t - (CONV_W - 1):])

    (x1s, q_st, cmp_halves, _, sel_st, win_st, _, _, _, _, qk_s, v_s, o_s, small_s, gate_st) = dense_in(
        x_sample.reshape(ns, D_MODEL), ns, qg, jnp.zeros((ns, LANES), BF16))
    cmp_s = jnp.concatenate([cmp_halves[0], cmp_halves[1]], axis=1)
    sel_s, win_s, gates_s = sel_st[0].T, win_st[0].T, gate_st[0].T
    q3 = q_st[0].T.reshape(ns, N_HEADS_A, LANES)
    n16s = past // CMP_STRIDE
    n_sel_s = -(-(past + 1) // SEL_BLOCK)
    n_sel_pad = -(-n_sel_s // LANES) * LANES
    cover_s = jnp.asarray(np.pad(_cover_np(n16s - 1, n_sel_s), ((0, 1), (0, n_sel_pad - n_sel_s))), BF16)
    bias_c = t_cmp[:, :n16s // LANES, :].reshape(N_HEADS_A, n16s)
    o_c, idx = _sample_cmp(page_table, to_chan(cache_cmp_kv[0]), w_big, b_big, g_cmp, q3, bias_c, cover_s, n_sel_s,
                           past)
    n_past_blocks = past // SEL_BLOCK
    two_lanes = lambda a: jnp.concatenate([a[:, :SEL_BLOCK], a[:, :SEL_BLOCK]], axis=-1)
    bsel = jnp.stack([two_lanes(t_sel[:, n_past_blocks - 2, :]), two_lanes(t_sel[:, n_past_blocks - 1, :])])
    bwin = t_win[:, :wb // LANES, :].reshape(N_HEADS_A, wb)
    b0 = jnp.broadcast_to(t_win[:, wb // LANES, 0:1], (N_HEADS_A, LANES))
    gate_b = jnp.broadcast_to(gates_s[:, :n_gate, None], (ns, n_gate, LANES))
    a_s, win_out = _sample_attn(idx[:, :, 0].reshape(-1), page_table, to_chan(cache_sel_kv[0]),
                                to_chan(cache_win_kv[0]), q3, sel_s[:, None, :], win_s[:, None, :], win_st[0],
                                o_c, gate_b, bsel, bwin, b0)
    m0 = jnp.pad(state_m[0], ((0, 0), (0, LANES - N_HEADS_M)))[:, None, :]
    m_s, c_s, n_s, m_new = _mlstm_step(qk_s[:, None, :], state_conv[0], v_s[:, None, :], o_s[:, None, :],
                                       small_s[:, None, :], state_C[0], state_n[0], m0, conv_w[0], conv_b2, bif, mn)
    a_dense = jnp.concatenate([a_s[:, h, (h // GROUP_R) * HEAD_DIM_A:(h // GROUP_R + 1) * HEAD_DIM_A]
                               for h in range(N_HEADS_A)], axis=1)
    y_sample = _post(x1s, a_dense, m_s.reshape(ns, -1), woa, wom, *w2)
    conv_new = jnp.concatenate([state_conv[0][:, 1:], qk_s[:, None, :]], axis=1)
    sample_states = (kv6(cmp_s, ns), kv6(sel_s, ns), from_chan(win_out), c_s[None], n_s[None],
                     m_new[None, :, 0, :N_HEADS_M], conv_new[None])

    return (y_prompt.reshape(nb, t, D_MODEL), y_sample.reshape(ns, 1, D_MODEL)) + prompt_states + sample_states
```
